```python
import math
import jax, jax.numpy as jnp
from jax import lax
import numpy as np

D_MODEL = 1024
BATCH = 4
SEQ = 8192
DEPTH = 2

HEAD_DIM = 64
ROPE_THETA = 10000.0
Q_BLOCK = 128
LN_EPS = 1e-5

A_HEADS = 4
A_KV_RANK = 128
A_IDX_HEADS = 4
A_IDX_DIM = 64
A_TOPK_MAX = 256

B_HEADS = 4
B_BLOCK = 256
B_TOPK_BLOCKS = 3

C_HEADS = 8
C_KV_HEADS = 2
C_WINDOW = 128

N_BRANCH = 3
A_W = A_HEADS * HEAD_DIM
B_W = B_HEADS * HEAD_DIM
C_W = C_HEADS * HEAD_DIM
MIX_W = A_W + B_W + C_W

IN_SIZES = (A_W, A_KV_RANK, A_IDX_HEADS * A_IDX_DIM, A_IDX_DIM, A_IDX_HEADS,
            B_W, B_W, B_W,
            C_W, C_KV_HEADS * HEAD_DIM, C_KV_HEADS * HEAD_DIM,
            N_BRANCH * D_MODEL)
N_IN = sum(IN_SIZES)

N_EXPERTS = 16
N_GROUPS = 4
EXPERTS_PER_GROUP = N_EXPERTS // N_GROUPS
TOP_K = 2
D_EXPERT = 512

DN_ALPHA = (2 * DEPTH) ** 0.25
DN_BETA = (8 * DEPTH) ** -0.25

kernel_name = "hybrid_dsa_moba_swa_groupmoe_deepnorm"


def _split_points(sizes):
    pts, acc = [], 0
    for s in sizes[:-1]:
        acc += s
        pts.append(acc)
    return pts


def layer_norm(x, g, b):
    xf = x.astype(jnp.float32)
    mu = jnp.mean(xf, -1, keepdims=True)
    var = jnp.mean(jnp.square(xf - mu), -1, keepdims=True)
    return ((xf - mu) * lax.rsqrt(var + LN_EPS) * g + b).astype(x.dtype)


def rope_tables(seq_len, dim):
    inv = 1.0 / (ROPE_THETA ** (jnp.arange(0, dim, 2, dtype=jnp.float32) / dim))
    ang = jnp.arange(seq_len, dtype=jnp.float32)[:, None] * inv[None, :]
    return jnp.cos(ang), jnp.sin(ang)


def apply_rope(t, cos, sin):
    t1, t2 = jnp.split(t, 2, axis=-1)
    c = cos[None, :, None, :].astype(t.dtype)
    s = sin[None, :, None, :].astype(t.dtype)
    return jnp.concatenate([t1 * c - t2 * s, t2 * c + t1 * s], axis=-1)


def dsa_attention(q, k, v, iq, ik, iw):
    Bn, S, HA, dh = q.shape
    topk = min(A_TOPK_MAX, S // 4)
    nqb = S // Q_BLOCK
    scale = dh ** -0.5
    key_pos = jnp.arange(S)
    gather = jax.vmap(lambda arr, idx: arr[idx])

    def block(i):
        t0 = i * Q_BLOCK
        qb = lax.dynamic_slice_in_dim(q, t0, Q_BLOCK, 1)
        iqb = lax.dynamic_slice_in_dim(iq, t0, Q_BLOCK, 1)
        iwb = lax.dynamic_slice_in_dim(iw, t0, Q_BLOCK, 1)
        qpos = t0 + jnp.arange(Q_BLOCK)
        dots = jnp.einsum('bqhd,bsd->bqhs', iqb, ik)
        score = jnp.einsum('bqh,bqhs->bqs', iwb, jax.nn.relu(dots)).astype(jnp.float32)
        causal = key_pos[None, :] <= qpos[:, None]
        score = jnp.where(causal[None], score, -jnp.inf)
        _, idx = lax.top_k(score, topk)
        valid = idx <= qpos[None, :, None]
        kg = gather(k, idx)
        vg = gather(v, idx)
        s = jnp.einsum('bqhd,bqkd->bhqk', qb, kg).astype(jnp.float32) * scale
        s = jnp.where(valid[:, None], s, -jnp.inf)
        p = jax.nn.softmax(s, axis=-1).astype(v.dtype)
        return jnp.einsum('bhqk,bqkd->bqhd', p, vg)

    out = lax.map(block, jnp.arange(nqb))
    return out.transpose(1, 0, 2, 3, 4).reshape(Bn, S, HA * dh)


def moba_attention(q, k, v):
    Bn, S, H, dh = q.shape
    nkb = -(-S // B_BLOCK)
    pad = nkb * B_BLOCK - S
    kp = jnp.pad(k, ((0, 0), (0, pad), (0, 0), (0, 0)))
    vp = jnp.pad(v, ((0, 0), (0, pad), (0, 0), (0, 0)))
    kblk = kp.reshape(Bn, nkb, B_BLOCK, H, dh)
    vblk = vp.reshape(Bn, nkb, B_BLOCK, H, dh)
    kmean = jnp.mean(kblk, axis=2)
    kbt = kblk.transpose(0, 3, 1, 2, 4)
    vbt = vblk.transpose(0, 3, 1, 2, 4)
    nsel = min(B_TOPK_BLOCKS, nkb)
    nqb = S // Q_BLOCK
    scale = dh ** -0.5
    gather_bh = jax.vmap(jax.vmap(lambda arr, idx: arr[idx]))
    blk_ids = jnp.arange(nkb)

    def block(i):
        t0 = i * Q_BLOCK
        cur = t0 // B_BLOCK
        qb = lax.dynamic_slice_in_dim(q, t0, Q_BLOCK, 1)
        qpos = t0 + jnp.arange(Q_BLOCK)
        gate = jnp.einsum('bqhd,bnhd->bhqn', qb, kmean).astype(jnp.float32)
        gate = jnp.where(blk_ids < cur, gate, -jnp.inf)
        _, sel = lax.top_k(gate, nsel)
        sel_valid = sel < cur
        kg = gather_bh(kbt, sel)
        vg = gather_bh(vbt, sel)
        s_sel = jnp.einsum('bqhd,bhqnkd->bhqnk', qb, kg).astype(jnp.float32) * scale
        s_sel = jnp.where(sel_valid[..., None], s_sel, -jnp.inf)
        s_sel = s_sel.reshape(Bn, H, Q_BLOCK, nsel * B_BLOCK)
        k_own = lax.dynamic_slice_in_dim(kp, cur * B_BLOCK, B_BLOCK, 1)
        v_own = lax.dynamic_slice_in_dim(vp, cur * B_BLOCK, B_BLOCK, 1)
        own_pos = cur * B_BLOCK + jnp.arange(B_BLOCK)
        s_own = jnp.einsum('bqhd,bkhd->bhqk', qb, k_own).astype(jnp.float32) * scale
        s_own = jnp.where((own_pos[None, :] <= qpos[:, None])[None, None], s_own, -jnp.inf)
        p = jax.nn.softmax(jnp.concatenate([s_own, s_sel], axis=-1), axis=-1).astype(v.dtype)
        p_own = p[..., :B_BLOCK]
        p_sel = p[..., B_BLOCK:].reshape(Bn, H, Q_BLOCK, nsel, B_BLOCK)
        return (jnp.einsum('bhqk,bkhd->bqhd', p_own, v_own)
                + jnp.einsum('bhqnk,bhqnkd->bqhd', p_sel, vg))

    out = lax.map(block, jnp.arange(nqb))
    return out.transpose(1, 0, 2, 3, 4).reshape(Bn, S, H * dh)


def swa_sink_attention(q, k, v, sinks):
    Bn, S, HC, dh = q.shape
    KV = k.shape[2]
    G = HC // KV
    W = C_WINDOW
    nb = S // W
    scale = dh ** -0.5
    qb = q.reshape(Bn, nb, W, KV, G, dh)
    kb = k.reshape(Bn, nb, W, KV, dh)
    vb = v.reshape(Bn, nb, W, KV, dh)
    kband = jnp.concatenate([jnp.pad(kb, ((0, 0), (1, 0), (0, 0), (0, 0), (0, 0)))[:, :-1], kb], axis=2)
    vband = jnp.concatenate([jnp.pad(vb, ((0, 0), (1, 0), (0, 0), (0, 0), (0, 0)))[:, :-1], vb], axis=2)
    s = jnp.einsum('bnqkgd,bnskd->bnkgqs', qb, kband).astype(jnp.float32) * scale
    qrel = jnp.arange(W)[:, None] + W
    krel = jnp.arange(2 * W)[None, :]
    diff = qrel - krel
    kabs = jnp.arange(nb)[:, None, None] * W - W + krel[None]
    mask = (diff >= 0)[None] & (diff < W)[None] & (kabs >= 0)
    s = jnp.where(mask[None, :, None, None], s, -jnp.inf)
    sink = jnp.broadcast_to(sinks.astype(jnp.float32).reshape(1, 1, KV, G, 1, 1), s.shape[:-1] + (1,))
    p = jax.nn.softmax(jnp.concatenate([s, sink], axis=-1), axis=-1)[..., :-1].astype(v.dtype)
    out = jnp.einsum('bnkgqs,bnskd->bnqkgd', p, vband)
    return out.reshape(Bn, S, HC * dh)


def mixer(x, w_in, a_w_uk, a_w_uv, c_sinks, w_branch, w_o, cos, sin):
    Bn, S, _ = x.shape
    h = x @ w_in
    (a_q, a_c, a_iq, a_ik, a_iw, b_q, b_k, b_v, c_q, c_k, c_v, gates) = jnp.split(
        h, _split_points(IN_SIZES), axis=-1)
    heads = lambda t, n: t.reshape(Bn, S, n, -1)
    qa = apply_rope(heads(a_q, A_HEADS), cos, sin)
    ka = apply_rope((a_c @ a_w_uk)[:, :, None, :], cos, sin)[:, :, 0]
    va = a_c @ a_w_uv
    iq = apply_rope(heads(a_iq, A_IDX_HEADS), cos, sin)
    ik = apply_rope(a_ik[:, :, None, :], cos, sin)[:, :, 0]
    o_a = dsa_attention(qa, ka, va, iq, ik, a_iw)
    o_b = moba_attention(apply_rope(heads(b_q, B_HEADS), cos, sin),
                         apply_rope(heads(b_k, B_HEADS), cos, sin),
                         heads(b_v, B_HEADS))
    o_c = swa_sink_attention(apply_rope(heads(c_q, C_HEADS), cos, sin),
                             apply_rope(heads(c_k, C_KV_HEADS), cos, sin),
                             heads(c_v, C_KV_HEADS), c_sinks)
    wa, wb, wc = jnp.split(w_branch, [A_W, A_W + B_W], axis=0)
    ga, gb, gc = jnp.split(jax.nn.sigmoid(gates), N_BRANCH, axis=-1)
    merged = ga * (o_a @ wa) + gb * (o_b @ wb) + gc * (o_c @ wc)
    return merged @ w_o


def moe(x, router_w, router_b, w_gate, w_up, w_down):
    Bn, S, D = x.shape
    xt = x.reshape(-1, D)
    aff = jax.nn.sigmoid((xt @ router_w).astype(jnp.float32))
    sel_score = aff + router_b.astype(jnp.float32)
    grp_score = lax.top_k(sel_score.reshape(-1, N_GROUPS, EXPERTS_PER_GROUP), TOP_K)[0].sum(-1)
    g_star = jnp.argmax(grp_score, axis=-1)
    in_group = (jnp.arange(N_EXPERTS) // EXPERTS_PER_GROUP)[None, :] == g_star[:, None]
    _, top_idx = lax.top_k(jnp.where(in_group, sel_score, -jnp.inf), TOP_K)
    top_aff = jnp.take_along_axis(aff, top_idx, axis=-1)
    wts = top_aff / jnp.sum(top_aff, -1, keepdims=True)
    combine = jnp.sum(jax.nn.one_hot(top_idx, N_EXPERTS, dtype=jnp.float32) * wts[..., None], axis=1)
    combine = combine.astype(x.dtype)
    y = jnp.zeros_like(xt)
    for e in range(N_EXPERTS):
        he = jax.nn.silu(xt @ w_gate[e]) * (xt @ w_up[e])
        y = y + combine[:, e:e + 1] * (he @ w_down[e])
    return y.reshape(Bn, S, D)


def setup_inputs(seed: int = 0) -> dict:
    key = jax.random.key(seed)
    ks = jax.random.split(key, 20)
    nrm = lambda k, shape, scale: jax.random.normal(k, shape, jnp.float32) * scale
    w_branch = jnp.concatenate([
        nrm(ks[5], (DEPTH, A_W, D_MODEL), A_W ** -0.5),
        nrm(ks[6], (DEPTH, B_W, D_MODEL), B_W ** -0.5),
        nrm(ks[7], (DEPTH, C_W, D_MODEL), C_W ** -0.5)], axis=1)
    return {
        "x": nrm(ks[0], (BATCH, SEQ, D_MODEL), 1.0),
        "w_in": nrm(ks[1], (DEPTH, D_MODEL, N_IN), D_MODEL ** -0.5),
        "a_w_uk": nrm(ks[2], (DEPTH, A_KV_RANK, HEAD_DIM), A_KV_RANK ** -0.5),
        "a_w_uv": nrm(ks[3], (DEPTH, A_KV_RANK, HEAD_DIM), A_KV_RANK ** -0.5),
        "c_sinks": nrm(ks[4], (DEPTH, C_HEADS), 1.0),
        "w_branch": w_branch,
        "w_o": nrm(ks[8], (DEPTH, D_MODEL, D_MODEL), D_MODEL ** -0.5 * DN_BETA),
        "ln1_g": 1.0 + nrm(ks[9], (DEPTH, D_MODEL), 0.01),
        "ln1_b": nrm(ks[10], (DEPTH, D_MODEL), 0.01),
        "router_w": nrm(ks[11], (D_MODEL, N_EXPERTS), D_MODEL ** -0.5),
        "router_b": nrm(ks[12], (N_EXPERTS,), 0.01),
        "moe_w_gate": nrm(ks[13], (DEPTH, N_EXPERTS, D_MODEL, D_EXPERT), D_MODEL ** -0.5),
        "moe_w_up": nrm(ks[14], (DEPTH, N_EXPERTS, D_MODEL, D_EXPERT), D_MODEL ** -0.5),
        "moe_w_down": nrm(ks[15], (DEPTH, N_EXPERTS, D_EXPERT, D_MODEL), D_EXPERT ** -0.5 * DN_BETA),
        "ln2_g": 1.0 + nrm(ks[16], (DEPTH, D_MODEL), 0.01),
        "ln2_b": nrm(ks[17], (DEPTH, D_MODEL), 0.01),
    }


def reference(x, w_in, a_w_uk, a_w_uv, c_sinks, w_branch, w_o, ln1_g, ln1_b,
              router_w, router_b, moe_w_gate, moe_w_up, moe_w_down, ln2_g, ln2_b):
    S = x.shape[1]
    cos, sin = rope_tables(S, HEAD_DIM)
    for l in range(DEPTH):
        y = mixer(x, w_in[l], a_w_uk[l], a_w_uv[l], c_sinks[l], w_branch[l], w_o[l], cos, sin)
        x = layer_norm(DN_ALPHA * x + y, ln1_g[l], ln1_b[l])
        y = moe(x, router_w, router_b, moe_w_gate[l], moe_w_up[l], moe_w_down[l])
        x = layer_norm(DN_ALPHA * x + y, ln2_g[l], ln2_b[l])
    return x
```

```python
import functools
import math

import jax
import jax.numpy as jnp
import numpy as np
from jax import lax
from jax.experimental import pallas as pl
from jax.experimental.pallas import tpu as pltpu

D_MODEL = 1024
DEPTH = 2
HEAD_DIM = 64
ROPE_THETA = 10000.0
LN_EPS = 1e-5
A_HEADS = 4
A_KV_RANK = 128
A_IDX_HEADS = 4
A_IDX_DIM = 64
A_TOPK_MAX = 256
B_HEADS = 4
B_BLOCK = 256
B_TOPK_BLOCKS = 3
C_HEADS = 8
C_KV_HEADS = 2
C_WINDOW = 128
N_BRANCH = 3
A_W = A_HEADS * HEAD_DIM
B_W = B_HEADS * HEAD_DIM
C_W = C_HEADS * HEAD_DIM
IN_SIZES = (A_W, A_KV_RANK, A_IDX_HEADS * A_IDX_DIM, A_IDX_DIM, A_IDX_HEADS,
            B_W, B_W, B_W, C_W, C_KV_HEADS * HEAD_DIM, C_KV_HEADS * HEAD_DIM,
            N_BRANCH * D_MODEL)
N_EXPERTS = 16
N_GROUPS = 4
EXPERTS_PER_GROUP = 4
D_EXPERT = 512
DN_ALPHA = (2 * DEPTH) ** 0.25
ATTN_SCALE = HEAD_DIM ** -0.5

LANES = 128
SUBLANES = 8
VMEM_LIMIT_BYTES = 56 * 1024 * 1024

NEG_BIG = -1e30
BF16 = jnp.bfloat16
F32 = jnp.float32

PROJ_TN = 256
COL_AQ = 0
COL_AIQ = 256
COL_BQ = 512
COL_BK = 768
COL_CQ = 1024
COL_CK = 1536
COL_AIK = 1664
N_ROPE_TILES = 7
COL_BV = 1792
COL_GATES = 2048
COL_AC = 5120
COL_AIW = 5248
COL_CV = 5376
N_PROJ = 5632

PAIRS = ((0, 1), (0, 2), (0, 3), (1, 2), (1, 3), (2, 3))
N_CLASSES = N_GROUPS * len(PAIRS)
MOE_TILE = 256


def _cparams(n_axes):
    return pltpu.CompilerParams(dimension_semantics=("arbitrary",) * n_axes,
                                vmem_limit_bytes=VMEM_LIMIT_BYTES)


def _proj_body(x_ref, w_ref, cos_ref, sin_ref, o_ref, *, n_rope, tn):
    j = pl.program_id(1)
    acc = jnp.dot(x_ref[...], w_ref[...], preferred_element_type=F32)

    @pl.when(j < n_rope)
    def _():
        lane = lax.broadcasted_iota(jnp.int32, acc.shape, 1)
        first = (lane & (HEAD_DIM - 1)) < (HEAD_DIM // 2)
        rot = jnp.where(first, pltpu.roll(acc, tn - HEAD_DIM // 2, 1), pltpu.roll(acc, HEAD_DIM // 2, 1))
        o_ref[...] = (acc * cos_ref[...] + rot * sin_ref[...]).astype(o_ref.dtype)

    @pl.when(j >= n_rope)
    def _():
        o_ref[...] = acc.astype(o_ref.dtype)


def _proj(x, x_col_block, k_dim, w, cos_t, sin_t, *, n_rope, tn, tm, seq):
    t_tokens = x.shape[0]
    n = w.shape[1]
    pos_blocks = seq // tm
    return pl.pallas_call(
        functools.partial(_proj_body, n_rope=n_rope, tn=tn),
        grid=(t_tokens // tm, n // tn),
        in_specs=[
            pl.BlockSpec((tm, k_dim), lambda i, j: (i, x_col_block)),
            pl.BlockSpec((k_dim, tn), lambda i, j: (0, j)),
            pl.BlockSpec((tm, tn), lambda i, j: (i % pos_blocks, 0)),
            pl.BlockSpec((tm, tn), lambda i, j: (i % pos_blocks, 0)),
        ],
        out_specs=pl.BlockSpec((tm, tn), lambda i, j: (i, j)),
        out_shape=jax.ShapeDtypeStruct((t_tokens, n), BF16),
        compiler_params=_cparams(2),
        name="proj_rope",
    )(x, w, cos_t, sin_t)


def _f32_to_key(x):
    b = lax.bitcast_convert_type(x, jnp.int32)
    return jnp.where(b < 0, b ^ jnp.int32(0x7FFFFFFF), b)


def _key_to_f32(k):
    b = jnp.where(k < 0, k ^ jnp.int32(0x7FFFFFFF), k)
    return lax.bitcast_convert_type(b, F32)


def _fold8(x, op):
    return op(x.reshape(x.shape[0] // SUBLANES, SUBLANES, x.shape[1]), axis=0)


def _dsa_body(qaT_ref, iqT_ref, iwT_ref, ik_ref, ka_ref, vaT_ref, o_ref, sc_ref, acc_ref, *, tq, kc, topk):
    i = pl.program_id(1)
    q0 = i * tq
    nch = (q0 + tq + kc - 1) // kc
    qpos = q0 + lax.broadcasted_iota(jnp.int32, (1, tq), 1)
    iqT = iqT_ref[0]
    w_idx = iwT_ref[0].astype(F32)

    def score_chunk(c, carry):
        mn, mx = carry
        k0 = pl.multiple_of(c * kc, kc)
        ikc = ik_ref[0, pl.ds(k0, kc), :]
        acc = jnp.zeros((kc, tq), F32)
        for h in range(A_IDX_HEADS):
            d = jnp.dot(ikc, iqT[h * A_IDX_DIM:(h + 1) * A_IDX_DIM, :], preferred_element_type=F32)
            acc = acc + w_idx[h:h + 1, :] * jnp.maximum(d, 0.0)
        kpos = k0 + lax.broadcasted_iota(jnp.int32, (kc, 1), 0)
        causal = kpos <= qpos
        sc_ref[pl.ds(k0, kc), :] = jnp.where(causal, acc, -jnp.inf)
        mn = jnp.minimum(mn, _fold8(jnp.where(causal, acc, jnp.inf), jnp.min))
        mx = jnp.maximum(mx, _fold8(jnp.where(causal, acc, -jnp.inf), jnp.max))
        return mn, mx

    mn8, mx8 = lax.fori_loop(0, nch, score_chunk,
                             (jnp.full((SUBLANES, tq), jnp.inf, F32), jnp.full((SUBLANES, tq), -jnp.inf, F32)))
    row_min = jnp.min(mn8, axis=0, keepdims=True)
    row_max = jnp.max(mx8, axis=0, keepdims=True)

    def count_ge(thr):
        def body(c, cnt):
            k0 = pl.multiple_of(c * kc, kc)
            s = sc_ref[pl.ds(k0, kc), :]
            return cnt + _fold8(jnp.where(s >= thr, 1.0, 0.0), jnp.sum)
        cnt8 = lax.fori_loop(0, nch, body, jnp.zeros((SUBLANES, tq), F32))
        return jnp.sum(cnt8, axis=0, keepdims=True)

    n_valid = (qpos + 1).astype(F32)
    few = n_valid <= float(topk)
    lo0 = _f32_to_key(row_min)
    hi0 = jnp.where(few, lo0 + 1, _f32_to_key(row_max) + 1)

    def bis_cond(st):
        lo, hi, _ = st
        return jnp.max(jnp.where(lo + 1 != hi, 1.0, 0.0)) > 0.0

    def bis_body(st):
        lo, hi, cnt_lo = st
        active = (lo + 1) != hi
        mid = lo + lax.shift_right_logical(hi - lo, 1)
        cnt = count_ge(_key_to_f32(mid))
        ge = cnt >= float(topk)
        exact = cnt == float(topk)
        new_lo = jnp.where(active & ge, mid, lo)
        new_hi = jnp.where(active & jnp.logical_not(ge), mid, hi)
        new_hi = jnp.where(active & exact, mid + 1, new_hi)
        new_cnt = jnp.where(active & ge, cnt, cnt_lo)
        return new_lo, new_hi, new_cnt

    lo, _, cnt_lo = lax.while_loop(bis_cond, bis_body, (lo0, hi0, n_valid))
    thr = _key_to_f32(lo)

    tie = jnp.logical_and(cnt_lo > float(topk), jnp.logical_not(few))

    @pl.when(jnp.max(jnp.where(tie, 1.0, 0.0)) > 0.0)
    def _():
        def gt_body(c, cnt):
            k0 = pl.multiple_of(c * kc, kc)
            s = sc_ref[pl.ds(k0, kc), :]
            return cnt + _fold8(jnp.where(s > thr, 1.0, 0.0), jnp.sum)
        gt8 = lax.fori_loop(0, nch, gt_body, jnp.zeros((SUBLANES, tq), F32))
        need = float(topk) - jnp.sum(gt8, axis=0, keepdims=True)
        r = lax.broadcasted_iota(jnp.int32, (kc, kc), 0)
        cidx = lax.broadcasted_iota(jnp.int32, (kc, kc), 1)
        tri = jnp.where(cidx <= r, 1.0, 0.0).astype(BF16)

        def tie_body(c, seen):
            k0 = pl.multiple_of(c * kc, kc)
            s = sc_ref[pl.ds(k0, kc), :]
            eqf = jnp.where(s == thr, jnp.where(tie, 1.0, 0.0), 0.0)
            pre = jnp.dot(tri, eqf.astype(BF16), preferred_element_type=F32) + seen
            drop = eqf * jnp.where(pre > need, 1.0, 0.0)
            sc_ref[pl.ds(k0, kc), :] = jnp.where(drop > 0.0, -jnp.inf, s)
            return seen + jnp.sum(eqf, axis=0, keepdims=True)

        lax.fori_loop(0, nch, tie_body, jnp.zeros((1, tq), F32))

    qT = [(qaT_ref[0, h * HEAD_DIM:(h + 1) * HEAD_DIM, :] * ATTN_SCALE).astype(BF16) for h in range(A_HEADS)]
    acc_ref[...] = jnp.zeros_like(acc_ref)
    ones_rows = jnp.ones((SUBLANES, kc), BF16)

    def attn_chunk(c, ms):
        k0 = pl.multiple_of(c * kc, kc)
        kac = ka_ref[0, pl.ds(k0, kc), :]
        v1 = jnp.concatenate([vaT_ref[0, c], ones_rows], axis=0)
        sel = sc_ref[pl.ds(k0, kc), :] >= thr
        new_ms = []
        for h in range(A_HEADS):
            s = jnp.dot(kac, qT[h], preferred_element_type=F32)
            s = jnp.where(sel, s, NEG_BIG)
            m_new = jnp.maximum(ms[h], jnp.max(s, axis=0, keepdims=True))
            alpha = jnp.exp(ms[h] - m_new)
            p = jnp.exp(s - m_new).astype(BF16)
            acc_ref[h] = acc_ref[h] * alpha + jnp.dot(v1, p, preferred_element_type=F32)
            new_ms.append(m_new)
        return tuple(new_ms)

    lax.fori_loop(0, nch, attn_chunk, tuple(jnp.full((1, tq), NEG_BIG, F32) for _ in range(A_HEADS)))
    for h in range(A_HEADS):
        a = acc_ref[h]
        o_ref[0, h * HEAD_DIM:(h + 1) * HEAD_DIM, :] = (a[:HEAD_DIM] / a[HEAD_DIM:HEAD_DIM + 1]).astype(o_ref.dtype)


def _dsa(qaT, iqT, iwT, ik, ka, vaT4, *, tq, kc, topk):
    bsz, _, seq = qaT.shape
    return pl.pallas_call(
        functools.partial(_dsa_body, tq=tq, kc=kc, topk=topk),
        grid=(bsz, seq // tq),
        in_specs=[
            pl.BlockSpec((1, A_W, tq), lambda b, i: (b, 0, i)),
            pl.BlockSpec((1, A_IDX_HEADS * A_IDX_DIM, tq), lambda b, i: (b, 0, i)),
            pl.BlockSpec((1, A_IDX_HEADS, tq), lambda b, i: (b, 0, i)),
            pl.BlockSpec((1, seq, A_IDX_DIM), lambda b, i: (b, 0, 0)),
            pl.BlockSpec((1, seq, HEAD_DIM), lambda b, i: (b, 0, 0)),
            pl.BlockSpec((1, seq // kc, HEAD_DIM, kc), lambda b, i: (b, 0, 0, 0)),
        ],
        out_specs=pl.BlockSpec((1, A_W, tq), lambda b, i: (b, 0, i)),
        out_shape=jax.ShapeDtypeStruct((bsz, A_W, seq), BF16),
        scratch_shapes=[pltpu.VMEM((seq, tq), F32),
                        pltpu.VMEM((A_HEADS, HEAD_DIM + SUBLANES, tq), F32)],
        compiler_params=_cparams(2),
        name="dsa_attention",
    )(qaT, iqT, iwT, ik, ka, vaT4)


def _kmean_body(k_ref, o_ref):
    o_ref[0] = jnp.mean(k_ref[...].astype(F32), axis=0, keepdims=True)


def _moba_kmean(h_all, *, n_tokens):
    nblk = n_tokens // B_BLOCK
    return pl.pallas_call(
        _kmean_body,
        grid=(nblk,),
        in_specs=[pl.BlockSpec((B_BLOCK, B_W), lambda i: (i, COL_BK // B_W))],
        out_specs=pl.BlockSpec((1, 1, B_W), lambda i: (i, 0, 0)),
        out_shape=jax.ShapeDtypeStruct((nblk, 1, B_W), F32),
        compiler_params=_cparams(1),
        name="moba_kmean",
    )(h_all)


def _moba_gate_body(qT_ref, km_ref, sel_ref, *, tq, nkb, nsel):
    i = pl.program_id(1)
    qpos = i * tq + lax.broadcasted_iota(jnp.int32, (1, tq), 1)
    cur = qpos // B_BLOCK
    row = lax.broadcasted_iota(jnp.int32, (nkb, tq), 0)
    past = row < cur
    for h in range(B_HEADS):
        g = jnp.dot(km_ref[0, h], qT_ref[0, h * HEAD_DIM:(h + 1) * HEAD_DIM, :], preferred_element_type=F32)
        g = jnp.where(past, g, -jnp.inf)
        sel = jnp.zeros((nkb, tq), F32)
        for _ in range(nsel):
            mx = jnp.max(g, axis=0, keepdims=True)
            idx = jnp.min(jnp.where(g == mx, row, nkb), axis=0, keepdims=True)
            pick = row == idx
            sel = jnp.where(pick, 1.0, sel)
            g = jnp.where(pick, -jnp.inf, g)
        sel_ref[0, h] = jnp.where(past, sel, 0.0)


def _moba_gate(bqT, kmean, *, tq):
    bsz, _, seq = bqT.shape
    nkb = seq // B_BLOCK
    nsel = min(B_TOPK_BLOCKS, nkb)
    return pl.pallas_call(
        functools.partial(_moba_gate_body, tq=tq, nkb=nkb, nsel=nsel),
        grid=(bsz, seq // tq),
        in_specs=[
            pl.BlockSpec((1, B_W, tq), lambda b, i: (b, 0, i)),
            pl.BlockSpec((1, B_HEADS, nkb, HEAD_DIM), lambda b, i: (b, 0, 0, 0)),
        ],
        out_specs=pl.BlockSpec((1, B_HEADS, nkb, tq), lambda b, i: (b, 0, 0, i)),
        out_shape=jax.ShapeDtypeStruct((bsz, B_HEADS, nkb, seq), F32),
        compiler_params=_cparams(2),
        name="moba_gate",
    )(bqT, kmean)


def _moba_body(qT_ref, k_ref, vT_ref, sel_ref, o_ref, acc_ref, *, tq):
    i = pl.program_id(1)
    qpos = i * tq + lax.broadcasted_iota(jnp.int32, (1, tq), 1)
    qT = [(qT_ref[0, h * HEAD_DIM:(h + 1) * HEAD_DIM, :] * ATTN_SCALE).astype(BF16) for h in range(B_HEADS)]
    acc_ref[...] = jnp.zeros_like(acc_ref)
    ones_rows = jnp.ones((SUBLANES, B_BLOCK), BF16)

    def step(j, ms, own):
        k0 = pl.multiple_of(j * B_BLOCK, B_BLOCK)
        new_ms = []
        for h in range(B_HEADS):
            kb = k_ref[0, h, pl.ds(k0, B_BLOCK), :]
            v1 = jnp.concatenate([vT_ref[0, h, j], ones_rows], axis=0)
            s = jnp.dot(kb, qT[h], preferred_element_type=F32)
            if own:
                kpos = k0 + lax.broadcasted_iota(jnp.int32, (B_BLOCK, 1), 0)
                s = jnp.where(kpos <= qpos, s, NEG_BIG)
            else:
                s = jnp.where(sel_ref[0, h, pl.ds(j, 1), :] > 0.0, s, NEG_BIG)
            m_new = jnp.maximum(ms[h], jnp.max(s, axis=0, keepdims=True))
            alpha = jnp.exp(ms[h] - m_new)
            p = jnp.exp(s - m_new).astype(BF16)
            acc_ref[h] = acc_ref[h] * alpha + jnp.dot(v1, p, preferred_element_type=F32)
            new_ms.append(m_new)
        return tuple(new_ms)

    ms0 = tuple(jnp.full((1, tq), NEG_BIG, F32) for _ in range(B_HEADS))
    ms = lax.fori_loop(0, i, lambda j, ms: step(j, ms, False), ms0)
    step(i, ms, True)
    for h in range(B_HEADS):
        a = acc_ref[h]
        o_ref[0, h * HEAD_DIM:(h + 1) * HEAD_DIM, :] = (a[:HEAD_DIM] / a[HEAD_DIM:HEAD_DIM + 1]).astype(o_ref.dtype)


def _moba(bqT, bk, bvT5, sel):
    bsz, _, seq = bqT.shape
    tq = B_BLOCK
    nkb = seq // B_BLOCK
    return pl.pallas_call(
        functools.partial(_moba_body, tq=tq),
        grid=(bsz, seq // tq),
        in_specs=[
            pl.BlockSpec((1, B_W, tq), lambda b, i: (b, 0, i)),
            pl.BlockSpec((1, B_HEADS, seq, HEAD_DIM), lambda b, i: (b, 0, 0, 0)),
            pl.BlockSpec((1, B_HEADS, nkb, HEAD_DIM, B_BLOCK), lambda b, i: (b, 0, 0, 0, 0)),
            pl.BlockSpec((1, B_HEADS, nkb, tq), lambda b, i: (b, 0, 0, i)),
        ],
        out_specs=pl.BlockSpec((1, B_W, tq), lambda b, i: (b, 0, i)),
        out_shape=jax.ShapeDtypeStruct((bsz, B_W, seq), BF16),
        scratch_shapes=[pltpu.VMEM((B_HEADS, HEAD_DIM + SUBLANES, tq), F32)],
        compiler_params=_cparams(2),
        name="moba_attention",
    )(bqT, bk, bvT5, sel)


def _swa_body(sink_ref, qT_ref, k0_ref, k1_ref, k2_ref, v0_ref, v1_ref, v2_ref, o_ref, *, tq):
    i = pl.program_id(1)
    q0 = i * tq
    qpos = q0 + lax.broadcasted_iota(jnp.int32, (1, tq), 1)
    nk = tq + C_WINDOW
    kpos = q0 - C_WINDOW + lax.broadcasted_iota(jnp.int32, (nk, 1), 0)
    diff = qpos - kpos
    ok = jnp.where(diff >= 0, jnp.where(diff < C_WINDOW, jnp.where(kpos >= 0, 1.0, 0.0), 0.0), 0.0) > 0.0
    ones_rows = jnp.ones((SUBLANES, nk), BF16)
    group = C_HEADS // C_KV_HEADS
    for kv in range(C_KV_HEADS):
        kk = jnp.concatenate([k0_ref[0, kv], k1_ref[0, kv], k2_ref[0, kv]], axis=0)
        vv = jnp.concatenate([v0_ref[0, kv], v1_ref[0, kv], v2_ref[0, kv]], axis=1)
        v1 = jnp.concatenate([vv, ones_rows], axis=0)
        for g in range(group):
            hd = kv * group + g
            sink = sink_ref[hd]
            qT = (qT_ref[0, hd * HEAD_DIM:(hd + 1) * HEAD_DIM, :] * ATTN_SCALE).astype(BF16)
            s = jnp.dot(kk, qT, preferred_element_type=F32)
            s = jnp.where(ok, s, NEG_BIG)
            m = jnp.maximum(jnp.max(s, axis=0, keepdims=True), sink)
            p = jnp.exp(s - m).astype(BF16)
            a = jnp.dot(v1, p, preferred_element_type=F32)
            denom = a[HEAD_DIM:HEAD_DIM + 1] + jnp.exp(sink - m)
            o_ref[0, hd * HEAD_DIM:(hd + 1) * HEAD_DIM, :] = (a[:HEAD_DIM] / denom).astype(o_ref.dtype)


def _swa(sinks, cqT, ck, cvT, *, tq):
    bsz, _, seq = cqT.shape
    r = tq // C_WINDOW
    assert r == 2

    def kspec(off):
        return pl.BlockSpec((1, C_KV_HEADS, C_WINDOW, HEAD_DIM),
                            lambda b, i, s: (b, 0, jnp.maximum(i * r + off, 0), 0))

    def vspec(off):
        return pl.BlockSpec((1, C_KV_HEADS, HEAD_DIM, C_WINDOW),
                            lambda b, i, s: (b, 0, 0, jnp.maximum(i * r + off, 0)))

    grid_spec = pltpu.PrefetchScalarGridSpec(
        num_scalar_prefetch=1,
        grid=(bsz, seq // tq),
        in_specs=[pl.BlockSpec((1, C_W, tq), lambda b, i, s: (b, 0, i)),
                  kspec(-1), kspec(0), kspec(1), vspec(-1), vspec(0), vspec(1)],
        out_specs=pl.BlockSpec((1, C_W, tq), lambda b, i, s: (b, 0, i)),
    )
    return pl.pallas_call(
        functools.partial(_swa_body, tq=tq),
        grid_spec=grid_spec,
        out_shape=jax.ShapeDtypeStruct((bsz, C_W, seq), BF16),
        compiler_params=_cparams(2),
        name="swa_attention",
    )(sinks, cqT, ck, ck, ck, cvT, cvT, cvT)


def _layer_norm(z, g, b):
    mu = jnp.mean(z, axis=-1, keepdims=True)
    zc = z - mu
    var = jnp.mean(zc * zc, axis=-1, keepdims=True)
    return zc * lax.rsqrt(var + LN_EPS) * g + b


def _split_bf16(a):
    hi = a.astype(BF16)
    lo = (a - hi.astype(F32)).astype(BF16)
    return hi, lo


def _router(x1, rw_ref, rb_ref, cls_ref, wt_ref):
    nt = (((1,), (1,)), ((), ()))
    xh, xl = _split_bf16(x1)
    wh, wl = _split_bf16(rw_ref[...])
    logits = (lax.dot_general(wh, xh, nt, preferred_element_type=F32)
              + lax.dot_general(wh, xl, nt, preferred_element_type=F32)
              + lax.dot_general(wl, xh, nt, preferred_element_type=F32))
    aff = jax.nn.sigmoid(logits)
    score = aff + rb_ref[...]
    tm = x1.shape[0]
    sc = [score[e:e + 1, :] for e in range(N_EXPERTS)]
    af = [aff[e:e + 1, :] for e in range(N_EXPERTS)]
    in_top = []
    grp_score = []
    for gq in range(N_GROUPS):
        gs = jnp.zeros((1, tm), F32)
        for a in range(EXPERTS_PER_GROUP):
            ea = gq * EXPERTS_PER_GROUP + a
            rank = jnp.zeros((1, tm), F32)
            for b in range(EXPERTS_PER_GROUP):
                if b == a:
                    continue
                eb = gq * EXPERTS_PER_GROUP + b
                beats = (sc[eb] >= sc[ea]) if b < a else (sc[eb] > sc[ea])
                rank = rank + jnp.where(beats, 1.0, 0.0)
            top = rank < 2.0
            in_top.append(top)
            gs = gs + jnp.where(top, sc[ea], 0.0)
        grp_score.append(gs)
    best = grp_score[0]
    gstar = jnp.zeros((1, tm), jnp.int32)
    for gq in range(1, N_GROUPS):
        better = grp_score[gq] > best
        best = jnp.where(better, grp_score[gq], best)
        gstar = jnp.where(better, gq, gstar)
    cls = jnp.zeros((1, tm), jnp.int32)
    w_lo = jnp.zeros((1, tm), F32)
    w_hi = jnp.zeros((1, tm), F32)
    for gq in range(N_GROUPS):
        is_g = gstar == gq
        for pi, (a, b) in enumerate(PAIRS):
            ea, eb = gq * EXPERTS_PER_GROUP + a, gq * EXPERTS_PER_GROUP + b
            hit = jnp.where(is_g, jnp.where(in_top[ea], jnp.where(in_top[eb], 1.0, 0.0), 0.0), 0.0) > 0.0
            cls = jnp.where(hit, gq * len(PAIRS) + pi, cls)
            tot = af[ea] + af[eb]
            w_lo = jnp.where(hit, af[ea] / tot, w_lo)
            w_hi = jnp.where(hit, af[eb] / tot, w_hi)
    cls_ref[...] = cls
    wt_ref[0:1, :] = w_lo
    wt_ref[1:2, :] = w_hi


def _merge_body(oa_ref, ob_ref, oc_ref, ga_ref, gb_ref, gc_ref, x_ref, wa_ref, wb_ref, wc_ref, wo_ref,
                g_ref, b_ref, rw_ref, rb_ref, x1_ref, cls_ref, wt_ref):
    def branch(o_ref, w_ref, gate_ref):
        y = jnp.dot(o_ref[...], w_ref[...], preferred_element_type=F32)
        return jax.nn.sigmoid(gate_ref[...].astype(F32)) * y

    merged = branch(oa_ref, wa_ref, ga_ref) + branch(ob_ref, wb_ref, gb_ref) + branch(oc_ref, wc_ref, gc_ref)
    y = jnp.dot(merged.astype(BF16), wo_ref[...], preferred_element_type=F32)
    x1 = _layer_norm(DN_ALPHA * x_ref[...] + y, g_ref[...], b_ref[...])
    x1_ref[...] = x1
    _router(x1, rw_ref, rb_ref, cls_ref, wt_ref)


def _merge(oa, ob, oc, h_all, x, wa, wb, wc, wo, ln_g, ln_b, rwT, rb, *, tm):
    t_tokens = x.shape[0]
    gate_blk = COL_GATES // D_MODEL
    full = lambda shape: pl.BlockSpec(shape, lambda i: (0,) * len(shape))
    return pl.pallas_call(
        _merge_body,
        grid=(t_tokens // tm,),
        in_specs=[
            pl.BlockSpec((tm, A_W), lambda i: (i, 0)),
            pl.BlockSpec((tm, B_W), lambda i: (i, 0)),
            pl.BlockSpec((tm, C_W), lambda i: (i, 0)),
            pl.BlockSpec((tm, D_MODEL), lambda i: (i, gate_blk)),
            pl.BlockSpec((tm, D_MODEL), lambda i: (i, gate_blk + 1)),
            pl.BlockSpec((tm, D_MODEL), lambda i: (i, gate_blk + 2)),
            pl.BlockSpec((tm, D_MODEL), lambda i: (i, 0)),
            full((A_W, D_MODEL)), full((B_W, D_MODEL)), full((C_W, D_MODEL)), full((D_MODEL, D_MODEL)),
            full((1, D_MODEL)), full((1, D_MODEL)), full((N_EXPERTS, D_MODEL)), full((N_EXPERTS, 1)),
        ],
        out_specs=[
            pl.BlockSpec((tm, D_MODEL), lambda i: (i, 0)),
            pl.BlockSpec((1, tm), lambda i: (0, i)),
            pl.BlockSpec((2, tm), lambda i: (0, i)),
        ],
        out_shape=[
            jax.ShapeDtypeStruct((t_tokens, D_MODEL), F32),
            jax.ShapeDtypeStruct((1, t_tokens), jnp.int32),
            jax.ShapeDtypeStruct((2, t_tokens), F32),
        ],
        compiler_params=_cparams(1),
        name="merge_ln_router",
    )(oa, ob, oc, h_all, h_all, h_all, x, wa, wb, wc, wo, ln_g, ln_b, rwT, rb)


def _gather_rows(idx_ref, src_hbm, dst_ref, sem, n_rows):
    def row_copy(r):
        return pltpu.make_async_copy(src_hbm.at[pl.ds(idx_ref[0, 0, r], 1)], dst_ref.at[pl.ds(r, 1)], sem)

    def start(r, _):
        row_copy(r).start()
        return 0

    def wait(r, _):
        row_copy(r).wait()
        return 0

    lax.fori_loop(0, n_rows, start, 0)
    lax.fori_loop(0, n_rows, wait, 0)


def _experts_body(e1_ref, e2_ref, nused_ref, src_ref, x_hbm, wt_ref,
                  wg1_ref, wu1_ref, wd1_ref, wg2_ref, wu2_ref, wd2_ref, o_ref, xbuf, sem):
    s = pl.program_id(0)

    @pl.when(s < nused_ref[0])
    def _():
        _gather_rows(src_ref, x_hbm, xbuf, sem, MOE_TILE)
        xb = xbuf[...].astype(BF16)
        wt = wt_ref[...]

        def expert(wg_ref, wu_ref, wd_ref):
            g = jnp.dot(xb, wg_ref[0], preferred_element_type=F32)
            u = jnp.dot(xb, wu_ref[0], preferred_element_type=F32)
            he = (g * jax.nn.sigmoid(g) * u).astype(BF16)
            return jnp.dot(he, wd_ref[0], preferred_element_type=F32)

        o_ref[...] = wt[:, 0:1] * expert(wg1_ref, wu1_ref, wd1_ref) + wt[:, 1:2] * expert(wg2_ref, wu2_ref, wd2_ref)

    @pl.when(s >= nused_ref[0])
    def _():
        o_ref[...] = jnp.zeros_like(o_ref)


def _experts(tile_e1, tile_e2, nused, src3, x1, wsort, wg, wu, wd):
    ntiles = src3.shape[0]
    wspec_up = lambda which: pl.BlockSpec((1, D_MODEL, D_EXPERT), lambda s, e1, e2, nu: ((e1, e2)[which][s], 0, 0))
    wspec_dn = lambda which: pl.BlockSpec((1, D_EXPERT, D_MODEL), lambda s, e1, e2, nu: ((e1, e2)[which][s], 0, 0))
    grid_spec = pltpu.PrefetchScalarGridSpec(
        num_scalar_prefetch=3,
        grid=(ntiles,),
        in_specs=[
            pl.BlockSpec((1, 1, MOE_TILE), lambda s, e1, e2, nu: (s, 0, 0), memory_space=pltpu.SMEM),
            pl.BlockSpec(memory_space=pl.ANY),
            pl.BlockSpec((MOE_TILE, 2), lambda s, e1, e2, nu: (s, 0)),
            wspec_up(0), wspec_up(0), wspec_dn(0), wspec_up(1), wspec_up(1), wspec_dn(1),
        ],
        out_specs=pl.BlockSpec((MOE_TILE, D_MODEL), lambda s, e1, e2, nu: (s, 0)),
        scratch_shapes=[pltpu.VMEM((MOE_TILE, D_MODEL), F32), pltpu.SemaphoreType.DMA(())],
    )
    return pl.pallas_call(
        _experts_body,
        grid_spec=grid_spec,
        out_shape=jax.ShapeDtypeStruct((ntiles * MOE_TILE, D_MODEL), F32),
        compiler_params=_cparams(1),
        name="moe_experts",
    )(tile_e1, tile_e2, nused, src3, x1, wsort, wg, wu, wd, wg, wu, wd)


def _final_body(pos_ref, y_hbm, x_ref, g_ref, b_ref, o_ref, ob_ref, ybuf, sem, *, tm):
    _gather_rows(pos_ref, y_hbm, ybuf, sem, tm)
    x2 = _layer_norm(DN_ALPHA * x_ref[...] + ybuf[...], g_ref[...], b_ref[...])
    o_ref[...] = x2
    ob_ref[...] = x2.astype(BF16)


def _final(pos3, y_sorted, x1, ln_g, ln_b, *, tm):
    t_tokens = x1.shape[0]
    return pl.pallas_call(
        functools.partial(_final_body, tm=tm),
        grid=(t_tokens // tm,),
        in_specs=[
            pl.BlockSpec((1, 1, tm), lambda i: (i, 0, 0), memory_space=pltpu.SMEM),
            pl.BlockSpec(memory_space=pl.ANY),
            pl.BlockSpec((tm, D_MODEL), lambda i: (i, 0)),
            pl.BlockSpec((1, D_MODEL), lambda i: (0, 0)),
            pl.BlockSpec((1, D_MODEL), lambda i: (0, 0)),
        ],
        out_specs=[pl.BlockSpec((tm, D_MODEL), lambda i: (i, 0)), pl.BlockSpec((tm, D_MODEL), lambda i: (i, 0))],
        out_shape=[jax.ShapeDtypeStruct((t_tokens, D_MODEL), F32), jax.ShapeDtypeStruct((t_tokens, D_MODEL), BF16)],
        scratch_shapes=[pltpu.VMEM((tm, D_MODEL), F32), pltpu.SemaphoreType.DMA(())],
        compiler_params=_cparams(1),
        name="moe_combine_ln",
    )(pos3, y_sorted, x1, ln_g, ln_b)


def _rope_tables(seq):
    inv = 1.0 / (ROPE_THETA ** (jnp.arange(0, HEAD_DIM, 2, dtype=F32) / HEAD_DIM))
    ang = jnp.arange(seq, dtype=F32)[:, None] * inv[None, :]
    cos, sin = jnp.cos(ang), jnp.sin(ang)
    cos_h = jnp.concatenate([cos, cos], axis=1)
    sin_h = jnp.concatenate([-sin, sin], axis=1)
    return cos_h, sin_h


def _reorder_w_in(w):
    pts = np.cumsum((0,) + IN_SIZES)
    sec = [w[:, pts[k]:pts[k + 1]] for k in range(len(IN_SIZES))]
    a_q, a_c, a_iq, a_ik, a_iw, b_q, b_k, b_v, c_q, c_k, c_v, gates = sec
    zeros = lambda n: jnp.zeros((w.shape[0], n), w.dtype)
    cols = [a_q, a_iq, b_q, b_k, c_q, c_k, a_ik, zeros(64),
            b_v, gates, a_c, a_iw, zeros(LANES - A_IDX_HEADS), c_v, zeros(LANES)]
    out = jnp.concatenate(cols, axis=1)
    assert out.shape[1] == N_PROJ
    return out.astype(BF16)


def _moe_plan(cls, wts, n_tokens):
    ntiles = n_tokens // MOE_TILE + N_CLASSES
    onehot = (cls[:, None] == jnp.arange(N_CLASSES, dtype=jnp.int32)[None, :]).astype(jnp.int32)
    csum = jnp.cumsum(onehot, axis=0)
    rank = jnp.sum(onehot * csum, axis=1) - 1
    counts = csum[-1]
    ptiles = (counts + MOE_TILE - 1) // MOE_TILE
    tile_end = jnp.cumsum(ptiles)
    tile_start = tile_end - ptiles
    pos = (tile_start[cls] * MOE_TILE + rank).astype(jnp.int32)
    nused = tile_end[-1:].astype(jnp.int32)
    tile_cls = jnp.minimum(jnp.searchsorted(tile_end, jnp.arange(ntiles, dtype=jnp.int32), side="right"),
                           N_CLASSES - 1).astype(jnp.int32)
    pair = np.array(PAIRS, dtype=np.int32)
    e_lo = jnp.asarray(np.repeat(np.arange(N_GROUPS), len(PAIRS)) * EXPERTS_PER_GROUP + np.tile(pair[:, 0], N_GROUPS), jnp.int32)
    e_hi = jnp.asarray(np.repeat(np.arange(N_GROUPS), len(PAIRS)) * EXPERTS_PER_GROUP + np.tile(pair[:, 1], N_GROUPS), jnp.int32)
    src = jnp.zeros((ntiles * MOE_TILE,), jnp.int32).at[pos].set(jnp.arange(n_tokens, dtype=jnp.int32))
    wsort = jnp.zeros((ntiles * MOE_TILE, 2), F32).at[pos].set(wts.T)
    return e_lo[tile_cls], e_hi[tile_cls], nused, src.reshape(ntiles, 1, MOE_TILE), pos, wsort


def kernel(x, w_in, a_w_uk, a_w_uv, c_sinks, w_branch, w_o, ln1_g, ln1_b, router_w, router_b,
           moe_w_gate, moe_w_up, moe_w_down, ln2_g, ln2_b):
    bsz, seq, _ = x.shape
    n_tokens = bsz * seq
    topk = min(A_TOPK_MAX, seq // 4)
    nkb = seq // B_BLOCK
    tq = 256
    kc = 256
    tm_proj = min(1024, seq)
    tm_row = 512

    cos_h, sin_h = _rope_tables(seq)
    cos_in = jnp.tile(cos_h, (1, PROJ_TN // HEAD_DIM))
    sin_in = jnp.tile(sin_h, (1, PROJ_TN // HEAD_DIM))
    cos_kv = jnp.concatenate([cos_h, jnp.ones((seq, HEAD_DIM), F32)], axis=1)
    sin_kv = jnp.concatenate([sin_h, jnp.zeros((seq, HEAD_DIM), F32)], axis=1)
    rwT = router_w.T
    rb = router_b.reshape(N_EXPERTS, 1)

    xf = x.reshape(n_tokens, D_MODEL)
    xb = xf.astype(BF16)
    for l in range(DEPTH):
        w_proj = _reorder_w_in(w_in[l])
        w_kv = jnp.concatenate([a_w_uk[l], a_w_uv[l]], axis=1).astype(BF16)
        h_all = _proj(xb, 0, D_MODEL, w_proj, cos_in, sin_in, n_rope=N_ROPE_TILES, tn=PROJ_TN, tm=tm_proj, seq=seq)
        kv = _proj(h_all, COL_AC // A_KV_RANK, A_KV_RANK, w_kv, cos_kv, sin_kv, n_rope=1, tn=LANES, tm=tm_proj, seq=seq)

        def colsT(c0, width):
            return h_all[:, c0:c0 + width].reshape(bsz, seq, width).transpose(0, 2, 1)

        qaT = colsT(COL_AQ, A_W)
        iqT = colsT(COL_AIQ, A_IDX_HEADS * A_IDX_DIM)
        iwT = colsT(COL_AIW, A_IDX_HEADS)
        ik = h_all[:, COL_AIK:COL_AIK + A_IDX_DIM].reshape(bsz, seq, A_IDX_DIM)
        ka = kv[:, :HEAD_DIM].reshape(bsz, seq, HEAD_DIM)
        vaT4 = kv[:, HEAD_DIM:].reshape(bsz, seq // kc, kc, HEAD_DIM).transpose(0, 1, 3, 2)
        oaT = _dsa(qaT, iqT, iwT, ik, ka, vaT4, tq=tq, kc=kc, topk=topk)

        bqT = colsT(COL_BQ, B_W)
        kmean = _moba_kmean(h_all, n_tokens=n_tokens)
        kmean = kmean.reshape(bsz, nkb, B_HEADS, HEAD_DIM).transpose(0, 2, 1, 3).astype(BF16)
        sel = _moba_gate(bqT, kmean, tq=tq)
        bk = h_all[:, COL_BK:COL_BK + B_W].reshape(bsz, seq, B_HEADS, HEAD_DIM).transpose(0, 2, 1, 3)
        bvT5 = (h_all[:, COL_BV:COL_BV + B_W].reshape(bsz, nkb, B_BLOCK, B_HEADS, HEAD_DIM)
                .transpose(0, 3, 1, 4, 2))
        obT = _moba(bqT, bk, bvT5, sel)

        cqT = colsT(COL_CQ, C_W)
        ck = h_all[:, COL_CK:COL_CK + C_KV_HEADS * HEAD_DIM].reshape(bsz, seq, C_KV_HEADS, HEAD_DIM).transpose(0, 2, 1, 3)
        cvT = h_all[:, COL_CV:COL_CV + C_KV_HEADS * HEAD_DIM].reshape(bsz, seq, C_KV_HEADS, HEAD_DIM).transpose(0, 2, 3, 1)
        ocT = _swa(c_sinks[l], cqT, ck, cvT, tq=tq)

        rows = lambda t: t.transpose(0, 2, 1).reshape(n_tokens, t.shape[1])
        wb_all = w_branch[l].astype(BF16)
        x1, cls, wts = _merge(rows(oaT), rows(obT), rows(ocT), h_all, xf,
                              wb_all[:A_W], wb_all[A_W:A_W + B_W], wb_all[A_W + B_W:], w_o[l].astype(BF16),
                              ln1_g[l].reshape(1, D_MODEL), ln1_b[l].reshape(1, D_MODEL), rwT, rb, tm=tm_row)

        e1, e2, nused, src3, pos, wsort = _moe_plan(cls[0], wts, n_tokens)
        y_sorted = _experts(e1, e2, nused, src3, x1, wsort,
                            moe_w_gate[l].astype(BF16), moe_w_up[l].astype(BF16), moe_w_down[l].astype(BF16))
        xf, xb = _final(pos.reshape(n_tokens // tm_row, 1, tm_row), y_sorted, x1,
                        ln2_g[l].reshape(1, D_MODEL), ln2_b[l].reshape(1, D_MODEL), tm=tm_row)
    return xf.reshape(bsz, seq, D_MODEL)
```

```python
import functools
import math

import jax
import jax.numpy as jnp
import numpy as np
from jax import lax
from jax.experimental import pallas as pl
from jax.experimental.pallas import tpu as pltpu

D_MODEL = 1024
DEPTH = 2
HEAD_DIM = 64
ROPE_THETA = 10000.0
LN_EPS = 1e-5
A_HEADS = 4
A_KV_RANK = 128
A_IDX_HEADS = 4
A_IDX_DIM = 64
A_TOPK_MAX = 256
B_HEADS = 4
B_BLOCK = 256
B_TOPK_BLOCKS = 3
C_HEADS = 8
C_KV_HEADS = 2
C_WINDOW = 128
N_BRANCH = 3
A_W = A_HEADS * HEAD_DIM
B_W = B_HEADS * HEAD_DIM
C_W = C_HEADS * HEAD_DIM
IN_SIZES = (A_W, A_KV_RANK, A_IDX_HEADS * A_IDX_DIM, A_IDX_DIM, A_IDX_HEADS,
            B_W, B_W, B_W, C_W, C_KV_HEADS * HEAD_DIM, C_KV_HEADS * HEAD_DIM,
            N_BRANCH * D_MODEL)
N_EXPERTS = 16
N_GROUPS = 4
EXPERTS_PER_GROUP = 4
D_EXPERT = 512
DN_ALPHA = (2 * DEPTH) ** 0.25
ATTN_SCALE = HEAD_DIM ** -0.5

LANES = 128
SUBLANES = 8
VMEM_LIMIT_BYTES = 56 * 1024 * 1024

NEG_BIG = -1e30
BF16 = jnp.bfloat16
F32 = jnp.float32

PROJ_TN = 256
HALF_TN = PROJ_TN // 2
HEADS_PER_TILE = PROJ_TN // HEAD_DIM
TILE_AQ = 0
TILE_AIQ = 1
TILE_BQ = 2
TILE_BK = 3
TILE_CQ = 4
TILE_CK_AIK = 6
N_ROPE = 7 * PROJ_TN
COL_GATES = 0
COL_BV = 3072
COL_AC = 3328
COL_AIW = 3456
COL_CV = 3584
N_PLAIN = 3840
KV_VA0 = HEAD_DIM // 2

PAIRS = ((0, 1), (0, 2), (0, 3), (1, 2), (1, 3), (2, 3))
N_CLASSES = N_GROUPS * len(PAIRS)
MOE_TILE = 256


def _cparams(n_axes):
    return pltpu.CompilerParams(dimension_semantics=("arbitrary",) * n_axes,
                                vmem_limit_bytes=VMEM_LIMIT_BYTES)


def _proj_rope_body(x_ref, w_ref, cos_ref, sin_ref, o_ref, *, n_sub):
    sub = x_ref.shape[0] // n_sub
    for mi in range(n_sub):
        rows = slice(mi * sub, (mi + 1) * sub)
        acc = jnp.dot(x_ref[rows, :], w_ref[...], preferred_element_type=F32)
        a1, a2 = acc[:, :HALF_TN], acc[:, HALF_TN:]
        c, sn = cos_ref[rows, :], sin_ref[rows, :]
        o_ref[rows, :HALF_TN] = (a1 * c - a2 * sn).astype(o_ref.dtype)
        o_ref[rows, HALF_TN:] = (a2 * c + a1 * sn).astype(o_ref.dtype)


def _proj_plain_body(x_ref, w_ref, o_ref, *, n_sub):
    sub = x_ref.shape[0] // n_sub
    for mi in range(n_sub):
        rows = slice(mi * sub, (mi + 1) * sub)
        o_ref[rows, :] = jnp.dot(x_ref[rows, :], w_ref[...], preferred_element_type=F32).astype(o_ref.dtype)


def _proj_rope(x, x_col_block, k_dim, w, cos_t, sin_t, *, tm, seq):
    t_tokens = x.shape[0]
    n = w.shape[1]
    pos_blocks = seq // tm
    return pl.pallas_call(
        functools.partial(_proj_rope_body, n_sub=max(tm // 256, 1)),
        grid=(t_tokens // tm, n // PROJ_TN),
        in_specs=[
            pl.BlockSpec((tm, k_dim), lambda i, j: (i, x_col_block)),
            pl.BlockSpec((k_dim, PROJ_TN), lambda i, j: (0, j)),
            pl.BlockSpec((tm, HALF_TN), lambda i, j: (i % pos_blocks, 0)),
            pl.BlockSpec((tm, HALF_TN), lambda i, j: (i % pos_blocks, 0)),
        ],
        out_specs=pl.BlockSpec((tm, PROJ_TN), lambda i, j: (i, j)),
        out_shape=jax.ShapeDtypeStruct((t_tokens, n), BF16),
        compiler_params=_cparams(2),
        name="proj_rope",
    )(x, w, cos_t, sin_t)


def _proj_plain(x, w, *, tm):
    t_tokens, k_dim = x.shape
    n = w.shape[1]
    return pl.pallas_call(
        functools.partial(_proj_plain_body, n_sub=max(tm // 256, 1)),
        grid=(t_tokens // tm, n // PROJ_TN),
        in_specs=[
            pl.BlockSpec((tm, k_dim), lambda i, j: (i, 0)),
            pl.BlockSpec((k_dim, PROJ_TN), lambda i, j: (0, j)),
        ],
        out_specs=pl.BlockSpec((tm, PROJ_TN), lambda i, j: (i, j)),
        out_shape=jax.ShapeDtypeStruct((t_tokens, n), BF16),
        compiler_params=_cparams(2),
        name="proj_plain",
    )(x, w)


def _f32_to_key(x):
    b = lax.bitcast_convert_type(x, jnp.int32)
    return jnp.where(b < 0, b ^ jnp.int32(0x7FFFFFFF), b)


def _key_to_f32(k):
    b = jnp.where(k < 0, k ^ jnp.int32(0x7FFFFFFF), k)
    return lax.bitcast_convert_type(b, F32)


def _fold8(x, op):
    return op(x.reshape(x.shape[0] // SUBLANES, SUBLANES, x.shape[1]), axis=0)


def _dsa_body(qaT_ref, iqT_ref, iwT_ref, ik_ref, ka_ref, vaT_ref, o_ref, sc_ref, acc_ref, sbuf_ref, *, tq, kc, topk):
    i = pl.program_id(1)
    q0 = i * tq
    nch = (q0 + tq + kc - 1) // kc
    qpos = q0 + lax.broadcasted_iota(jnp.int32, (1, tq), 1)
    iqT = iqT_ref[0]
    w_idx = iwT_ref[0].astype(F32)

    def score_chunk(c, carry):
        mn, mx = carry
        k0 = pl.multiple_of(c * kc, kc)
        ikc = ik_ref[0, pl.ds(k0, kc), :]
        acc = jnp.zeros((kc, tq), F32)
        for h in range(A_IDX_HEADS):
            d = jnp.dot(ikc, iqT[h * A_IDX_DIM:(h + 1) * A_IDX_DIM, :], preferred_element_type=F32)
            acc = acc + w_idx[h:h + 1, :] * jnp.maximum(d, 0.0)
        kpos = k0 + lax.broadcasted_iota(jnp.int32, (kc, 1), 0)
        causal = kpos <= qpos
        sc_ref[pl.ds(k0, kc), :] = jnp.where(causal, acc, -jnp.inf)
        mn = jnp.minimum(mn, _fold8(jnp.where(causal, acc, jnp.inf), jnp.min))
        mx = jnp.maximum(mx, _fold8(jnp.where(causal, acc, -jnp.inf), jnp.max))
        return mn, mx

    mn8, mx8 = lax.fori_loop(0, nch, score_chunk,
                             (jnp.full((SUBLANES, tq), jnp.inf, F32), jnp.full((SUBLANES, tq), -jnp.inf, F32)))
    row_min = jnp.min(mn8, axis=0, keepdims=True)
    row_max = jnp.max(mx8, axis=0, keepdims=True)

    def count_ge(thr):
        def body(c, cnt):
            k0 = pl.multiple_of(c * kc, kc)
            s = sc_ref[pl.ds(k0, kc), :]
            return cnt + _fold8(jnp.where(s >= thr, 1.0, 0.0), jnp.sum)
        cnt8 = lax.fori_loop(0, nch, body, jnp.zeros((SUBLANES, tq), F32))
        return jnp.sum(cnt8, axis=0, keepdims=True)

    n_valid = (qpos + 1).astype(F32)
    few = n_valid <= float(topk)
    lo0 = _f32_to_key(row_min)
    hi0 = jnp.where(few, lo0 + 1, _f32_to_key(row_max) + 1)

    def bis_cond(st):
        lo, hi, _ = st
        return jnp.max(jnp.where(lo + 1 != hi, 1.0, 0.0)) > 0.0

    def bis_body(st):
        lo, hi, cnt_lo = st
        active = (lo + 1) != hi
        mid = lo + lax.shift_right_logical(hi - lo, 1)
        cnt = count_ge(_key_to_f32(mid))
        ge = cnt >= float(topk)
        exact = cnt == float(topk)
        new_lo = jnp.where(active & ge, mid, lo)
        new_hi = jnp.where(active & jnp.logical_not(ge), mid, hi)
        new_hi = jnp.where(active & exact, mid + 1, new_hi)
        new_cnt = jnp.where(active & ge, cnt, cnt_lo)
        return new_lo, new_hi, new_cnt

    lo, _, cnt_lo = lax.while_loop(bis_cond, bis_body, (lo0, hi0, n_valid))
    thr = _key_to_f32(lo)

    tie = jnp.logical_and(cnt_lo > float(topk), jnp.logical_not(few))

    @pl.when(jnp.max(jnp.where(tie, 1.0, 0.0)) > 0.0)
    def _():
        def gt_body(c, cnt):
            k0 = pl.multiple_of(c * kc, kc)
            s = sc_ref[pl.ds(k0, kc), :]
            return cnt + _fold8(jnp.where(s > thr, 1.0, 0.0), jnp.sum)
        gt8 = lax.fori_loop(0, nch, gt_body, jnp.zeros((SUBLANES, tq), F32))
        need = float(topk) - jnp.sum(gt8, axis=0, keepdims=True)
        r = lax.broadcasted_iota(jnp.int32, (kc, kc), 0)
        cidx = lax.broadcasted_iota(jnp.int32, (kc, kc), 1)
        tri = jnp.where(cidx <= r, 1.0, 0.0).astype(BF16)

        def tie_body(c, seen):
            k0 = pl.multiple_of(c * kc, kc)
            s = sc_ref[pl.ds(k0, kc), :]
            eqf = jnp.where(s == thr, jnp.where(tie, 1.0, 0.0), 0.0)
            pre = jnp.dot(tri, eqf.astype(BF16), preferred_element_type=F32) + seen
            drop = eqf * jnp.where(pre > need, 1.0, 0.0)
            sc_ref[pl.ds(k0, kc), :] = jnp.where(drop > 0.0, -jnp.inf, s)
            return seen + jnp.sum(eqf, axis=0, keepdims=True)

        lax.fori_loop(0, nch, tie_body, jnp.zeros((1, tq), F32))

    qT = jnp.concatenate([(qaT_ref[0, h * HEAD_DIM:(h + 1) * HEAD_DIM, :] * ATTN_SCALE).astype(BF16)
                          for h in range(A_HEADS)], axis=1)
    acc_ref[...] = jnp.zeros_like(acc_ref)
    ones_rows = jnp.ones((SUBLANES, kc), BF16)

    def issue_scores(c, slot):
        k0 = pl.multiple_of(jnp.minimum(c, nch - 1) * kc, kc)
        sbuf_ref[slot] = jnp.dot(ka_ref[0, pl.ds(k0, kc), :], qT, preferred_element_type=F32)

    def softmax_pv(c, slot, m):
        k0 = pl.multiple_of(c * kc, kc)
        v1 = jnp.concatenate([vaT_ref[0, c], ones_rows], axis=0)
        sel = sc_ref[pl.ds(k0, kc), :] >= thr
        s = jnp.concatenate([jnp.where(sel, sbuf_ref[slot, :, h * tq:(h + 1) * tq], NEG_BIG)
                             for h in range(A_HEADS)], axis=1)
        m_new = jnp.maximum(m, jnp.max(s, axis=0, keepdims=True))
        alpha = jnp.exp(m - m_new)
        p = jnp.exp(s - m_new).astype(BF16)
        acc_ref[...] = acc_ref[...] * alpha + jnp.dot(v1, p, preferred_element_type=F32)
        return m_new

    def chunk_pair(j, m):
        c0 = 2 * j
        issue_scores(c0 + 1, 1)
        m = softmax_pv(c0, 0, m)

        def odd_chunk(m):
            issue_scores(c0 + 2, 0)
            return softmax_pv(c0 + 1, 1, m)

        return lax.cond(c0 + 1 < nch, odd_chunk, lambda m: m, m)

    issue_scores(0, 0)
    lax.fori_loop(0, (nch + 1) // 2, chunk_pair, jnp.full((1, A_HEADS * tq), NEG_BIG, F32))
    a = acc_ref[...]
    o = a[:HEAD_DIM] / a[HEAD_DIM:HEAD_DIM + 1]
    for h in range(A_HEADS):
        o_ref[0, h * HEAD_DIM:(h + 1) * HEAD_DIM, :] = o[:, h * tq:(h + 1) * tq].astype(o_ref.dtype)


def _dsa(qaT, iqT, iwT, ik, ka, vaT4, *, tq, kc, topk):
    bsz, _, seq = qaT.shape
    return pl.pallas_call(
        functools.partial(_dsa_body, tq=tq, kc=kc, topk=topk),
        grid=(bsz, seq // tq),
        in_specs=[
            pl.BlockSpec((1, A_W, tq), lambda b, i: (b, 0, i)),
            pl.BlockSpec((1, A_IDX_HEADS * A_IDX_DIM, tq), lambda b, i: (b, 0, i)),
            pl.BlockSpec((1, A_IDX_HEADS, tq), lambda b, i: (b, 0, i)),
            pl.BlockSpec((1, seq, A_IDX_DIM), lambda b, i: (b, 0, 0)),
            pl.BlockSpec((1, seq, HEAD_DIM), lambda b, i: (b, 0, 0)),
            pl.BlockSpec((1, seq // kc, HEAD_DIM, kc), lambda b, i: (b, 0, 0, 0)),
        ],
        out_specs=pl.BlockSpec((1, A_W, tq), lambda b, i: (b, 0, i)),
        out_shape=jax.ShapeDtypeStruct((bsz, A_W, seq), BF16),
        scratch_shapes=[pltpu.VMEM((seq, tq), F32),
                        pltpu.VMEM((HEAD_DIM + SUBLANES, A_HEADS * tq), F32),
                        pltpu.VMEM((2, kc, A_HEADS * tq), F32)],
        compiler_params=_cparams(2),
        name="dsa_attention",
    )(qaT, iqT, iwT, ik, ka, vaT4)


def _kmean_body(k_ref, o_ref):
    o_ref[0] = jnp.mean(k_ref[...].astype(F32), axis=0, keepdims=True)


def _moba_kmean(h_rope, *, n_tokens):
    nblk = n_tokens // B_BLOCK
    return pl.pallas_call(
        _kmean_body,
        grid=(nblk,),
        in_specs=[pl.BlockSpec((B_BLOCK, B_W), lambda i: (i, TILE_BK))],
        out_specs=pl.BlockSpec((1, 1, B_W), lambda i: (i, 0, 0)),
        out_shape=jax.ShapeDtypeStruct((nblk, 1, B_W), F32),
        compiler_params=_cparams(1),
        name="moba_kmean",
    )(h_rope)


def _moba_gate_body(qT_ref, km_ref, sel_ref, *, tq, nkb, nsel):
    i = pl.program_id(1)
    qpos = i * tq + lax.broadcasted_iota(jnp.int32, (1, tq), 1)
    cur = qpos // B_BLOCK
    row = lax.broadcasted_iota(jnp.int32, (nkb, tq), 0)
    past = row < cur
    for h in range(B_HEADS):
        g = jnp.dot(km_ref[0, h], qT_ref[0, h * HEAD_DIM:(h + 1) * HEAD_DIM, :], preferred_element_type=F32)
        g = jnp.where(past, g, -jnp.inf)
        sel = jnp.zeros((nkb, tq), F32)
        for _ in range(nsel):
            mx = jnp.max(g, axis=0, keepdims=True)
            idx = jnp.min(jnp.where(g == mx, row, nkb), axis=0, keepdims=True)
            pick = row == idx
            sel = jnp.where(pick, 1.0, sel)
            g = jnp.where(pick, -jnp.inf, g)
        sel_ref[0, h] = jnp.where(past, sel, 0.0)


def _moba_gate(bqT, kmean, *, tq):
    bsz, _, seq = bqT.shape
    nkb = seq // B_BLOCK
    nsel = min(B_TOPK_BLOCKS, nkb)
    return pl.pallas_call(
        functools.partial(_moba_gate_body, tq=tq, nkb=nkb, nsel=nsel),
        grid=(bsz, seq // tq),
        in_specs=[
            pl.BlockSpec((1, B_W, tq), lambda b, i: (b, 0, i)),
            pl.BlockSpec((1, B_HEADS, nkb, HEAD_DIM), lambda b, i: (b, 0, 0, 0)),
        ],
        out_specs=pl.BlockSpec((1, B_HEADS, nkb, tq), lambda b, i: (b, 0, 0, i)),
        out_shape=jax.ShapeDtypeStruct((bsz, B_HEADS, nkb, seq), F32),
        compiler_params=_cparams(2),
        name="moba_gate",
    )(bqT, kmean)


def _moba_body(qT_ref, k_ref, vT_ref, sel_ref, o_ref, acc_ref, sbuf_ref, *, tq):
    i = pl.program_id(1)
    qT = [(qT_ref[0, h * HEAD_DIM:(h + 1) * HEAD_DIM, :] * ATTN_SCALE).astype(BF16) for h in range(B_HEADS)]
    ones_rows = jnp.ones((SUBLANES, B_BLOCK), BF16)
    n_past = i

    def values(h, j):
        return jnp.concatenate([vT_ref[0, h, j], ones_rows], axis=0)

    def issue_scores(j, slot):
        k0 = pl.multiple_of(jnp.minimum(j, jnp.maximum(n_past - 1, 0)) * B_BLOCK, B_BLOCK)
        for h in range(B_HEADS):
            sbuf_ref[slot, h] = jnp.dot(k_ref[0, h, pl.ds(k0, B_BLOCK), :], qT[h], preferred_element_type=F32)

    def softmax_pv(j, slot, ms):
        ps, alphas, new_ms = [], [], []
        for h in range(B_HEADS):
            s = jnp.where(sel_ref[0, h, pl.ds(j, 1), :] > 0.0, sbuf_ref[slot, h], NEG_BIG)
            m_new = jnp.maximum(ms[h], jnp.max(s, axis=0, keepdims=True))
            alphas.append(jnp.exp(ms[h] - m_new))
            ps.append(jnp.exp(s - m_new).astype(BF16))
            new_ms.append(m_new)
        for h in range(B_HEADS):
            acc_ref[h] = acc_ref[h] * alphas[h] + jnp.dot(values(h, j), ps[h], preferred_element_type=F32)
        return tuple(new_ms)

    own0 = pl.multiple_of(i * B_BLOCK, B_BLOCK)
    s_own = [jnp.dot(k_ref[0, h, pl.ds(own0, B_BLOCK), :], qT[h], preferred_element_type=F32) for h in range(B_HEADS)]
    issue_scores(0, 0)
    causal = lax.broadcasted_iota(jnp.int32, (B_BLOCK, tq), 0) <= lax.broadcasted_iota(jnp.int32, (B_BLOCK, tq), 1)
    ms = []
    for h in range(B_HEADS):
        s = jnp.where(causal, s_own[h], NEG_BIG)
        m0 = jnp.max(s, axis=0, keepdims=True)
        acc_ref[h] = jnp.dot(values(h, i), jnp.exp(s - m0).astype(BF16), preferred_element_type=F32)
        ms.append(m0)

    def block_pair(jj, ms):
        j0 = 2 * jj
        issue_scores(j0 + 1, 1)
        ms = softmax_pv(j0, 0, ms)

        def odd_block(ms):
            issue_scores(j0 + 2, 0)
            return softmax_pv(j0 + 1, 1, ms)

        return lax.cond(j0 + 1 < n_past, odd_block, lambda ms: ms, ms)

    lax.fori_loop(0, (n_past + 1) // 2, block_pair, tuple(ms))
    for h in range(B_HEADS):
        a = acc_ref[h]
        o_ref[0, h * HEAD_DIM:(h + 1) * HEAD_DIM, :] = (a[:HEAD_DIM] / a[HEAD_DIM:HEAD_DIM + 1]).astype(o_ref.dtype)


def _moba(bqT, bk, bvT5, sel):
    bsz, _, seq = bqT.shape
    tq = B_BLOCK
    nkb = seq // B_BLOCK
    return pl.pallas_call(
        functools.partial(_moba_body, tq=tq),
        grid=(bsz, seq // tq),
        in_specs=[
            pl.BlockSpec((1, B_W, tq), lambda b, i: (b, 0, i)),
            pl.BlockSpec((1, B_HEADS, seq, HEAD_DIM), lambda b, i: (b, 0, 0, 0)),
            pl.BlockSpec((1, B_HEADS, nkb, HEAD_DIM, B_BLOCK), lambda b, i: (b, 0, 0, 0, 0)),
            pl.BlockSpec((1, B_HEADS, nkb, tq), lambda b, i: (b, 0, 0, i)),
        ],
        out_specs=pl.BlockSpec((1, B_W, tq), lambda b, i: (b, 0, i)),
        out_shape=jax.ShapeDtypeStruct((bsz, B_W, seq), BF16),
        scratch_shapes=[pltpu.VMEM((B_HEADS, HEAD_DIM + SUBLANES, tq), F32),
                        pltpu.VMEM((2, B_HEADS, B_BLOCK, tq), F32)],
        compiler_params=_cparams(2),
        name="moba_attention",
    )(bqT, bk, bvT5, sel)


def _swa_body(sink_ref, qT_ref, k0_ref, k1_ref, k2_ref, v0_ref, v1_ref, v2_ref, o_ref, *, tq):
    i = pl.program_id(1)
    q0 = i * tq
    qpos = q0 + lax.broadcasted_iota(jnp.int32, (1, tq), 1)
    nk = tq + C_WINDOW
    kpos = q0 - C_WINDOW + lax.broadcasted_iota(jnp.int32, (nk, 1), 0)
    diff = qpos - kpos
    ok = jnp.where(diff >= 0, jnp.where(diff < C_WINDOW, jnp.where(kpos >= 0, 1.0, 0.0), 0.0), 0.0) > 0.0
    ones_rows = jnp.ones((SUBLANES, nk), BF16)
    group = C_HEADS // C_KV_HEADS
    for kv in range(C_KV_HEADS):
        kk = jnp.concatenate([k0_ref[0, kv], k1_ref[0, kv], k2_ref[0, kv]], axis=0)
        vv = jnp.concatenate([v0_ref[0, kv], v1_ref[0, kv], v2_ref[0, kv]], axis=1)
        v1 = jnp.concatenate([vv, ones_rows], axis=0)
        heads = [kv * group + g for g in range(group)]
        qT = jnp.concatenate([(qT_ref[0, hd * HEAD_DIM:(hd + 1) * HEAD_DIM, :] * ATTN_SCALE).astype(BF16)
                              for hd in heads], axis=1)
        sink = jnp.concatenate([jnp.full((1, tq), sink_ref[hd], F32) for hd in heads], axis=1)
        s = jnp.dot(kk, qT, preferred_element_type=F32)
        s = jnp.concatenate([jnp.where(ok, s[:, g * tq:(g + 1) * tq], NEG_BIG) for g in range(group)], axis=1)
        m = jnp.maximum(jnp.max(s, axis=0, keepdims=True), sink)
        p = jnp.exp(s - m).astype(BF16)
        a = jnp.dot(v1, p, preferred_element_type=F32)
        o = a[:HEAD_DIM] / (a[HEAD_DIM:HEAD_DIM + 1] + jnp.exp(sink - m))
        for g, hd in enumerate(heads):
            o_ref[0, hd * HEAD_DIM:(hd + 1) * HEAD_DIM, :] = o[:, g * tq:(g + 1) * tq].astype(o_ref.dtype)


def _swa(sinks, cqT, ck, cvT, *, tq):
    bsz, _, seq = cqT.shape
    r = tq // C_WINDOW
    assert r == 2

    def kspec(off):
        return pl.BlockSpec((1, C_KV_HEADS, C_WINDOW, HEAD_DIM),
                            lambda b, i, s: (b, 0, jnp.maximum(i * r + off, 0), 0))

    def vspec(off):
        return pl.BlockSpec((1, C_KV_HEADS, HEAD_DIM, C_WINDOW),
                            lambda b, i, s: (b, 0, 0, jnp.maximum(i * r + off, 0)))

    grid_spec = pltpu.PrefetchScalarGridSpec(
        num_scalar_prefetch=1,
        grid=(bsz, seq // tq),
        in_specs=[pl.BlockSpec((1, C_W, tq), lambda b, i, s: (b, 0, i)),
                  kspec(-1), kspec(0), kspec(1), vspec(-1), vspec(0), vspec(1)],
        out_specs=pl.BlockSpec((1, C_W, tq), lambda b, i, s: (b, 0, i)),
    )
    return pl.pallas_call(
        functools.partial(_swa_body, tq=tq),
        grid_spec=grid_spec,
        out_shape=jax.ShapeDtypeStruct((bsz, C_W, seq), BF16),
        compiler_params=_cparams(2),
        name="swa_attention",
    )(sinks, cqT, ck, ck, ck, cvT, cvT, cvT)


def _layer_norm(z, g, b):
    mu = jnp.mean(z, axis=-1, keepdims=True)
    zc = z - mu
    var = jnp.mean(zc * zc, axis=-1, keepdims=True)
    return zc * lax.rsqrt(var + LN_EPS) * g + b


def _split_bf16(a):
    hi = a.astype(BF16)
    lo = (a - hi.astype(F32)).astype(BF16)
    return hi, lo


def _router(x1, rw_ref, rb_ref, cls_ref, wt_ref):
    nt = (((1,), (1,)), ((), ()))
    xh, xl = _split_bf16(x1)
    wh, wl = _split_bf16(rw_ref[...])
    logits = (lax.dot_general(wh, xh, nt, preferred_element_type=F32)
              + lax.dot_general(wh, xl, nt, preferred_element_type=F32)
              + lax.dot_general(wl, xh, nt, preferred_element_type=F32))
    aff = jax.nn.sigmoid(logits)
    score = aff + rb_ref[...]
    tm = x1.shape[0]
    sc = [score[e:e + 1, :] for e in range(N_EXPERTS)]
    af = [aff[e:e + 1, :] for e in range(N_EXPERTS)]
    in_top = []
    grp_score = []
    for gq in range(N_GROUPS):
        gs = jnp.zeros((1, tm), F32)
        for a in range(EXPERTS_PER_GROUP):
            ea = gq * EXPERTS_PER_GROUP + a
            rank = jnp.zeros((1, tm), F32)
            for b in range(EXPERTS_PER_GROUP):
                if b == a:
                    continue
                eb = gq * EXPERTS_PER_GROUP + b
                beats = (sc[eb] >= sc[ea]) if b < a else (sc[eb] > sc[ea])
                rank = rank + jnp.where(beats, 1.0, 0.0)
            top = rank < 2.0
            in_top.append(top)
            gs = gs + jnp.where(top, sc[ea], 0.0)
        grp_score.append(gs)
    best = grp_score[0]
    gstar = jnp.zeros((1, tm), jnp.int32)
    for gq in range(1, N_GROUPS):
        better = grp_score[gq] > best
        best = jnp.where(better, grp_score[gq], best)
        gstar = jnp.where(better, gq, gstar)
    cls = jnp.zeros((1, tm), jnp.int32)
    w_lo = jnp.zeros((1, tm), F32)
    w_hi = jnp.zeros((1, tm), F32)
    for gq in range(N_GROUPS):
        is_g = gstar == gq
        for pi, (a, b) in enumerate(PAIRS):
            ea, eb = gq * EXPERTS_PER_GROUP + a, gq * EXPERTS_PER_GROUP + b
            hit = jnp.where(is_g, jnp.where(in_top[ea], jnp.where(in_top[eb], 1.0, 0.0), 0.0), 0.0) > 0.0
            cls = jnp.where(hit, gq * len(PAIRS) + pi, cls)
            tot = af[ea] + af[eb]
            w_lo = jnp.where(hit, af[ea] / tot, w_lo)
            w_hi = jnp.where(hit, af[eb] / tot, w_hi)
    cls_ref[...] = cls
    wt_ref[0:1, :] = w_lo
    wt_ref[1:2, :] = w_hi


def _merge_body(oa_ref, ob_ref, oc_ref, ga_ref, gb_ref, gc_ref, x_ref, wa_ref, wb_ref, wc_ref, wo_ref,
                g_ref, b_ref, rw_ref, rb_ref, x1_ref, cls_ref, wt_ref):
    def branch(o_ref, w_ref, gate_ref):
        y = jnp.dot(o_ref[...], w_ref[...], preferred_element_type=F32)
        return jax.nn.sigmoid(gate_ref[...].astype(F32)) * y

    merged = branch(oa_ref, wa_ref, ga_ref) + branch(ob_ref, wb_ref, gb_ref) + branch(oc_ref, wc_ref, gc_ref)
    y = jnp.dot(merged.astype(BF16), wo_ref[...], preferred_element_type=F32)
    x1 = _layer_norm(DN_ALPHA * x_ref[...] + y, g_ref[...], b_ref[...])
    x1_ref[...] = x1
    _router(x1, rw_ref, rb_ref, cls_ref, wt_ref)


def _merge(oa, ob, oc, h_plain, x, wa, wb, wc, wo, ln_g, ln_b, rwT, rb, *, tm):
    t_tokens = x.shape[0]
    gate_blk = COL_GATES // D_MODEL
    full = lambda shape: pl.BlockSpec(shape, lambda i: (0,) * len(shape))
    return pl.pallas_call(
        _merge_body,
        grid=(t_tokens // tm,),
        in_specs=[
            pl.BlockSpec((tm, A_W), lambda i: (i, 0)),
            pl.BlockSpec((tm, B_W), lambda i: (i, 0)),
            pl.BlockSpec((tm, C_W), lambda i: (i, 0)),
            pl.BlockSpec((tm, D_MODEL), lambda i: (i, gate_blk)),
            pl.BlockSpec((tm, D_MODEL), lambda i: (i, gate_blk + 1)),
            pl.BlockSpec((tm, D_MODEL), lambda i: (i, gate_blk + 2)),
            pl.BlockSpec((tm, D_MODEL), lambda i: (i, 0)),
            full((A_W, D_MODEL)), full((B_W, D_MODEL)), full((C_W, D_MODEL)), full((D_MODEL, D_MODEL)),
            full((1, D_MODEL)), full((1, D_MODEL)), full((N_EXPERTS, D_MODEL)), full((N_EXPERTS, 1)),
        ],
        out_specs=[
            pl.BlockSpec((tm, D_MODEL), lambda i: (i, 0)),
            pl.BlockSpec((1, tm), lambda i: (0, i)),
            pl.BlockSpec((2, tm), lambda i: (0, i)),
        ],
        out_shape=[
            jax.ShapeDtypeStruct((t_tokens, D_MODEL), F32),
            jax.ShapeDtypeStruct((1, t_tokens), jnp.int32),
            jax.ShapeDtypeStruct((2, t_tokens), F32),
        ],
        compiler_params=_cparams(1),
        name="merge_ln_router",
    )(oa, ob, oc, h_plain, h_plain, h_plain, x, wa, wb, wc, wo, ln_g, ln_b, rwT, rb)


def _gather_start(idx_ref, src_hbm, dst_ref, sem, n_rows):
    def start(r, _):
        pltpu.make_async_copy(src_hbm.at[pl.ds(idx_ref[0, 0, r], 1)], dst_ref.at[pl.ds(r, 1)], sem).start()
        return 0

    lax.fori_loop(0, n_rows, start, 0, unroll=8)


def _gather_wait(src_hbm, dst_ref, sem, n_rows):
    pltpu.make_async_copy(src_hbm.at[pl.ds(0, n_rows)], dst_ref, sem).wait()


def _experts_body(e1_ref, e2_ref, nused_ref, src_ref, src_next_ref, x_hbm, wt_ref,
                  wg1_ref, wu1_ref, wd1_ref, wg2_ref, wu2_ref, wd2_ref, o_ref, xbuf, sem):
    s = pl.program_id(0)
    nused = nused_ref[0]
    slot = lax.rem(s, 2)

    @pl.when(s == 0)
    def _():
        _gather_start(src_ref, x_hbm, xbuf.at[0], sem.at[0], MOE_TILE)

    @pl.when(s + 1 < nused)
    def _():
        _gather_start(src_next_ref, x_hbm, xbuf.at[1 - slot], sem.at[1 - slot], MOE_TILE)

    @pl.when(s < nused)
    def _():
        _gather_wait(x_hbm, xbuf.at[slot], sem.at[slot], MOE_TILE)
        xb = xbuf[slot].astype(BF16)
        wt = wt_ref[...]

        def expert(wg_ref, wu_ref, wd_ref):
            g = jnp.dot(xb, wg_ref[0], preferred_element_type=F32)
            u = jnp.dot(xb, wu_ref[0], preferred_element_type=F32)
            he = (g * jax.nn.sigmoid(g) * u).astype(BF16)
            return jnp.dot(he, wd_ref[0], preferred_element_type=F32)

        o_ref[...] = wt[:, 0:1] * expert(wg1_ref, wu1_ref, wd1_ref) + wt[:, 1:2] * expert(wg2_ref, wu2_ref, wd2_ref)

    @pl.when(s >= nused)
    def _():
        o_ref[...] = jnp.zeros_like(o_ref)


def _experts(tile_e1, tile_e2, nused, src3, x1, wsort, wg, wu, wd):
    ntiles = src3.shape[0]
    wspec_up = lambda which: pl.BlockSpec((1, D_MODEL, D_EXPERT), lambda s, e1, e2, nu: ((e1, e2)[which][s], 0, 0))
    wspec_dn = lambda which: pl.BlockSpec((1, D_EXPERT, D_MODEL), lambda s, e1, e2, nu: ((e1, e2)[which][s], 0, 0))
    grid_spec = pltpu.PrefetchScalarGridSpec(
        num_scalar_prefetch=3,
        grid=(ntiles,),
        in_specs=[
            pl.BlockSpec((1, 1, MOE_TILE), lambda s, e1, e2, nu: (s, 0, 0), memory_space=pltpu.SMEM),
            pl.BlockSpec((1, 1, MOE_TILE), lambda s, e1, e2, nu: (jnp.minimum(s + 1, ntiles - 1), 0, 0),
                         memory_space=pltpu.SMEM),
            pl.BlockSpec(memory_space=pl.ANY),
            pl.BlockSpec((MOE_TILE, 2), lambda s, e1, e2, nu: (s, 0)),
            wspec_up(0), wspec_up(0), wspec_dn(0), wspec_up(1), wspec_up(1), wspec_dn(1),
        ],
        out_specs=pl.BlockSpec((MOE_TILE, D_MODEL), lambda s, e1, e2, nu: (s, 0)),
        scratch_shapes=[pltpu.VMEM((2, MOE_TILE, D_MODEL), F32), pltpu.SemaphoreType.DMA((2,))],
    )
    return pl.pallas_call(
        _experts_body,
        grid_spec=grid_spec,
        out_shape=jax.ShapeDtypeStruct((ntiles * MOE_TILE, D_MODEL), F32),
        compiler_params=_cparams(1),
        name="moe_experts",
    )(tile_e1, tile_e2, nused, src3, src3, x1, wsort, wg, wu, wd, wg, wu, wd)


def _final_body(pos_ref, pos_next_ref, y_hbm, x_ref, g_ref, b_ref, o_ref, ob_ref, ybuf, sem, *, tm):
    s = pl.program_id(0)
    slot = lax.rem(s, 2)

    @pl.when(s == 0)
    def _():
        _gather_start(pos_ref, y_hbm, ybuf.at[0], sem.at[0], tm)

    @pl.when(s + 1 < pl.num_programs(0))
    def _():
        _gather_start(pos_next_ref, y_hbm, ybuf.at[1 - slot], sem.at[1 - slot], tm)

    _gather_wait(y_hbm, ybuf.at[slot], sem.at[slot], tm)
    x2 = _layer_norm(DN_ALPHA * x_ref[...] + ybuf[slot], g_ref[...], b_ref[...])
    o_ref[...] = x2
    ob_ref[...] = x2.astype(BF16)


def _final(pos3, y_sorted, x1, ln_g, ln_b, *, tm):
    t_tokens = x1.shape[0]
    nsteps = t_tokens // tm
    return pl.pallas_call(
        functools.partial(_final_body, tm=tm),
        grid=(nsteps,),
        in_specs=[
            pl.BlockSpec((1, 1, tm), lambda i: (i, 0, 0), memory_space=pltpu.SMEM),
            pl.BlockSpec((1, 1, tm), lambda i: (jnp.minimum(i + 1, nsteps - 1), 0, 0), memory_space=pltpu.SMEM),
            pl.BlockSpec(memory_space=pl.ANY),
            pl.BlockSpec((tm, D_MODEL), lambda i: (i, 0)),
            pl.BlockSpec((1, D_MODEL), lambda i: (0, 0)),
            pl.BlockSpec((1, D_MODEL), lambda i: (0, 0)),
        ],
        out_specs=[pl.BlockSpec((tm, D_MODEL), lambda i: (i, 0)), pl.BlockSpec((tm, D_MODEL), lambda i: (i, 0))],
        out_shape=[jax.ShapeDtypeStruct((t_tokens, D_MODEL), F32), jax.ShapeDtypeStruct((t_tokens, D_MODEL), BF16)],
        scratch_shapes=[pltpu.VMEM((2, tm, D_MODEL), F32), pltpu.SemaphoreType.DMA((2,))],
        compiler_params=_cparams(1),
        name="moe_combine_ln",
    )(pos3, pos3, y_sorted, x1, ln_g, ln_b)


def _rope_tables(seq):
    inv = 1.0 / (ROPE_THETA ** (jnp.arange(0, HEAD_DIM, 2, dtype=F32) / HEAD_DIM))
    ang = jnp.arange(seq, dtype=F32)[:, None] * inv[None, :]
    return jnp.cos(ang), jnp.sin(ang)


def _halves_layout(w_heads):
    d, n = w_heads.shape
    t = w_heads.reshape(d, n // PROJ_TN, HEADS_PER_TILE, 2, HEAD_DIM // 2)
    return t.transpose(0, 1, 3, 2, 4).reshape(d, n)


def _heads_layout(h_tiles):
    t, n = h_tiles.shape
    v = h_tiles.reshape(t, n // PROJ_TN, 2, HEADS_PER_TILE, HEAD_DIM // 2)
    return v.transpose(0, 1, 3, 2, 4).reshape(t, n)


def _reorder_w_in(w):
    pts = np.cumsum((0,) + IN_SIZES)
    sec = [w[:, pts[k]:pts[k + 1]] for k in range(len(IN_SIZES))]
    a_q, a_c, a_iq, a_ik, a_iw, b_q, b_k, b_v, c_q, c_k, c_v, gates = sec
    zeros = lambda n: jnp.zeros((w.shape[0], n), w.dtype)
    w_rope = _halves_layout(jnp.concatenate([a_q, a_iq, b_q, b_k, c_q, c_k, a_ik, zeros(HEAD_DIM)], axis=1))
    w_plain = jnp.concatenate([gates, b_v, a_c, a_iw, zeros(LANES - A_IDX_HEADS), c_v, zeros(LANES)], axis=1)
    assert w_rope.shape[1] == N_ROPE and w_plain.shape[1] == N_PLAIN
    return w_rope.astype(BF16), w_plain.astype(BF16)


def _kv_weights(w_uk, w_uv):
    half = HEAD_DIM // 2
    zeros = lambda n: jnp.zeros((w_uk.shape[0], n), w_uk.dtype)
    return jnp.concatenate([w_uk[:, :half], w_uv, zeros(HALF_TN - half - HEAD_DIM),
                            w_uk[:, half:], zeros(HALF_TN - half)], axis=1).astype(BF16)


def _moe_plan(cls, wts, n_tokens):
    ntiles = n_tokens // MOE_TILE + N_CLASSES
    onehot = (cls[:, None] == jnp.arange(N_CLASSES, dtype=jnp.int32)[None, :]).astype(jnp.int32)
    csum = jnp.cumsum(onehot, axis=0)
    rank = jnp.sum(onehot * csum, axis=1) - 1
    counts = csum[-1]
    ptiles = (counts + MOE_TILE - 1) // MOE_TILE
    tile_end = jnp.cumsum(ptiles)
    tile_start = tile_end - ptiles
    pos = (tile_start[cls] * MOE_TILE + rank).astype(jnp.int32)
    nused = tile_end[-1:].astype(jnp.int32)
    tile_ids = jnp.arange(ntiles, dtype=jnp.int32)
    tile_cls = jnp.minimum(jnp.sum((tile_ids[:, None] >= tile_end[None, :]).astype(jnp.int32), axis=1),
                           N_CLASSES - 1).astype(jnp.int32)
    pair = np.array(PAIRS, dtype=np.int32)
    e_lo = jnp.asarray(np.repeat(np.arange(N_GROUPS), len(PAIRS)) * EXPERTS_PER_GROUP + np.tile(pair[:, 0], N_GROUPS), jnp.int32)
    e_hi = jnp.asarray(np.repeat(np.arange(N_GROUPS), len(PAIRS)) * EXPERTS_PER_GROUP + np.tile(pair[:, 1], N_GROUPS), jnp.int32)
    src = jnp.zeros((ntiles * MOE_TILE,), jnp.int32).at[pos].set(jnp.arange(n_tokens, dtype=jnp.int32))
    wsort = jnp.zeros((ntiles * MOE_TILE, 2), F32).at[pos].set(wts.T)
    return e_lo[tile_cls], e_hi[tile_cls], nused, src.reshape(ntiles, 1, MOE_TILE), pos, wsort


def kernel(x, w_in, a_w_uk, a_w_uv, c_sinks, w_branch, w_o, ln1_g, ln1_b, router_w, router_b,
           moe_w_gate, moe_w_up, moe_w_down, ln2_g, ln2_b):
    bsz, seq, _ = x.shape
    n_tokens = bsz * seq
    topk = min(A_TOPK_MAX, seq // 4)
    nkb = seq // B_BLOCK
    tq = 256
    kc = 256
    tm_proj = min(1024, seq)
    tm_row = 512
    half = HEAD_DIM // 2

    cos, sin = _rope_tables(seq)
    cos_in = jnp.tile(cos, (1, HEADS_PER_TILE))
    sin_in = jnp.tile(sin, (1, HEADS_PER_TILE))
    cos_kv = jnp.concatenate([cos, jnp.ones((seq, HALF_TN - half), F32)], axis=1)
    sin_kv = jnp.concatenate([sin, jnp.zeros((seq, HALF_TN - half), F32)], axis=1)
    rwT = router_w.T
    rb = router_b.reshape(N_EXPERTS, 1)

    def to_T(t2d):
        return t2d.reshape(bsz, seq, t2d.shape[1]).transpose(0, 2, 1)

    xf = x.reshape(n_tokens, D_MODEL)
    xb = xf.astype(BF16)
    for l in range(DEPTH):
        w_rope, w_plain = _reorder_w_in(w_in[l])
        h_rope = _proj_rope(xb, 0, D_MODEL, w_rope, cos_in, sin_in, tm=tm_proj, seq=seq)
        h_plain = _proj_plain(xb, w_plain, tm=tm_proj)
        kv = _proj_rope(h_plain, COL_AC // A_KV_RANK, A_KV_RANK, _kv_weights(a_w_uk[l], a_w_uv[l]),
                        cos_kv, sin_kv, tm=tm_proj, seq=seq)

        def rope_heads(tile0, ntiles):
            return _heads_layout(h_rope[:, tile0 * PROJ_TN:(tile0 + ntiles) * PROJ_TN])

        ckik = rope_heads(TILE_CK_AIK, 1)

        qaT = to_T(rope_heads(TILE_AQ, 1))
        iqT = to_T(rope_heads(TILE_AIQ, 1))
        iwT = to_T(h_plain[:, COL_AIW:COL_AIW + A_IDX_HEADS])
        ik = ckik[:, 2 * HEAD_DIM:3 * HEAD_DIM].reshape(bsz, seq, A_IDX_DIM)
        ka = jnp.concatenate([kv[:, :half], kv[:, HALF_TN:HALF_TN + half]], axis=1).reshape(bsz, seq, HEAD_DIM)
        vaT4 = kv[:, KV_VA0:KV_VA0 + HEAD_DIM].reshape(bsz, seq // kc, kc, HEAD_DIM).transpose(0, 1, 3, 2)
        oaT = _dsa(qaT, iqT, iwT, ik, ka, vaT4, tq=tq, kc=kc, topk=topk)

        bqT = to_T(rope_heads(TILE_BQ, 1))
        kmean = _heads_layout(_moba_kmean(h_rope, n_tokens=n_tokens).reshape(n_tokens // B_BLOCK, B_W))
        kmean = kmean.reshape(bsz, nkb, B_HEADS, HEAD_DIM).transpose(0, 2, 1, 3).astype(BF16)
        sel = _moba_gate(bqT, kmean, tq=tq)
        bk = rope_heads(TILE_BK, 1).reshape(bsz, seq, B_HEADS, HEAD_DIM).transpose(0, 2, 1, 3)
        bvT5 = (h_plain[:, COL_BV:COL_BV + B_W].reshape(bsz, nkb, B_BLOCK, B_HEADS, HEAD_DIM)
                .transpose(0, 3, 1, 4, 2))
        obT = _moba(bqT, bk, bvT5, sel)

        cqT = to_T(rope_heads(TILE_CQ, 2))
        ck = ckik[:, :C_KV_HEADS * HEAD_DIM].reshape(bsz, seq, C_KV_HEADS, HEAD_DIM).transpose(0, 2, 1, 3)
        cvT = (h_plain[:, COL_CV:COL_CV + C_KV_HEADS * HEAD_DIM].reshape(bsz, seq, C_KV_HEADS, HEAD_DIM)
               .transpose(0, 2, 3, 1))
        ocT = _swa(c_sinks[l], cqT, ck, cvT, tq=tq)

        rows = lambda t: t.transpose(0, 2, 1).reshape(n_tokens, t.shape[1])
        wb_all = w_branch[l].astype(BF16)
        x1, cls, wts = _merge(rows(oaT), rows(obT), rows(ocT), h_plain, xf,
                              wb_all[:A_W], wb_all[A_W:A_W + B_W], wb_all[A_W + B_W:], w_o[l].astype(BF16),
                              ln1_g[l].reshape(1, D_MODEL), ln1_b[l].reshape(1, D_MODEL), rwT, rb, tm=tm_row)

        e1, e2, nused, src3, pos, wsort = _moe_plan(cls[0], wts, n_tokens)
        y_sorted = _experts(e1, e2, nused, src3, x1, wsort,
                            moe_w_gate[l].astype(BF16), moe_w_up[l].astype(BF16), moe_w_down[l].astype(BF16))
        xf, xb = _final(pos.reshape(n_tokens // tm_row, 1, tm_row), y_sorted, x1,
                        ln2_g[l].reshape(1, D_MODEL), ln2_b[l].reshape(1, D_MODEL), tm=tm_row)
    return xf.reshape(bsz, seq, D_MODEL)
```

```python
import functools
import math

import jax
import jax.numpy as jnp
import numpy as np
from jax import lax
from jax.experimental import pallas as pl
from jax.experimental.pallas import tpu as pltpu

D_MODEL = 1024
DEPTH = 2
HEAD_DIM = 64
ROPE_THETA = 10000.0
LN_EPS = 1e-5
A_HEADS = 4
A_KV_RANK = 128
A_IDX_HEADS = 4
A_IDX_DIM = 64
A_TOPK_MAX = 256
B_HEADS = 4
B_BLOCK = 256
B_TOPK_BLOCKS = 3
C_HEADS = 8
C_KV_HEADS = 2
C_WINDOW = 128
N_BRANCH = 3
A_W = A_HEADS * HEAD_DIM
B_W = B_HEADS * HEAD_DIM
C_W = C_HEADS * HEAD_DIM
IN_SIZES = (A_W, A_KV_RANK, A_IDX_HEADS * A_IDX_DIM, A_IDX_DIM, A_IDX_HEADS,
            B_W, B_W, B_W, C_W, C_KV_HEADS * HEAD_DIM, C_KV_HEADS * HEAD_DIM,
            N_BRANCH * D_MODEL)
N_EXPERTS = 16
N_GROUPS = 4
EXPERTS_PER_GROUP = 4
D_EXPERT = 512
DN_ALPHA = (2 * DEPTH) ** 0.25
ATTN_SCALE = HEAD_DIM ** -0.5

LANES = 128
SUBLANES = 8
VMEM_LIMIT_BYTES = 56 * 1024 * 1024

NEG_BIG = -1e30
BF16 = jnp.bfloat16
F32 = jnp.float32

PROJ_TN = 256
HALF_TN = PROJ_TN // 2
HEADS_PER_TILE = PROJ_TN // HEAD_DIM
TILE_AQ = 0
TILE_AIQ = 1
TILE_BQ = 2
TILE_BK = 3
TILE_CQ = 4
TILE_CK_AIK = 6
N_ROPE = 7 * PROJ_TN
COL_GATES = 0
COL_BV = 3072
COL_AC = 3328
COL_AIW = 3456
COL_CV = 3584
N_PLAIN = 3840
KV_VA0 = HEAD_DIM // 2

PAIRS = ((0, 1), (0, 2), (0, 3), (1, 2), (1, 3), (2, 3))
N_CLASSES = N_GROUPS * len(PAIRS)
MOE_TILE = 256

VALUE_MID_PASSES = 16
PEEL_MAX = 2.0
MIN_NORMAL_KEY = 0x00800000
MAX_SELECT_PASSES = 96


def _cparams(n_axes):
    return pltpu.CompilerParams(dimension_semantics=("arbitrary",) * n_axes,
                                vmem_limit_bytes=VMEM_LIMIT_BYTES)


def _proj_rope_body(x_ref, w_ref, cos_ref, sin_ref, o_ref, *, n_sub):
    sub = x_ref.shape[0] // n_sub
    for mi in range(n_sub):
        rows = slice(mi * sub, (mi + 1) * sub)
        acc = jnp.dot(x_ref[rows, :], w_ref[...], preferred_element_type=F32)
        a1, a2 = acc[:, :HALF_TN], acc[:, HALF_TN:]
        c, sn = cos_ref[rows, :], sin_ref[rows, :]
        o_ref[rows, :HALF_TN] = (a1 * c - a2 * sn).astype(o_ref.dtype)
        o_ref[rows, HALF_TN:] = (a2 * c + a1 * sn).astype(o_ref.dtype)


def _proj_plain_body(x_ref, w_ref, o_ref, *, n_sub):
    sub = x_ref.shape[0] // n_sub
    for mi in range(n_sub):
        rows = slice(mi * sub, (mi + 1) * sub)
        o_ref[rows, :] = jnp.dot(x_ref[rows, :], w_ref[...], preferred_element_type=F32).astype(o_ref.dtype)


def _proj_rope(x, x_col_block, k_dim, w, cos_t, sin_t, *, tm, seq):
    t_tokens = x.shape[0]
    n = w.shape[1]
    pos_blocks = seq // tm
    return pl.pallas_call(
        functools.partial(_proj_rope_body, n_sub=max(tm // 256, 1)),
        grid=(t_tokens // tm, n // PROJ_TN),
        in_specs=[
            pl.BlockSpec((tm, k_dim), lambda i, j: (i, x_col_block)),
            pl.BlockSpec((k_dim, PROJ_TN), lambda i, j: (0, j)),
            pl.BlockSpec((tm, HALF_TN), lambda i, j: (i % pos_blocks, 0)),
            pl.BlockSpec((tm, HALF_TN), lambda i, j: (i % pos_blocks, 0)),
        ],
        out_specs=pl.BlockSpec((tm, PROJ_TN), lambda i, j: (i, j)),
        out_shape=jax.ShapeDtypeStruct((t_tokens, n), BF16),
        compiler_params=_cparams(2),
        name="proj_rope",
    )(x, w, cos_t, sin_t)


def _proj_plain(x, w, *, tm):
    t_tokens, k_dim = x.shape
    n = w.shape[1]
    return pl.pallas_call(
        functools.partial(_proj_plain_body, n_sub=max(tm // 256, 1)),
        grid=(t_tokens // tm, n // PROJ_TN),
        in_specs=[
            pl.BlockSpec((tm, k_dim), lambda i, j: (i, 0)),
            pl.BlockSpec((k_dim, PROJ_TN), lambda i, j: (0, j)),
        ],
        out_specs=pl.BlockSpec((tm, PROJ_TN), lambda i, j: (i, j)),
        out_shape=jax.ShapeDtypeStruct((t_tokens, n), BF16),
        compiler_params=_cparams(2),
        name="proj_plain",
    )(x, w)


def _f32_to_key(x):
    b = lax.bitcast_convert_type(x, jnp.int32)
    return jnp.where(b < 0, b ^ jnp.int32(0x7FFFFFFF), b)


def _key_to_f32(k):
    b = jnp.where(k < 0, k ^ jnp.int32(0x7FFFFFFF), k)
    return lax.bitcast_convert_type(b, F32)


def _fold8(x, op):
    return op(x.reshape(x.shape[0] // SUBLANES, SUBLANES, x.shape[1]), axis=0)


def _dsa_body(qaT_ref, iqT_ref, iwT_ref, ik_ref, ka_ref, vaT_ref, o_ref, sc_ref, acc_ref, sbuf_ref, *, tq, kc, topk):
    i = pl.program_id(1)
    q0 = i * tq
    nch = (q0 + tq + kc - 1) // kc
    qpos = q0 + lax.broadcasted_iota(jnp.int32, (1, tq), 1)
    iqT = iqT_ref[0]
    w_idx = iwT_ref[0].astype(F32)

    def score_chunk(c, carry):
        mn, mx = carry
        k0 = pl.multiple_of(c * kc, kc)
        ikc = ik_ref[0, pl.ds(k0, kc), :]
        acc = jnp.zeros((kc, tq), F32)
        for h in range(A_IDX_HEADS):
            d = jnp.dot(ikc, iqT[h * A_IDX_DIM:(h + 1) * A_IDX_DIM, :], preferred_element_type=F32)
            acc = acc + w_idx[h:h + 1, :] * jnp.maximum(d, 0.0)
        kpos = k0 + lax.broadcasted_iota(jnp.int32, (kc, 1), 0)
        causal = kpos <= qpos
        sc_ref[pl.ds(k0, kc), :] = jnp.where(causal, acc, -jnp.inf)
        mn = jnp.minimum(mn, _fold8(jnp.where(causal, acc, jnp.inf), jnp.min))
        mx = jnp.maximum(mx, _fold8(jnp.where(causal, acc, -jnp.inf), jnp.max))
        return mn, mx

    mn8, mx8 = lax.fori_loop(0, nch, score_chunk,
                             (jnp.full((SUBLANES, tq), jnp.inf, F32), jnp.full((SUBLANES, tq), -jnp.inf, F32)))
    row_min = jnp.min(mn8, axis=0, keepdims=True)
    row_max = jnp.max(mx8, axis=0, keepdims=True)

    pc = 2 * kc
    npair = (nch + 1) // 2

    @pl.when(nch % 2 == 1)
    def _():
        sc_ref[pl.ds(pl.multiple_of(nch * kc, kc), kc), :] = jnp.full((kc, tq), -jnp.inf, F32)

    def count_ge(thr):
        def body(j, cnt):
            s = sc_ref[pl.ds(pl.multiple_of(j * pc, pc), pc), :]
            return cnt + _fold8(jnp.where(s >= thr, 1.0, 0.0), jnp.sum)
        cnt8 = lax.fori_loop(0, npair, body, jnp.zeros((SUBLANES, tq), F32))
        return jnp.sum(cnt8, axis=0, keepdims=True)

    def min_ge(thr):
        def body(j, mn):
            s = sc_ref[pl.ds(pl.multiple_of(j * pc, pc), pc), :]
            return jnp.minimum(mn, _fold8(jnp.where(s >= thr, s, jnp.inf), jnp.min))
        mn8 = lax.fori_loop(0, npair, body, jnp.full((SUBLANES, tq), jnp.inf, F32))
        return jnp.min(mn8, axis=0, keepdims=True)

    def count_zero():
        def body(j, carry):
            c0, cp = carry
            s = sc_ref[pl.ds(pl.multiple_of(j * pc, pc), pc), :]
            return (c0 + _fold8(jnp.where(s >= 0.0, 1.0, 0.0), jnp.sum),
                    cp + _fold8(jnp.where(s > 0.0, 1.0, 0.0), jnp.sum))
        z8 = jnp.zeros((SUBLANES, tq), F32)
        c0, cp = lax.fori_loop(0, npair, body, (z8, z8))
        return jnp.sum(c0, axis=0, keepdims=True), jnp.sum(cp, axis=0, keepdims=True)

    kf = float(topk)
    n_valid = (qpos + 1).astype(F32)
    few = n_valid <= kf
    cnt_nonneg, cnt_pos = count_zero()
    is_zero = jnp.logical_and(cnt_pos < kf, cnt_nonneg >= kf)
    is_pos = cnt_pos >= kf
    lo0 = jnp.where(is_zero, 0, jnp.where(is_pos, MIN_NORMAL_KEY, _f32_to_key(row_min)))
    hi0 = jnp.where(is_zero, 1, jnp.where(is_pos, _f32_to_key(row_max) + 1, -1))
    cnt0 = jnp.where(is_zero, cnt_nonneg, jnp.where(is_pos, cnt_pos, n_valid))
    lo0 = jnp.where(few, _f32_to_key(row_min), lo0)
    hi0 = jnp.where(few, lo0 + 1, hi0)

    def unfinished(lo, hi, cnt_lo):
        return jnp.logical_and(lo + 1 != hi, cnt_lo != kf)

    def bis_cond(st):
        lo, hi, cnt_lo, it = st
        return jnp.logical_and(jnp.max(jnp.where(unfinished(lo, hi, cnt_lo), 1.0, 0.0)) > 0.0, it < MAX_SELECT_PASSES)

    def bisect(st):
        lo, hi, cnt_lo, it = st
        nd = unfinished(lo, hi, cnt_lo)
        mid_val = _f32_to_key(0.5 * _key_to_f32(lo) + 0.5 * _key_to_f32(hi))
        mid_key = lo + lax.shift_right_logical(hi - lo, 1)
        mid = jnp.where(it < VALUE_MID_PASSES, jnp.clip(mid_val, lo + 1, hi - 1), mid_key)
        cnt = count_ge(_key_to_f32(mid))
        ge = cnt >= kf
        up = jnp.logical_and(nd, ge)
        down = jnp.logical_and(nd, jnp.logical_not(ge))
        return jnp.where(up, mid, lo), jnp.where(down, mid, hi), jnp.where(up, cnt, cnt_lo), it + 1

    def peel(st):
        lo, hi, cnt_lo, it = st
        nd = unfinished(lo, hi, cnt_lo)
        k1 = jnp.maximum(_f32_to_key(min_ge(_key_to_f32(lo))), lo)
        cnt = count_ge(_key_to_f32(k1 + 1))
        ge = cnt >= kf
        up = jnp.logical_and(nd, ge)
        down = jnp.logical_and(nd, jnp.logical_not(ge))
        new_lo = jnp.where(nd, jnp.where(ge, k1 + 1, k1), lo)
        return new_lo, jnp.where(down, k1 + 1, hi), jnp.where(up, cnt, cnt_lo), it + 2

    def bis_body(st):
        lo, hi, cnt_lo, it = st
        excess = jnp.max(jnp.where(unfinished(lo, hi, cnt_lo), cnt_lo - kf, 0.0))
        return lax.cond(excess > PEEL_MAX, bisect, peel, st)

    lo, _, cnt_lo, _ = lax.while_loop(bis_cond, bis_body, (lo0, hi0, cnt0, jnp.int32(0)))
    thr = _key_to_f32(lo)

    tie = jnp.logical_and(cnt_lo > kf, jnp.logical_not(few))

    @pl.when(jnp.max(jnp.where(tie, 1.0, 0.0)) > 0.0)
    def _():
        def gt_body(c, cnt):
            k0 = pl.multiple_of(c * kc, kc)
            s = sc_ref[pl.ds(k0, kc), :]
            return cnt + _fold8(jnp.where(s > thr, 1.0, 0.0), jnp.sum)
        gt8 = lax.fori_loop(0, nch, gt_body, jnp.zeros((SUBLANES, tq), F32))
        need = float(topk) - jnp.sum(gt8, axis=0, keepdims=True)
        r = lax.broadcasted_iota(jnp.int32, (kc, kc), 0)
        cidx = lax.broadcasted_iota(jnp.int32, (kc, kc), 1)
        tri = jnp.where(cidx <= r, 1.0, 0.0).astype(BF16)

        def tie_body(c, seen):
            k0 = pl.multiple_of(c * kc, kc)
            s = sc_ref[pl.ds(k0, kc), :]
            eqf = jnp.where(s == thr, jnp.where(tie, 1.0, 0.0), 0.0)
            pre = jnp.dot(tri, eqf.astype(BF16), preferred_element_type=F32) + seen
            drop = eqf * jnp.where(pre > need, 1.0, 0.0)
            sc_ref[pl.ds(k0, kc), :] = jnp.where(drop > 0.0, -jnp.inf, s)
            return seen + jnp.sum(eqf, axis=0, keepdims=True)

        lax.fori_loop(0, nch, tie_body, jnp.zeros((1, tq), F32))

    qT = jnp.concatenate([(qaT_ref[0, h * HEAD_DIM:(h + 1) * HEAD_DIM, :] * ATTN_SCALE).astype(BF16)
                          for h in range(A_HEADS)], axis=1)
    acc_ref[...] = jnp.zeros_like(acc_ref)
    ones_rows = jnp.ones((SUBLANES, kc), BF16)

    def issue_scores(c, slot):
        k0 = pl.multiple_of(jnp.minimum(c, nch - 1) * kc, kc)
        sbuf_ref[slot] = jnp.dot(ka_ref[0, pl.ds(k0, kc), :], qT, preferred_element_type=F32)

    def softmax_pv(c, slot, m):
        k0 = pl.multiple_of(c * kc, kc)
        v1 = jnp.concatenate([vaT_ref[0, c], ones_rows], axis=0)
        sel = sc_ref[pl.ds(k0, kc), :] >= thr
        s = jnp.concatenate([jnp.where(sel, sbuf_ref[slot, :, h * tq:(h + 1) * tq], NEG_BIG)
                             for h in range(A_HEADS)], axis=1)
        m_new = jnp.maximum(m, jnp.max(s, axis=0, keepdims=True))
        alpha = jnp.exp(m - m_new)
        p = jnp.exp(s - m_new).astype(BF16)
        acc_ref[...] = acc_ref[...] * alpha + jnp.dot(v1, p, preferred_element_type=F32)
        return m_new

    def chunk_pair(j, m):
        c0 = 2 * j
        issue_scores(c0 + 1, 1)
        m = softmax_pv(c0, 0, m)

        def odd_chunk(m):
            issue_scores(c0 + 2, 0)
            return softmax_pv(c0 + 1, 1, m)

        return lax.cond(c0 + 1 < nch, odd_chunk, lambda m: m, m)

    issue_scores(0, 0)
    lax.fori_loop(0, (nch + 1) // 2, chunk_pair, jnp.full((1, A_HEADS * tq), NEG_BIG, F32))
    a = acc_ref[...]
    o = a[:HEAD_DIM] / a[HEAD_DIM:HEAD_DIM + 1]
    for h in range(A_HEADS):
        o_ref[0, h * HEAD_DIM:(h + 1) * HEAD_DIM, :] = o[:, h * tq:(h + 1) * tq].astype(o_ref.dtype)


def _dsa(qaT, iqT, iwT, ik, ka, vaT4, *, tq, kc, topk):
    bsz, _, seq = qaT.shape
    return pl.pallas_call(
        functools.partial(_dsa_body, tq=tq, kc=kc, topk=topk),
        grid=(bsz, seq // tq),
        in_specs=[
            pl.BlockSpec((1, A_W, tq), lambda b, i: (b, 0, i)),
            pl.BlockSpec((1, A_IDX_HEADS * A_IDX_DIM, tq), lambda b, i: (b, 0, i)),
            pl.BlockSpec((1, A_IDX_HEADS, tq), lambda b, i: (b, 0, i)),
            pl.BlockSpec((1, seq, A_IDX_DIM), lambda b, i: (b, 0, 0)),
            pl.BlockSpec((1, seq, HEAD_DIM), lambda b, i: (b, 0, 0)),
            pl.BlockSpec((1, seq // kc, HEAD_DIM, kc), lambda b, i: (b, 0, 0, 0)),
        ],
        out_specs=pl.BlockSpec((1, A_W, tq), lambda b, i: (b, 0, i)),
        out_shape=jax.ShapeDtypeStruct((bsz, A_W, seq), BF16),
        scratch_shapes=[pltpu.VMEM((seq + kc, tq), F32),
                        pltpu.VMEM((HEAD_DIM + SUBLANES, A_HEADS * tq), F32),
                        pltpu.VMEM((2, kc, A_HEADS * tq), F32)],
        compiler_params=_cparams(2),
        name="dsa_attention",
    )(qaT, iqT, iwT, ik, ka, vaT4)


def _kmean_body(k_ref, o_ref):
    o_ref[0] = jnp.mean(k_ref[...].astype(F32), axis=0, keepdims=True)


def _moba_kmean(h_rope, *, n_tokens):
    nblk = n_tokens // B_BLOCK
    return pl.pallas_call(
        _kmean_body,
        grid=(nblk,),
        in_specs=[pl.BlockSpec((B_BLOCK, B_W), lambda i: (i, TILE_BK))],
        out_specs=pl.BlockSpec((1, 1, B_W), lambda i: (i, 0, 0)),
        out_shape=jax.ShapeDtypeStruct((nblk, 1, B_W), F32),
        compiler_params=_cparams(1),
        name="moba_kmean",
    )(h_rope)


def _moba_gate_body(qT_ref, km_ref, sel_ref, *, tq, nkb, nsel):
    i = pl.program_id(1)
    qpos = i * tq + lax.broadcasted_iota(jnp.int32, (1, tq), 1)
    cur = qpos // B_BLOCK
    row = lax.broadcasted_iota(jnp.int32, (nkb, tq), 0)
    past = row < cur
    for h in range(B_HEADS):
        g = jnp.dot(km_ref[0, h], qT_ref[0, h * HEAD_DIM:(h + 1) * HEAD_DIM, :], preferred_element_type=F32)
        g = jnp.where(past, g, -jnp.inf)
        sel = jnp.zeros((nkb, tq), F32)
        for _ in range(nsel):
            mx = jnp.max(g, axis=0, keepdims=True)
            idx = jnp.min(jnp.where(g == mx, row, nkb), axis=0, keepdims=True)
            pick = row == idx
            sel = jnp.where(pick, 1.0, sel)
            g = jnp.where(pick, -jnp.inf, g)
        sel_ref[0, h] = jnp.where(past, sel, 0.0)


def _moba_gate(bqT, kmean, *, tq):
    bsz, _, seq = bqT.shape
    nkb = seq // B_BLOCK
    nsel = min(B_TOPK_BLOCKS, nkb)
    return pl.pallas_call(
        functools.partial(_moba_gate_body, tq=tq, nkb=nkb, nsel=nsel),
        grid=(bsz, seq // tq),
        in_specs=[
            pl.BlockSpec((1, B_W, tq), lambda b, i: (b, 0, i)),
            pl.BlockSpec((1, B_HEADS, nkb, HEAD_DIM), lambda b, i: (b, 0, 0, 0)),
        ],
        out_specs=pl.BlockSpec((1, B_HEADS, nkb, tq), lambda b, i: (b, 0, 0, i)),
        out_shape=jax.ShapeDtypeStruct((bsz, B_HEADS, nkb, seq), F32),
        compiler_params=_cparams(2),
        name="moba_gate",
    )(bqT, kmean)


def _moba_body(qT_ref, k_ref, vT_ref, sel_ref, o_ref, acc_ref, sbuf_ref, *, tq):
    i = pl.program_id(1)
    qT = [(qT_ref[0, h * HEAD_DIM:(h + 1) * HEAD_DIM, :] * ATTN_SCALE).astype(BF16) for h in range(B_HEADS)]
    ones_rows = jnp.ones((SUBLANES, B_BLOCK), BF16)
    n_past = i

    def values(h, j):
        return jnp.concatenate([vT_ref[0, h, j], ones_rows], axis=0)

    def issue_scores(j, slot):
        k0 = pl.multiple_of(jnp.minimum(j, jnp.maximum(n_past - 1, 0)) * B_BLOCK, B_BLOCK)
        for h in range(B_HEADS):
            sbuf_ref[slot, h] = jnp.dot(k_ref[0, h, pl.ds(k0, B_BLOCK), :], qT[h], preferred_element_type=F32)

    def softmax_pv(j, slot, ms):
        ps, alphas, new_ms = [], [], []
        for h in range(B_HEADS):
            s = jnp.where(sel_ref[0, h, pl.ds(j, 1), :] > 0.0, sbuf_ref[slot, h], NEG_BIG)
            m_new = jnp.maximum(ms[h], jnp.max(s, axis=0, keepdims=True))
            alphas.append(jnp.exp(ms[h] - m_new))
            ps.append(jnp.exp(s - m_new).astype(BF16))
            new_ms.append(m_new)
        for h in range(B_HEADS):
            acc_ref[h] = acc_ref[h] * alphas[h] + jnp.dot(values(h, j), ps[h], preferred_element_type=F32)
        return tuple(new_ms)

    own0 = pl.multiple_of(i * B_BLOCK, B_BLOCK)
    s_own = [jnp.dot(k_ref[0, h, pl.ds(own0, B_BLOCK), :], qT[h], preferred_element_type=F32) for h in range(B_HEADS)]
    issue_scores(0, 0)
    causal = lax.broadcasted_iota(jnp.int32, (B_BLOCK, tq), 0) <= lax.broadcasted_iota(jnp.int32, (B_BLOCK, tq), 1)
    ms = []
    for h in range(B_HEADS):
        s = jnp.where(causal, s_own[h], NEG_BIG)
        m0 = jnp.max(s, axis=0, keepdims=True)
        acc_ref[h] = jnp.dot(values(h, i), jnp.exp(s - m0).astype(BF16), preferred_element_type=F32)
        ms.append(m0)

    def block_pair(jj, ms):
        j0 = 2 * jj
        issue_scores(j0 + 1, 1)
        ms = softmax_pv(j0, 0, ms)

        def odd_block(ms):
            issue_scores(j0 + 2, 0)
            return softmax_pv(j0 + 1, 1, ms)

        return lax.cond(j0 + 1 < n_past, odd_block, lambda ms: ms, ms)

    lax.fori_loop(0, (n_past + 1) // 2, block_pair, tuple(ms))
    for h in range(B_HEADS):
        a = acc_ref[h]
        o_ref[0, h * HEAD_DIM:(h + 1) * HEAD_DIM, :] = (a[:HEAD_DIM] / a[HEAD_DIM:HEAD_DIM + 1]).astype(o_ref.dtype)


def _moba(bqT, bk, bvT5, sel):
    bsz, _, seq = bqT.shape
    tq = B_BLOCK
    nkb = seq // B_BLOCK
    return pl.pallas_call(
        functools.partial(_moba_body, tq=tq),
        grid=(bsz, seq // tq),
        in_specs=[
            pl.BlockSpec((1, B_W, tq), lambda b, i: (b, 0, i)),
            pl.BlockSpec((1, B_HEADS, seq, HEAD_DIM), lambda b, i: (b, 0, 0, 0)),
            pl.BlockSpec((1, B_HEADS, nkb, HEAD_DIM, B_BLOCK), lambda b, i: (b, 0, 0, 0, 0)),
            pl.BlockSpec((1, B_HEADS, nkb, tq), lambda b, i: (b, 0, 0, i)),
        ],
        out_specs=pl.BlockSpec((1, B_W, tq), lambda b, i: (b, 0, i)),
        out_shape=jax.ShapeDtypeStruct((bsz, B_W, seq), BF16),
        scratch_shapes=[pltpu.VMEM((B_HEADS, HEAD_DIM + SUBLANES, tq), F32),
                        pltpu.VMEM((2, B_HEADS, B_BLOCK, tq), F32)],
        compiler_params=_cparams(2),
        name="moba_attention",
    )(bqT, bk, bvT5, sel)


def _swa_body(sink_ref, qT_ref, k0_ref, k1_ref, k2_ref, v0_ref, v1_ref, v2_ref, o_ref, *, tq):
    i = pl.program_id(1)
    q0 = i * tq
    qpos = q0 + lax.broadcasted_iota(jnp.int32, (1, tq), 1)
    nk = tq + C_WINDOW
    kpos = q0 - C_WINDOW + lax.broadcasted_iota(jnp.int32, (nk, 1), 0)
    diff = qpos - kpos
    ok = jnp.where(diff >= 0, jnp.where(diff < C_WINDOW, jnp.where(kpos >= 0, 1.0, 0.0), 0.0), 0.0) > 0.0
    ones_rows = jnp.ones((SUBLANES, nk), BF16)
    group = C_HEADS // C_KV_HEADS
    for kv in range(C_KV_HEADS):
        kk = jnp.concatenate([k0_ref[0, kv], k1_ref[0, kv], k2_ref[0, kv]], axis=0)
        vv = jnp.concatenate([v0_ref[0, kv], v1_ref[0, kv], v2_ref[0, kv]], axis=1)
        v1 = jnp.concatenate([vv, ones_rows], axis=0)
        heads = [kv * group + g for g in range(group)]
        qT = jnp.concatenate([(qT_ref[0, hd * HEAD_DIM:(hd + 1) * HEAD_DIM, :] * ATTN_SCALE).astype(BF16)
                              for hd in heads], axis=1)
        sink = jnp.concatenate([jnp.full((1, tq), sink_ref[hd], F32) for hd in heads], axis=1)
        s = jnp.dot(kk, qT, preferred_element_type=F32)
        s = jnp.concatenate([jnp.where(ok, s[:, g * tq:(g + 1) * tq], NEG_BIG) for g in range(group)], axis=1)
        m = jnp.maximum(jnp.max(s, axis=0, keepdims=True), sink)
        p = jnp.exp(s - m).astype(BF16)
        a = jnp.dot(v1, p, preferred_element_type=F32)
        o = a[:HEAD_DIM] / (a[HEAD_DIM:HEAD_DIM + 1] + jnp.exp(sink - m))
        for g, hd in enumerate(heads):
            o_ref[0, hd * HEAD_DIM:(hd + 1) * HEAD_DIM, :] = o[:, g * tq:(g + 1) * tq].astype(o_ref.dtype)


def _swa(sinks, cqT, ck, cvT, *, tq):
    bsz, _, seq = cqT.shape
    r = tq // C_WINDOW
    assert r == 2

    def kspec(off):
        return pl.BlockSpec((1, C_KV_HEADS, C_WINDOW, HEAD_DIM),
                            lambda b, i, s: (b, 0, jnp.maximum(i * r + off, 0), 0))

    def vspec(off):
        return pl.BlockSpec((1, C_KV_HEADS, HEAD_DIM, C_WINDOW),
                            lambda b, i, s: (b, 0, 0, jnp.maximum(i * r + off, 0)))

    grid_spec = pltpu.PrefetchScalarGridSpec(
        num_scalar_prefetch=1,
        grid=(bsz, seq // tq),
        in_specs=[pl.BlockSpec((1, C_W, tq), lambda b, i, s: (b, 0, i)),
                  kspec(-1), kspec(0), kspec(1), vspec(-1), vspec(0), vspec(1)],
        out_specs=pl.BlockSpec((1, C_W, tq), lambda b, i, s: (b, 0, i)),
    )
    return pl.pallas_call(
        functools.partial(_swa_body, tq=tq),
        grid_spec=grid_spec,
        out_shape=jax.ShapeDtypeStruct((bsz, C_W, seq), BF16),
        compiler_params=_cparams(2),
        name="swa_attention",
    )(sinks, cqT, ck, ck, ck, cvT, cvT, cvT)


def _layer_norm(z, g, b):
    mu = jnp.mean(z, axis=-1, keepdims=True)
    zc = z - mu
    var = jnp.mean(zc * zc, axis=-1, keepdims=True)
    return zc * lax.rsqrt(var + LN_EPS) * g + b


def _split_bf16(a):
    hi = a.astype(BF16)
    lo = (a - hi.astype(F32)).astype(BF16)
    return hi, lo


def _router(x1, rw_ref, rb_ref, cls_ref, wt_ref):
    nt = (((1,), (1,)), ((), ()))
    xh, xl = _split_bf16(x1)
    wh, wl = _split_bf16(rw_ref[...])
    logits = (lax.dot_general(wh, xh, nt, preferred_element_type=F32)
              + lax.dot_general(wh, xl, nt, preferred_element_type=F32)
              + lax.dot_general(wl, xh, nt, preferred_element_type=F32))
    aff = jax.nn.sigmoid(logits)
    score = aff + rb_ref[...]
    tm = x1.shape[0]
    sc = [score[e:e + 1, :] for e in range(N_EXPERTS)]
    af = [aff[e:e + 1, :] for e in range(N_EXPERTS)]
    in_top = []
    grp_score = []
    for gq in range(N_GROUPS):
        gs = jnp.zeros((1, tm), F32)
        for a in range(EXPERTS_PER_GROUP):
            ea = gq * EXPERTS_PER_GROUP + a
            rank = jnp.zeros((1, tm), F32)
            for b in range(EXPERTS_PER_GROUP):
                if b == a:
                    continue
                eb = gq * EXPERTS_PER_GROUP + b
                beats = (sc[eb] >= sc[ea]) if b < a else (sc[eb] > sc[ea])
                rank = rank + jnp.where(beats, 1.0, 0.0)
            top = rank < 2.0
            in_top.append(top)
            gs = gs + jnp.where(top, sc[ea], 0.0)
        grp_score.append(gs)
    best = grp_score[0]
    gstar = jnp.zeros((1, tm), jnp.int32)
    for gq in range(1, N_GROUPS):
        better = grp_score[gq] > best
        best = jnp.where(better, grp_score[gq], best)
        gstar = jnp.where(better, gq, gstar)
    cls = jnp.zeros((1, tm), jnp.int32)
    w_lo = jnp.zeros((1, tm), F32)
    w_hi = jnp.zeros((1, tm), F32)
    for gq in range(N_GROUPS):
        is_g = gstar == gq
        for pi, (a, b) in enumerate(PAIRS):
            ea, eb = gq * EXPERTS_PER_GROUP + a, gq * EXPERTS_PER_GROUP + b
            hit = jnp.where(is_g, jnp.where(in_top[ea], jnp.where(in_top[eb], 1.0, 0.0), 0.0), 0.0) > 0.0
            cls = jnp.where(hit, gq * len(PAIRS) + pi, cls)
            tot = af[ea] + af[eb]
            w_lo = jnp.where(hit, af[ea] / tot, w_lo)
            w_hi = jnp.where(hit, af[eb] / tot, w_hi)
    cls_ref[...] = cls
    wt_ref[0:1, :] = w_lo
    wt_ref[1:2, :] = w_hi


def _merge_body(oa_ref, ob_ref, oc_ref, ga_ref, gb_ref, gc_ref, x_ref, wa_ref, wb_ref, wc_ref, wo_ref,
                g_ref, b_ref, rw_ref, rb_ref, x1_ref, cls_ref, wt_ref):
    def branch(o_ref, w_ref, gate_ref):
        y = jnp.dot(o_ref[...], w_ref[...], preferred_element_type=F32)
        return jax.nn.sigmoid(gate_ref[...].astype(F32)) * y

    merged = branch(oa_ref, wa_ref, ga_ref) + branch(ob_ref, wb_ref, gb_ref) + branch(oc_ref, wc_ref, gc_ref)
    y = jnp.dot(merged.astype(BF16), wo_ref[...], preferred_element_type=F32)
    x1 = _layer_norm(DN_ALPHA * x_ref[...] + y, g_ref[...], b_ref[...])
    x1_ref[...] = x1
    _router(x1, rw_ref, rb_ref, cls_ref, wt_ref)


def _merge(oa, ob, oc, h_plain, x, wa, wb, wc, wo, ln_g, ln_b, rwT, rb, *, tm):
    t_tokens = x.shape[0]
    gate_blk = COL_GATES // D_MODEL
    full = lambda shape: pl.BlockSpec(shape, lambda i: (0,) * len(shape))
    return pl.pallas_call(
        _merge_body,
        grid=(t_tokens // tm,),
        in_specs=[
            pl.BlockSpec((tm, A_W), lambda i: (i, 0)),
            pl.BlockSpec((tm, B_W), lambda i: (i, 0)),
            pl.BlockSpec((tm, C_W), lambda i: (i, 0)),
            pl.BlockSpec((tm, D_MODEL), lambda i: (i, gate_blk)),
            pl.BlockSpec((tm, D_MODEL), lambda i: (i, gate_blk + 1)),
            pl.BlockSpec((tm, D_MODEL), lambda i: (i, gate_blk + 2)),
            pl.BlockSpec((tm, D_MODEL), lambda i: (i, 0)),
            full((A_W, D_MODEL)), full((B_W, D_MODEL)), full((C_W, D_MODEL)), full((D_MODEL, D_MODEL)),
            full((1, D_MODEL)), full((1, D_MODEL)), full((N_EXPERTS, D_MODEL)), full((N_EXPERTS, 1)),
        ],
        out_specs=[
            pl.BlockSpec((tm, D_MODEL), lambda i: (i, 0)),
            pl.BlockSpec((1, tm), lambda i: (0, i)),
            pl.BlockSpec((2, tm), lambda i: (0, i)),
        ],
        out_shape=[
            jax.ShapeDtypeStruct((t_tokens, D_MODEL), F32),
            jax.ShapeDtypeStruct((1, t_tokens), jnp.int32),
            jax.ShapeDtypeStruct((2, t_tokens), F32),
        ],
        compiler_params=_cparams(1),
        name="merge_ln_router",
    )(oa, ob, oc, h_plain, h_plain, h_plain, x, wa, wb, wc, wo, ln_g, ln_b, rwT, rb)


def _gather_start(idx_ref, src_hbm, dst_ref, sem, n_rows):
    def start(r, _):
        pltpu.make_async_copy(src_hbm.at[pl.ds(idx_ref[0, 0, r], 1)], dst_ref.at[pl.ds(r, 1)], sem).start()
        return 0

    lax.fori_loop(0, n_rows, start, 0, unroll=8)


def _gather_wait(src_hbm, dst_ref, sem, n_rows):
    pltpu.make_async_copy(src_hbm.at[pl.ds(0, n_rows)], dst_ref, sem).wait()


def _experts_body(e1_ref, e2_ref, nused_ref, src_ref, src_next_ref, x_hbm, wt_ref,
                  wg1_ref, wu1_ref, wd1_ref, wg2_ref, wu2_ref, wd2_ref, o_ref, xbuf, sem):
    s = pl.program_id(0)
    nused = nused_ref[0]
    slot = lax.rem(s, 2)

    @pl.when(s == 0)
    def _():
        _gather_start(src_ref, x_hbm, xbuf.at[0], sem.at[0], MOE_TILE)

    @pl.when(s + 1 < nused)
    def _():
        _gather_start(src_next_ref, x_hbm, xbuf.at[1 - slot], sem.at[1 - slot], MOE_TILE)

    @pl.when(s < nused)
    def _():
        _gather_wait(x_hbm, xbuf.at[slot], sem.at[slot], MOE_TILE)
        xb = xbuf[slot].astype(BF16)
        wt = wt_ref[...]

        def expert(wg_ref, wu_ref, wd_ref):
            g = jnp.dot(xb, wg_ref[0], preferred_element_type=F32)
            u = jnp.dot(xb, wu_ref[0], preferred_element_type=F32)
            he = (g * jax.nn.sigmoid(g) * u).astype(BF16)
            return jnp.dot(he, wd_ref[0], preferred_element_type=F32)

        o_ref[...] = wt[:, 0:1] * expert(wg1_ref, wu1_ref, wd1_ref) + wt[:, 1:2] * expert(wg2_ref, wu2_ref, wd2_ref)

    @pl.when(s >= nused)
    def _():
        o_ref[...] = jnp.zeros_like(o_ref)


def _experts(tile_e1, tile_e2, nused, src3, x1, wsort, wg, wu, wd):
    ntiles = src3.shape[0]
    wspec_up = lambda which: pl.BlockSpec((1, D_MODEL, D_EXPERT), lambda s, e1, e2, nu: ((e1, e2)[which][s], 0, 0))
    wspec_dn = lambda which: pl.BlockSpec((1, D_EXPERT, D_MODEL), lambda s, e1, e2, nu: ((e1, e2)[which][s], 0, 0))
    grid_spec = pltpu.PrefetchScalarGridSpec(
        num_scalar_prefetch=3,
        grid=(ntiles,),
        in_specs=[
            pl.BlockSpec((1, 1, MOE_TILE), lambda s, e1, e2, nu: (s, 0, 0), memory_space=pltpu.SMEM),
            pl.BlockSpec((1, 1, MOE_TILE), lambda s, e1, e2, nu: (jnp.minimum(s + 1, ntiles - 1), 0, 0),
                         memory_space=pltpu.SMEM),
            pl.BlockSpec(memory_space=pl.ANY),
            pl.BlockSpec((MOE_TILE, 2), lambda s, e1, e2, nu: (s, 0)),
            wspec_up(0), wspec_up(0), wspec_dn(0), wspec_up(1), wspec_up(1), wspec_dn(1),
        ],
        out_specs=pl.BlockSpec((MOE_TILE, D_MODEL), lambda s, e1, e2, nu: (s, 0)),
        scratch_shapes=[pltpu.VMEM((2, MOE_TILE, D_MODEL), F32), pltpu.SemaphoreType.DMA((2,))],
    )
    return pl.pallas_call(
        _experts_body,
        grid_spec=grid_spec,
        out_shape=jax.ShapeDtypeStruct((ntiles * MOE_TILE, D_MODEL), F32),
        compiler_params=_cparams(1),
        name="moe_experts",
    )(tile_e1, tile_e2, nused, src3, src3, x1, wsort, wg, wu, wd, wg, wu, wd)


def _final_body(pos_ref, pos_next_ref, y_hbm, x_ref, g_ref, b_ref, o_ref, ob_ref, ybuf, sem, *, tm):
    s = pl.program_id(0)
    slot = lax.rem(s, 2)

    @pl.when(s == 0)
    def _():
        _gather_start(pos_ref, y_hbm, ybuf.at[0], sem.at[0], tm)

    @pl.when(s + 1 < pl.num_programs(0))
    def _():
        _gather_start(pos_next_ref, y_hbm, ybuf.at[1 - slot], sem.at[1 - slot], tm)

    _gather_wait(y_hbm, ybuf.at[slot], sem.at[slot], tm)
    x2 = _layer_norm(DN_ALPHA * x_ref[...] + ybuf[slot], g_ref[...], b_ref[...])
    o_ref[...] = x2
    ob_ref[...] = x2.astype(BF16)


def _final(pos3, y_sorted, x1, ln_g, ln_b, *, tm):
    t_tokens = x1.shape[0]
    nsteps = t_tokens // tm
    return pl.pallas_call(
        functools.partial(_final_body, tm=tm),
        grid=(nsteps,),
        in_specs=[
            pl.BlockSpec((1, 1, tm), lambda i: (i, 0, 0), memory_space=pltpu.SMEM),
            pl.BlockSpec((1, 1, tm), lambda i: (jnp.minimum(i + 1, nsteps - 1), 0, 0), memory_space=pltpu.SMEM),
            pl.BlockSpec(memory_space=pl.ANY),
            pl.BlockSpec((tm, D_MODEL), lambda i: (i, 0)),
            pl.BlockSpec((1, D_MODEL), lambda i: (0, 0)),
            pl.BlockSpec((1, D_MODEL), lambda i: (0, 0)),
        ],
        out_specs=[pl.BlockSpec((tm, D_MODEL), lambda i: (i, 0)), pl.BlockSpec((tm, D_MODEL), lambda i: (i, 0))],
        out_shape=[jax.ShapeDtypeStruct((t_tokens, D_MODEL), F32), jax.ShapeDtypeStruct((t_tokens, D_MODEL), BF16)],
        scratch_shapes=[pltpu.VMEM((2, tm, D_MODEL), F32), pltpu.SemaphoreType.DMA((2,))],
        compiler_params=_cparams(1),
        name="moe_combine_ln",
    )(pos3, pos3, y_sorted, x1, ln_g, ln_b)


def _rope_tables(seq):
    inv = 1.0 / (ROPE_THETA ** (jnp.arange(0, HEAD_DIM, 2, dtype=F32) / HEAD_DIM))
    ang = jnp.arange(seq, dtype=F32)[:, None] * inv[None, :]
    return jnp.cos(ang), jnp.sin(ang)


def _halves_layout(w_heads):
    d, n = w_heads.shape
    t = w_heads.reshape(d, n // PROJ_TN, HEADS_PER_TILE, 2, HEAD_DIM // 2)
    return t.transpose(0, 1, 3, 2, 4).reshape(d, n)


def _heads_layout(h_tiles):
    t, n = h_tiles.shape
    v = h_tiles.reshape(t, n // PROJ_TN, 2, HEADS_PER_TILE, HEAD_DIM // 2)
    return v.transpose(0, 1, 3, 2, 4).reshape(t, n)


def _reorder_w_in(w):
    pts = np.cumsum((0,) + IN_SIZES)
    sec = [w[:, pts[k]:pts[k + 1]] for k in range(len(IN_SIZES))]
    a_q, a_c, a_iq, a_ik, a_iw, b_q, b_k, b_v, c_q, c_k, c_v, gates = sec
    zeros = lambda n: jnp.zeros((w.shape[0], n), w.dtype)
    w_rope = _halves_layout(jnp.concatenate([a_q, a_iq, b_q, b_k, c_q, c_k, a_ik, zeros(HEAD_DIM)], axis=1))
    w_plain = jnp.concatenate([gates, b_v, a_c, a_iw, zeros(LANES - A_IDX_HEADS), c_v, zeros(LANES)], axis=1)
    assert w_rope.shape[1] == N_ROPE and w_plain.shape[1] == N_PLAIN
    return w_rope.astype(BF16), w_plain.astype(BF16)


def _kv_weights(w_uk, w_uv):
    half = HEAD_DIM // 2
    zeros = lambda n: jnp.zeros((w_uk.shape[0], n), w_uk.dtype)
    return jnp.concatenate([w_uk[:, :half], w_uv, zeros(HALF_TN - half - HEAD_DIM),
                            w_uk[:, half:], zeros(HALF_TN - half)], axis=1).astype(BF16)


def _moe_plan(cls, wts, n_tokens):
    ntiles = n_tokens // MOE_TILE + N_CLASSES
    onehot = (cls[:, None] == jnp.arange(N_CLASSES, dtype=jnp.int32)[None, :]).astype(jnp.int32)
    csum = jnp.cumsum(onehot, axis=0)
    rank = jnp.sum(onehot * csum, axis=1) - 1
    counts = csum[-1]
    ptiles = (counts + MOE_TILE - 1) // MOE_TILE
    tile_end = jnp.cumsum(ptiles)
    tile_start = tile_end - ptiles
    pos = (tile_start[cls] * MOE_TILE + rank).astype(jnp.int32)
    nused = tile_end[-1:].astype(jnp.int32)
    tile_ids = jnp.arange(ntiles, dtype=jnp.int32)
    tile_cls = jnp.minimum(jnp.sum((tile_ids[:, None] >= tile_end[None, :]).astype(jnp.int32), axis=1),
                           N_CLASSES - 1).astype(jnp.int32)
    pair = np.array(PAIRS, dtype=np.int32)
    e_lo = jnp.asarray(np.repeat(np.arange(N_GROUPS), len(PAIRS)) * EXPERTS_PER_GROUP + np.tile(pair[:, 0], N_GROUPS), jnp.int32)
    e_hi = jnp.asarray(np.repeat(np.arange(N_GROUPS), len(PAIRS)) * EXPERTS_PER_GROUP + np.tile(pair[:, 1], N_GROUPS), jnp.int32)
    src = jnp.zeros((ntiles * MOE_TILE,), jnp.int32).at[pos].set(jnp.arange(n_tokens, dtype=jnp.int32))
    wsort = jnp.zeros((ntiles * MOE_TILE, 2), F32).at[pos].set(wts.T)
    return e_lo[tile_cls], e_hi[tile_cls], nused, src.reshape(ntiles, 1, MOE_TILE), pos, wsort


def kernel(x, w_in, a_w_uk, a_w_uv, c_sinks, w_branch, w_o, ln1_g, ln1_b, router_w, router_b,
           moe_w_gate, moe_w_up, moe_w_down, ln2_g, ln2_b):
    bsz, seq, _ = x.shape
    n_tokens = bsz * seq
    topk = min(A_TOPK_MAX, seq // 4)
    nkb = seq // B_BLOCK
    tq = 256
    kc = 256
    tm_proj = min(1024, seq)
    tm_row = 512
    half = HEAD_DIM // 2

    cos, sin = _rope_tables(seq)
    cos_in = jnp.tile(cos, (1, HEADS_PER_TILE))
    sin_in = jnp.tile(sin, (1, HEADS_PER_TILE))
    cos_kv = jnp.concatenate([cos, jnp.ones((seq, HALF_TN - half), F32)], axis=1)
    sin_kv = jnp.concatenate([sin, jnp.zeros((seq, HALF_TN - half), F32)], axis=1)
    rwT = router_w.T
    rb = router_b.reshape(N_EXPERTS, 1)

    def to_T(t2d):
        return t2d.reshape(bsz, seq, t2d.shape[1]).transpose(0, 2, 1)

    xf = x.reshape(n_tokens, D_MODEL)
    xb = xf.astype(BF16)
    for l in range(DEPTH):
        w_rope, w_plain = _reorder_w_in(w_in[l])
        h_rope = _proj_rope(xb, 0, D_MODEL, w_rope, cos_in, sin_in, tm=tm_proj, seq=seq)
        h_plain = _proj_plain(xb, w_plain, tm=tm_proj)
        kv = _proj_rope(h_plain, COL_AC // A_KV_RANK, A_KV_RANK, _kv_weights(a_w_uk[l], a_w_uv[l]),
                        cos_kv, sin_kv, tm=tm_proj, seq=seq)

        def rope_heads(tile0, ntiles):
            return _heads_layout(h_rope[:, tile0 * PROJ_TN:(tile0 + ntiles) * PROJ_TN])

        ckik = rope_heads(TILE_CK_AIK, 1)

        qaT = to_T(rope_heads(TILE_AQ, 1))
        iqT = to_T(rope_heads(TILE_AIQ, 1))
        iwT = to_T(h_plain[:, COL_AIW:COL_AIW + A_IDX_HEADS])
        ik = ckik[:, 2 * HEAD_DIM:3 * HEAD_DIM].reshape(bsz, seq, A_IDX_DIM)
        ka = jnp.concatenate([kv[:, :half], kv[:, HALF_TN:HALF_TN + half]], axis=1).reshape(bsz, seq, HEAD_DIM)
        vaT4 = kv[:, KV_VA0:KV_VA0 + HEAD_DIM].reshape(bsz, seq // kc, kc, HEAD_DIM).transpose(0, 1, 3, 2)
        oaT = _dsa(qaT, iqT, iwT, ik, ka, vaT4, tq=tq, kc=kc, topk=topk)

        bqT = to_T(rope_heads(TILE_BQ, 1))
        kmean = _heads_layout(_moba_kmean(h_rope, n_tokens=n_tokens).reshape(n_tokens // B_BLOCK, B_W))
        kmean = kmean.reshape(bsz, nkb, B_HEADS, HEAD_DIM).transpose(0, 2, 1, 3).astype(BF16)
        sel = _moba_gate(bqT, kmean, tq=tq)
        bk = rope_heads(TILE_BK, 1).reshape(bsz, seq, B_HEADS, HEAD_DIM).transpose(0, 2, 1, 3)
        bvT5 = (h_plain[:, COL_BV:COL_BV + B_W].reshape(bsz, nkb, B_BLOCK, B_HEADS, HEAD_DIM)
                .transpose(0, 3, 1, 4, 2))
        obT = _moba(bqT, bk, bvT5, sel)

        cqT = to_T(rope_heads(TILE_CQ, 2))
        ck = ckik[:, :C_KV_HEADS * HEAD_DIM].reshape(bsz, seq, C_KV_HEADS, HEAD_DIM).transpose(0, 2, 1, 3)
        cvT = (h_plain[:, COL_CV:COL_CV + C_KV_HEADS * HEAD_DIM].reshape(bsz, seq, C_KV_HEADS, HEAD_DIM)
               .transpose(0, 2, 3, 1))
        ocT = _swa(c_sinks[l], cqT, ck, cvT, tq=tq)

        rows = lambda t: t.transpose(0, 2, 1).reshape(n_tokens, t.shape[1])
        wb_all = w_branch[l].astype(BF16)
        x1, cls, wts = _merge(rows(oaT), rows(obT), rows(ocT), h_plain, xf,
                              wb_all[:A_W], wb_all[A_W:A_W + B_W], wb_all[A_W + B_W:], w_o[l].astype(BF16),
                              ln1_g[l].reshape(1, D_MODEL), ln1_b[l].reshape(1, D_MODEL), rwT, rb, tm=tm_row)

        e1, e2, nused, src3, pos, wsort = _moe_plan(cls[0], wts, n_tokens)
        y_sorted = _experts(e1, e2, nused, src3, x1, wsort,
                            moe_w_gate[l].astype(BF16), moe_w_up[l].astype(BF16), moe_w_down[l].astype(BF16))
        xf, xb = _final(pos.reshape(n_tokens // tm_row, 1, tm_row), y_sorted, x1,
                        ln2_g[l].reshape(1, D_MODEL), ln2_b[l].reshape(1, D_MODEL), tm=tm_row)
    return xf.reshape(bsz, seq, D_MODEL)
```

```python
import functools
import math

import jax
import jax.numpy as jnp
import numpy as np
from jax import lax
from jax.experimental import pallas as pl
from jax.experimental.pallas import tpu as pltpu

D_MODEL = 1024
DEPTH = 2
HEAD_DIM = 64
ROPE_THETA = 10000.0
LN_EPS = 1e-5
A_HEADS = 4
A_KV_RANK = 128
A_IDX_HEADS = 4
A_IDX_DIM = 64
A_TOPK_MAX = 256
B_HEADS = 4
B_BLOCK = 256
B_TOPK_BLOCKS = 3
C_HEADS = 8
C_KV_HEADS = 2
C_WINDOW = 128
N_BRANCH = 3
A_W = A_HEADS * HEAD_DIM
B_W = B_HEADS * HEAD_DIM
C_W = C_HEADS * HEAD_DIM
IN_SIZES = (A_W, A_KV_RANK, A_IDX_HEADS * A_IDX_DIM, A_IDX_DIM, A_IDX_HEADS,
            B_W, B_W, B_W, C_W, C_KV_HEADS * HEAD_DIM, C_KV_HEADS * HEAD_DIM,
            N_BRANCH * D_MODEL)
N_EXPERTS = 16
N_GROUPS = 4
EXPERTS_PER_GROUP = 4
D_EXPERT = 512
DN_ALPHA = (2 * DEPTH) ** 0.25
ATTN_SCALE = HEAD_DIM ** -0.5

LANES = 128
SUBLANES = 8
VMEM_LIMIT_BYTES = 56 * 1024 * 1024

NEG_BIG = -1e30
BF16 = jnp.bfloat16
F32 = jnp.float32

PROJ_TN = 256
HALF_TN = PROJ_TN // 2
HEADS_PER_TILE = PROJ_TN // HEAD_DIM
TILE_AQ = 0
TILE_AIQ = 1
TILE_BQ = 2
TILE_BK = 3
TILE_CQ = 4
TILE_CK_AIK = 6
N_ROPE = 7 * PROJ_TN
COL_GATES = 0
COL_BV = 3072
COL_AC = 3328
COL_AIW = 3456
COL_CV = 3584
N_PLAIN = 3840
KV_VA0 = HEAD_DIM // 2

PAIRS = ((0, 1), (0, 2), (0, 3), (1, 2), (1, 3), (2, 3))
N_CLASSES = N_GROUPS * len(PAIRS)
MOE_TILE = 256

VALUE_MID_PASSES = 16
PEEL_MAX = 2.0
MIN_NORMAL_KEY = 0x00800000
MAX_SELECT_PASSES = 96


def _cparams(n_axes):
    return pltpu.CompilerParams(dimension_semantics=("arbitrary",) * n_axes,
                                vmem_limit_bytes=VMEM_LIMIT_BYTES)


def _proj_rope_body(x_ref, w_ref, cos_ref, sin_ref, o_ref, *, n_sub):
    sub = x_ref.shape[0] // n_sub
    for mi in range(n_sub):
        rows = slice(mi * sub, (mi + 1) * sub)
        acc = jnp.dot(x_ref[rows, :], w_ref[...], preferred_element_type=F32)
        a1, a2 = acc[:, :HALF_TN], acc[:, HALF_TN:]
        c, sn = cos_ref[rows, :], sin_ref[rows, :]
        o_ref[rows, :HALF_TN] = (a1 * c - a2 * sn).astype(o_ref.dtype)
        o_ref[rows, HALF_TN:] = (a2 * c + a1 * sn).astype(o_ref.dtype)


def _proj_plain_body(x_ref, w_ref, o_ref, *, n_sub):
    sub = x_ref.shape[0] // n_sub
    for mi in range(n_sub):
        rows = slice(mi * sub, (mi + 1) * sub)
        o_ref[rows, :] = jnp.dot(x_ref[rows, :], w_ref[...], preferred_element_type=F32).astype(o_ref.dtype)


def _proj_rope(x, x_col_block, k_dim, w, cos_t, sin_t, *, tm, seq):
    t_tokens = x.shape[0]
    n = w.shape[1]
    pos_blocks = seq // tm
    return pl.pallas_call(
        functools.partial(_proj_rope_body, n_sub=max(tm // 256, 1)),
        grid=(t_tokens // tm, n // PROJ_TN),
        in_specs=[
            pl.BlockSpec((tm, k_dim), lambda i, j: (i, x_col_block)),
            pl.BlockSpec((k_dim, PROJ_TN), lambda i, j: (0, j)),
            pl.BlockSpec((tm, HALF_TN), lambda i, j: (i % pos_blocks, 0)),
            pl.BlockSpec((tm, HALF_TN), lambda i, j: (i % pos_blocks, 0)),
        ],
        out_specs=pl.BlockSpec((tm, PROJ_TN), lambda i, j: (i, j)),
        out_shape=jax.ShapeDtypeStruct((t_tokens, n), BF16),
        compiler_params=_cparams(2),
        name="proj_rope",
    )(x, w, cos_t, sin_t)


def _proj_plain(x, w, *, tm):
    t_tokens, k_dim = x.shape
    n = w.shape[1]
    return pl.pallas_call(
        functools.partial(_proj_plain_body, n_sub=max(tm // 256, 1)),
        grid=(t_tokens // tm, n // PROJ_TN),
        in_specs=[
            pl.BlockSpec((tm, k_dim), lambda i, j: (i, 0)),
            pl.BlockSpec((k_dim, PROJ_TN), lambda i, j: (0, j)),
        ],
        out_specs=pl.BlockSpec((tm, PROJ_TN), lambda i, j: (i, j)),
        out_shape=jax.ShapeDtypeStruct((t_tokens, n), BF16),
        compiler_params=_cparams(2),
        name="proj_plain",
    )(x, w)


def _f32_to_key(x):
    b = lax.bitcast_convert_type(x, jnp.int32)
    return jnp.where(b < 0, b ^ jnp.int32(0x7FFFFFFF), b)


def _key_to_f32(k):
    b = jnp.where(k < 0, k ^ jnp.int32(0x7FFFFFFF), k)
    return lax.bitcast_convert_type(b, F32)


def _fold8(x, op):
    return op(x.reshape(x.shape[0] // SUBLANES, SUBLANES, x.shape[1]), axis=0)


def _tile_T(x_ref):
    return x_ref[...].astype(F32).T


def _head_rows(xT, h):
    half = HEAD_DIM // 2
    return xT[h * half:(h + 1) * half], xT[HALF_TN + h * half:HALF_TN + (h + 1) * half]


def _place_head(first, second, slot):
    half = HEAD_DIM // 2
    w = first.shape[1]
    before, after = slot * half, HALF_TN - (slot + 1) * half
    pieces = []
    for part in (first, second):
        pieces += [jnp.zeros((before, w), part.dtype)] * (before > 0) + [part] + [jnp.zeros((after, w), part.dtype)] * (after > 0)
    return jnp.concatenate(pieces, axis=0)


def _store_rows(o_ref, oT_heads):
    o_ref[...] = jnp.concatenate(oT_heads, axis=0).T.astype(o_ref.dtype)


IK_SLOT = 2


def _dsa_body(q_ref, iq_ref, iw_ref, ik_ref, kv_ref, vaT_ref, o_ref, sc_ref, acc_ref, sbuf_ref, *, tq, kc, topk):
    i = pl.program_id(1)
    q0 = i * tq
    nch = (q0 + tq + kc - 1) // kc
    qpos = q0 + lax.broadcasted_iota(jnp.int32, (1, tq), 1)
    iqT = _tile_T(iq_ref)
    iq_ops = [_place_head(*[p.astype(BF16) for p in _head_rows(iqT, h)], IK_SLOT) for h in range(A_IDX_HEADS)]
    w_idx = iw_ref[...].astype(F32).T[:A_IDX_HEADS]

    def score_chunk(c, carry, diagonal):
        mn, mx = carry
        k0 = pl.multiple_of(c * kc, kc)
        ikc = ik_ref[pl.ds(k0, kc), :]
        acc = jnp.zeros((kc, tq), F32)
        for h in range(A_IDX_HEADS):
            d = jnp.dot(ikc, iq_ops[h], preferred_element_type=F32)
            acc = acc + w_idx[h:h + 1, :] * jnp.maximum(d, 0.0)
        if diagonal:
            causal = (k0 + lax.broadcasted_iota(jnp.int32, (kc, 1), 0)) <= qpos
            lo_part, hi_part = jnp.where(causal, acc, jnp.inf), jnp.where(causal, acc, -jnp.inf)
        else:
            lo_part = hi_part = acc
        sc_ref[pl.ds(k0, kc), :] = hi_part
        return jnp.minimum(mn, _fold8(lo_part, jnp.min)), jnp.maximum(mx, _fold8(hi_part, jnp.max))

    n_full = (q0 + 1) // kc
    carry = lax.fori_loop(0, n_full, functools.partial(score_chunk, diagonal=False),
                          (jnp.full((SUBLANES, tq), jnp.inf, F32), jnp.full((SUBLANES, tq), -jnp.inf, F32)))
    mn8, mx8 = lax.fori_loop(n_full, nch, functools.partial(score_chunk, diagonal=True), carry)
    row_min = jnp.min(mn8, axis=0, keepdims=True)
    row_max = jnp.max(mx8, axis=0, keepdims=True)

    pc = 2 * kc
    npair = (nch + 1) // 2

    @pl.when(nch % 2 == 1)
    def _():
        sc_ref[pl.ds(pl.multiple_of(nch * kc, kc), kc), :] = jnp.full((kc, tq), -jnp.inf, F32)

    def count_ge(thr):
        def body(j, cnt):
            s = sc_ref[pl.ds(pl.multiple_of(j * pc, pc), pc), :]
            return cnt + _fold8(jnp.where(s >= thr, 1.0, 0.0), jnp.sum)
        cnt8 = lax.fori_loop(0, npair, body, jnp.zeros((SUBLANES, tq), F32))
        return jnp.sum(cnt8, axis=0, keepdims=True)

    def min_ge(thr):
        def body(j, mn):
            s = sc_ref[pl.ds(pl.multiple_of(j * pc, pc), pc), :]
            return jnp.minimum(mn, _fold8(jnp.where(s >= thr, s, jnp.inf), jnp.min))
        mn8 = lax.fori_loop(0, npair, body, jnp.full((SUBLANES, tq), jnp.inf, F32))
        return jnp.min(mn8, axis=0, keepdims=True)

    def count_zero():
        def body(j, carry):
            c0, cp = carry
            s = sc_ref[pl.ds(pl.multiple_of(j * pc, pc), pc), :]
            return (c0 + _fold8(jnp.where(s >= 0.0, 1.0, 0.0), jnp.sum),
                    cp + _fold8(jnp.where(s > 0.0, 1.0, 0.0), jnp.sum))
        z8 = jnp.zeros((SUBLANES, tq), F32)
        c0, cp = lax.fori_loop(0, npair, body, (z8, z8))
        return jnp.sum(c0, axis=0, keepdims=True), jnp.sum(cp, axis=0, keepdims=True)

    kf = float(topk)
    n_valid = (qpos + 1).astype(F32)
    few = n_valid <= kf
    cnt_nonneg, cnt_pos = count_zero()
    is_zero = jnp.logical_and(cnt_pos < kf, cnt_nonneg >= kf)
    is_pos = cnt_pos >= kf
    lo0 = jnp.where(is_zero, 0, jnp.where(is_pos, MIN_NORMAL_KEY, _f32_to_key(row_min)))
    hi0 = jnp.where(is_zero, 1, jnp.where(is_pos, _f32_to_key(row_max) + 1, -1))
    cnt0 = jnp.where(is_zero, cnt_nonneg, jnp.where(is_pos, cnt_pos, n_valid))
    lo0 = jnp.where(few, _f32_to_key(row_min), lo0)
    hi0 = jnp.where(few, lo0 + 1, hi0)

    def unfinished(lo, hi, cnt_lo):
        return jnp.logical_and(lo + 1 != hi, cnt_lo != kf)

    def bis_cond(st):
        lo, hi, cnt_lo, it = st
        return jnp.logical_and(jnp.max(jnp.where(unfinished(lo, hi, cnt_lo), 1.0, 0.0)) > 0.0, it < MAX_SELECT_PASSES)

    def bisect(st):
        lo, hi, cnt_lo, it = st
        nd = unfinished(lo, hi, cnt_lo)
        mid_val = _f32_to_key(0.5 * _key_to_f32(lo) + 0.5 * _key_to_f32(hi))
        mid_key = lo + lax.shift_right_logical(hi - lo, 1)
        mid = jnp.where(it < VALUE_MID_PASSES, jnp.clip(mid_val, lo + 1, hi - 1), mid_key)
        cnt = count_ge(_key_to_f32(mid))
        ge = cnt >= kf
        up = jnp.logical_and(nd, ge)
        down = jnp.logical_and(nd, jnp.logical_not(ge))
        return jnp.where(up, mid, lo), jnp.where(down, mid, hi), jnp.where(up, cnt, cnt_lo), it + 1

    def peel(st):
        lo, hi, cnt_lo, it = st
        nd = unfinished(lo, hi, cnt_lo)
        k1 = jnp.maximum(_f32_to_key(min_ge(_key_to_f32(lo))), lo)
        cnt = count_ge(_key_to_f32(k1 + 1))
        ge = cnt >= kf
        up = jnp.logical_and(nd, ge)
        down = jnp.logical_and(nd, jnp.logical_not(ge))
        new_lo = jnp.where(nd, jnp.where(ge, k1 + 1, k1), lo)
        return new_lo, jnp.where(down, k1 + 1, hi), jnp.where(up, cnt, cnt_lo), it + 2

    def bis_body(st):
        lo, hi, cnt_lo, it = st
        excess = jnp.max(jnp.where(unfinished(lo, hi, cnt_lo), cnt_lo - kf, 0.0))
        return lax.cond(excess > PEEL_MAX, bisect, peel, st)

    lo, _, cnt_lo, _ = lax.while_loop(bis_cond, bis_body, (lo0, hi0, cnt0, jnp.int32(0)))
    thr = _key_to_f32(lo)

    tie = jnp.logical_and(cnt_lo > kf, jnp.logical_not(few))

    @pl.when(jnp.max(jnp.where(tie, 1.0, 0.0)) > 0.0)
    def _():
        def count_gt():
            def gt_body(c, cnt):
                k0 = pl.multiple_of(c * kc, kc)
                s = sc_ref[pl.ds(k0, kc), :]
                return cnt + _fold8(jnp.where(s > thr, 1.0, 0.0), jnp.sum)
            gt8 = lax.fori_loop(0, nch, gt_body, jnp.zeros((SUBLANES, tq), F32))
            return jnp.sum(gt8, axis=0, keepdims=True)

        zero_thr = thr == 0.0
        other = jnp.max(jnp.where(jnp.logical_and(tie, jnp.logical_not(zero_thr)), 1.0, 0.0)) > 0.0
        gt = lax.cond(other, count_gt, lambda: cnt_pos)
        need = kf - jnp.where(zero_thr, cnt_pos, gt)
        r = lax.broadcasted_iota(jnp.int32, (kc, kc), 0)
        cidx = lax.broadcasted_iota(jnp.int32, (kc, kc), 1)
        tri = jnp.where(cidx <= r, 1.0, 0.0).astype(BF16)

        def tie_body(c, seen):
            k0 = pl.multiple_of(c * kc, kc)
            s = sc_ref[pl.ds(k0, kc), :]
            eqf = jnp.where(s == thr, jnp.where(tie, 1.0, 0.0), 0.0)
            pre = jnp.dot(tri, eqf.astype(BF16), preferred_element_type=F32) + seen
            drop = eqf * jnp.where(pre > need, 1.0, 0.0)
            sc_ref[pl.ds(k0, kc), :] = jnp.where(drop > 0.0, -jnp.inf, s)
            return seen + jnp.sum(eqf, axis=0, keepdims=True)

        lax.fori_loop(0, nch, tie_body, jnp.zeros((1, tq), F32))

    qT_all = (_tile_T(q_ref) * ATTN_SCALE).astype(BF16)
    qT = jnp.concatenate([_place_head(*_head_rows(qT_all, h), 0) for h in range(A_HEADS)], axis=1)
    acc_ref[...] = jnp.zeros_like(acc_ref)
    ones_rows = jnp.ones((SUBLANES, kc), BF16)

    def issue_scores(c, slot):
        k0 = pl.multiple_of(jnp.minimum(c, nch - 1) * kc, kc)
        sbuf_ref[slot] = jnp.dot(kv_ref[pl.ds(k0, kc), :], qT, preferred_element_type=F32)

    def softmax_pv(c, slot, m):
        k0 = pl.multiple_of(c * kc, kc)
        v1 = jnp.concatenate([vaT_ref[0, c], ones_rows], axis=0)
        sel = sc_ref[pl.ds(k0, kc), :] >= thr
        s = jnp.concatenate([jnp.where(sel, sbuf_ref[slot, :, h * tq:(h + 1) * tq], NEG_BIG)
                             for h in range(A_HEADS)], axis=1)
        m_new = jnp.maximum(m, jnp.max(s, axis=0, keepdims=True))
        alpha = jnp.exp(m - m_new)
        p = jnp.exp(s - m_new).astype(BF16)
        acc_ref[...] = acc_ref[...] * alpha + jnp.dot(v1, p, preferred_element_type=F32)
        return m_new

    def chunk_pair(j, m):
        c0 = 2 * j
        issue_scores(c0 + 1, 1)
        m = softmax_pv(c0, 0, m)

        def odd_chunk(m):
            issue_scores(c0 + 2, 0)
            return softmax_pv(c0 + 1, 1, m)

        return lax.cond(c0 + 1 < nch, odd_chunk, lambda m: m, m)

    issue_scores(0, 0)
    lax.fori_loop(0, (nch + 1) // 2, chunk_pair, jnp.full((1, A_HEADS * tq), NEG_BIG, F32))
    a = acc_ref[...]
    o = a[:HEAD_DIM] / a[HEAD_DIM:HEAD_DIM + 1]
    _store_rows(o_ref, [o[:, h * tq:(h + 1) * tq] for h in range(A_HEADS)])


def _dsa(h_rope, h_plain, kv, vaT4, *, bsz, seq, tq, kc, topk):
    nq = seq // tq
    return pl.pallas_call(
        functools.partial(_dsa_body, tq=tq, kc=kc, topk=topk),
        grid=(bsz, nq),
        in_specs=[
            pl.BlockSpec((tq, PROJ_TN), lambda b, i: (b * nq + i, TILE_AQ)),
            pl.BlockSpec((tq, PROJ_TN), lambda b, i: (b * nq + i, TILE_AIQ)),
            pl.BlockSpec((tq, LANES), lambda b, i: (b * nq + i, COL_AIW // LANES)),
            pl.BlockSpec((seq, PROJ_TN), lambda b, i: (b, TILE_CK_AIK)),
            pl.BlockSpec((seq, PROJ_TN), lambda b, i: (b, 0)),
            pl.BlockSpec((1, seq // kc, HEAD_DIM, kc), lambda b, i: (b, 0, 0, 0)),
        ],
        out_specs=pl.BlockSpec((tq, A_W), lambda b, i: (b * nq + i, 0)),
        out_shape=jax.ShapeDtypeStruct((bsz * seq, A_W), BF16),
        scratch_shapes=[pltpu.VMEM((seq + kc, tq), F32),
                        pltpu.VMEM((HEAD_DIM + SUBLANES, A_HEADS * tq), F32),
                        pltpu.VMEM((2, kc, A_HEADS * tq), F32)],
        compiler_params=_cparams(2),
        name="dsa_attention",
    )(h_rope, h_rope, h_plain, h_rope, kv, vaT4)


def _kmean_body(k_ref, o_ref):
    o_ref[0] = jnp.mean(k_ref[...].astype(F32), axis=0, keepdims=True)


def _moba_kmean(h_rope, *, n_tokens):
    nblk = n_tokens // B_BLOCK
    return pl.pallas_call(
        _kmean_body,
        grid=(nblk,),
        in_specs=[pl.BlockSpec((B_BLOCK, B_W), lambda i: (i, TILE_BK))],
        out_specs=pl.BlockSpec((1, 1, B_W), lambda i: (i, 0, 0)),
        out_shape=jax.ShapeDtypeStruct((nblk, 1, B_W), F32),
        compiler_params=_cparams(1),
        name="moba_kmean",
    )(h_rope)


def _moba_gate_body(q_ref, km_ref, sel_ref, *, tq, nkb, nsel):
    i = pl.program_id(1)
    qpos = i * tq + lax.broadcasted_iota(jnp.int32, (1, tq), 1)
    cur = qpos // B_BLOCK
    row = lax.broadcasted_iota(jnp.int32, (nkb, tq), 0)
    past = row < cur
    qT = _tile_T(q_ref).astype(BF16)
    km = km_ref[0]
    col_head = (lax.broadcasted_iota(jnp.int32, km.shape, 1) % HALF_TN) // (HEAD_DIM // 2)
    for h in range(B_HEADS):
        km_h = jnp.where(col_head == h, km, 0.0).astype(BF16)
        g = jnp.dot(km_h, qT, preferred_element_type=F32)
        g = jnp.where(past, g, -jnp.inf)
        sel = jnp.zeros((nkb, tq), F32)
        for _ in range(nsel):
            mx = jnp.max(g, axis=0, keepdims=True)
            idx = jnp.min(jnp.where(g == mx, row, nkb), axis=0, keepdims=True)
            pick = row == idx
            sel = jnp.where(pick, 1.0, sel)
            g = jnp.where(pick, -jnp.inf, g)
        sel_ref[0, h] = jnp.where(past, sel, 0.0)


def _moba_gate(h_rope, kmean, *, bsz, seq, tq):
    nkb = seq // B_BLOCK
    nsel = min(B_TOPK_BLOCKS, nkb)
    nq = seq // tq
    return pl.pallas_call(
        functools.partial(_moba_gate_body, tq=tq, nkb=nkb, nsel=nsel),
        grid=(bsz, nq),
        in_specs=[
            pl.BlockSpec((tq, PROJ_TN), lambda b, i: (b * nq + i, TILE_BQ)),
            pl.BlockSpec((1, nkb, B_W), lambda b, i: (b, 0, 0)),
        ],
        out_specs=pl.BlockSpec((1, B_HEADS, nkb, tq), lambda b, i: (b, 0, 0, i)),
        out_shape=jax.ShapeDtypeStruct((bsz, B_HEADS, nkb, seq), F32),
        compiler_params=_cparams(2),
        name="moba_gate",
    )(h_rope, kmean)


def _moba_body(q_ref, k_ref, vT_ref, sel_ref, o_ref, acc_ref, sbuf_ref, *, tq):
    i = pl.program_id(1)
    qT_all = (_tile_T(q_ref) * ATTN_SCALE).astype(BF16)
    qT = [_place_head(*_head_rows(qT_all, h), h) for h in range(B_HEADS)]
    ones_rows = jnp.ones((SUBLANES, B_BLOCK), BF16)
    n_past = i

    def values(h, j):
        return jnp.concatenate([vT_ref[0, h, j], ones_rows], axis=0)

    def issue_scores(j, slot):
        k0 = pl.multiple_of(jnp.minimum(j, jnp.maximum(n_past - 1, 0)) * B_BLOCK, B_BLOCK)
        for h in range(B_HEADS):
            sbuf_ref[slot, h] = jnp.dot(k_ref[pl.ds(k0, B_BLOCK), :], qT[h], preferred_element_type=F32)

    def softmax_pv(j, slot, ms):
        ps, alphas, new_ms = [], [], []
        for h in range(B_HEADS):
            s = jnp.where(sel_ref[0, h, pl.ds(j, 1), :] > 0.0, sbuf_ref[slot, h], NEG_BIG)
            m_new = jnp.maximum(ms[h], jnp.max(s, axis=0, keepdims=True))
            alphas.append(jnp.exp(ms[h] - m_new))
            ps.append(jnp.exp(s - m_new).astype(BF16))
            new_ms.append(m_new)
        for h in range(B_HEADS):
            acc_ref[h] = acc_ref[h] * alphas[h] + jnp.dot(values(h, j), ps[h], preferred_element_type=F32)
        return tuple(new_ms)

    own0 = pl.multiple_of(i * B_BLOCK, B_BLOCK)
    s_own = [jnp.dot(k_ref[pl.ds(own0, B_BLOCK), :], qT[h], preferred_element_type=F32) for h in range(B_HEADS)]
    issue_scores(0, 0)
    causal = lax.broadcasted_iota(jnp.int32, (B_BLOCK, tq), 0) <= lax.broadcasted_iota(jnp.int32, (B_BLOCK, tq), 1)
    ms = []
    for h in range(B_HEADS):
        s = jnp.where(causal, s_own[h], NEG_BIG)
        m0 = jnp.max(s, axis=0, keepdims=True)
        acc_ref[h] = jnp.dot(values(h, i), jnp.exp(s - m0).astype(BF16), preferred_element_type=F32)
        ms.append(m0)

    def block_pair(jj, ms):
        j0 = 2 * jj
        issue_scores(j0 + 1, 1)
        ms = softmax_pv(j0, 0, ms)

        def odd_block(ms):
            issue_scores(j0 + 2, 0)
            return softmax_pv(j0 + 1, 1, ms)

        return lax.cond(j0 + 1 < n_past, odd_block, lambda ms: ms, ms)

    lax.fori_loop(0, (n_past + 1) // 2, block_pair, tuple(ms))
    outs = []
    for h in range(B_HEADS):
        a = acc_ref[h]
        outs.append(a[:HEAD_DIM] / a[HEAD_DIM:HEAD_DIM + 1])
    _store_rows(o_ref, outs)


def _moba(h_rope, bvT5, sel, *, bsz, seq):
    tq = B_BLOCK
    nkb = seq // B_BLOCK
    nq = seq // tq
    return pl.pallas_call(
        functools.partial(_moba_body, tq=tq),
        grid=(bsz, nq),
        in_specs=[
            pl.BlockSpec((tq, PROJ_TN), lambda b, i: (b * nq + i, TILE_BQ)),
            pl.BlockSpec((seq, PROJ_TN), lambda b, i: (b, TILE_BK)),
            pl.BlockSpec((1, B_HEADS, nkb, HEAD_DIM, B_BLOCK), lambda b, i: (b, 0, 0, 0, 0)),
            pl.BlockSpec((1, B_HEADS, nkb, tq), lambda b, i: (b, 0, 0, i)),
        ],
        out_specs=pl.BlockSpec((tq, B_W), lambda b, i: (b * nq + i, 0)),
        out_shape=jax.ShapeDtypeStruct((bsz * seq, B_W), BF16),
        scratch_shapes=[pltpu.VMEM((B_HEADS, HEAD_DIM + SUBLANES, tq), F32),
                        pltpu.VMEM((2, B_HEADS, B_BLOCK, tq), F32)],
        compiler_params=_cparams(2),
        name="moba_attention",
    )(h_rope, h_rope, bvT5, sel)


def _swa_body(sink_ref, q0_ref, q1_ref, k0_ref, k1_ref, k2_ref, v0_ref, v1_ref, v2_ref, o_ref, *, tq):
    i = pl.program_id(1)
    q0 = i * tq
    qpos = q0 + lax.broadcasted_iota(jnp.int32, (1, tq), 1)
    nk = tq + C_WINDOW
    kpos = q0 - C_WINDOW + lax.broadcasted_iota(jnp.int32, (nk, 1), 0)
    diff = qpos - kpos
    ok = jnp.where(diff >= 0, jnp.where(diff < C_WINDOW, jnp.where(kpos >= 0, 1.0, 0.0), 0.0), 0.0) > 0.0
    ones_rows = jnp.ones((SUBLANES, nk), BF16)
    group = C_HEADS // C_KV_HEADS
    assert group == HEADS_PER_TILE
    kk = jnp.concatenate([k0_ref[...], k1_ref[...], k2_ref[...]], axis=0)
    outs = []
    for kv, q_ref in enumerate((q0_ref, q1_ref)):
        vv = jnp.concatenate([v0_ref[0, kv], v1_ref[0, kv], v2_ref[0, kv]], axis=1)
        v1 = jnp.concatenate([vv, ones_rows], axis=0)
        heads = [kv * group + g for g in range(group)]
        qT_all = (_tile_T(q_ref) * ATTN_SCALE).astype(BF16)
        qT = jnp.concatenate([_place_head(*_head_rows(qT_all, g), kv) for g in range(group)], axis=1)
        sink = jnp.concatenate([jnp.full((1, tq), sink_ref[hd], F32) for hd in heads], axis=1)
        s = jnp.dot(kk, qT, preferred_element_type=F32)
        s = jnp.concatenate([jnp.where(ok, s[:, g * tq:(g + 1) * tq], NEG_BIG) for g in range(group)], axis=1)
        m = jnp.maximum(jnp.max(s, axis=0, keepdims=True), sink)
        p = jnp.exp(s - m).astype(BF16)
        a = jnp.dot(v1, p, preferred_element_type=F32)
        o = a[:HEAD_DIM] / (a[HEAD_DIM:HEAD_DIM + 1] + jnp.exp(sink - m))
        outs += [o[:, g * tq:(g + 1) * tq] for g in range(group)]
    _store_rows(o_ref, outs)


def _swa(sinks, h_rope, cvT, *, bsz, seq, tq):
    r = tq // C_WINDOW
    assert r == 2
    nq = seq // tq
    nwb = seq // C_WINDOW

    def kspec(off):
        return pl.BlockSpec((C_WINDOW, PROJ_TN),
                            lambda b, i, s: (b * nwb + jnp.maximum(i * r + off, 0), TILE_CK_AIK))

    def vspec(off):
        return pl.BlockSpec((1, C_KV_HEADS, HEAD_DIM, C_WINDOW),
                            lambda b, i, s: (b, 0, 0, jnp.maximum(i * r + off, 0)))

    grid_spec = pltpu.PrefetchScalarGridSpec(
        num_scalar_prefetch=1,
        grid=(bsz, nq),
        in_specs=[pl.BlockSpec((tq, PROJ_TN), lambda b, i, s: (b * nq + i, TILE_CQ)),
                  pl.BlockSpec((tq, PROJ_TN), lambda b, i, s: (b * nq + i, TILE_CQ + 1)),
                  kspec(-1), kspec(0), kspec(1), vspec(-1), vspec(0), vspec(1)],
        out_specs=pl.BlockSpec((tq, C_W), lambda b, i, s: (b * nq + i, 0)),
    )
    return pl.pallas_call(
        functools.partial(_swa_body, tq=tq),
        grid_spec=grid_spec,
        out_shape=jax.ShapeDtypeStruct((bsz * seq, C_W), BF16),
        compiler_params=_cparams(2),
        name="swa_attention",
    )(sinks, h_rope, h_rope, h_rope, h_rope, h_rope, cvT, cvT, cvT)


def _layer_norm(z, g, b):
    mu = jnp.mean(z, axis=-1, keepdims=True)
    zc = z - mu
    var = jnp.mean(zc * zc, axis=-1, keepdims=True)
    return zc * lax.rsqrt(var + LN_EPS) * g + b


def _split_bf16(a):
    hi = a.astype(BF16)
    lo = (a - hi.astype(F32)).astype(BF16)
    return hi, lo


def _router(x1, rw_ref, rb_ref, cls_ref, wt_ref):
    nt = (((1,), (1,)), ((), ()))
    xh, xl = _split_bf16(x1)
    wh, wl = _split_bf16(rw_ref[...])
    logits = (lax.dot_general(wh, xh, nt, preferred_element_type=F32)
              + lax.dot_general(wh, xl, nt, preferred_element_type=F32)
              + lax.dot_general(wl, xh, nt, preferred_element_type=F32))
    aff = jax.nn.sigmoid(logits)
    score = aff + rb_ref[...]
    tm = x1.shape[0]
    sc = [score[e:e + 1, :] for e in range(N_EXPERTS)]
    af = [aff[e:e + 1, :] for e in range(N_EXPERTS)]
    in_top = []
    grp_score = []
    for gq in range(N_GROUPS):
        gs = jnp.zeros((1, tm), F32)
        for a in range(EXPERTS_PER_GROUP):
            ea = gq * EXPERTS_PER_GROUP + a
            rank = jnp.zeros((1, tm), F32)
            for b in range(EXPERTS_PER_GROUP):
                if b == a:
                    continue
                eb = gq * EXPERTS_PER_GROUP + b
                beats = (sc[eb] >= sc[ea]) if b < a else (sc[eb] > sc[ea])
                rank = rank + jnp.where(beats, 1.0, 0.0)
            top = rank < 2.0
            in_top.append(top)
            gs = gs + jnp.where(top, sc[ea], 0.0)
        grp_score.append(gs)
    best = grp_score[0]
    gstar = jnp.zeros((1, tm), jnp.int32)
    for gq in range(1, N_GROUPS):
        better = grp_score[gq] > best
        best = jnp.where(better, grp_score[gq], best)
        gstar = jnp.where(better, gq, gstar)
    cls = jnp.zeros((1, tm), jnp.int32)
    w_lo = jnp.zeros((1, tm), F32)
    w_hi = jnp.zeros((1, tm), F32)
    for gq in range(N_GROUPS):
        is_g = gstar == gq
        for pi, (a, b) in enumerate(PAIRS):
            ea, eb = gq * EXPERTS_PER_GROUP + a, gq * EXPERTS_PER_GROUP + b
            hit = jnp.where(is_g, jnp.where(in_top[ea], jnp.where(in_top[eb], 1.0, 0.0), 0.0), 0.0) > 0.0
            cls = jnp.where(hit, gq * len(PAIRS) + pi, cls)
            tot = af[ea] + af[eb]
            w_lo = jnp.where(hit, af[ea] / tot, w_lo)
            w_hi = jnp.where(hit, af[eb] / tot, w_hi)
    cls_ref[...] = cls
    wt_ref[0:1, :] = w_lo
    wt_ref[1:2, :] = w_hi


def _merge_body(oa_ref, ob_ref, oc_ref, ga_ref, gb_ref, gc_ref, x_ref, wa_ref, wb_ref, wc_ref, wo_ref,
                g_ref, b_ref, rw_ref, rb_ref, x1_ref, cls_ref, wt_ref):
    def branch(o_ref, w_ref, gate_ref):
        y = jnp.dot(o_ref[...], w_ref[...], preferred_element_type=F32)
        return jax.nn.sigmoid(gate_ref[...].astype(F32)) * y

    merged = branch(oa_ref, wa_ref, ga_ref) + branch(ob_ref, wb_ref, gb_ref) + branch(oc_ref, wc_ref, gc_ref)
    y = jnp.dot(merged.astype(BF16), wo_ref[...], preferred_element_type=F32)
    x1 = _layer_norm(DN_ALPHA * x_ref[...] + y, g_ref[...], b_ref[...])
    x1_ref[...] = x1
    _router(x1, rw_ref, rb_ref, cls_ref, wt_ref)


def _merge(oa, ob, oc, h_plain, x, wa, wb, wc, wo, ln_g, ln_b, rwT, rb, *, tm):
    t_tokens = x.shape[0]
    gate_blk = COL_GATES // D_MODEL
    full = lambda shape: pl.BlockSpec(shape, lambda i: (0,) * len(shape))
    return pl.pallas_call(
        _merge_body,
        grid=(t_tokens // tm,),
        in_specs=[
            pl.BlockSpec((tm, A_W), lambda i: (i, 0)),
            pl.BlockSpec((tm, B_W), lambda i: (i, 0)),
            pl.BlockSpec((tm, C_W), lambda i: (i, 0)),
            pl.BlockSpec((tm, D_MODEL), lambda i: (i, gate_blk)),
            pl.BlockSpec((tm, D_MODEL), lambda i: (i, gate_blk + 1)),
            pl.BlockSpec((tm, D_MODEL), lambda i: (i, gate_blk + 2)),
            pl.BlockSpec((tm, D_MODEL), lambda i: (i, 0)),
            full((A_W, D_MODEL)), full((B_W, D_MODEL)), full((C_W, D_MODEL)), full((D_MODEL, D_MODEL)),
            full((1, D_MODEL)), full((1, D_MODEL)), full((N_EXPERTS, D_MODEL)), full((N_EXPERTS, 1)),
        ],
        out_specs=[
            pl.BlockSpec((tm, D_MODEL), lambda i: (i, 0)),
            pl.BlockSpec((1, tm), lambda i: (0, i)),
            pl.BlockSpec((2, tm), lambda i: (0, i)),
        ],
        out_shape=[
            jax.ShapeDtypeStruct((t_tokens, D_MODEL), F32),
            jax.ShapeDtypeStruct((1, t_tokens), jnp.int32),
            jax.ShapeDtypeStruct((2, t_tokens), F32),
        ],
        compiler_params=_cparams(1),
        name="merge_ln_router",
    )(oa, ob, oc, h_plain, h_plain, h_plain, x, wa, wb, wc, wo, ln_g, ln_b, rwT, rb)


def _gather_start(idx_ref, src_hbm, dst_ref, sem, n_rows):
    def start(r, _):
        pltpu.make_async_copy(src_hbm.at[pl.ds(idx_ref[0, 0, r], 1)], dst_ref.at[pl.ds(r, 1)], sem).start()
        return 0

    lax.fori_loop(0, n_rows, start, 0, unroll=8)


def _gather_wait(src_hbm, dst_ref, sem, n_rows):
    pltpu.make_async_copy(src_hbm.at[pl.ds(0, n_rows)], dst_ref, sem).wait()


def _experts_body(e1_ref, e2_ref, nused_ref, src_ref, src_next_ref, x_hbm, wt_ref,
                  wg1_ref, wu1_ref, wd1_ref, wg2_ref, wu2_ref, wd2_ref, o_ref, xbuf, sem):
    s = pl.program_id(0)
    nused = nused_ref[0]
    slot = lax.rem(s, 2)

    @pl.when(s == 0)
    def _():
        _gather_start(src_ref, x_hbm, xbuf.at[0], sem.at[0], MOE_TILE)

    @pl.when(s + 1 < nused)
    def _():
        _gather_start(src_next_ref, x_hbm, xbuf.at[1 - slot], sem.at[1 - slot], MOE_TILE)

    @pl.when(s < nused)
    def _():
        _gather_wait(x_hbm, xbuf.at[slot], sem.at[slot], MOE_TILE)
        xb = xbuf[slot].astype(BF16)
        wt = wt_ref[...]

        def expert(wg_ref, wu_ref, wd_ref):
            g = jnp.dot(xb, wg_ref[0], preferred_element_type=F32)
            u = jnp.dot(xb, wu_ref[0], preferred_element_type=F32)
            he = (g * jax.nn.sigmoid(g) * u).astype(BF16)
            return jnp.dot(he, wd_ref[0], preferred_element_type=F32)

        o_ref[...] = wt[:, 0:1] * expert(wg1_ref, wu1_ref, wd1_ref) + wt[:, 1:2] * expert(wg2_ref, wu2_ref, wd2_ref)

    @pl.when(s >= nused)
    def _():
        o_ref[...] = jnp.zeros_like(o_ref)


def _experts(tile_e1, tile_e2, nused, src3, x1, wsort, wg, wu, wd):
    ntiles = src3.shape[0]
    wspec_up = lambda which: pl.BlockSpec((1, D_MODEL, D_EXPERT), lambda s, e1, e2, nu: ((e1, e2)[which][s], 0, 0))
    wspec_dn = lambda which: pl.BlockSpec((1, D_EXPERT, D_MODEL), lambda s, e1, e2, nu: ((e1, e2)[which][s], 0, 0))
    grid_spec = pltpu.PrefetchScalarGridSpec(
        num_scalar_prefetch=3,
        grid=(ntiles,),
        in_specs=[
            pl.BlockSpec((1, 1, MOE_TILE), lambda s, e1, e2, nu: (s, 0, 0), memory_space=pltpu.SMEM),
            pl.BlockSpec((1, 1, MOE_TILE), lambda s, e1, e2, nu: (jnp.minimum(s + 1, ntiles - 1), 0, 0),
                         memory_space=pltpu.SMEM),
            pl.BlockSpec(memory_space=pl.ANY),
            pl.BlockSpec((MOE_TILE, 2), lambda s, e1, e2, nu: (s, 0)),
            wspec_up(0), wspec_up(0), wspec_dn(0), wspec_up(1), wspec_up(1), wspec_dn(1),
        ],
        out_specs=pl.BlockSpec((MOE_TILE, D_MODEL), lambda s, e1, e2, nu: (s, 0)),
        scratch_shapes=[pltpu.VMEM((2, MOE_TILE, D_MODEL), F32), pltpu.SemaphoreType.DMA((2,))],
    )
    return pl.pallas_call(
        _experts_body,
        grid_spec=grid_spec,
        out_shape=jax.ShapeDtypeStruct((ntiles * MOE_TILE, D_MODEL), F32),
        compiler_params=_cparams(1),
        name="moe_experts",
    )(tile_e1, tile_e2, nused, src3, src3, x1, wsort, wg, wu, wd, wg, wu, wd)


def _final_body(pos_ref, pos_next_ref, y_hbm, x_ref, g_ref, b_ref, o_ref, ob_ref, ybuf, sem, *, tm):
    s = pl.program_id(0)
    slot = lax.rem(s, 2)

    @pl.when(s == 0)
    def _():
        _gather_start(pos_ref, y_hbm, ybuf.at[0], sem.at[0], tm)

    @pl.when(s + 1 < pl.num_programs(0))
    def _():
        _gather_start(pos_next_ref, y_hbm, ybuf.at[1 - slot], sem.at[1 - slot], tm)

    _gather_wait(y_hbm, ybuf.at[slot], sem.at[slot], tm)
    x2 = _layer_norm(DN_ALPHA * x_ref[...] + ybuf[slot], g_ref[...], b_ref[...])
    o_ref[...] = x2
    ob_ref[...] = x2.astype(BF16)


def _final(pos3, y_sorted, x1, ln_g, ln_b, *, tm):
    t_tokens = x1.shape[0]
    nsteps = t_tokens // tm
    return pl.pallas_call(
        functools.partial(_final_body, tm=tm),
        grid=(nsteps,),
        in_specs=[
            pl.BlockSpec((1, 1, tm), lambda i: (i, 0, 0), memory_space=pltpu.SMEM),
            pl.BlockSpec((1, 1, tm), lambda i: (jnp.minimum(i + 1, nsteps - 1), 0, 0), memory_space=pltpu.SMEM),
            pl.BlockSpec(memory_space=pl.ANY),
            pl.BlockSpec((tm, D_MODEL), lambda i: (i, 0)),
            pl.BlockSpec((1, D_MODEL), lambda i: (0, 0)),
            pl.BlockSpec((1, D_MODEL), lambda i: (0, 0)),
        ],
        out_specs=[pl.BlockSpec((tm, D_MODEL), lambda i: (i, 0)), pl.BlockSpec((tm, D_MODEL), lambda i: (i, 0))],
        out_shape=[jax.ShapeDtypeStruct((t_tokens, D_MODEL), F32), jax.ShapeDtypeStruct((t_tokens, D_MODEL), BF16)],
        scratch_shapes=[pltpu.VMEM((2, tm, D_MODEL), F32), pltpu.SemaphoreType.DMA((2,))],
        compiler_params=_cparams(1),
        name="moe_combine_ln",
    )(pos3, pos3, y_sorted, x1, ln_g, ln_b)


def _rope_tables(seq):
    inv = 1.0 / (ROPE_THETA ** (jnp.arange(0, HEAD_DIM, 2, dtype=F32) / HEAD_DIM))
    ang = jnp.arange(seq, dtype=F32)[:, None] * inv[None, :]
    return jnp.cos(ang), jnp.sin(ang)


def _halves_layout(w_heads):
    d, n = w_heads.shape
    t = w_heads.reshape(d, n // PROJ_TN, HEADS_PER_TILE, 2, HEAD_DIM // 2)
    return t.transpose(0, 1, 3, 2, 4).reshape(d, n)


def _reorder_w_in(w):
    pts = np.cumsum((0,) + IN_SIZES)
    sec = [w[:, pts[k]:pts[k + 1]] for k in range(len(IN_SIZES))]
    a_q, a_c, a_iq, a_ik, a_iw, b_q, b_k, b_v, c_q, c_k, c_v, gates = sec
    zeros = lambda n: jnp.zeros((w.shape[0], n), w.dtype)
    w_rope = _halves_layout(jnp.concatenate([a_q, a_iq, b_q, b_k, c_q, c_k, a_ik, zeros(HEAD_DIM)], axis=1))
    w_plain = jnp.concatenate([gates, b_v, a_c, a_iw, zeros(LANES - A_IDX_HEADS), c_v, zeros(LANES)], axis=1)
    assert w_rope.shape[1] == N_ROPE and w_plain.shape[1] == N_PLAIN
    return w_rope.astype(BF16), w_plain.astype(BF16)


def _kv_weights(w_uk, w_uv):
    half = HEAD_DIM // 2
    zeros = lambda n: jnp.zeros((w_uk.shape[0], n), w_uk.dtype)
    return jnp.concatenate([w_uk[:, :half], w_uv, zeros(HALF_TN - half - HEAD_DIM),
                            w_uk[:, half:], zeros(HALF_TN - half)], axis=1).astype(BF16)


def _moe_plan(cls, wts, n_tokens):
    ntiles = n_tokens // MOE_TILE + N_CLASSES
    onehot = (cls[:, None] == jnp.arange(N_CLASSES, dtype=jnp.int32)[None, :]).astype(jnp.int32)
    csum = jnp.cumsum(onehot, axis=0)
    rank = jnp.sum(onehot * csum, axis=1) - 1
    counts = csum[-1]
    ptiles = (counts + MOE_TILE - 1) // MOE_TILE
    tile_end = jnp.cumsum(ptiles)
    tile_start = tile_end - ptiles
    pos = (tile_start[cls] * MOE_TILE + rank).astype(jnp.int32)
    nused = tile_end[-1:].astype(jnp.int32)
    tile_ids = jnp.arange(ntiles, dtype=jnp.int32)
    tile_cls = jnp.minimum(jnp.sum((tile_ids[:, None] >= tile_end[None, :]).astype(jnp.int32), axis=1),
                           N_CLASSES - 1).astype(jnp.int32)
    pair = np.array(PAIRS, dtype=np.int32)
    e_lo = jnp.asarray(np.repeat(np.arange(N_GROUPS), len(PAIRS)) * EXPERTS_PER_GROUP + np.tile(pair[:, 0], N_GROUPS), jnp.int32)
    e_hi = jnp.asarray(np.repeat(np.arange(N_GROUPS), len(PAIRS)) * EXPERTS_PER_GROUP + np.tile(pair[:, 1], N_GROUPS), jnp.int32)
    src = jnp.zeros((ntiles * MOE_TILE,), jnp.int32).at[pos].set(jnp.arange(n_tokens, dtype=jnp.int32))
    wsort = jnp.zeros((ntiles * MOE_TILE, 2), F32).at[pos].set(wts.T)
    return e_lo[tile_cls], e_hi[tile_cls], nused, src.reshape(ntiles, 1, MOE_TILE), pos, wsort


def kernel(x, w_in, a_w_uk, a_w_uv, c_sinks, w_branch, w_o, ln1_g, ln1_b, router_w, router_b,
           moe_w_gate, moe_w_up, moe_w_down, ln2_g, ln2_b):
    bsz, seq, _ = x.shape
    n_tokens = bsz * seq
    topk = min(A_TOPK_MAX, seq // 4)
    nkb = seq // B_BLOCK
    tq = 256
    kc = 256
    tm_proj = min(1024, seq)
    tm_row = 512
    half = HEAD_DIM // 2

    cos, sin = _rope_tables(seq)
    cos_in = jnp.tile(cos, (1, HEADS_PER_TILE))
    sin_in = jnp.tile(sin, (1, HEADS_PER_TILE))
    cos_kv = jnp.concatenate([cos, jnp.ones((seq, HALF_TN - half), F32)], axis=1)
    sin_kv = jnp.concatenate([sin, jnp.zeros((seq, HALF_TN - half), F32)], axis=1)
    rwT = router_w.T
    rb = router_b.reshape(N_EXPERTS, 1)

    xf = x.reshape(n_tokens, D_MODEL)
    xb = xf.astype(BF16)
    for l in range(DEPTH):
        w_rope, w_plain = _reorder_w_in(w_in[l])
        h_rope = _proj_rope(xb, 0, D_MODEL, w_rope, cos_in, sin_in, tm=tm_proj, seq=seq)
        h_plain = _proj_plain(xb, w_plain, tm=tm_proj)
        kv = _proj_rope(h_plain, COL_AC // A_KV_RANK, A_KV_RANK, _kv_weights(a_w_uk[l], a_w_uv[l]),
                        cos_kv, sin_kv, tm=tm_proj, seq=seq)

        vaT4 = kv[:, KV_VA0:KV_VA0 + HEAD_DIM].reshape(bsz, seq // kc, kc, HEAD_DIM).transpose(0, 1, 3, 2)
        bvT5 = (h_plain[:, COL_BV:COL_BV + B_W].reshape(bsz, nkb, B_BLOCK, B_HEADS, HEAD_DIM)
                .transpose(0, 3, 1, 4, 2))
        cvT = (h_plain[:, COL_CV:COL_CV + C_KV_HEADS * HEAD_DIM].reshape(bsz, seq, C_KV_HEADS, HEAD_DIM)
               .transpose(0, 2, 3, 1))

        o_a = _dsa(h_rope, h_plain, kv, vaT4, bsz=bsz, seq=seq, tq=tq, kc=kc, topk=topk)
        kmean = _moba_kmean(h_rope, n_tokens=n_tokens).reshape(bsz, nkb, B_W)
        sel = _moba_gate(h_rope, kmean, bsz=bsz, seq=seq, tq=tq)
        o_b = _moba(h_rope, bvT5, sel, bsz=bsz, seq=seq)
        o_c = _swa(c_sinks[l], h_rope, cvT, bsz=bsz, seq=seq, tq=tq)

        wb_all = w_branch[l].astype(BF16)
        x1, cls, wts = _merge(o_a, o_b, o_c, h_plain, xf,
                              wb_all[:A_W], wb_all[A_W:A_W + B_W], wb_all[A_W + B_W:], w_o[l].astype(BF16),
                              ln1_g[l].reshape(1, D_MODEL), ln1_b[l].reshape(1, D_MODEL), rwT, rb, tm=tm_row)

        e1, e2, nused, src3, pos, wsort = _moe_plan(cls[0], wts, n_tokens)
        y_sorted = _experts(e1, e2, nused, src3, x1, wsort,
                            moe_w_gate[l].astype(BF16), moe_w_up[l].astype(BF16), moe_w_down[l].astype(BF16))
        xf, xb = _final(pos.reshape(n_tokens // tm_row, 1, tm_row), y_sorted, x1,
                        ln2_g[l].reshape(1, D_MODEL), ln2_b[l].reshape(1, D_MODEL), tm=tm_row)
    return xf.reshape(bsz, seq, D_MODEL)
```

```python
import functools
import math

import jax
import jax.numpy as jnp
import numpy as np
from jax import lax
from jax.experimental import pallas as pl
from jax.experimental.pallas import tpu as pltpu

D_MODEL = 1024
DEPTH = 2
HEAD_DIM = 64
ROPE_THETA = 10000.0
LN_EPS = 1e-5
A_HEADS = 4
A_KV_RANK = 128
A_IDX_HEADS = 4
A_IDX_DIM = 64
A_TOPK_MAX = 256
B_HEADS = 4
B_BLOCK = 256
B_TOPK_BLOCKS = 3
C_HEADS = 8
C_KV_HEADS = 2
C_WINDOW = 128
N_BRANCH = 3
A_W = A_HEADS * HEAD_DIM
B_W = B_HEADS * HEAD_DIM
C_W = C_HEADS * HEAD_DIM
IN_SIZES = (A_W, A_KV_RANK, A_IDX_HEADS * A_IDX_DIM, A_IDX_DIM, A_IDX_HEADS,
            B_W, B_W, B_W, C_W, C_KV_HEADS * HEAD_DIM, C_KV_HEADS * HEAD_DIM,
            N_BRANCH * D_MODEL)
N_EXPERTS = 16
N_GROUPS = 4
EXPERTS_PER_GROUP = 4
D_EXPERT = 512
DN_ALPHA = (2 * DEPTH) ** 0.25
ATTN_SCALE = HEAD_DIM ** -0.5
LOG2E = math.log2(math.e)
Q_SCALE = ATTN_SCALE * LOG2E

LANES = 128
SUBLANES = 8
VMEM_LIMIT_BYTES = 56 * 1024 * 1024

NEG_BIG = -1e30
BF16 = jnp.bfloat16
F32 = jnp.float32

PROJ_TN = 256
HALF_TN = PROJ_TN // 2
HEADS_PER_TILE = PROJ_TN // HEAD_DIM
TILE_AQ = 0
TILE_AIQ = 1
TILE_BQ = 2
TILE_BK = 3
TILE_CQ = 4
TILE_CK_AIK = 6
N_ROPE = 7 * PROJ_TN
COL_GATES = 0
COL_BV = 3072
COL_AC = 3328
COL_AIW = 3456
COL_CV = 3584
N_PLAIN = 3840
KV_VA0 = HEAD_DIM // 2

PAIRS = ((0, 1), (0, 2), (0, 3), (1, 2), (1, 3), (2, 3))
N_CLASSES = N_GROUPS * len(PAIRS)
MOE_TILE = 256

VALUE_MID_PASSES = 16
PEEL_MAX = 2.0
MIN_NORMAL_KEY = 0x00800000
MAX_SELECT_PASSES = 96


def _cparams(n_axes):
    return pltpu.CompilerParams(dimension_semantics=("arbitrary",) * n_axes,
                                vmem_limit_bytes=VMEM_LIMIT_BYTES)


def _proj_rope_body(x_ref, w_ref, cos_ref, sin_ref, o_ref, *, n_sub):
    sub = x_ref.shape[0] // n_sub
    for mi in range(n_sub):
        rows = slice(mi * sub, (mi + 1) * sub)
        acc = jnp.dot(x_ref[rows, :], w_ref[...], preferred_element_type=F32)
        a1, a2 = acc[:, :HALF_TN], acc[:, HALF_TN:]
        c, sn = cos_ref[rows, :], sin_ref[rows, :]
        o_ref[rows, :HALF_TN] = (a1 * c - a2 * sn).astype(o_ref.dtype)
        o_ref[rows, HALF_TN:] = (a2 * c + a1 * sn).astype(o_ref.dtype)


def _proj_plain_body(x_ref, w_ref, o_ref, *, n_sub):
    sub = x_ref.shape[0] // n_sub
    for mi in range(n_sub):
        rows = slice(mi * sub, (mi + 1) * sub)
        o_ref[rows, :] = jnp.dot(x_ref[rows, :], w_ref[...], preferred_element_type=F32).astype(o_ref.dtype)


def _proj_rope(x, x_col_block, k_dim, w, cos_t, sin_t, *, tm, seq):
    t_tokens = x.shape[0]
    n = w.shape[1]
    pos_blocks = seq // tm
    return pl.pallas_call(
        functools.partial(_proj_rope_body, n_sub=max(tm // 256, 1)),
        grid=(t_tokens // tm, n // PROJ_TN),
        in_specs=[
            pl.BlockSpec((tm, k_dim), lambda i, j: (i, x_col_block)),
            pl.BlockSpec((k_dim, PROJ_TN), lambda i, j: (0, j)),
            pl.BlockSpec((tm, HALF_TN), lambda i, j: (i % pos_blocks, 0)),
            pl.BlockSpec((tm, HALF_TN), lambda i, j: (i % pos_blocks, 0)),
        ],
        out_specs=pl.BlockSpec((tm, PROJ_TN), lambda i, j: (i, j)),
        out_shape=jax.ShapeDtypeStruct((t_tokens, n), BF16),
        compiler_params=_cparams(2),
        name="proj_rope",
    )(x, w, cos_t, sin_t)


def _proj_plain(x, w, *, tm):
    t_tokens, k_dim = x.shape
    n = w.shape[1]
    return pl.pallas_call(
        functools.partial(_proj_plain_body, n_sub=max(tm // 256, 1)),
        grid=(t_tokens // tm, n // PROJ_TN),
        in_specs=[
            pl.BlockSpec((tm, k_dim), lambda i, j: (i, 0)),
            pl.BlockSpec((k_dim, PROJ_TN), lambda i, j: (0, j)),
        ],
        out_specs=pl.BlockSpec((tm, PROJ_TN), lambda i, j: (i, j)),
        out_shape=jax.ShapeDtypeStruct((t_tokens, n), BF16),
        compiler_params=_cparams(2),
        name="proj_plain",
    )(x, w)


def _f32_to_key(x):
    b = lax.bitcast_convert_type(x, jnp.int32)
    return jnp.where(b < 0, b ^ jnp.int32(0x7FFFFFFF), b)


def _key_to_f32(k):
    b = jnp.where(k < 0, k ^ jnp.int32(0x7FFFFFFF), k)
    return lax.bitcast_convert_type(b, F32)


def _fold8(x, op):
    return op(x.reshape(x.shape[0] // SUBLANES, SUBLANES, x.shape[1]), axis=0)


def _tile_T(x_ref):
    return x_ref[...].astype(F32).T


def _head_rows(xT, h):
    half = HEAD_DIM // 2
    return xT[h * half:(h + 1) * half], xT[HALF_TN + h * half:HALF_TN + (h + 1) * half]


def _place_head(first, second, slot):
    half = HEAD_DIM // 2
    w = first.shape[1]
    before, after = slot * half, HALF_TN - (slot + 1) * half
    pieces = []
    for part in (first, second):
        pieces += [jnp.zeros((before, w), part.dtype)] * (before > 0) + [part] + [jnp.zeros((after, w), part.dtype)] * (after > 0)
    return jnp.concatenate(pieces, axis=0)


def _store_rows(o_ref, oT_heads):
    o_ref[...] = jnp.concatenate(oT_heads, axis=0).T.astype(o_ref.dtype)


IK_SLOT = 2


def _dsa_body(q_ref, iq_ref, iw_ref, ik_ref, kv_ref, vaT_ref, o_ref, sc_ref, acc_ref, sbuf_ref, *, tq, kc, topk):
    i = pl.program_id(1)
    q0 = i * tq
    nch = (q0 + tq + kc - 1) // kc
    qpos = q0 + lax.broadcasted_iota(jnp.int32, (1, tq), 1)
    iqT = _tile_T(iq_ref)
    iq_ops = [_place_head(*[p.astype(BF16) for p in _head_rows(iqT, h)], IK_SLOT) for h in range(A_IDX_HEADS)]
    w_idx = iw_ref[...].astype(F32).T[:A_IDX_HEADS]

    def score_chunk(c, carry, diagonal):
        mn, mx = carry
        k0 = pl.multiple_of(c * kc, kc)
        ikc = ik_ref[pl.ds(k0, kc), :]
        acc = jnp.zeros((kc, tq), F32)
        for h in range(A_IDX_HEADS):
            d = jnp.dot(ikc, iq_ops[h], preferred_element_type=F32)
            acc = acc + w_idx[h:h + 1, :] * jnp.maximum(d, 0.0)
        if diagonal:
            causal = (k0 + lax.broadcasted_iota(jnp.int32, (kc, 1), 0)) <= qpos
            lo_part, hi_part = jnp.where(causal, acc, jnp.inf), jnp.where(causal, acc, -jnp.inf)
        else:
            lo_part = hi_part = acc
        sc_ref[pl.ds(k0, kc), :] = hi_part
        return jnp.minimum(mn, _fold8(lo_part, jnp.min)), jnp.maximum(mx, _fold8(hi_part, jnp.max))

    n_full = (q0 + 1) // kc
    carry = lax.fori_loop(0, n_full, functools.partial(score_chunk, diagonal=False),
                          (jnp.full((SUBLANES, tq), jnp.inf, F32), jnp.full((SUBLANES, tq), -jnp.inf, F32)))
    mn8, mx8 = lax.fori_loop(n_full, nch, functools.partial(score_chunk, diagonal=True), carry)
    row_min = jnp.min(mn8, axis=0, keepdims=True)
    row_max = jnp.max(mx8, axis=0, keepdims=True)

    pc = 2 * kc
    npair = (nch + 1) // 2
    n_chunks_total = vaT_ref.shape[1]

    @pl.when(nch % 2 == 1)
    def _():
        sc_ref[pl.ds(pl.multiple_of(nch * kc, kc), kc), :] = jnp.full((kc, tq), -jnp.inf, F32)

    def count_ge(thr):
        def body(j, cnt):
            s = sc_ref[pl.ds(pl.multiple_of(j * pc, pc), pc), :]
            return cnt + _fold8(jnp.where(s >= thr, 1.0, 0.0), jnp.sum)
        cnt8 = lax.fori_loop(0, npair, body, jnp.zeros((SUBLANES, tq), F32))
        return jnp.sum(cnt8, axis=0, keepdims=True)

    def min_ge(thr):
        def body(j, mn):
            s = sc_ref[pl.ds(pl.multiple_of(j * pc, pc), pc), :]
            return jnp.minimum(mn, _fold8(jnp.where(s >= thr, s, jnp.inf), jnp.min))
        mn8 = lax.fori_loop(0, npair, body, jnp.full((SUBLANES, tq), jnp.inf, F32))
        return jnp.min(mn8, axis=0, keepdims=True)

    def count_zero():
        def body(j, carry):
            c0, cp = carry
            s = sc_ref[pl.ds(pl.multiple_of(j * pc, pc), pc), :]
            return (c0 + _fold8(jnp.where(s >= 0.0, 1.0, 0.0), jnp.sum),
                    cp + _fold8(jnp.where(s > 0.0, 1.0, 0.0), jnp.sum))
        z8 = jnp.zeros((SUBLANES, tq), F32)
        c0, cp = lax.fori_loop(0, npair, body, (z8, z8))
        return jnp.sum(c0, axis=0, keepdims=True), jnp.sum(cp, axis=0, keepdims=True)

    kf = float(topk)
    n_valid = (qpos + 1).astype(F32)
    few = n_valid <= kf
    cnt_nonneg, cnt_pos = count_zero()
    is_zero = jnp.logical_and(cnt_pos < kf, cnt_nonneg >= kf)
    is_pos = cnt_pos >= kf
    lo0 = jnp.where(is_zero, 0, jnp.where(is_pos, MIN_NORMAL_KEY, _f32_to_key(row_min)))
    hi0 = jnp.where(is_zero, 1, jnp.where(is_pos, _f32_to_key(row_max) + 1, -1))
    cnt0 = jnp.where(is_zero, cnt_nonneg, jnp.where(is_pos, cnt_pos, n_valid))
    lo0 = jnp.where(few, _f32_to_key(row_min), lo0)
    hi0 = jnp.where(few, lo0 + 1, hi0)

    def unfinished(lo, hi, cnt_lo):
        return jnp.logical_and(lo + 1 != hi, cnt_lo != kf)

    def bis_cond(st):
        lo, hi, cnt_lo, it = st
        return jnp.logical_and(jnp.max(jnp.where(unfinished(lo, hi, cnt_lo), 1.0, 0.0)) > 0.0, it < MAX_SELECT_PASSES)

    def bisect(st):
        lo, hi, cnt_lo, it = st
        nd = unfinished(lo, hi, cnt_lo)
        mid_val = _f32_to_key(0.5 * _key_to_f32(lo) + 0.5 * _key_to_f32(hi))
        mid_key = lo + lax.shift_right_logical(hi - lo, 1)
        mid = jnp.where(it < VALUE_MID_PASSES, jnp.clip(mid_val, lo + 1, hi - 1), mid_key)
        cnt = count_ge(_key_to_f32(mid))
        ge = cnt >= kf
        up = jnp.logical_and(nd, ge)
        down = jnp.logical_and(nd, jnp.logical_not(ge))
        return jnp.where(up, mid, lo), jnp.where(down, mid, hi), jnp.where(up, cnt, cnt_lo), it + 1

    def peel(st):
        lo, hi, cnt_lo, it = st
        nd = unfinished(lo, hi, cnt_lo)
        k1 = jnp.maximum(_f32_to_key(min_ge(_key_to_f32(lo))), lo)
        cnt = count_ge(_key_to_f32(k1 + 1))
        ge = cnt >= kf
        up = jnp.logical_and(nd, ge)
        down = jnp.logical_and(nd, jnp.logical_not(ge))
        new_lo = jnp.where(nd, jnp.where(ge, k1 + 1, k1), lo)
        return new_lo, jnp.where(down, k1 + 1, hi), jnp.where(up, cnt, cnt_lo), it + 2

    def bis_body(st):
        lo, hi, cnt_lo, it = st
        excess = jnp.max(jnp.where(unfinished(lo, hi, cnt_lo), cnt_lo - kf, 0.0))
        return lax.cond(excess > PEEL_MAX, bisect, peel, st)

    lo, _, cnt_lo, _ = lax.while_loop(bis_cond, bis_body, (lo0, hi0, cnt0, jnp.int32(0)))
    thr = _key_to_f32(lo)

    tie = jnp.logical_and(cnt_lo > kf, jnp.logical_not(few))

    @pl.when(jnp.max(jnp.where(tie, 1.0, 0.0)) > 0.0)
    def _():
        def count_gt():
            def gt_body(c, cnt):
                k0 = pl.multiple_of(c * kc, kc)
                s = sc_ref[pl.ds(k0, kc), :]
                return cnt + _fold8(jnp.where(s > thr, 1.0, 0.0), jnp.sum)
            gt8 = lax.fori_loop(0, nch, gt_body, jnp.zeros((SUBLANES, tq), F32))
            return jnp.sum(gt8, axis=0, keepdims=True)

        zero_thr = thr == 0.0
        other = jnp.max(jnp.where(jnp.logical_and(tie, jnp.logical_not(zero_thr)), 1.0, 0.0)) > 0.0
        gt = lax.cond(other, count_gt, lambda: cnt_pos)
        need = kf - jnp.where(zero_thr, cnt_pos, gt)
        r = lax.broadcasted_iota(jnp.int32, (kc, kc), 0)
        cidx = lax.broadcasted_iota(jnp.int32, (kc, kc), 1)
        tri = jnp.where(cidx <= r, 1.0, 0.0).astype(BF16)

        def tie_body(c, seen):
            k0 = pl.multiple_of(c * kc, kc)
            s = sc_ref[pl.ds(k0, kc), :]
            eqf = jnp.where(s == thr, jnp.where(tie, 1.0, 0.0), 0.0)
            pre = jnp.dot(tri, eqf.astype(BF16), preferred_element_type=F32) + seen
            drop = eqf * jnp.where(pre > need, 1.0, 0.0)
            sc_ref[pl.ds(k0, kc), :] = jnp.where(drop > 0.0, -jnp.inf, s)
            return seen + jnp.sum(eqf, axis=0, keepdims=True)

        lax.fori_loop(0, nch, tie_body, jnp.zeros((1, tq), F32))

    qT_all = (_tile_T(q_ref) * Q_SCALE).astype(BF16)
    qT = jnp.concatenate([_place_head(*_head_rows(qT_all, h), 0) for h in range(A_HEADS)], axis=1)
    acc_ref[...] = jnp.zeros_like(acc_ref)
    ones_rows = jnp.ones((SUBLANES, kc), BF16)

    def issue_scores(c, slot):
        k0 = pl.multiple_of(jnp.minimum(c, n_chunks_total - 1) * kc, kc)
        sbuf_ref[slot] = jnp.dot(kv_ref[pl.ds(k0, kc), :], qT, preferred_element_type=F32)

    def softmax_pv(c, slot, m):
        k0 = pl.multiple_of(c * kc, kc)
        v1 = jnp.concatenate([vaT_ref[0, c], ones_rows], axis=0)
        sel = sc_ref[pl.ds(k0, kc), :] >= thr
        s = jnp.concatenate([jnp.where(sel, sbuf_ref[slot, :, h * tq:(h + 1) * tq], NEG_BIG)
                             for h in range(A_HEADS)], axis=1)
        m_new = jnp.maximum(m, jnp.max(s, axis=0, keepdims=True))
        alpha = jnp.exp2(m - m_new)
        p = jnp.exp2((s - m_new).astype(BF16))
        acc_ref[...] = acc_ref[...] * alpha + jnp.dot(v1, p, preferred_element_type=F32)
        return m_new

    def chunk_pair(j, m):
        c0 = 2 * j
        issue_scores(c0 + 1, 1)
        m = softmax_pv(c0, 0, m)
        issue_scores(c0 + 2, 0)
        return softmax_pv(c0 + 1, 1, m)

    issue_scores(0, 0)
    lax.fori_loop(0, npair, chunk_pair, jnp.full((1, A_HEADS * tq), NEG_BIG, F32))
    a = acc_ref[...]
    o = a[:HEAD_DIM] / a[HEAD_DIM:HEAD_DIM + 1]
    _store_rows(o_ref, [o[:, h * tq:(h + 1) * tq] for h in range(A_HEADS)])


def _dsa(h_rope, h_plain, kv, vaT4, *, bsz, seq, tq, kc, topk):
    nq = seq // tq
    return pl.pallas_call(
        functools.partial(_dsa_body, tq=tq, kc=kc, topk=topk),
        grid=(bsz, nq),
        in_specs=[
            pl.BlockSpec((tq, PROJ_TN), lambda b, i: (b * nq + i, TILE_AQ)),
            pl.BlockSpec((tq, PROJ_TN), lambda b, i: (b * nq + i, TILE_AIQ)),
            pl.BlockSpec((tq, LANES), lambda b, i: (b * nq + i, COL_AIW // LANES)),
            pl.BlockSpec((seq, PROJ_TN), lambda b, i: (b, TILE_CK_AIK)),
            pl.BlockSpec((seq, PROJ_TN), lambda b, i: (b, 0)),
            pl.BlockSpec((1, seq // kc, HEAD_DIM, kc), lambda b, i: (b, 0, 0, 0)),
        ],
        out_specs=pl.BlockSpec((tq, A_W), lambda b, i: (b * nq + i, 0)),
        out_shape=jax.ShapeDtypeStruct((bsz * seq, A_W), BF16),
        scratch_shapes=[pltpu.VMEM((seq + kc, tq), F32),
                        pltpu.VMEM((HEAD_DIM + SUBLANES, A_HEADS * tq), F32),
                        pltpu.VMEM((2, kc, A_HEADS * tq), F32)],
        compiler_params=_cparams(2),
        name="dsa_attention",
    )(h_rope, h_rope, h_plain, h_rope, kv, vaT4)


def _kmean_body(k_ref, o_ref):
    o_ref[0] = jnp.mean(k_ref[...].astype(F32), axis=0, keepdims=True)


def _moba_kmean(h_rope, *, n_tokens):
    nblk = n_tokens // B_BLOCK
    return pl.pallas_call(
        _kmean_body,
        grid=(nblk,),
        in_specs=[pl.BlockSpec((B_BLOCK, B_W), lambda i: (i, TILE_BK))],
        out_specs=pl.BlockSpec((1, 1, B_W), lambda i: (i, 0, 0)),
        out_shape=jax.ShapeDtypeStruct((nblk, 1, B_W), F32),
        compiler_params=_cparams(1),
        name="moba_kmean",
    )(h_rope)


def _moba_gate_body(q_ref, km_ref, sel_ref, *, tq, nkb, nsel):
    i = pl.program_id(1)
    qpos = i * tq + lax.broadcasted_iota(jnp.int32, (1, tq), 1)
    cur = qpos // B_BLOCK
    row = lax.broadcasted_iota(jnp.int32, (nkb, tq), 0)
    past = row < cur
    qT = _tile_T(q_ref).astype(BF16)
    km = km_ref[0]
    col_head = (lax.broadcasted_iota(jnp.int32, km.shape, 1) % HALF_TN) // (HEAD_DIM // 2)
    for h in range(B_HEADS):
        km_h = jnp.where(col_head == h, km, 0.0).astype(BF16)
        g = jnp.dot(km_h, qT, preferred_element_type=F32)
        g = jnp.where(past, g, -jnp.inf)
        sel = jnp.zeros((nkb, tq), F32)
        for _ in range(nsel):
            mx = jnp.max(g, axis=0, keepdims=True)
            idx = jnp.min(jnp.where(g == mx, row, nkb), axis=0, keepdims=True)
            pick = row == idx
            sel = jnp.where(pick, 1.0, sel)
            g = jnp.where(pick, -jnp.inf, g)
        sel_ref[0, h] = jnp.where(past, sel, 0.0)


def _moba_gate(h_rope, kmean, *, bsz, seq, tq):
    nkb = seq // B_BLOCK
    nsel = min(B_TOPK_BLOCKS, nkb)
    nq = seq // tq
    return pl.pallas_call(
        functools.partial(_moba_gate_body, tq=tq, nkb=nkb, nsel=nsel),
        grid=(bsz, nq),
        in_specs=[
            pl.BlockSpec((tq, PROJ_TN), lambda b, i: (b * nq + i, TILE_BQ)),
            pl.BlockSpec((1, nkb, B_W), lambda b, i: (b, 0, 0)),
        ],
        out_specs=pl.BlockSpec((1, B_HEADS, nkb, tq), lambda b, i: (b, 0, 0, i)),
        out_shape=jax.ShapeDtypeStruct((bsz, B_HEADS, nkb, seq), F32),
        compiler_params=_cparams(2),
        name="moba_gate",
    )(h_rope, kmean)


def _moba_body(q_ref, k_ref, vT_ref, sel_ref, o_ref, acc_ref, sbuf_ref, *, tq):
    i = pl.program_id(1)
    qT_all = (_tile_T(q_ref) * Q_SCALE).astype(BF16)
    qT = [_place_head(*_head_rows(qT_all, h), h) for h in range(B_HEADS)]
    ones_rows = jnp.ones((SUBLANES, B_BLOCK), BF16)
    n_past = i

    def values(h, j):
        return jnp.concatenate([vT_ref[0, h, j], ones_rows], axis=0)

    def issue_scores(j, slot):
        k0 = pl.multiple_of(jnp.minimum(j, i) * B_BLOCK, B_BLOCK)
        for h in range(B_HEADS):
            sbuf_ref[slot, h] = jnp.dot(k_ref[pl.ds(k0, B_BLOCK), :], qT[h], preferred_element_type=F32)

    def softmax_pv(j, slot, ms):
        ps, alphas, new_ms = [], [], []
        for h in range(B_HEADS):
            s = jnp.where(sel_ref[0, h, pl.ds(j, 1), :] > 0.0, sbuf_ref[slot, h], NEG_BIG)
            m_new = jnp.maximum(ms[h], jnp.max(s, axis=0, keepdims=True))
            alphas.append(jnp.exp2(ms[h] - m_new))
            ps.append(jnp.exp2((s - m_new).astype(BF16)))
            new_ms.append(m_new)
        for h in range(B_HEADS):
            acc_ref[h] = acc_ref[h] * alphas[h] + jnp.dot(values(h, j), ps[h], preferred_element_type=F32)
        return tuple(new_ms)

    own0 = pl.multiple_of(i * B_BLOCK, B_BLOCK)
    s_own = [jnp.dot(k_ref[pl.ds(own0, B_BLOCK), :], qT[h], preferred_element_type=F32) for h in range(B_HEADS)]
    issue_scores(0, 0)
    causal = lax.broadcasted_iota(jnp.int32, (B_BLOCK, tq), 0) <= lax.broadcasted_iota(jnp.int32, (B_BLOCK, tq), 1)
    ms = []
    for h in range(B_HEADS):
        s = jnp.where(causal, s_own[h], NEG_BIG)
        m0 = jnp.max(s, axis=0, keepdims=True)
        acc_ref[h] = jnp.dot(values(h, i), jnp.exp2((s - m0).astype(BF16)), preferred_element_type=F32)
        ms.append(m0)

    def block_pair(jj, ms):
        j0 = 2 * jj
        issue_scores(j0 + 1, 1)
        ms = softmax_pv(j0, 0, ms)
        issue_scores(j0 + 2, 0)
        return softmax_pv(j0 + 1, 1, ms)

    lax.fori_loop(0, (n_past + 1) // 2, block_pair, tuple(ms))
    outs = []
    for h in range(B_HEADS):
        a = acc_ref[h]
        outs.append(a[:HEAD_DIM] / a[HEAD_DIM:HEAD_DIM + 1])
    _store_rows(o_ref, outs)


def _moba(h_rope, bvT5, sel, *, bsz, seq):
    tq = B_BLOCK
    nkb = seq // B_BLOCK
    nq = seq // tq
    return pl.pallas_call(
        functools.partial(_moba_body, tq=tq),
        grid=(bsz, nq),
        in_specs=[
            pl.BlockSpec((tq, PROJ_TN), lambda b, i: (b * nq + i, TILE_BQ)),
            pl.BlockSpec((seq, PROJ_TN), lambda b, i: (b, TILE_BK)),
            pl.BlockSpec((1, B_HEADS, nkb, HEAD_DIM, B_BLOCK), lambda b, i: (b, 0, 0, 0, 0)),
            pl.BlockSpec((1, B_HEADS, nkb, tq), lambda b, i: (b, 0, 0, i)),
        ],
        out_specs=pl.BlockSpec((tq, B_W), lambda b, i: (b * nq + i, 0)),
        out_shape=jax.ShapeDtypeStruct((bsz * seq, B_W), BF16),
        scratch_shapes=[pltpu.VMEM((B_HEADS, HEAD_DIM + SUBLANES, tq), F32),
                        pltpu.VMEM((2, B_HEADS, B_BLOCK, tq), F32)],
        compiler_params=_cparams(2),
        name="moba_attention",
    )(h_rope, h_rope, bvT5, sel)


def _swa_body(sink_ref, q0_ref, q1_ref, k0_ref, k1_ref, k2_ref, v0_ref, v1_ref, v2_ref, o_ref, *, tq):
    i = pl.program_id(1)
    q0 = i * tq
    qpos = q0 + lax.broadcasted_iota(jnp.int32, (1, tq), 1)
    nk = tq + C_WINDOW
    kpos = q0 - C_WINDOW + lax.broadcasted_iota(jnp.int32, (nk, 1), 0)
    diff = qpos - kpos
    ok = jnp.where(diff >= 0, jnp.where(diff < C_WINDOW, jnp.where(kpos >= 0, 1.0, 0.0), 0.0), 0.0) > 0.0
    ones_rows = jnp.ones((SUBLANES, nk), BF16)
    group = C_HEADS // C_KV_HEADS
    assert group == HEADS_PER_TILE
    kk = jnp.concatenate([k0_ref[...], k1_ref[...], k2_ref[...]], axis=0)
    outs = []
    for kv, q_ref in enumerate((q0_ref, q1_ref)):
        vv = jnp.concatenate([v0_ref[0, kv], v1_ref[0, kv], v2_ref[0, kv]], axis=1)
        v1 = jnp.concatenate([vv, ones_rows], axis=0)
        heads = [kv * group + g for g in range(group)]
        qT_all = (_tile_T(q_ref) * Q_SCALE).astype(BF16)
        qT = jnp.concatenate([_place_head(*_head_rows(qT_all, g), kv) for g in range(group)], axis=1)
        sink = jnp.concatenate([jnp.full((1, tq), sink_ref[hd] * LOG2E, F32) for hd in heads], axis=1)
        s = jnp.dot(kk, qT, preferred_element_type=F32)
        s = jnp.concatenate([jnp.where(ok, s[:, g * tq:(g + 1) * tq], NEG_BIG) for g in range(group)], axis=1)
        m = jnp.maximum(jnp.max(s, axis=0, keepdims=True), sink)
        p = jnp.exp2((s - m).astype(BF16))
        a = jnp.dot(v1, p, preferred_element_type=F32)
        o = a[:HEAD_DIM] / (a[HEAD_DIM:HEAD_DIM + 1] + jnp.exp2(sink - m))
        outs += [o[:, g * tq:(g + 1) * tq] for g in range(group)]
    _store_rows(o_ref, outs)


def _swa(sinks, h_rope, cvT, *, bsz, seq, tq):
    r = tq // C_WINDOW
    assert r == 2
    nq = seq // tq
    nwb = seq // C_WINDOW

    def kspec(off):
        return pl.BlockSpec((C_WINDOW, PROJ_TN),
                            lambda b, i, s: (b * nwb + jnp.maximum(i * r + off, 0), TILE_CK_AIK))

    def vspec(off):
        return pl.BlockSpec((1, C_KV_HEADS, HEAD_DIM, C_WINDOW),
                            lambda b, i, s: (b, 0, 0, jnp.maximum(i * r + off, 0)))

    grid_spec = pltpu.PrefetchScalarGridSpec(
        num_scalar_prefetch=1,
        grid=(bsz, nq),
        in_specs=[pl.BlockSpec((tq, PROJ_TN), lambda b, i, s: (b * nq + i, TILE_CQ)),
                  pl.BlockSpec((tq, PROJ_TN), lambda b, i, s: (b * nq + i, TILE_CQ + 1)),
                  kspec(-1), kspec(0), kspec(1), vspec(-1), vspec(0), vspec(1)],
        out_specs=pl.BlockSpec((tq, C_W), lambda b, i, s: (b * nq + i, 0)),
    )
    return pl.pallas_call(
        functools.partial(_swa_body, tq=tq),
        grid_spec=grid_spec,
        out_shape=jax.ShapeDtypeStruct((bsz * seq, C_W), BF16),
        compiler_params=_cparams(2),
        name="swa_attention",
    )(sinks, h_rope, h_rope, h_rope, h_rope, h_rope, cvT, cvT, cvT)


def _layer_norm(z, g, b):
    mu = jnp.mean(z, axis=-1, keepdims=True)
    zc = z - mu
    var = jnp.mean(zc * zc, axis=-1, keepdims=True)
    return zc * lax.rsqrt(var + LN_EPS) * g + b


def _split_bf16(a):
    hi = a.astype(BF16)
    lo = (a - hi.astype(F32)).astype(BF16)
    return hi, lo


def _router(x1, rw_ref, rb_ref, cls_ref, wt_ref):
    nt = (((1,), (1,)), ((), ()))
    xh, xl = _split_bf16(x1)
    wh, wl = _split_bf16(rw_ref[...])
    logits = (lax.dot_general(wh, xh, nt, preferred_element_type=F32)
              + lax.dot_general(wh, xl, nt, preferred_element_type=F32)
              + lax.dot_general(wl, xh, nt, preferred_element_type=F32))
    aff = jax.nn.sigmoid(logits)
    score = aff + rb_ref[...]
    tm = x1.shape[0]
    sc = [score[e:e + 1, :] for e in range(N_EXPERTS)]
    af = [aff[e:e + 1, :] for e in range(N_EXPERTS)]
    in_top = []
    grp_score = []
    for gq in range(N_GROUPS):
        gs = jnp.zeros((1, tm), F32)
        for a in range(EXPERTS_PER_GROUP):
            ea = gq * EXPERTS_PER_GROUP + a
            rank = jnp.zeros((1, tm), F32)
            for b in range(EXPERTS_PER_GROUP):
                if b == a:
                    continue
                eb = gq * EXPERTS_PER_GROUP + b
                beats = (sc[eb] >= sc[ea]) if b < a else (sc[eb] > sc[ea])
                rank = rank + jnp.where(beats, 1.0, 0.0)
            top = rank < 2.0
            in_top.append(top)
            gs = gs + jnp.where(top, sc[ea], 0.0)
        grp_score.append(gs)
    best = grp_score[0]
    gstar = jnp.zeros((1, tm), jnp.int32)
    for gq in range(1, N_GROUPS):
        better = grp_score[gq] > best
        best = jnp.where(better, grp_score[gq], best)
        gstar = jnp.where(better, gq, gstar)
    cls = jnp.zeros((1, tm), jnp.int32)
    w_lo = jnp.zeros((1, tm), F32)
    w_hi = jnp.zeros((1, tm), F32)
    for gq in range(N_GROUPS):
        is_g = gstar == gq
        for pi, (a, b) in enumerate(PAIRS):
            ea, eb = gq * EXPERTS_PER_GROUP + a, gq * EXPERTS_PER_GROUP + b
            hit = jnp.where(is_g, jnp.where(in_top[ea], jnp.where(in_top[eb], 1.0, 0.0), 0.0), 0.0) > 0.0
            cls = jnp.where(hit, gq * len(PAIRS) + pi, cls)
            tot = af[ea] + af[eb]
            w_lo = jnp.where(hit, af[ea] / tot, w_lo)
            w_hi = jnp.where(hit, af[eb] / tot, w_hi)
    cls_ref[...] = cls
    wt_ref[0:1, :] = w_lo
    wt_ref[1:2, :] = w_hi


def _merge_body(oa_ref, ob_ref, oc_ref, ga_ref, gb_ref, gc_ref, x_ref, wa_ref, wb_ref, wc_ref, wo_ref,
                g_ref, b_ref, rw_ref, rb_ref, x1_ref, cls_ref, wt_ref):
    def branch(o_ref, w_ref, gate_ref):
        y = jnp.dot(o_ref[...], w_ref[...], preferred_element_type=F32)
        return jax.nn.sigmoid(gate_ref[...].astype(F32)) * y

    merged = branch(oa_ref, wa_ref, ga_ref) + branch(ob_ref, wb_ref, gb_ref) + branch(oc_ref, wc_ref, gc_ref)
    y = jnp.dot(merged.astype(BF16), wo_ref[...], preferred_element_type=F32)
    x1 = _layer_norm(DN_ALPHA * x_ref[...] + y, g_ref[...], b_ref[...])
    x1_ref[...] = x1
    _router(x1, rw_ref, rb_ref, cls_ref, wt_ref)


def _merge(oa, ob, oc, h_plain, x, wa, wb, wc, wo, ln_g, ln_b, rwT, rb, *, tm):
    t_tokens = x.shape[0]
    gate_blk = COL_GATES // D_MODEL
    full = lambda shape: pl.BlockSpec(shape, lambda i: (0,) * len(shape))
    return pl.pallas_call(
        _merge_body,
        grid=(t_tokens // tm,),
        in_specs=[
            pl.BlockSpec((tm, A_W), lambda i: (i, 0)),
            pl.BlockSpec((tm, B_W), lambda i: (i, 0)),
            pl.BlockSpec((tm, C_W), lambda i: (i, 0)),
            pl.BlockSpec((tm, D_MODEL), lambda i: (i, gate_blk)),
            pl.BlockSpec((tm, D_MODEL), lambda i: (i, gate_blk + 1)),
            pl.BlockSpec((tm, D_MODEL), lambda i: (i, gate_blk + 2)),
            pl.BlockSpec((tm, D_MODEL), lambda i: (i, 0)),
            full((A_W, D_MODEL)), full((B_W, D_MODEL)), full((C_W, D_MODEL)), full((D_MODEL, D_MODEL)),
            full((1, D_MODEL)), full((1, D_MODEL)), full((N_EXPERTS, D_MODEL)), full((N_EXPERTS, 1)),
        ],
        out_specs=[
            pl.BlockSpec((tm, D_MODEL), lambda i: (i, 0)),
            pl.BlockSpec((1, tm), lambda i: (0, i)),
            pl.BlockSpec((2, tm), lambda i: (0, i)),
        ],
        out_shape=[
            jax.ShapeDtypeStruct((t_tokens, D_MODEL), F32),
            jax.ShapeDtypeStruct((1, t_tokens), jnp.int32),
            jax.ShapeDtypeStruct((2, t_tokens), F32),
        ],
        compiler_params=_cparams(1),
        name="merge_ln_router",
    )(oa, ob, oc, h_plain, h_plain, h_plain, x, wa, wb, wc, wo, ln_g, ln_b, rwT, rb)


def _gather_start(idx_ref, src_hbm, dst_ref, sem, n_rows):
    def start(r, _):
        pltpu.make_async_copy(src_hbm.at[pl.ds(idx_ref[0, 0, r], 1)], dst_ref.at[pl.ds(r, 1)], sem).start()
        return 0

    lax.fori_loop(0, n_rows, start, 0, unroll=8)


def _gather_wait(src_hbm, dst_ref, sem, n_rows):
    pltpu.make_async_copy(src_hbm.at[pl.ds(0, n_rows)], dst_ref, sem).wait()


def _experts_body(e1_ref, e2_ref, nused_ref, src_ref, src_next_ref, x_hbm, wt_ref,
                  wg1_ref, wu1_ref, wd1_ref, wg2_ref, wu2_ref, wd2_ref, o_ref, xbuf, sem):
    s = pl.program_id(0)
    nused = nused_ref[0]
    slot = lax.rem(s, 2)

    @pl.when(s == 0)
    def _():
        _gather_start(src_ref, x_hbm, xbuf.at[0], sem.at[0], MOE_TILE)

    @pl.when(s + 1 < nused)
    def _():
        _gather_start(src_next_ref, x_hbm, xbuf.at[1 - slot], sem.at[1 - slot], MOE_TILE)

    @pl.when(s < nused)
    def _():
        _gather_wait(x_hbm, xbuf.at[slot], sem.at[slot], MOE_TILE)
        xb = xbuf[slot].astype(BF16)
        wt = wt_ref[...]

        def expert(wg_ref, wu_ref, wd_ref):
            g = jnp.dot(xb, wg_ref[0], preferred_element_type=F32)
            u = jnp.dot(xb, wu_ref[0], preferred_element_type=F32)
            he = (g * jax.nn.sigmoid(g) * u).astype(BF16)
            return jnp.dot(he, wd_ref[0], preferred_element_type=F32)

        o_ref[...] = wt[:, 0:1] * expert(wg1_ref, wu1_ref, wd1_ref) + wt[:, 1:2] * expert(wg2_ref, wu2_ref, wd2_ref)

    @pl.when(s >= nused)
    def _():
        o_ref[...] = jnp.zeros_like(o_ref)


def _experts(tile_e1, tile_e2, nused, src3, x1, wsort, wg, wu, wd):
    ntiles = src3.shape[0]
    wspec_up = lambda which: pl.BlockSpec((1, D_MODEL, D_EXPERT), lambda s, e1, e2, nu: ((e1, e2)[which][s], 0, 0))
    wspec_dn = lambda which: pl.BlockSpec((1, D_EXPERT, D_MODEL), lambda s, e1, e2, nu: ((e1, e2)[which][s], 0, 0))
    grid_spec = pltpu.PrefetchScalarGridSpec(
        num_scalar_prefetch=3,
        grid=(ntiles,),
        in_specs=[
            pl.BlockSpec((1, 1, MOE_TILE), lambda s, e1, e2, nu: (s, 0, 0), memory_space=pltpu.SMEM),
            pl.BlockSpec((1, 1, MOE_TILE), lambda s, e1, e2, nu: (jnp.minimum(s + 1, ntiles - 1), 0, 0),
                         memory_space=pltpu.SMEM),
            pl.BlockSpec(memory_space=pl.ANY),
            pl.BlockSpec((MOE_TILE, 2), lambda s, e1, e2, nu: (s, 0)),
            wspec_up(0), wspec_up(0), wspec_dn(0), wspec_up(1), wspec_up(1), wspec_dn(1),
        ],
        out_specs=pl.BlockSpec((MOE_TILE, D_MODEL), lambda s, e1, e2, nu: (s, 0)),
        scratch_shapes=[pltpu.VMEM((2, MOE_TILE, D_MODEL), F32), pltpu.SemaphoreType.DMA((2,))],
    )
    return pl.pallas_call(
        _experts_body,
        grid_spec=grid_spec,
        out_shape=jax.ShapeDtypeStruct((ntiles * MOE_TILE, D_MODEL), F32),
        compiler_params=_cparams(1),
        name="moe_experts",
    )(tile_e1, tile_e2, nused, src3, src3, x1, wsort, wg, wu, wd, wg, wu, wd)


def _final_body(pos_ref, pos_next_ref, y_hbm, x_ref, g_ref, b_ref, o_ref, ob_ref, ybuf, sem, *, tm):
    s = pl.program_id(0)
    slot = lax.rem(s, 2)

    @pl.when(s == 0)
    def _():
        _gather_start(pos_ref, y_hbm, ybuf.at[0], sem.at[0], tm)

    @pl.when(s + 1 < pl.num_programs(0))
    def _():
        _gather_start(pos_next_ref, y_hbm, ybuf.at[1 - slot], sem.at[1 - slot], tm)

    _gather_wait(y_hbm, ybuf.at[slot], sem.at[slot], tm)
    x2 = _layer_norm(DN_ALPHA * x_ref[...] + ybuf[slot], g_ref[...], b_ref[...])
    o_ref[...] = x2
    ob_ref[...] = x2.astype(BF16)


def _final(pos3, y_sorted, x1, ln_g, ln_b, *, tm):
    t_tokens = x1.shape[0]
    nsteps = t_tokens // tm
    return pl.pallas_call(
        functools.partial(_final_body, tm=tm),
        grid=(nsteps,),
        in_specs=[
            pl.BlockSpec((1, 1, tm), lambda i: (i, 0, 0), memory_space=pltpu.SMEM),
            pl.BlockSpec((1, 1, tm), lambda i: (jnp.minimum(i + 1, nsteps - 1), 0, 0), memory_space=pltpu.SMEM),
            pl.BlockSpec(memory_space=pl.ANY),
            pl.BlockSpec((tm, D_MODEL), lambda i: (i, 0)),
            pl.BlockSpec((1, D_MODEL), lambda i: (0, 0)),
            pl.BlockSpec((1, D_MODEL), lambda i: (0, 0)),
        ],
        out_specs=[pl.BlockSpec((tm, D_MODEL), lambda i: (i, 0)), pl.BlockSpec((tm, D_MODEL), lambda i: (i, 0))],
        out_shape=[jax.ShapeDtypeStruct((t_tokens, D_MODEL), F32), jax.ShapeDtypeStruct((t_tokens, D_MODEL), BF16)],
        scratch_shapes=[pltpu.VMEM((2, tm, D_MODEL), F32), pltpu.SemaphoreType.DMA((2,))],
        compiler_params=_cparams(1),
        name="moe_combine_ln",
    )(pos3, pos3, y_sorted, x1, ln_g, ln_b)


def _rope_tables(seq):
    inv = 1.0 / (ROPE_THETA ** (jnp.arange(0, HEAD_DIM, 2, dtype=F32) / HEAD_DIM))
    ang = jnp.arange(seq, dtype=F32)[:, None] * inv[None, :]
    return jnp.cos(ang), jnp.sin(ang)


def _halves_layout(w_heads):
    d, n = w_heads.shape
    t = w_heads.reshape(d, n // PROJ_TN, HEADS_PER_TILE, 2, HEAD_DIM // 2)
    return t.transpose(0, 1, 3, 2, 4).reshape(d, n)


def _reorder_w_in(w):
    pts = np.cumsum((0,) + IN_SIZES)
    sec = [w[:, pts[k]:pts[k + 1]] for k in range(len(IN_SIZES))]
    a_q, a_c, a_iq, a_ik, a_iw, b_q, b_k, b_v, c_q, c_k, c_v, gates = sec
    zeros = lambda n: jnp.zeros((w.shape[0], n), w.dtype)
    w_rope = _halves_layout(jnp.concatenate([a_q, a_iq, b_q, b_k, c_q, c_k, a_ik, zeros(HEAD_DIM)], axis=1))
    w_plain = jnp.concatenate([gates, b_v, a_c, a_iw, zeros(LANES - A_IDX_HEADS), c_v, zeros(LANES)], axis=1)
    assert w_rope.shape[1] == N_ROPE and w_plain.shape[1] == N_PLAIN
    return w_rope.astype(BF16), w_plain.astype(BF16)


def _kv_weights(w_uk, w_uv):
    half = HEAD_DIM // 2
    zeros = lambda n: jnp.zeros((w_uk.shape[0], n), w_uk.dtype)
    return jnp.concatenate([w_uk[:, :half], w_uv, zeros(HALF_TN - half - HEAD_DIM),
                            w_uk[:, half:], zeros(HALF_TN - half)], axis=1).astype(BF16)


def _moe_plan(cls, wts, n_tokens):
    ntiles = n_tokens // MOE_TILE + N_CLASSES
    onehot = (cls[:, None] == jnp.arange(N_CLASSES, dtype=jnp.int32)[None, :]).astype(jnp.int32)
    csum = jnp.cumsum(onehot, axis=0)
    rank = jnp.sum(onehot * csum, axis=1) - 1
    counts = csum[-1]
    ptiles = (counts + MOE_TILE - 1) // MOE_TILE
    tile_end = jnp.cumsum(ptiles)
    tile_start = tile_end - ptiles
    pos = (tile_start[cls] * MOE_TILE + rank).astype(jnp.int32)
    nused = tile_end[-1:].astype(jnp.int32)
    tile_ids = jnp.arange(ntiles, dtype=jnp.int32)
    tile_cls = jnp.minimum(jnp.sum((tile_ids[:, None] >= tile_end[None, :]).astype(jnp.int32), axis=1),
                           N_CLASSES - 1).astype(jnp.int32)
    pair = np.array(PAIRS, dtype=np.int32)
    e_lo = jnp.asarray(np.repeat(np.arange(N_GROUPS), len(PAIRS)) * EXPERTS_PER_GROUP + np.tile(pair[:, 0], N_GROUPS), jnp.int32)
    e_hi = jnp.asarray(np.repeat(np.arange(N_GROUPS), len(PAIRS)) * EXPERTS_PER_GROUP + np.tile(pair[:, 1], N_GROUPS), jnp.int32)
    src = jnp.zeros((ntiles * MOE_TILE,), jnp.int32).at[pos].set(jnp.arange(n_tokens, dtype=jnp.int32))
    wsort = jnp.zeros((ntiles * MOE_TILE, 2), F32).at[pos].set(wts.T)
    return e_lo[tile_cls], e_hi[tile_cls], nused, src.reshape(ntiles, 1, MOE_TILE), pos, wsort


def kernel(x, w_in, a_w_uk, a_w_uv, c_sinks, w_branch, w_o, ln1_g, ln1_b, router_w, router_b,
           moe_w_gate, moe_w_up, moe_w_down, ln2_g, ln2_b):
    bsz, seq, _ = x.shape
    n_tokens = bsz * seq
    topk = min(A_TOPK_MAX, seq // 4)
    nkb = seq // B_BLOCK
    tq = 256
    kc = 256
    tm_proj = min(1024, seq)
    tm_row = 512
    half = HEAD_DIM // 2

    cos, sin = _rope_tables(seq)
    cos_in = jnp.tile(cos, (1, HEADS_PER_TILE))
    sin_in = jnp.tile(sin, (1, HEADS_PER_TILE))
    cos_kv = jnp.concatenate([cos, jnp.ones((seq, HALF_TN - half), F32)], axis=1)
    sin_kv = jnp.concatenate([sin, jnp.zeros((seq, HALF_TN - half), F32)], axis=1)
    rwT = router_w.T
    rb = router_b.reshape(N_EXPERTS, 1)

    xf = x.reshape(n_tokens, D_MODEL)
    xb = xf.astype(BF16)
    for l in range(DEPTH):
        w_rope, w_plain = _reorder_w_in(w_in[l])
        h_rope = _proj_rope(xb, 0, D_MODEL, w_rope, cos_in, sin_in, tm=tm_proj, seq=seq)
        h_plain = _proj_plain(xb, w_plain, tm=tm_proj)
        kv = _proj_rope(h_plain, COL_AC // A_KV_RANK, A_KV_RANK, _kv_weights(a_w_uk[l], a_w_uv[l]),
                        cos_kv, sin_kv, tm=tm_proj, seq=seq)

        vaT4 = kv[:, KV_VA0:KV_VA0 + HEAD_DIM].reshape(bsz, seq // kc, kc, HEAD_DIM).transpose(0, 1, 3, 2)
        bvT5 = (h_plain[:, COL_BV:COL_BV + B_W].reshape(bsz, nkb, B_BLOCK, B_HEADS, HEAD_DIM)
                .transpose(0, 3, 1, 4, 2))
        cvT = (h_plain[:, COL_CV:COL_CV + C_KV_HEADS * HEAD_DIM].reshape(bsz, seq, C_KV_HEADS, HEAD_DIM)
               .transpose(0, 2, 3, 1))

        o_a = _dsa(h_rope, h_plain, kv, vaT4, bsz=bsz, seq=seq, tq=tq, kc=kc, topk=topk)
        kmean = _moba_kmean(h_rope, n_tokens=n_tokens).reshape(bsz, nkb, B_W)
        sel = _moba_gate(h_rope, kmean, bsz=bsz, seq=seq, tq=tq)
        o_b = _moba(h_rope, bvT5, sel, bsz=bsz, seq=seq)
        o_c = _swa(c_sinks[l], h_rope, cvT, bsz=bsz, seq=seq, tq=tq)

        wb_all = w_branch[l].astype(BF16)
        x1, cls, wts = _merge(o_a, o_b, o_c, h_plain, xf,
                              wb_all[:A_W], wb_all[A_W:A_W + B_W], wb_all[A_W + B_W:], w_o[l].astype(BF16),
                              ln1_g[l].reshape(1, D_MODEL), ln1_b[l].reshape(1, D_MODEL), rwT, rb, tm=tm_row)

        e1, e2, nused, src3, pos, wsort = _moe_plan(cls[0], wts, n_tokens)
        y_sorted = _experts(e1, e2, nused, src3, x1, wsort,
                            moe_w_gate[l].astype(BF16), moe_w_up[l].astype(BF16), moe_w_down[l].astype(BF16))
        xf, xb = _final(pos.reshape(n_tokens // tm_row, 1, tm_row), y_sorted, x1,
                        ln2_g[l].reshape(1, D_MODEL), ln2_b[l].reshape(1, D_MODEL), tm=tm_row)
    return xf.reshape(bsz, seq, D_MODEL)
```

```python
import functools
import math

import jax
import jax.numpy as jnp
import numpy as np
from jax import lax
from jax.experimental import pallas as pl
from jax.experimental.pallas import tpu as pltpu

D_MODEL = 1024
DEPTH = 2
HEAD_DIM = 64
ROPE_THETA = 10000.0
LN_EPS = 1e-5
A_HEADS = 4
A_KV_RANK = 128
A_IDX_HEADS = 4
A_IDX_DIM = 64
A_TOPK_MAX = 256
B_HEADS = 4
B_BLOCK = 256
B_TOPK_BLOCKS = 3
C_HEADS = 8
C_KV_HEADS = 2
C_WINDOW = 128
N_BRANCH = 3
A_W = A_HEADS * HEAD_DIM
B_W = B_HEADS * HEAD_DIM
C_W = C_HEADS * HEAD_DIM
IN_SIZES = (A_W, A_KV_RANK, A_IDX_HEADS * A_IDX_DIM, A_IDX_DIM, A_IDX_HEADS,
            B_W, B_W, B_W, C_W, C_KV_HEADS * HEAD_DIM, C_KV_HEADS * HEAD_DIM,
            N_BRANCH * D_MODEL)
N_EXPERTS = 16
N_GROUPS = 4
EXPERTS_PER_GROUP = 4
D_EXPERT = 512
DN_ALPHA = (2 * DEPTH) ** 0.25
ATTN_SCALE = HEAD_DIM ** -0.5
LOG2E = math.log2(math.e)
Q_SCALE = ATTN_SCALE * LOG2E

LANES = 128
SUBLANES = 8
VMEM_LIMIT_BYTES = 56 * 1024 * 1024

NEG_BIG = -1e30
BF16 = jnp.bfloat16
F32 = jnp.float32

PROJ_TN = 256
HALF_TN = PROJ_TN // 2
HEADS_PER_TILE = PROJ_TN // HEAD_DIM
TILE_AQ = 0
TILE_AIQ = 1
TILE_BQ = 2
TILE_BK = 3
TILE_CQ = 4
TILE_CK_AIK = 6
N_ROPE = 7 * PROJ_TN
COL_GATES = 0
COL_BV = 3072
COL_AC = 3328
COL_AIW = 3456
COL_CV = 3584
N_PLAIN = 3840
KV_VA0 = HEAD_DIM // 2

PAIRS = ((0, 1), (0, 2), (0, 3), (1, 2), (1, 3), (2, 3))
N_CLASSES = N_GROUPS * len(PAIRS)
MOE_TILE = 256

DSA_TQ = 512

VALUE_MID_PASSES = 16
PEEL_MAX = 2.0
MIN_NORMAL_KEY = 0x00800000
MAX_SELECT_PASSES = 96


def _cparams(n_axes):
    return pltpu.CompilerParams(dimension_semantics=("arbitrary",) * n_axes,
                                vmem_limit_bytes=VMEM_LIMIT_BYTES)


def _proj_rope_body(x_ref, w_ref, cos_ref, sin_ref, o_ref, *, n_sub):
    sub = x_ref.shape[0] // n_sub
    for mi in range(n_sub):
        rows = slice(mi * sub, (mi + 1) * sub)
        acc = jnp.dot(x_ref[rows, :], w_ref[...], preferred_element_type=F32)
        a1, a2 = acc[:, :HALF_TN], acc[:, HALF_TN:]
        c, sn = cos_ref[rows, :], sin_ref[rows, :]
        o_ref[rows, :HALF_TN] = (a1 * c - a2 * sn).astype(o_ref.dtype)
        o_ref[rows, HALF_TN:] = (a2 * c + a1 * sn).astype(o_ref.dtype)


def _proj_plain_body(x_ref, w_ref, o_ref, *, n_sub):
    sub = x_ref.shape[0] // n_sub
    for mi in range(n_sub):
        rows = slice(mi * sub, (mi + 1) * sub)
        o_ref[rows, :] = jnp.dot(x_ref[rows, :], w_ref[...], preferred_element_type=F32).astype(o_ref.dtype)


def _proj_rope(x, x_col_block, k_dim, w, cos_t, sin_t, *, tm, seq):
    t_tokens = x.shape[0]
    n = w.shape[1]
    pos_blocks = seq // tm
    return pl.pallas_call(
        functools.partial(_proj_rope_body, n_sub=max(tm // 256, 1)),
        grid=(t_tokens // tm, n // PROJ_TN),
        in_specs=[
            pl.BlockSpec((tm, k_dim), lambda i, j: (i, x_col_block)),
            pl.BlockSpec((k_dim, PROJ_TN), lambda i, j: (0, j)),
            pl.BlockSpec((tm, HALF_TN), lambda i, j: (i % pos_blocks, 0)),
            pl.BlockSpec((tm, HALF_TN), lambda i, j: (i % pos_blocks, 0)),
        ],
        out_specs=pl.BlockSpec((tm, PROJ_TN), lambda i, j: (i, j)),
        out_shape=jax.ShapeDtypeStruct((t_tokens, n), BF16),
        compiler_params=_cparams(2),
        name="proj_rope",
    )(x, w, cos_t, sin_t)


def _proj_plain(x, w, *, tm):
    t_tokens, k_dim = x.shape
    n = w.shape[1]
    return pl.pallas_call(
        functools.partial(_proj_plain_body, n_sub=max(tm // 256, 1)),
        grid=(t_tokens // tm, n // PROJ_TN),
        in_specs=[
            pl.BlockSpec((tm, k_dim), lambda i, j: (i, 0)),
            pl.BlockSpec((k_dim, PROJ_TN), lambda i, j: (0, j)),
        ],
        out_specs=pl.BlockSpec((tm, PROJ_TN), lambda i, j: (i, j)),
        out_shape=jax.ShapeDtypeStruct((t_tokens, n), BF16),
        compiler_params=_cparams(2),
        name="proj_plain",
    )(x, w)


def _f32_to_key(x):
    b = lax.bitcast_convert_type(x, jnp.int32)
    return jnp.where(b < 0, b ^ jnp.int32(0x7FFFFFFF), b)


def _key_to_f32(k):
    b = jnp.where(k < 0, k ^ jnp.int32(0x7FFFFFFF), k)
    return lax.bitcast_convert_type(b, F32)


def _fold8(x, op):
    return op(x.reshape(x.shape[0] // SUBLANES, SUBLANES, x.shape[1]), axis=0)


def _tile_T(x_ref):
    return x_ref[...].astype(F32).T


def _head_rows(xT, h):
    half = HEAD_DIM // 2
    return xT[h * half:(h + 1) * half], xT[HALF_TN + h * half:HALF_TN + (h + 1) * half]


def _place_head(first, second, slot):
    half = HEAD_DIM // 2
    w = first.shape[1]
    before, after = slot * half, HALF_TN - (slot + 1) * half
    pieces = []
    for part in (first, second):
        pieces += [jnp.zeros((before, w), part.dtype)] * (before > 0) + [part] + [jnp.zeros((after, w), part.dtype)] * (after > 0)
    return jnp.concatenate(pieces, axis=0)


def _store_rows(o_ref, oT_heads):
    o_ref[...] = jnp.concatenate(oT_heads, axis=0).T.astype(o_ref.dtype)


IK_SLOT = 2


def _dsa_body(q_ref, iq_ref, iw_ref, ik_ref, kv_ref, vaT_ref, o_ref, sc_ref, acc_ref, sbuf_ref, *, tq, kc, topk):
    i = pl.program_id(1)
    q0 = i * tq
    nch = (q0 + tq + kc - 1) // kc
    qpos = q0 + lax.broadcasted_iota(jnp.int32, (1, tq), 1)
    iqT = _tile_T(iq_ref)
    iq_ops = [_place_head(*[p.astype(BF16) for p in _head_rows(iqT, h)], IK_SLOT) for h in range(A_IDX_HEADS)]
    w_idx = iw_ref[...].astype(F32).T[:A_IDX_HEADS]

    def score_chunk(c, carry, diagonal):
        mn, mx = carry
        k0 = pl.multiple_of(c * kc, kc)
        ikc = ik_ref[pl.ds(k0, kc), :]
        acc = jnp.zeros((kc, tq), F32)
        for h in range(A_IDX_HEADS):
            d = jnp.dot(ikc, iq_ops[h], preferred_element_type=F32)
            acc = acc + w_idx[h:h + 1, :] * jnp.maximum(d, 0.0)
        if diagonal:
            causal = (k0 + lax.broadcasted_iota(jnp.int32, (kc, 1), 0)) <= qpos
            lo_part, hi_part = jnp.where(causal, acc, jnp.inf), jnp.where(causal, acc, -jnp.inf)
        else:
            lo_part = hi_part = acc
        sc_ref[pl.ds(k0, kc), :] = hi_part
        return jnp.minimum(mn, _fold8(lo_part, jnp.min)), jnp.maximum(mx, _fold8(hi_part, jnp.max))

    n_full = (q0 + 1) // kc
    carry = lax.fori_loop(0, n_full, functools.partial(score_chunk, diagonal=False),
                          (jnp.full((SUBLANES, tq), jnp.inf, F32), jnp.full((SUBLANES, tq), -jnp.inf, F32)))
    mn8, mx8 = lax.fori_loop(n_full, nch, functools.partial(score_chunk, diagonal=True), carry)
    row_min = jnp.min(mn8, axis=0, keepdims=True)
    row_max = jnp.max(mx8, axis=0, keepdims=True)

    pc = 2 * kc
    npair = (nch + 1) // 2
    n_chunks_total = vaT_ref.shape[1]

    @pl.when(nch % 2 == 1)
    def _():
        sc_ref[pl.ds(pl.multiple_of(nch * kc, kc), kc), :] = jnp.full((kc, tq), -jnp.inf, F32)

    def count_ge(thr):
        def body(j, cnt):
            s = sc_ref[pl.ds(pl.multiple_of(j * pc, pc), pc), :]
            return cnt + _fold8(jnp.where(s >= thr, 1.0, 0.0), jnp.sum)
        cnt8 = lax.fori_loop(0, npair, body, jnp.zeros((SUBLANES, tq), F32))
        return jnp.sum(cnt8, axis=0, keepdims=True)

    def min_ge(thr):
        def body(j, mn):
            s = sc_ref[pl.ds(pl.multiple_of(j * pc, pc), pc), :]
            return jnp.minimum(mn, _fold8(jnp.where(s >= thr, s, jnp.inf), jnp.min))
        mn8 = lax.fori_loop(0, npair, body, jnp.full((SUBLANES, tq), jnp.inf, F32))
        return jnp.min(mn8, axis=0, keepdims=True)

    def count_zero():
        def body(j, carry):
            c0, cp = carry
            s = sc_ref[pl.ds(pl.multiple_of(j * pc, pc), pc), :]
            return (c0 + _fold8(jnp.where(s >= 0.0, 1.0, 0.0), jnp.sum),
                    cp + _fold8(jnp.where(s > 0.0, 1.0, 0.0), jnp.sum))
        z8 = jnp.zeros((SUBLANES, tq), F32)
        c0, cp = lax.fori_loop(0, npair, body, (z8, z8))
        return jnp.sum(c0, axis=0, keepdims=True), jnp.sum(cp, axis=0, keepdims=True)

    kf = float(topk)
    n_valid = (qpos + 1).astype(F32)
    few = n_valid <= kf
    cnt_nonneg, cnt_pos = count_zero()
    is_zero = jnp.logical_and(cnt_pos < kf, cnt_nonneg >= kf)
    is_pos = cnt_pos >= kf
    lo0 = jnp.where(is_zero, 0, jnp.where(is_pos, MIN_NORMAL_KEY, _f32_to_key(row_min)))
    hi0 = jnp.where(is_zero, 1, jnp.where(is_pos, _f32_to_key(row_max) + 1, -1))
    cnt0 = jnp.where(is_zero, cnt_nonneg, jnp.where(is_pos, cnt_pos, n_valid))
    lo0 = jnp.where(few, _f32_to_key(row_min), lo0)
    hi0 = jnp.where(few, lo0 + 1, hi0)

    def unfinished(lo, hi, cnt_lo):
        return jnp.logical_and(lo + 1 != hi, cnt_lo != kf)

    def bis_cond(st):
        lo, hi, cnt_lo, it = st
        return jnp.logical_and(jnp.max(jnp.where(unfinished(lo, hi, cnt_lo), 1.0, 0.0)) > 0.0, it < MAX_SELECT_PASSES)

    def bisect(st):
        lo, hi, cnt_lo, it = st
        nd = unfinished(lo, hi, cnt_lo)
        mid_val = _f32_to_key(0.5 * _key_to_f32(lo) + 0.5 * _key_to_f32(hi))
        mid_key = lo + lax.shift_right_logical(hi - lo, 1)
        mid = jnp.where(it < VALUE_MID_PASSES, jnp.clip(mid_val, lo + 1, hi - 1), mid_key)
        cnt = count_ge(_key_to_f32(mid))
        ge = cnt >= kf
        up = jnp.logical_and(nd, ge)
        down = jnp.logical_and(nd, jnp.logical_not(ge))
        return jnp.where(up, mid, lo), jnp.where(down, mid, hi), jnp.where(up, cnt, cnt_lo), it + 1

    def peel(st):
        lo, hi, cnt_lo, it = st
        nd = unfinished(lo, hi, cnt_lo)
        k1 = jnp.maximum(_f32_to_key(min_ge(_key_to_f32(lo))), lo)
        cnt = count_ge(_key_to_f32(k1 + 1))
        ge = cnt >= kf
        up = jnp.logical_and(nd, ge)
        down = jnp.logical_and(nd, jnp.logical_not(ge))
        new_lo = jnp.where(nd, jnp.where(ge, k1 + 1, k1), lo)
        return new_lo, jnp.where(down, k1 + 1, hi), jnp.where(up, cnt, cnt_lo), it + 2

    def bis_body(st):
        lo, hi, cnt_lo, it = st
        excess = jnp.max(jnp.where(unfinished(lo, hi, cnt_lo), cnt_lo - kf, 0.0))
        return lax.cond(excess > PEEL_MAX, bisect, peel, st)

    lo, _, cnt_lo, _ = lax.while_loop(bis_cond, bis_body, (lo0, hi0, cnt0, jnp.int32(0)))
    thr = _key_to_f32(lo)

    tie = jnp.logical_and(cnt_lo > kf, jnp.logical_not(few))

    @pl.when(jnp.max(jnp.where(tie, 1.0, 0.0)) > 0.0)
    def _():
        def count_gt():
            def gt_body(c, cnt):
                k0 = pl.multiple_of(c * kc, kc)
                s = sc_ref[pl.ds(k0, kc), :]
                return cnt + _fold8(jnp.where(s > thr, 1.0, 0.0), jnp.sum)
            gt8 = lax.fori_loop(0, nch, gt_body, jnp.zeros((SUBLANES, tq), F32))
            return jnp.sum(gt8, axis=0, keepdims=True)

        zero_thr = thr == 0.0
        other = jnp.max(jnp.where(jnp.logical_and(tie, jnp.logical_not(zero_thr)), 1.0, 0.0)) > 0.0
        gt = lax.cond(other, count_gt, lambda: cnt_pos)
        need = kf - jnp.where(zero_thr, cnt_pos, gt)
        r = lax.broadcasted_iota(jnp.int32, (kc, kc), 0)
        cidx = lax.broadcasted_iota(jnp.int32, (kc, kc), 1)
        tri = jnp.where(cidx <= r, 1.0, 0.0).astype(BF16)

        def tie_body(c, seen):
            k0 = pl.multiple_of(c * kc, kc)
            s = sc_ref[pl.ds(k0, kc), :]
            eqf = jnp.where(s == thr, jnp.where(tie, 1.0, 0.0), 0.0)
            pre = jnp.dot(tri, eqf.astype(BF16), preferred_element_type=F32) + seen
            drop = eqf * jnp.where(pre > need, 1.0, 0.0)
            sc_ref[pl.ds(k0, kc), :] = jnp.where(drop > 0.0, -jnp.inf, s)
            return seen + jnp.sum(eqf, axis=0, keepdims=True)

        lax.fori_loop(0, nch, tie_body, jnp.zeros((1, tq), F32))

    qT_all = (_tile_T(q_ref) * Q_SCALE).astype(BF16)
    qT = jnp.concatenate([_place_head(*_head_rows(qT_all, h), 0) for h in range(A_HEADS)], axis=1)
    acc_ref[...] = jnp.zeros_like(acc_ref)
    ones_rows = jnp.ones((SUBLANES, kc), BF16)

    def issue_scores(c, slot):
        k0 = pl.multiple_of(jnp.minimum(c, n_chunks_total - 1) * kc, kc)
        sbuf_ref[slot] = jnp.dot(kv_ref[pl.ds(k0, kc), :], qT, preferred_element_type=F32)

    def softmax_pv(c, slot, m):
        k0 = pl.multiple_of(c * kc, kc)
        v1 = jnp.concatenate([vaT_ref[0, c], ones_rows], axis=0)
        sel = sc_ref[pl.ds(k0, kc), :] >= thr
        s = jnp.concatenate([jnp.where(sel, sbuf_ref[slot, :, h * tq:(h + 1) * tq], NEG_BIG)
                             for h in range(A_HEADS)], axis=1)
        m_new = jnp.maximum(m, jnp.max(s, axis=0, keepdims=True))
        alpha = jnp.exp2(m - m_new)
        p = jnp.exp2((s - m_new).astype(BF16))
        acc_ref[...] = acc_ref[...] * alpha + jnp.dot(v1, p, preferred_element_type=F32)
        return m_new

    def chunk_pair(j, m):
        c0 = 2 * j
        issue_scores(c0 + 1, 1)
        m = softmax_pv(c0, 0, m)
        issue_scores(c0 + 2, 0)
        return softmax_pv(c0 + 1, 1, m)

    issue_scores(0, 0)
    lax.fori_loop(0, npair, chunk_pair, jnp.full((1, A_HEADS * tq), NEG_BIG, F32))
    a = acc_ref[...]
    o = a[:HEAD_DIM] / a[HEAD_DIM:HEAD_DIM + 1]
    _store_rows(o_ref, [o[:, h * tq:(h + 1) * tq] for h in range(A_HEADS)])


def _dsa(h_rope, h_plain, kv, vaT4, *, bsz, seq, tq, kc, topk):
    nq = seq // tq
    return pl.pallas_call(
        functools.partial(_dsa_body, tq=tq, kc=kc, topk=topk),
        grid=(bsz, nq),
        in_specs=[
            pl.BlockSpec((tq, PROJ_TN), lambda b, i: (b * nq + i, TILE_AQ)),
            pl.BlockSpec((tq, PROJ_TN), lambda b, i: (b * nq + i, TILE_AIQ)),
            pl.BlockSpec((tq, LANES), lambda b, i: (b * nq + i, COL_AIW // LANES)),
            pl.BlockSpec((seq, PROJ_TN), lambda b, i: (b, TILE_CK_AIK)),
            pl.BlockSpec((seq, PROJ_TN), lambda b, i: (b, 0)),
            pl.BlockSpec((1, seq // kc, HEAD_DIM, kc), lambda b, i: (b, 0, 0, 0)),
        ],
        out_specs=pl.BlockSpec((tq, A_W), lambda b, i: (b * nq + i, 0)),
        out_shape=jax.ShapeDtypeStruct((bsz * seq, A_W), BF16),
        scratch_shapes=[pltpu.VMEM((seq + kc, tq), F32),
                        pltpu.VMEM((HEAD_DIM + SUBLANES, A_HEADS * tq), F32),
                        pltpu.VMEM((2, kc, A_HEADS * tq), F32)],
        compiler_params=_cparams(2),
        name="dsa_attention",
    )(h_rope, h_rope, h_plain, h_rope, kv, vaT4)


def _kmean_body(k_ref, o_ref):
    o_ref[0] = jnp.mean(k_ref[...].astype(F32), axis=0, keepdims=True)


def _moba_kmean(h_rope, *, n_tokens):
    nblk = n_tokens // B_BLOCK
    return pl.pallas_call(
        _kmean_body,
        grid=(nblk,),
        in_specs=[pl.BlockSpec((B_BLOCK, B_W), lambda i: (i, TILE_BK))],
        out_specs=pl.BlockSpec((1, 1, B_W), lambda i: (i, 0, 0)),
        out_shape=jax.ShapeDtypeStruct((nblk, 1, B_W), F32),
        compiler_params=_cparams(1),
        name="moba_kmean",
    )(h_rope)


def _moba_gate_body(q_ref, km_ref, sel_ref, *, tq, nkb, nsel):
    i = pl.program_id(1)
    qpos = i * tq + lax.broadcasted_iota(jnp.int32, (1, tq), 1)
    cur = qpos // B_BLOCK
    row = lax.broadcasted_iota(jnp.int32, (nkb, tq), 0)
    past = row < cur
    qT = _tile_T(q_ref).astype(BF16)
    km = km_ref[0]
    col_head = (lax.broadcasted_iota(jnp.int32, km.shape, 1) % HALF_TN) // (HEAD_DIM // 2)
    for h in range(B_HEADS):
        km_h = jnp.where(col_head == h, km, 0.0).astype(BF16)
        g = jnp.dot(km_h, qT, preferred_element_type=F32)
        g = jnp.where(past, g, -jnp.inf)
        sel = jnp.zeros((nkb, tq), F32)
        for _ in range(nsel):
            mx = jnp.max(g, axis=0, keepdims=True)
            idx = jnp.min(jnp.where(g == mx, row, nkb), axis=0, keepdims=True)
            pick = row == idx
            sel = jnp.where(pick, 1.0, sel)
            g = jnp.where(pick, -jnp.inf, g)
        sel_ref[0, h] = jnp.where(past, sel, 0.0)


def _moba_gate(h_rope, kmean, *, bsz, seq, tq):
    nkb = seq // B_BLOCK
    nsel = min(B_TOPK_BLOCKS, nkb)
    nq = seq // tq
    return pl.pallas_call(
        functools.partial(_moba_gate_body, tq=tq, nkb=nkb, nsel=nsel),
        grid=(bsz, nq),
        in_specs=[
            pl.BlockSpec((tq, PROJ_TN), lambda b, i: (b * nq + i, TILE_BQ)),
            pl.BlockSpec((1, nkb, B_W), lambda b, i: (b, 0, 0)),
        ],
        out_specs=pl.BlockSpec((1, B_HEADS, nkb, tq), lambda b, i: (b, 0, 0, i)),
        out_shape=jax.ShapeDtypeStruct((bsz, B_HEADS, nkb, seq), F32),
        compiler_params=_cparams(2),
        name="moba_gate",
    )(h_rope, kmean)


def _moba_body(q_ref, k_ref, vT_ref, sel_ref, o_ref, acc_ref, sbuf_ref, *, tq):
    i = pl.program_id(1)
    qT_all = (_tile_T(q_ref) * Q_SCALE).astype(BF16)
    qT = [_place_head(*_head_rows(qT_all, h), h) for h in range(B_HEADS)]
    ones_rows = jnp.ones((SUBLANES, B_BLOCK), BF16)
    n_past = i

    def values(h, j):
        return jnp.concatenate([vT_ref[0, h, j], ones_rows], axis=0)

    def issue_scores(j, slot):
        k0 = pl.multiple_of(jnp.minimum(j, i) * B_BLOCK, B_BLOCK)
        for h in range(B_HEADS):
            sbuf_ref[slot, h] = jnp.dot(k_ref[pl.ds(k0, B_BLOCK), :], qT[h], preferred_element_type=F32)

    def softmax_pv(j, slot, ms):
        ps, alphas, new_ms = [], [], []
        for h in range(B_HEADS):
            s = jnp.where(sel_ref[0, h, pl.ds(j, 1), :] > 0.0, sbuf_ref[slot, h], NEG_BIG)
            m_new = jnp.maximum(ms[h], jnp.max(s, axis=0, keepdims=True))
            alphas.append(jnp.exp2(ms[h] - m_new))
            ps.append(jnp.exp2((s - m_new).astype(BF16)))
            new_ms.append(m_new)
        for h in range(B_HEADS):
            acc_ref[h] = acc_ref[h] * alphas[h] + jnp.dot(values(h, j), ps[h], preferred_element_type=F32)
        return tuple(new_ms)

    own0 = pl.multiple_of(i * B_BLOCK, B_BLOCK)
    s_own = [jnp.dot(k_ref[pl.ds(own0, B_BLOCK), :], qT[h], preferred_element_type=F32) for h in range(B_HEADS)]
    issue_scores(0, 0)
    causal = lax.broadcasted_iota(jnp.int32, (B_BLOCK, tq), 0) <= lax.broadcasted_iota(jnp.int32, (B_BLOCK, tq), 1)
    ms = []
    for h in range(B_HEADS):
        s = jnp.where(causal, s_own[h], NEG_BIG)
        m0 = jnp.max(s, axis=0, keepdims=True)
        acc_ref[h] = jnp.dot(values(h, i), jnp.exp2((s - m0).astype(BF16)), preferred_element_type=F32)
        ms.append(m0)

    def block_pair(jj, ms):
        j0 = 2 * jj
        issue_scores(j0 + 1, 1)
        ms = softmax_pv(j0, 0, ms)
        issue_scores(j0 + 2, 0)
        return softmax_pv(j0 + 1, 1, ms)

    lax.fori_loop(0, (n_past + 1) // 2, block_pair, tuple(ms))
    outs = []
    for h in range(B_HEADS):
        a = acc_ref[h]
        outs.append(a[:HEAD_DIM] / a[HEAD_DIM:HEAD_DIM + 1])
    _store_rows(o_ref, outs)


def _moba(h_rope, bvT5, sel, *, bsz, seq):
    tq = B_BLOCK
    nkb = seq // B_BLOCK
    nq = seq // tq
    return pl.pallas_call(
        functools.partial(_moba_body, tq=tq),
        grid=(bsz, nq),
        in_specs=[
            pl.BlockSpec((tq, PROJ_TN), lambda b, i: (b * nq + i, TILE_BQ)),
            pl.BlockSpec((seq, PROJ_TN), lambda b, i: (b, TILE_BK)),
            pl.BlockSpec((1, B_HEADS, nkb, HEAD_DIM, B_BLOCK), lambda b, i: (b, 0, 0, 0, 0)),
            pl.BlockSpec((1, B_HEADS, nkb, tq), lambda b, i: (b, 0, 0, i)),
        ],
        out_specs=pl.BlockSpec((tq, B_W), lambda b, i: (b * nq + i, 0)),
        out_shape=jax.ShapeDtypeStruct((bsz * seq, B_W), BF16),
        scratch_shapes=[pltpu.VMEM((B_HEADS, HEAD_DIM + SUBLANES, tq), F32),
                        pltpu.VMEM((2, B_HEADS, B_BLOCK, tq), F32)],
        compiler_params=_cparams(2),
        name="moba_attention",
    )(h_rope, h_rope, bvT5, sel)


def _swa_body(sink_ref, q0_ref, q1_ref, k0_ref, k1_ref, k2_ref, v0_ref, v1_ref, v2_ref, o_ref, *, tq):
    i = pl.program_id(1)
    q0 = i * tq
    qpos = q0 + lax.broadcasted_iota(jnp.int32, (1, tq), 1)
    nk = tq + C_WINDOW
    kpos = q0 - C_WINDOW + lax.broadcasted_iota(jnp.int32, (nk, 1), 0)
    diff = qpos - kpos
    ok = jnp.where(diff >= 0, jnp.where(diff < C_WINDOW, jnp.where(kpos >= 0, 1.0, 0.0), 0.0), 0.0) > 0.0
    ones_rows = jnp.ones((SUBLANES, nk), BF16)
    group = C_HEADS // C_KV_HEADS
    assert group == HEADS_PER_TILE
    kk = jnp.concatenate([k0_ref[...], k1_ref[...], k2_ref[...]], axis=0)
    outs = []
    for kv, q_ref in enumerate((q0_ref, q1_ref)):
        vv = jnp.concatenate([v0_ref[0, kv], v1_ref[0, kv], v2_ref[0, kv]], axis=1)
        v1 = jnp.concatenate([vv, ones_rows], axis=0)
        heads = [kv * group + g for g in range(group)]
        qT_all = (_tile_T(q_ref) * Q_SCALE).astype(BF16)
        qT = jnp.concatenate([_place_head(*_head_rows(qT_all, g), kv) for g in range(group)], axis=1)
        sink = jnp.concatenate([jnp.full((1, tq), sink_ref[hd] * LOG2E, F32) for hd in heads], axis=1)
        s = jnp.dot(kk, qT, preferred_element_type=F32)
        s = jnp.concatenate([jnp.where(ok, s[:, g * tq:(g + 1) * tq], NEG_BIG) for g in range(group)], axis=1)
        m = jnp.maximum(jnp.max(s, axis=0, keepdims=True), sink)
        p = jnp.exp2((s - m).astype(BF16))
        a = jnp.dot(v1, p, preferred_element_type=F32)
        o = a[:HEAD_DIM] / (a[HEAD_DIM:HEAD_DIM + 1] + jnp.exp2(sink - m))
        outs += [o[:, g * tq:(g + 1) * tq] for g in range(group)]
    _store_rows(o_ref, outs)


def _swa(sinks, h_rope, cvT, *, bsz, seq, tq):
    r = tq // C_WINDOW
    assert r == 2
    nq = seq // tq
    nwb = seq // C_WINDOW

    def kspec(off):
        return pl.BlockSpec((C_WINDOW, PROJ_TN),
                            lambda b, i, s: (b * nwb + jnp.maximum(i * r + off, 0), TILE_CK_AIK))

    def vspec(off):
        return pl.BlockSpec((1, C_KV_HEADS, HEAD_DIM, C_WINDOW),
                            lambda b, i, s: (b, 0, 0, jnp.maximum(i * r + off, 0)))

    grid_spec = pltpu.PrefetchScalarGridSpec(
        num_scalar_prefetch=1,
        grid=(bsz, nq),
        in_specs=[pl.BlockSpec((tq, PROJ_TN), lambda b, i, s: (b * nq + i, TILE_CQ)),
                  pl.BlockSpec((tq, PROJ_TN), lambda b, i, s: (b * nq + i, TILE_CQ + 1)),
                  kspec(-1), kspec(0), kspec(1), vspec(-1), vspec(0), vspec(1)],
        out_specs=pl.BlockSpec((tq, C_W), lambda b, i, s: (b * nq + i, 0)),
    )
    return pl.pallas_call(
        functools.partial(_swa_body, tq=tq),
        grid_spec=grid_spec,
        out_shape=jax.ShapeDtypeStruct((bsz * seq, C_W), BF16),
        compiler_params=_cparams(2),
        name="swa_attention",
    )(sinks, h_rope, h_rope, h_rope, h_rope, h_rope, cvT, cvT, cvT)


def _layer_norm(z, g, b):
    mu = jnp.mean(z, axis=-1, keepdims=True)
    zc = z - mu
    var = jnp.mean(zc * zc, axis=-1, keepdims=True)
    return zc * lax.rsqrt(var + LN_EPS) * g + b


def _split_bf16(a):
    hi = a.astype(BF16)
    lo = (a - hi.astype(F32)).astype(BF16)
    return hi, lo


def _router(x1, rw_ref, rb_ref, cls_ref, wt_ref):
    nt = (((1,), (1,)), ((), ()))
    xh, xl = _split_bf16(x1)
    wh, wl = _split_bf16(rw_ref[...])
    logits = (lax.dot_general(wh, xh, nt, preferred_element_type=F32)
              + lax.dot_general(wh, xl, nt, preferred_element_type=F32)
              + lax.dot_general(wl, xh, nt, preferred_element_type=F32))
    aff = jax.nn.sigmoid(logits)
    score = aff + rb_ref[...]
    tm = x1.shape[0]
    sc = [score[e:e + 1, :] for e in range(N_EXPERTS)]
    af = [aff[e:e + 1, :] for e in range(N_EXPERTS)]
    in_top = []
    grp_score = []
    for gq in range(N_GROUPS):
        gs = jnp.zeros((1, tm), F32)
        for a in range(EXPERTS_PER_GROUP):
            ea = gq * EXPERTS_PER_GROUP + a
            rank = jnp.zeros((1, tm), F32)
            for b in range(EXPERTS_PER_GROUP):
                if b == a:
                    continue
                eb = gq * EXPERTS_PER_GROUP + b
                beats = (sc[eb] >= sc[ea]) if b < a else (sc[eb] > sc[ea])
                rank = rank + jnp.where(beats, 1.0, 0.0)
            top = rank < 2.0
            in_top.append(top)
            gs = gs + jnp.where(top, sc[ea], 0.0)
        grp_score.append(gs)
    best = grp_score[0]
    gstar = jnp.zeros((1, tm), jnp.int32)
    for gq in range(1, N_GROUPS):
        better = grp_score[gq] > best
        best = jnp.where(better, grp_score[gq], best)
        gstar = jnp.where(better, gq, gstar)
    cls = jnp.zeros((1, tm), jnp.int32)
    w_lo = jnp.zeros((1, tm), F32)
    w_hi = jnp.zeros((1, tm), F32)
    for gq in range(N_GROUPS):
        is_g = gstar == gq
        for pi, (a, b) in enumerate(PAIRS):
            ea, eb = gq * EXPERTS_PER_GROUP + a, gq * EXPERTS_PER_GROUP + b
            hit = jnp.where(is_g, jnp.where(in_top[ea], jnp.where(in_top[eb], 1.0, 0.0), 0.0), 0.0) > 0.0
            cls = jnp.where(hit, gq * len(PAIRS) + pi, cls)
            tot = af[ea] + af[eb]
            w_lo = jnp.where(hit, af[ea] / tot, w_lo)
            w_hi = jnp.where(hit, af[eb] / tot, w_hi)
    cls_ref[...] = cls
    wt_ref[0:1, :] = w_lo
    wt_ref[1:2, :] = w_hi


def _merge_body(oa_ref, ob_ref, oc_ref, ga_ref, gb_ref, gc_ref, x_ref, wa_ref, wb_ref, wc_ref, wo_ref,
                g_ref, b_ref, rw_ref, rb_ref, x1_ref, cls_ref, wt_ref):
    def branch(o_ref, w_ref, gate_ref):
        y = jnp.dot(o_ref[...], w_ref[...], preferred_element_type=F32)
        return jax.nn.sigmoid(gate_ref[...].astype(F32)) * y

    merged = branch(oa_ref, wa_ref, ga_ref) + branch(ob_ref, wb_ref, gb_ref) + branch(oc_ref, wc_ref, gc_ref)
    y = jnp.dot(merged.astype(BF16), wo_ref[...], preferred_element_type=F32)
    x1 = _layer_norm(DN_ALPHA * x_ref[...] + y, g_ref[...], b_ref[...])
    x1_ref[...] = x1
    _router(x1, rw_ref, rb_ref, cls_ref, wt_ref)


def _merge(oa, ob, oc, h_plain, x, wa, wb, wc, wo, ln_g, ln_b, rwT, rb, *, tm):
    t_tokens = x.shape[0]
    gate_blk = COL_GATES // D_MODEL
    full = lambda shape: pl.BlockSpec(shape, lambda i: (0,) * len(shape))
    return pl.pallas_call(
        _merge_body,
        grid=(t_tokens // tm,),
        in_specs=[
            pl.BlockSpec((tm, A_W), lambda i: (i, 0)),
            pl.BlockSpec((tm, B_W), lambda i: (i, 0)),
            pl.BlockSpec((tm, C_W), lambda i: (i, 0)),
            pl.BlockSpec((tm, D_MODEL), lambda i: (i, gate_blk)),
            pl.BlockSpec((tm, D_MODEL), lambda i: (i, gate_blk + 1)),
            pl.BlockSpec((tm, D_MODEL), lambda i: (i, gate_blk + 2)),
            pl.BlockSpec((tm, D_MODEL), lambda i: (i, 0)),
            full((A_W, D_MODEL)), full((B_W, D_MODEL)), full((C_W, D_MODEL)), full((D_MODEL, D_MODEL)),
            full((1, D_MODEL)), full((1, D_MODEL)), full((N_EXPERTS, D_MODEL)), full((N_EXPERTS, 1)),
        ],
        out_specs=[
            pl.BlockSpec((tm, D_MODEL), lambda i: (i, 0)),
            pl.BlockSpec((1, tm), lambda i: (0, i)),
            pl.BlockSpec((2, tm), lambda i: (0, i)),
        ],
        out_shape=[
            jax.ShapeDtypeStruct((t_tokens, D_MODEL), F32),
            jax.ShapeDtypeStruct((1, t_tokens), jnp.int32),
            jax.ShapeDtypeStruct((2, t_tokens), F32),
        ],
        compiler_params=_cparams(1),
        name="merge_ln_router",
    )(oa, ob, oc, h_plain, h_plain, h_plain, x, wa, wb, wc, wo, ln_g, ln_b, rwT, rb)


def _gather_start(idx_ref, src_hbm, dst_ref, sem, n_rows):
    def start(r, _):
        pltpu.make_async_copy(src_hbm.at[pl.ds(idx_ref[0, 0, r], 1)], dst_ref.at[pl.ds(r, 1)], sem).start()
        return 0

    lax.fori_loop(0, n_rows, start, 0, unroll=8)


def _gather_wait(src_hbm, dst_ref, sem, n_rows):
    pltpu.make_async_copy(src_hbm.at[pl.ds(0, n_rows)], dst_ref, sem).wait()


def _experts_body(e1_ref, e2_ref, nused_ref, src_ref, src_next_ref, x_hbm, wt_ref,
                  wg1_ref, wu1_ref, wd1_ref, wg2_ref, wu2_ref, wd2_ref, o_ref, xbuf, sem):
    s = pl.program_id(0)
    nused = nused_ref[0]
    slot = lax.rem(s, 2)

    @pl.when(s == 0)
    def _():
        _gather_start(src_ref, x_hbm, xbuf.at[0], sem.at[0], MOE_TILE)

    @pl.when(s + 1 < nused)
    def _():
        _gather_start(src_next_ref, x_hbm, xbuf.at[1 - slot], sem.at[1 - slot], MOE_TILE)

    @pl.when(s < nused)
    def _():
        _gather_wait(x_hbm, xbuf.at[slot], sem.at[slot], MOE_TILE)
        xb = xbuf[slot].astype(BF16)
        wt = wt_ref[...]

        def expert(wg_ref, wu_ref, wd_ref):
            g = jnp.dot(xb, wg_ref[0], preferred_element_type=F32)
            u = jnp.dot(xb, wu_ref[0], preferred_element_type=F32)
            he = (g * jax.nn.sigmoid(g) * u).astype(BF16)
            return jnp.dot(he, wd_ref[0], preferred_element_type=F32)

        o_ref[...] = wt[:, 0:1] * expert(wg1_ref, wu1_ref, wd1_ref) + wt[:, 1:2] * expert(wg2_ref, wu2_ref, wd2_ref)

    @pl.when(s >= nused)
    def _():
        o_ref[...] = jnp.zeros_like(o_ref)


def _experts(tile_e1, tile_e2, nused, src3, x1, wsort, wg, wu, wd):
    ntiles = src3.shape[0]
    wspec_up = lambda which: pl.BlockSpec((1, D_MODEL, D_EXPERT), lambda s, e1, e2, nu: ((e1, e2)[which][s], 0, 0))
    wspec_dn = lambda which: pl.BlockSpec((1, D_EXPERT, D_MODEL), lambda s, e1, e2, nu: ((e1, e2)[which][s], 0, 0))
    grid_spec = pltpu.PrefetchScalarGridSpec(
        num_scalar_prefetch=3,
        grid=(ntiles,),
        in_specs=[
            pl.BlockSpec((1, 1, MOE_TILE), lambda s, e1, e2, nu: (s, 0, 0), memory_space=pltpu.SMEM),
            pl.BlockSpec((1, 1, MOE_TILE), lambda s, e1, e2, nu: (jnp.minimum(s + 1, ntiles - 1), 0, 0),
                         memory_space=pltpu.SMEM),
            pl.BlockSpec(memory_space=pl.ANY),
            pl.BlockSpec((MOE_TILE, 2), lambda s, e1, e2, nu: (s, 0)),
            wspec_up(0), wspec_up(0), wspec_dn(0), wspec_up(1), wspec_up(1), wspec_dn(1),
        ],
        out_specs=pl.BlockSpec((MOE_TILE, D_MODEL), lambda s, e1, e2, nu: (s, 0)),
        scratch_shapes=[pltpu.VMEM((2, MOE_TILE, D_MODEL), F32), pltpu.SemaphoreType.DMA((2,))],
    )
    return pl.pallas_call(
        _experts_body,
        grid_spec=grid_spec,
        out_shape=jax.ShapeDtypeStruct((ntiles * MOE_TILE, D_MODEL), F32),
        compiler_params=_cparams(1),
        name="moe_experts",
    )(tile_e1, tile_e2, nused, src3, src3, x1, wsort, wg, wu, wd, wg, wu, wd)


def _final_body(pos_ref, pos_next_ref, y_hbm, x_ref, g_ref, b_ref, o_ref, ob_ref, ybuf, sem, *, tm):
    s = pl.program_id(0)
    slot = lax.rem(s, 2)

    @pl.when(s == 0)
    def _():
        _gather_start(pos_ref, y_hbm, ybuf.at[0], sem.at[0], tm)

    @pl.when(s + 1 < pl.num_programs(0))
    def _():
        _gather_start(pos_next_ref, y_hbm, ybuf.at[1 - slot], sem.at[1 - slot], tm)

    _gather_wait(y_hbm, ybuf.at[slot], sem.at[slot], tm)
    x2 = _layer_norm(DN_ALPHA * x_ref[...] + ybuf[slot], g_ref[...], b_ref[...])
    o_ref[...] = x2
    ob_ref[...] = x2.astype(BF16)


def _final(pos3, y_sorted, x1, ln_g, ln_b, *, tm):
    t_tokens = x1.shape[0]
    nsteps = t_tokens // tm
    return pl.pallas_call(
        functools.partial(_final_body, tm=tm),
        grid=(nsteps,),
        in_specs=[
            pl.BlockSpec((1, 1, tm), lambda i: (i, 0, 0), memory_space=pltpu.SMEM),
            pl.BlockSpec((1, 1, tm), lambda i: (jnp.minimum(i + 1, nsteps - 1), 0, 0), memory_space=pltpu.SMEM),
            pl.BlockSpec(memory_space=pl.ANY),
            pl.BlockSpec((tm, D_MODEL), lambda i: (i, 0)),
            pl.BlockSpec((1, D_MODEL), lambda i: (0, 0)),
            pl.BlockSpec((1, D_MODEL), lambda i: (0, 0)),
        ],
        out_specs=[pl.BlockSpec((tm, D_MODEL), lambda i: (i, 0)), pl.BlockSpec((tm, D_MODEL), lambda i: (i, 0))],
        out_shape=[jax.ShapeDtypeStruct((t_tokens, D_MODEL), F32), jax.ShapeDtypeStruct((t_tokens, D_MODEL), BF16)],
        scratch_shapes=[pltpu.VMEM((2, tm, D_MODEL), F32), pltpu.SemaphoreType.DMA((2,))],
        compiler_params=_cparams(1),
        name="moe_combine_ln",
    )(pos3, pos3, y_sorted, x1, ln_g, ln_b)


def _rope_tables(seq):
    inv = 1.0 / (ROPE_THETA ** (jnp.arange(0, HEAD_DIM, 2, dtype=F32) / HEAD_DIM))
    ang = jnp.arange(seq, dtype=F32)[:, None] * inv[None, :]
    return jnp.cos(ang), jnp.sin(ang)


def _halves_layout(w_heads):
    d, n = w_heads.shape
    t = w_heads.reshape(d, n // PROJ_TN, HEADS_PER_TILE, 2, HEAD_DIM // 2)
    return t.transpose(0, 1, 3, 2, 4).reshape(d, n)


def _reorder_w_in(w):
    pts = np.cumsum((0,) + IN_SIZES)
    sec = [w[:, pts[k]:pts[k + 1]] for k in range(len(IN_SIZES))]
    a_q, a_c, a_iq, a_ik, a_iw, b_q, b_k, b_v, c_q, c_k, c_v, gates = sec
    zeros = lambda n: jnp.zeros((w.shape[0], n), w.dtype)
    w_rope = _halves_layout(jnp.concatenate([a_q, a_iq, b_q, b_k, c_q, c_k, a_ik, zeros(HEAD_DIM)], axis=1))
    w_plain = jnp.concatenate([gates, b_v, a_c, a_iw, zeros(LANES - A_IDX_HEADS), c_v, zeros(LANES)], axis=1)
    assert w_rope.shape[1] == N_ROPE and w_plain.shape[1] == N_PLAIN
    return w_rope.astype(BF16), w_plain.astype(BF16)


def _kv_weights(w_uk, w_uv):
    half = HEAD_DIM // 2
    zeros = lambda n: jnp.zeros((w_uk.shape[0], n), w_uk.dtype)
    return jnp.concatenate([w_uk[:, :half], w_uv, zeros(HALF_TN - half - HEAD_DIM),
                            w_uk[:, half:], zeros(HALF_TN - half)], axis=1).astype(BF16)


def _moe_plan(cls, wts, n_tokens):
    ntiles = n_tokens // MOE_TILE + N_CLASSES
    onehot = (cls[:, None] == jnp.arange(N_CLASSES, dtype=jnp.int32)[None, :]).astype(jnp.int32)
    csum = jnp.cumsum(onehot, axis=0)
    rank = jnp.sum(onehot * csum, axis=1) - 1
    counts = csum[-1]
    ptiles = (counts + MOE_TILE - 1) // MOE_TILE
    tile_end = jnp.cumsum(ptiles)
    tile_start = tile_end - ptiles
    pos = (tile_start[cls] * MOE_TILE + rank).astype(jnp.int32)
    nused = tile_end[-1:].astype(jnp.int32)
    tile_ids = jnp.arange(ntiles, dtype=jnp.int32)
    tile_cls = jnp.minimum(jnp.sum((tile_ids[:, None] >= tile_end[None, :]).astype(jnp.int32), axis=1),
                           N_CLASSES - 1).astype(jnp.int32)
    pair = np.array(PAIRS, dtype=np.int32)
    e_lo = jnp.asarray(np.repeat(np.arange(N_GROUPS), len(PAIRS)) * EXPERTS_PER_GROUP + np.tile(pair[:, 0], N_GROUPS), jnp.int32)
    e_hi = jnp.asarray(np.repeat(np.arange(N_GROUPS), len(PAIRS)) * EXPERTS_PER_GROUP + np.tile(pair[:, 1], N_GROUPS), jnp.int32)
    src = jnp.zeros((ntiles * MOE_TILE,), jnp.int32).at[pos].set(jnp.arange(n_tokens, dtype=jnp.int32))
    wsort = jnp.zeros((ntiles * MOE_TILE, 2), F32).at[pos].set(wts.T)
    return e_lo[tile_cls], e_hi[tile_cls], nused, src.reshape(ntiles, 1, MOE_TILE), pos, wsort


def kernel(x, w_in, a_w_uk, a_w_uv, c_sinks, w_branch, w_o, ln1_g, ln1_b, router_w, router_b,
           moe_w_gate, moe_w_up, moe_w_down, ln2_g, ln2_b):
    bsz, seq, _ = x.shape
    n_tokens = bsz * seq
    topk = min(A_TOPK_MAX, seq // 4)
    nkb = seq // B_BLOCK
    tq = 256
    kc = 256
    tm_proj = min(1024, seq)
    tm_row = 512
    half = HEAD_DIM // 2

    cos, sin = _rope_tables(seq)
    cos_in = jnp.tile(cos, (1, HEADS_PER_TILE))
    sin_in = jnp.tile(sin, (1, HEADS_PER_TILE))
    cos_kv = jnp.concatenate([cos, jnp.ones((seq, HALF_TN - half), F32)], axis=1)
    sin_kv = jnp.concatenate([sin, jnp.zeros((seq, HALF_TN - half), F32)], axis=1)
    rwT = router_w.T
    rb = router_b.reshape(N_EXPERTS, 1)

    xf = x.reshape(n_tokens, D_MODEL)
    xb = xf.astype(BF16)
    for l in range(DEPTH):
        w_rope, w_plain = _reorder_w_in(w_in[l])
        h_rope = _proj_rope(xb, 0, D_MODEL, w_rope, cos_in, sin_in, tm=tm_proj, seq=seq)
        h_plain = _proj_plain(xb, w_plain, tm=tm_proj)
        kv = _proj_rope(h_plain, COL_AC // A_KV_RANK, A_KV_RANK, _kv_weights(a_w_uk[l], a_w_uv[l]),
                        cos_kv, sin_kv, tm=tm_proj, seq=seq)

        vaT4 = kv[:, KV_VA0:KV_VA0 + HEAD_DIM].reshape(bsz, seq // kc, kc, HEAD_DIM).transpose(0, 1, 3, 2)
        bvT5 = (h_plain[:, COL_BV:COL_BV + B_W].reshape(bsz, nkb, B_BLOCK, B_HEADS, HEAD_DIM)
                .transpose(0, 3, 1, 4, 2))
        cvT = (h_plain[:, COL_CV:COL_CV + C_KV_HEADS * HEAD_DIM].reshape(bsz, seq, C_KV_HEADS, HEAD_DIM)
               .transpose(0, 2, 3, 1))

        o_a = _dsa(h_rope, h_plain, kv, vaT4, bsz=bsz, seq=seq, tq=min(DSA_TQ, seq), kc=kc, topk=topk)
        kmean = _moba_kmean(h_rope, n_tokens=n_tokens).reshape(bsz, nkb, B_W)
        sel = _moba_gate(h_rope, kmean, bsz=bsz, seq=seq, tq=tq)
        o_b = _moba(h_rope, bvT5, sel, bsz=bsz, seq=seq)
        o_c = _swa(c_sinks[l], h_rope, cvT, bsz=bsz, seq=seq, tq=tq)

        wb_all = w_branch[l].astype(BF16)
        x1, cls, wts = _merge(o_a, o_b, o_c, h_plain, xf,
                              wb_all[:A_W], wb_all[A_W:A_W + B_W], wb_all[A_W + B_W:], w_o[l].astype(BF16),
                              ln1_g[l].reshape(1, D_MODEL), ln1_b[l].reshape(1, D_MODEL), rwT, rb, tm=tm_row)

        e1, e2, nused, src3, pos, wsort = _moe_plan(cls[0], wts, n_tokens)
        y_sorted = _experts(e1, e2, nused, src3, x1, wsort,
                            moe_w_gate[l].astype(BF16), moe_w_up[l].astype(BF16), moe_w_down[l].astype(BF16))
        xf, xb = _final(pos.reshape(n_tokens // tm_row, 1, tm_row), y_sorted, x1,
                        ln2_g[l].reshape(1, D_MODEL), ln2_b[l].reshape(1, D_MODEL), tm=tm_row)
    return xf.reshape(bsz, seq, D_MODEL)
```

```python
import functools
import math

import jax
import jax.numpy as jnp
import numpy as np
from jax import lax
from jax.experimental import pallas as pl
from jax.experimental.pallas import tpu as pltpu

D_MODEL = 1024
DEPTH = 2
HEAD_DIM = 64
ROPE_THETA = 10000.0
LN_EPS = 1e-5
A_HEADS = 4
A_KV_RANK = 128
A_IDX_HEADS = 4
A_IDX_DIM = 64
A_TOPK_MAX = 256
B_HEADS = 4
B_BLOCK = 256
B_TOPK_BLOCKS = 3
C_HEADS = 8
C_KV_HEADS = 2
C_WINDOW = 128
N_BRANCH = 3
A_W = A_HEADS * HEAD_DIM
B_W = B_HEADS * HEAD_DIM
C_W = C_HEADS * HEAD_DIM
IN_SIZES = (A_W, A_KV_RANK, A_IDX_HEADS * A_IDX_DIM, A_IDX_DIM, A_IDX_HEADS,
            B_W, B_W, B_W, C_W, C_KV_HEADS * HEAD_DIM, C_KV_HEADS * HEAD_DIM,
            N_BRANCH * D_MODEL)
N_EXPERTS = 16
N_GROUPS = 4
EXPERTS_PER_GROUP = 4
D_EXPERT = 512
DN_ALPHA = (2 * DEPTH) ** 0.25
ATTN_SCALE = HEAD_DIM ** -0.5
LOG2E = math.log2(math.e)
Q_SCALE = ATTN_SCALE * LOG2E

LANES = 128
SUBLANES = 8
VMEM_LIMIT_BYTES = 56 * 1024 * 1024

NEG_BIG = -1e30
BF16 = jnp.bfloat16
F32 = jnp.float32

PROJ_TN = 256
HALF_TN = PROJ_TN // 2
HEADS_PER_TILE = PROJ_TN // HEAD_DIM
TILE_AQ = 0
TILE_AIQ = 1
TILE_BQ = 2
TILE_BK = 3
TILE_CQ = 4
TILE_CK_AIK = 6
N_ROPE = 7 * PROJ_TN
COL_GATES = 0
COL_BV = 3072
COL_AC = 3328
COL_AIW = 3456
COL_CV = 3584
N_PLAIN = 3840
KV_VA0 = HEAD_DIM // 2

PAIRS = ((0, 1), (0, 2), (0, 3), (1, 2), (1, 3), (2, 3))
N_CLASSES = N_GROUPS * len(PAIRS)
MOE_TILE = 256
X1_COLS = D_MODEL + LANES

DSA_TQ = 512
DSA_KC = 256

VALUE_MID_PASSES = 16
PEEL_MAX = 2.0
MIN_NORMAL_KEY = 0x00800000
MAX_SELECT_PASSES = 96


def _cparams(n_axes):
    return pltpu.CompilerParams(dimension_semantics=("arbitrary",) * n_axes,
                                vmem_limit_bytes=VMEM_LIMIT_BYTES)


def _proj_rope_body(x_ref, w_ref, cos_ref, sin_ref, o_ref, *, n_sub):
    sub = x_ref.shape[0] // n_sub
    for mi in range(n_sub):
        rows = slice(mi * sub, (mi + 1) * sub)
        acc = jnp.dot(x_ref[rows, :], w_ref[0], preferred_element_type=F32)
        a1, a2 = acc[:, :HALF_TN], acc[:, HALF_TN:]
        c, sn = cos_ref[rows, :], sin_ref[rows, :]
        o_ref[rows, :HALF_TN] = (a1 * c - a2 * sn).astype(o_ref.dtype)
        o_ref[rows, HALF_TN:] = (a2 * c + a1 * sn).astype(o_ref.dtype)


def _proj_plain_body(x_ref, w_ref, o_ref, *, n_sub):
    sub = x_ref.shape[0] // n_sub
    for mi in range(n_sub):
        rows = slice(mi * sub, (mi + 1) * sub)
        o_ref[rows, :] = jnp.dot(x_ref[rows, :], w_ref[0], preferred_element_type=F32).astype(o_ref.dtype)


def _column_tiles(w):
    k_dim, n = w.shape
    return w.reshape(k_dim, n // PROJ_TN, PROJ_TN).transpose(1, 0, 2)


def _proj_rope(x, x_col_block, k_dim, w, cos_t, sin_t, *, tm, seq):
    t_tokens = x.shape[0]
    n = w.shape[1]
    pos_blocks = seq // tm
    return pl.pallas_call(
        functools.partial(_proj_rope_body, n_sub=max(tm // 256, 1)),
        grid=(t_tokens // tm, n // PROJ_TN),
        in_specs=[
            pl.BlockSpec((tm, k_dim), lambda i, j: (i, x_col_block)),
            pl.BlockSpec((1, k_dim, PROJ_TN), lambda i, j: (j, 0, 0)),
            pl.BlockSpec((tm, HALF_TN), lambda i, j: (i % pos_blocks, 0)),
            pl.BlockSpec((tm, HALF_TN), lambda i, j: (i % pos_blocks, 0)),
        ],
        out_specs=pl.BlockSpec((tm, PROJ_TN), lambda i, j: (i, j)),
        out_shape=jax.ShapeDtypeStruct((t_tokens, n), BF16),
        compiler_params=_cparams(2),
        name="proj_rope",
    )(x, _column_tiles(w), cos_t, sin_t)


def _proj_plain(x, w, *, tm):
    t_tokens, k_dim = x.shape
    n = w.shape[1]
    return pl.pallas_call(
        functools.partial(_proj_plain_body, n_sub=max(tm // 256, 1)),
        grid=(t_tokens // tm, n // PROJ_TN),
        in_specs=[
            pl.BlockSpec((tm, k_dim), lambda i, j: (i, 0)),
            pl.BlockSpec((1, k_dim, PROJ_TN), lambda i, j: (j, 0, 0)),
        ],
        out_specs=pl.BlockSpec((tm, PROJ_TN), lambda i, j: (i, j)),
        out_shape=jax.ShapeDtypeStruct((t_tokens, n), BF16),
        compiler_params=_cparams(2),
        name="proj_plain",
    )(x, _column_tiles(w))


def _f32_to_key(x):
    b = lax.bitcast_convert_type(x, jnp.int32)
    return jnp.where(b < 0, b ^ jnp.int32(0x7FFFFFFF), b)


def _key_to_f32(k):
    b = jnp.where(k < 0, k ^ jnp.int32(0x7FFFFFFF), k)
    return lax.bitcast_convert_type(b, F32)


def _fold8(x, op):
    return op(x.reshape(x.shape[0] // SUBLANES, SUBLANES, x.shape[1]), axis=0)


def _tile_T(x_ref):
    return x_ref[...].astype(F32).T


def _head_rows(xT, h):
    half = HEAD_DIM // 2
    return xT[h * half:(h + 1) * half], xT[HALF_TN + h * half:HALF_TN + (h + 1) * half]


def _place_head(first, second, slot):
    half = HEAD_DIM // 2
    w = first.shape[1]
    before, after = slot * half, HALF_TN - (slot + 1) * half
    pieces = []
    for part in (first, second):
        pieces += [jnp.zeros((before, w), part.dtype)] * (before > 0) + [part] + [jnp.zeros((after, w), part.dtype)] * (after > 0)
    return jnp.concatenate(pieces, axis=0)


def _store_rows(o_ref, oT_heads):
    o_ref[...] = jnp.concatenate(oT_heads, axis=0).T.astype(o_ref.dtype)


IK_SLOT = 2


def _dsa_body(q_ref, iq_ref, iw_ref, ik_ref, kv_ref, vaT_ref, o_ref, sc_ref, acc_ref, sbuf_ref, *, tq, kc, topk):
    i = pl.program_id(1)
    q0 = i * tq
    nch = (q0 + tq + kc - 1) // kc
    qpos = q0 + lax.broadcasted_iota(jnp.int32, (1, tq), 1)
    iqT = _tile_T(iq_ref)
    iq_ops = [_place_head(*[p.astype(BF16) for p in _head_rows(iqT, h)], IK_SLOT) for h in range(A_IDX_HEADS)]
    w_idx = iw_ref[...].astype(F32).T[:A_IDX_HEADS]

    def score_chunk(c, carry, diagonal):
        mn, mx = carry
        k0 = pl.multiple_of(c * kc, kc)
        ikc = ik_ref[pl.ds(k0, kc), :]
        acc = jnp.zeros((kc, tq), F32)
        for h in range(A_IDX_HEADS):
            d = jnp.dot(ikc, iq_ops[h], preferred_element_type=F32)
            acc = acc + w_idx[h:h + 1, :] * jnp.maximum(d, 0.0)
        if diagonal:
            causal = (k0 + lax.broadcasted_iota(jnp.int32, (kc, 1), 0)) <= qpos
            lo_part, hi_part = jnp.where(causal, acc, jnp.inf), jnp.where(causal, acc, -jnp.inf)
        else:
            lo_part = hi_part = acc
        sc_ref[pl.ds(k0, kc), :] = hi_part
        return jnp.minimum(mn, _fold8(lo_part, jnp.min)), jnp.maximum(mx, _fold8(hi_part, jnp.max))

    n_full = (q0 + 1) // kc
    carry = lax.fori_loop(0, n_full, functools.partial(score_chunk, diagonal=False),
                          (jnp.full((SUBLANES, tq), jnp.inf, F32), jnp.full((SUBLANES, tq), -jnp.inf, F32)))
    mn8, mx8 = lax.fori_loop(n_full, nch, functools.partial(score_chunk, diagonal=True), carry)
    row_min = jnp.min(mn8, axis=0, keepdims=True)
    row_max = jnp.max(mx8, axis=0, keepdims=True)

    pc = 2 * kc
    npair = (nch + 1) // 2
    n_chunks_total = vaT_ref.shape[1]

    @pl.when(nch % 2 == 1)
    def _():
        sc_ref[pl.ds(pl.multiple_of(nch * kc, kc), kc), :] = jnp.full((kc, tq), -jnp.inf, F32)

    def count_ge(thr):
        def body(j, cnt):
            s = sc_ref[pl.ds(pl.multiple_of(j * pc, pc), pc), :]
            return cnt + _fold8(jnp.where(s >= thr, 1.0, 0.0), jnp.sum)
        cnt8 = lax.fori_loop(0, npair, body, jnp.zeros((SUBLANES, tq), F32))
        return jnp.sum(cnt8, axis=0, keepdims=True)

    def min_ge(thr):
        def body(j, mn):
            s = sc_ref[pl.ds(pl.multiple_of(j * pc, pc), pc), :]
            return jnp.minimum(mn, _fold8(jnp.where(s >= thr, s, jnp.inf), jnp.min))
        mn8 = lax.fori_loop(0, npair, body, jnp.full((SUBLANES, tq), jnp.inf, F32))
        return jnp.min(mn8, axis=0, keepdims=True)

    def count_zero():
        def body(j, carry):
            c0, cp = carry
            s = sc_ref[pl.ds(pl.multiple_of(j * pc, pc), pc), :]
            return (c0 + _fold8(jnp.where(s >= 0.0, 1.0, 0.0), jnp.sum),
                    cp + _fold8(jnp.where(s > 0.0, 1.0, 0.0), jnp.sum))
        z8 = jnp.zeros((SUBLANES, tq), F32)
        c0, cp = lax.fori_loop(0, npair, body, (z8, z8))
        return jnp.sum(c0, axis=0, keepdims=True), jnp.sum(cp, axis=0, keepdims=True)

    kf = float(topk)
    n_valid = (qpos + 1).astype(F32)
    few = n_valid <= kf
    cnt_nonneg, cnt_pos = count_zero()
    is_zero = jnp.logical_and(cnt_pos < kf, cnt_nonneg >= kf)
    is_pos = cnt_pos >= kf
    lo0 = jnp.where(is_zero, 0, jnp.where(is_pos, MIN_NORMAL_KEY, _f32_to_key(row_min)))
    hi0 = jnp.where(is_zero, 1, jnp.where(is_pos, _f32_to_key(row_max) + 1, -1))
    cnt0 = jnp.where(is_zero, cnt_nonneg, jnp.where(is_pos, cnt_pos, n_valid))
    lo0 = jnp.where(few, _f32_to_key(row_min), lo0)
    hi0 = jnp.where(few, lo0 + 1, hi0)

    def unfinished(lo, hi, cnt_lo):
        return jnp.logical_and(lo + 1 != hi, cnt_lo != kf)

    def bis_cond(st):
        lo, hi, cnt_lo, it = st
        return jnp.logical_and(jnp.max(jnp.where(unfinished(lo, hi, cnt_lo), 1.0, 0.0)) > 0.0, it < MAX_SELECT_PASSES)

    def bisect(st):
        lo, hi, cnt_lo, it = st
        nd = unfinished(lo, hi, cnt_lo)
        mid_val = _f32_to_key(0.5 * _key_to_f32(lo) + 0.5 * _key_to_f32(hi))
        mid_key = lo + lax.shift_right_logical(hi - lo, 1)
        mid = jnp.where(it < VALUE_MID_PASSES, jnp.clip(mid_val, lo + 1, hi - 1), mid_key)
        cnt = count_ge(_key_to_f32(mid))
        ge = cnt >= kf
        up = jnp.logical_and(nd, ge)
        down = jnp.logical_and(nd, jnp.logical_not(ge))
        return jnp.where(up, mid, lo), jnp.where(down, mid, hi), jnp.where(up, cnt, cnt_lo), it + 1

    def peel(st):
        lo, hi, cnt_lo, it = st
        nd = unfinished(lo, hi, cnt_lo)
        k1 = jnp.maximum(_f32_to_key(min_ge(_key_to_f32(lo))), lo)
        cnt = count_ge(_key_to_f32(k1 + 1))
        ge = cnt >= kf
        up = jnp.logical_and(nd, ge)
        down = jnp.logical_and(nd, jnp.logical_not(ge))
        new_lo = jnp.where(nd, jnp.where(ge, k1 + 1, k1), lo)
        return new_lo, jnp.where(down, k1 + 1, hi), jnp.where(up, cnt, cnt_lo), it + 2

    def bis_body(st):
        lo, hi, cnt_lo, it = st
        excess = jnp.max(jnp.where(unfinished(lo, hi, cnt_lo), cnt_lo - kf, 0.0))
        return lax.cond(excess > PEEL_MAX, bisect, peel, st)

    lo, _, cnt_lo, _ = lax.while_loop(bis_cond, bis_body, (lo0, hi0, cnt0, jnp.int32(0)))
    thr = _key_to_f32(lo)

    tie = jnp.logical_and(cnt_lo > kf, jnp.logical_not(few))

    @pl.when(jnp.max(jnp.where(tie, 1.0, 0.0)) > 0.0)
    def _():
        def count_gt():
            def gt_body(c, cnt):
                k0 = pl.multiple_of(c * kc, kc)
                s = sc_ref[pl.ds(k0, kc), :]
                return cnt + _fold8(jnp.where(s > thr, 1.0, 0.0), jnp.sum)
            gt8 = lax.fori_loop(0, nch, gt_body, jnp.zeros((SUBLANES, tq), F32))
            return jnp.sum(gt8, axis=0, keepdims=True)

        zero_thr = thr == 0.0
        other = jnp.max(jnp.where(jnp.logical_and(tie, jnp.logical_not(zero_thr)), 1.0, 0.0)) > 0.0
        gt = lax.cond(other, count_gt, lambda: cnt_pos)
        need = kf - jnp.where(zero_thr, cnt_pos, gt)
        r = lax.broadcasted_iota(jnp.int32, (kc, kc), 0)
        cidx = lax.broadcasted_iota(jnp.int32, (kc, kc), 1)
        tri = jnp.where(cidx <= r, 1.0, 0.0).astype(BF16)

        def tie_body(c, seen):
            k0 = pl.multiple_of(c * kc, kc)
            s = sc_ref[pl.ds(k0, kc), :]
            eqf = jnp.where(s == thr, jnp.where(tie, 1.0, 0.0), 0.0)
            pre = jnp.dot(tri, eqf.astype(BF16), preferred_element_type=F32) + seen
            drop = eqf * jnp.where(pre > need, 1.0, 0.0)
            sc_ref[pl.ds(k0, kc), :] = jnp.where(drop > 0.0, -jnp.inf, s)
            return seen + jnp.sum(eqf, axis=0, keepdims=True)

        lax.fori_loop(0, nch, tie_body, jnp.zeros((1, tq), F32))

    qT_all = (_tile_T(q_ref) * Q_SCALE).astype(BF16)
    qT = jnp.concatenate([_place_head(*_head_rows(qT_all, h), 0) for h in range(A_HEADS)], axis=1)
    acc_ref[...] = jnp.zeros_like(acc_ref)
    ones_rows = jnp.ones((SUBLANES, kc), BF16)

    def issue_scores(c, slot):
        k0 = pl.multiple_of(jnp.minimum(c, n_chunks_total - 1) * kc, kc)
        sbuf_ref[slot] = jnp.dot(kv_ref[pl.ds(k0, kc), :], qT, preferred_element_type=F32)

    def softmax_pv(c, slot, m):
        k0 = pl.multiple_of(c * kc, kc)
        v1 = jnp.concatenate([vaT_ref[0, c], ones_rows], axis=0)
        sel = sc_ref[pl.ds(k0, kc), :] >= thr
        s = jnp.concatenate([jnp.where(sel, sbuf_ref[slot, :, h * tq:(h + 1) * tq], NEG_BIG)
                             for h in range(A_HEADS)], axis=1)
        m_new = jnp.maximum(m, jnp.max(s, axis=0, keepdims=True))
        alpha = jnp.exp2(m - m_new)
        p = jnp.exp2((s - m_new).astype(BF16))
        acc_ref[...] = acc_ref[...] * alpha + jnp.dot(v1, p, preferred_element_type=F32)
        return m_new

    def chunk_pair(j, m):
        c0 = 2 * j
        issue_scores(c0 + 1, 1)
        m = softmax_pv(c0, 0, m)
        issue_scores(c0 + 2, 0)
        return softmax_pv(c0 + 1, 1, m)

    issue_scores(0, 0)
    lax.fori_loop(0, npair, chunk_pair, jnp.full((1, A_HEADS * tq), NEG_BIG, F32))
    a = acc_ref[...]
    o = a[:HEAD_DIM] / a[HEAD_DIM:HEAD_DIM + 1]
    _store_rows(o_ref, [o[:, h * tq:(h + 1) * tq] for h in range(A_HEADS)])


def _dsa(h_rope, h_plain, kv, vaT4, *, bsz, seq, tq, kc, topk):
    nq = seq // tq
    return pl.pallas_call(
        functools.partial(_dsa_body, tq=tq, kc=kc, topk=topk),
        grid=(bsz, nq),
        in_specs=[
            pl.BlockSpec((tq, PROJ_TN), lambda b, i: (b * nq + i, TILE_AQ)),
            pl.BlockSpec((tq, PROJ_TN), lambda b, i: (b * nq + i, TILE_AIQ)),
            pl.BlockSpec((tq, LANES), lambda b, i: (b * nq + i, COL_AIW // LANES)),
            pl.BlockSpec((seq, PROJ_TN), lambda b, i: (b, TILE_CK_AIK)),
            pl.BlockSpec((seq, PROJ_TN), lambda b, i: (b, 0)),
            pl.BlockSpec((1, seq // kc, HEAD_DIM, kc), lambda b, i: (b, 0, 0, 0)),
        ],
        out_specs=pl.BlockSpec((tq, A_W), lambda b, i: (b * nq + i, 0)),
        out_shape=jax.ShapeDtypeStruct((bsz * seq, A_W), BF16),
        scratch_shapes=[pltpu.VMEM((seq + kc, tq), F32),
                        pltpu.VMEM((HEAD_DIM + SUBLANES, A_HEADS * tq), F32),
                        pltpu.VMEM((2, kc, A_HEADS * tq), F32)],
        compiler_params=_cparams(2),
        name="dsa_attention",
    )(h_rope, h_rope, h_plain, h_rope, kv, vaT4)


def _kmean_body(k_ref, o_ref):
    o_ref[0] = jnp.mean(k_ref[...].astype(F32), axis=0, keepdims=True)


def _moba_kmean(h_rope, *, n_tokens):
    nblk = n_tokens // B_BLOCK
    return pl.pallas_call(
        _kmean_body,
        grid=(nblk,),
        in_specs=[pl.BlockSpec((B_BLOCK, B_W), lambda i: (i, TILE_BK))],
        out_specs=pl.BlockSpec((1, 1, B_W), lambda i: (i, 0, 0)),
        out_shape=jax.ShapeDtypeStruct((nblk, 1, B_W), F32),
        compiler_params=_cparams(1),
        name="moba_kmean",
    )(h_rope)


def _moba_gate_body(q_ref, km_ref, sel_ref, *, tq, nkb, nsel):
    i = pl.program_id(1)
    qpos = i * tq + lax.broadcasted_iota(jnp.int32, (1, tq), 1)
    cur = qpos // B_BLOCK
    row = lax.broadcasted_iota(jnp.int32, (nkb, tq), 0)
    past = row < cur
    qT = _tile_T(q_ref).astype(BF16)
    km = km_ref[0]
    col_head = (lax.broadcasted_iota(jnp.int32, km.shape, 1) % HALF_TN) // (HEAD_DIM // 2)
    for h in range(B_HEADS):
        km_h = jnp.where(col_head == h, km, 0.0).astype(BF16)
        g = jnp.dot(km_h, qT, preferred_element_type=F32)
        g = jnp.where(past, g, -jnp.inf)
        sel = jnp.zeros((nkb, tq), F32)
        for _ in range(nsel):
            mx = jnp.max(g, axis=0, keepdims=True)
            idx = jnp.min(jnp.where(g == mx, row, nkb), axis=0, keepdims=True)
            pick = row == idx
            sel = jnp.where(pick, 1.0, sel)
            g = jnp.where(pick, -jnp.inf, g)
        sel_ref[0, h] = jnp.where(past, sel, 0.0)


def _moba_gate(h_rope, kmean, *, bsz, seq, tq):
    nkb = seq // B_BLOCK
    nsel = min(B_TOPK_BLOCKS, nkb)
    nq = seq // tq
    return pl.pallas_call(
        functools.partial(_moba_gate_body, tq=tq, nkb=nkb, nsel=nsel),
        grid=(bsz, nq),
        in_specs=[
            pl.BlockSpec((tq, PROJ_TN), lambda b, i: (b * nq + i, TILE_BQ)),
            pl.BlockSpec((1, nkb, B_W), lambda b, i: (b, 0, 0)),
        ],
        out_specs=pl.BlockSpec((1, B_HEADS, nkb, tq), lambda b, i: (b, 0, 0, i)),
        out_shape=jax.ShapeDtypeStruct((bsz, B_HEADS, nkb, seq), F32),
        compiler_params=_cparams(2),
        name="moba_gate",
    )(h_rope, kmean)


def _moba_body(q_ref, k_ref, vT_ref, sel_ref, o_ref, acc_ref, sbuf_ref, *, tq):
    i = pl.program_id(1)
    qT_all = (_tile_T(q_ref) * Q_SCALE).astype(BF16)
    qT = [_place_head(*_head_rows(qT_all, h), h) for h in range(B_HEADS)]
    ones_rows = jnp.ones((SUBLANES, B_BLOCK), BF16)
    n_past = i

    def values(h, j):
        return jnp.concatenate([vT_ref[0, h, j], ones_rows], axis=0)

    def issue_scores(j, slot):
        k0 = pl.multiple_of(jnp.minimum(j, i) * B_BLOCK, B_BLOCK)
        for h in range(B_HEADS):
            sbuf_ref[slot, h] = jnp.dot(k_ref[pl.ds(k0, B_BLOCK), :], qT[h], preferred_element_type=F32)

    def softmax_pv(j, slot, ms):
        ps, alphas, new_ms = [], [], []
        for h in range(B_HEADS):
            s = jnp.where(sel_ref[0, h, pl.ds(j, 1), :] > 0.0, sbuf_ref[slot, h], NEG_BIG)
            m_new = jnp.maximum(ms[h], jnp.max(s, axis=0, keepdims=True))
            alphas.append(jnp.exp2(ms[h] - m_new))
            ps.append(jnp.exp2((s - m_new).astype(BF16)))
            new_ms.append(m_new)
        for h in range(B_HEADS):
            acc_ref[h] = acc_ref[h] * alphas[h] + jnp.dot(values(h, j), ps[h], preferred_element_type=F32)
        return tuple(new_ms)

    own0 = pl.multiple_of(i * B_BLOCK, B_BLOCK)
    s_own = [jnp.dot(k_ref[pl.ds(own0, B_BLOCK), :], qT[h], preferred_element_type=F32) for h in range(B_HEADS)]
    issue_scores(0, 0)
    causal = lax.broadcasted_iota(jnp.int32, (B_BLOCK, tq), 0) <= lax.broadcasted_iota(jnp.int32, (B_BLOCK, tq), 1)
    ms = []
    for h in range(B_HEADS):
        s = jnp.where(causal, s_own[h], NEG_BIG)
        m0 = jnp.max(s, axis=0, keepdims=True)
        acc_ref[h] = jnp.dot(values(h, i), jnp.exp2((s - m0).astype(BF16)), preferred_element_type=F32)
        ms.append(m0)

    def block_pair(jj, ms):
        j0 = 2 * jj
        issue_scores(j0 + 1, 1)
        ms = softmax_pv(j0, 0, ms)
        issue_scores(j0 + 2, 0)
        return softmax_pv(j0 + 1, 1, ms)

    lax.fori_loop(0, (n_past + 1) // 2, block_pair, tuple(ms))
    outs = []
    for h in range(B_HEADS):
        a = acc_ref[h]
        outs.append(a[:HEAD_DIM] / a[HEAD_DIM:HEAD_DIM + 1])
    _store_rows(o_ref, outs)


def _moba(h_rope, bvT5, sel, *, bsz, seq):
    tq = B_BLOCK
    nkb = seq // B_BLOCK
    nq = seq // tq
    return pl.pallas_call(
        functools.partial(_moba_body, tq=tq),
        grid=(bsz, nq),
        in_specs=[
            pl.BlockSpec((tq, PROJ_TN), lambda b, i: (b * nq + i, TILE_BQ)),
            pl.BlockSpec((seq, PROJ_TN), lambda b, i: (b, TILE_BK)),
            pl.BlockSpec((1, B_HEADS, nkb, HEAD_DIM, B_BLOCK), lambda b, i: (b, 0, 0, 0, 0)),
            pl.BlockSpec((1, B_HEADS, nkb, tq), lambda b, i: (b, 0, 0, i)),
        ],
        out_specs=pl.BlockSpec((tq, B_W), lambda b, i: (b * nq + i, 0)),
        out_shape=jax.ShapeDtypeStruct((bsz * seq, B_W), BF16),
        scratch_shapes=[pltpu.VMEM((B_HEADS, HEAD_DIM + SUBLANES, tq), F32),
                        pltpu.VMEM((2, B_HEADS, B_BLOCK, tq), F32)],
        compiler_params=_cparams(2),
        name="moba_attention",
    )(h_rope, h_rope, bvT5, sel)


def _swa_body(sink_ref, q0_ref, q1_ref, k0_ref, k1_ref, k2_ref, v0_ref, v1_ref, v2_ref, o_ref, *, tq):
    i = pl.program_id(1)
    q0 = i * tq
    qpos = q0 + lax.broadcasted_iota(jnp.int32, (1, tq), 1)
    nk = tq + C_WINDOW
    kpos = q0 - C_WINDOW + lax.broadcasted_iota(jnp.int32, (nk, 1), 0)
    diff = qpos - kpos
    ok = jnp.where(diff >= 0, jnp.where(diff < C_WINDOW, jnp.where(kpos >= 0, 1.0, 0.0), 0.0), 0.0) > 0.0
    ones_rows = jnp.ones((SUBLANES, nk), BF16)
    group = C_HEADS // C_KV_HEADS
    assert group == HEADS_PER_TILE
    kk = jnp.concatenate([k0_ref[...], k1_ref[...], k2_ref[...]], axis=0)
    outs = []
    for kv, q_ref in enumerate((q0_ref, q1_ref)):
        vv = jnp.concatenate([v0_ref[0, kv], v1_ref[0, kv], v2_ref[0, kv]], axis=1)
        v1 = jnp.concatenate([vv, ones_rows], axis=0)
        heads = [kv * group + g for g in range(group)]
        qT_all = (_tile_T(q_ref) * Q_SCALE).astype(BF16)
        qT = jnp.concatenate([_place_head(*_head_rows(qT_all, g), kv) for g in range(group)], axis=1)
        sink = jnp.concatenate([jnp.full((1, tq), sink_ref[hd] * LOG2E, F32) for hd in heads], axis=1)
        s = jnp.dot(kk, qT, preferred_element_type=F32)
        s = jnp.concatenate([jnp.where(ok, s[:, g * tq:(g + 1) * tq], NEG_BIG) for g in range(group)], axis=1)
        m = jnp.maximum(jnp.max(s, axis=0, keepdims=True), sink)
        p = jnp.exp2((s - m).astype(BF16))
        a = jnp.dot(v1, p, preferred_element_type=F32)
        o = a[:HEAD_DIM] / (a[HEAD_DIM:HEAD_DIM + 1] + jnp.exp2(sink - m))
        outs += [o[:, g * tq:(g + 1) * tq] for g in range(group)]
    _store_rows(o_ref, outs)


def _swa(sinks, h_rope, cvT, *, bsz, seq, tq):
    r = tq // C_WINDOW
    assert r == 2
    nq = seq // tq
    nwb = seq // C_WINDOW

    def kspec(off):
        return pl.BlockSpec((C_WINDOW, PROJ_TN),
                            lambda b, i, s: (b * nwb + jnp.maximum(i * r + off, 0), TILE_CK_AIK))

    def vspec(off):
        return pl.BlockSpec((1, C_KV_HEADS, HEAD_DIM, C_WINDOW),
                            lambda b, i, s: (b, 0, 0, jnp.maximum(i * r + off, 0)))

    grid_spec = pltpu.PrefetchScalarGridSpec(
        num_scalar_prefetch=1,
        grid=(bsz, nq),
        in_specs=[pl.BlockSpec((tq, PROJ_TN), lambda b, i, s: (b * nq + i, TILE_CQ)),
                  pl.BlockSpec((tq, PROJ_TN), lambda b, i, s: (b * nq + i, TILE_CQ + 1)),
                  kspec(-1), kspec(0), kspec(1), vspec(-1), vspec(0), vspec(1)],
        out_specs=pl.BlockSpec((tq, C_W), lambda b, i, s: (b * nq + i, 0)),
    )
    return pl.pallas_call(
        functools.partial(_swa_body, tq=tq),
        grid_spec=grid_spec,
        out_shape=jax.ShapeDtypeStruct((bsz * seq, C_W), BF16),
        compiler_params=_cparams(2),
        name="swa_attention",
    )(sinks, h_rope, h_rope, h_rope, h_rope, h_rope, cvT, cvT, cvT)


def _layer_norm(z, g, b):
    mu = jnp.mean(z, axis=-1, keepdims=True)
    zc = z - mu
    var = jnp.mean(zc * zc, axis=-1, keepdims=True)
    return zc * lax.rsqrt(var + LN_EPS) * g + b


def _split_bf16(a):
    hi = a.astype(BF16)
    lo = (a - hi.astype(F32)).astype(BF16)
    return hi, lo


def _router(x1, rw_ref, rb_ref, cls_ref):
    nt = (((1,), (1,)), ((), ()))
    xh, xl = _split_bf16(x1)
    wh, wl = _split_bf16(rw_ref[...])
    logits = (lax.dot_general(wh, xh, nt, preferred_element_type=F32)
              + lax.dot_general(wh, xl, nt, preferred_element_type=F32)
              + lax.dot_general(wl, xh, nt, preferred_element_type=F32))
    aff = jax.nn.sigmoid(logits)
    score = aff + rb_ref[...]
    tm = x1.shape[0]
    sc = [score[e:e + 1, :] for e in range(N_EXPERTS)]
    af = [aff[e:e + 1, :] for e in range(N_EXPERTS)]
    in_top = []
    grp_score = []
    for gq in range(N_GROUPS):
        gs = jnp.zeros((1, tm), F32)
        for a in range(EXPERTS_PER_GROUP):
            ea = gq * EXPERTS_PER_GROUP + a
            rank = jnp.zeros((1, tm), F32)
            for b in range(EXPERTS_PER_GROUP):
                if b == a:
                    continue
                eb = gq * EXPERTS_PER_GROUP + b
                beats = (sc[eb] >= sc[ea]) if b < a else (sc[eb] > sc[ea])
                rank = rank + jnp.where(beats, 1.0, 0.0)
            top = rank < 2.0
            in_top.append(top)
            gs = gs + jnp.where(top, sc[ea], 0.0)
        grp_score.append(gs)
    best = grp_score[0]
    gstar = jnp.zeros((1, tm), jnp.int32)
    for gq in range(1, N_GROUPS):
        better = grp_score[gq] > best
        best = jnp.where(better, grp_score[gq], best)
        gstar = jnp.where(better, gq, gstar)
    cls = jnp.zeros((1, tm), jnp.int32)
    w_lo = jnp.zeros((1, tm), F32)
    w_hi = jnp.zeros((1, tm), F32)
    for gq in range(N_GROUPS):
        is_g = gstar == gq
        for pi, (a, b) in enumerate(PAIRS):
            ea, eb = gq * EXPERTS_PER_GROUP + a, gq * EXPERTS_PER_GROUP + b
            hit = jnp.where(is_g, jnp.where(in_top[ea], jnp.where(in_top[eb], 1.0, 0.0), 0.0), 0.0) > 0.0
            cls = jnp.where(hit, gq * len(PAIRS) + pi, cls)
            tot = af[ea] + af[eb]
            w_lo = jnp.where(hit, af[ea] / tot, w_lo)
            w_hi = jnp.where(hit, af[eb] / tot, w_hi)
    cls_ref[...] = cls
    return w_lo, w_hi


def _merge_body(oa_ref, ob_ref, oc_ref, ga_ref, gb_ref, gc_ref, x_ref, wa_ref, wb_ref, wc_ref, wo_ref,
                g_ref, b_ref, rw_ref, rb_ref, x1_ref, cls_ref):
    def branch(o_ref, w_ref, gate_ref):
        y = jnp.dot(o_ref[...], w_ref[...], preferred_element_type=F32)
        return jax.nn.sigmoid(gate_ref[...].astype(F32)) * y

    merged = branch(oa_ref, wa_ref, ga_ref) + branch(ob_ref, wb_ref, gb_ref) + branch(oc_ref, wc_ref, gc_ref)
    y = jnp.dot(merged.astype(BF16), wo_ref[...], preferred_element_type=F32)
    x1 = _layer_norm(DN_ALPHA * x_ref[...] + y, g_ref[...], b_ref[...])
    w_lo, w_hi = _router(x1, rw_ref, rb_ref, cls_ref)
    x1_ref[:, :D_MODEL] = x1
    x1_ref[:, D_MODEL:] = jnp.concatenate([w_lo, w_hi, jnp.zeros((LANES - 2, x1.shape[0]), F32)], axis=0).T


def _merge(oa, ob, oc, h_plain, x, wa, wb, wc, wo, ln_g, ln_b, rwT, rb, *, tm):
    t_tokens = x.shape[0]
    gate_blk = COL_GATES // D_MODEL
    full = lambda shape: pl.BlockSpec(shape, lambda i: (0,) * len(shape))
    return pl.pallas_call(
        _merge_body,
        grid=(t_tokens // tm,),
        in_specs=[
            pl.BlockSpec((tm, A_W), lambda i: (i, 0)),
            pl.BlockSpec((tm, B_W), lambda i: (i, 0)),
            pl.BlockSpec((tm, C_W), lambda i: (i, 0)),
            pl.BlockSpec((tm, D_MODEL), lambda i: (i, gate_blk)),
            pl.BlockSpec((tm, D_MODEL), lambda i: (i, gate_blk + 1)),
            pl.BlockSpec((tm, D_MODEL), lambda i: (i, gate_blk + 2)),
            pl.BlockSpec((tm, D_MODEL), lambda i: (i, 0)),
            full((A_W, D_MODEL)), full((B_W, D_MODEL)), full((C_W, D_MODEL)), full((D_MODEL, D_MODEL)),
            full((1, D_MODEL)), full((1, D_MODEL)), full((N_EXPERTS, D_MODEL)), full((N_EXPERTS, 1)),
        ],
        out_specs=[
            pl.BlockSpec((tm, X1_COLS), lambda i: (i, 0)),
            pl.BlockSpec((1, tm), lambda i: (0, i)),
        ],
        out_shape=[
            jax.ShapeDtypeStruct((t_tokens, X1_COLS), F32),
            jax.ShapeDtypeStruct((1, t_tokens), jnp.int32),
        ],
        compiler_params=_cparams(1),
        name="merge_ln_router",
    )(oa, ob, oc, h_plain, h_plain, h_plain, x, wa, wb, wc, wo, ln_g, ln_b, rwT, rb)


def _gather_start(idx_ref, src_hbm, dst_ref, sem, n_rows):
    def start(r, _):
        pltpu.make_async_copy(src_hbm.at[pl.ds(idx_ref[0, 0, r], 1)], dst_ref.at[pl.ds(r, 1)], sem).start()
        return 0

    lax.fori_loop(0, n_rows, start, 0, unroll=8)


def _gather_wait(src_hbm, dst_ref, sem, n_rows):
    pltpu.make_async_copy(src_hbm.at[pl.ds(0, n_rows)], dst_ref, sem).wait()


def _experts_body(e1_ref, e2_ref, nused_ref, src_ref, src_next_ref, x_hbm,
                  wg1_ref, wu1_ref, wd1_ref, wg2_ref, wu2_ref, wd2_ref, o_ref, xbuf, sem):
    s = pl.program_id(0)
    nused = nused_ref[0]
    slot = lax.rem(s, 2)

    @pl.when(s == 0)
    def _():
        _gather_start(src_ref, x_hbm, xbuf.at[0], sem.at[0], MOE_TILE)

    @pl.when(s + 1 < nused)
    def _():
        _gather_start(src_next_ref, x_hbm, xbuf.at[1 - slot], sem.at[1 - slot], MOE_TILE)

    @pl.when(s < nused)
    def _():
        _gather_wait(x_hbm, xbuf.at[slot], sem.at[slot], MOE_TILE)
        xb = xbuf[slot, :, :D_MODEL].astype(BF16)
        wt = xbuf[slot, :, D_MODEL:D_MODEL + 2]

        def expert(wg_ref, wu_ref, wd_ref):
            g = jnp.dot(xb, wg_ref[0], preferred_element_type=F32)
            u = jnp.dot(xb, wu_ref[0], preferred_element_type=F32)
            he = (g * jax.nn.sigmoid(g) * u).astype(BF16)
            return jnp.dot(he, wd_ref[0], preferred_element_type=F32)

        o_ref[...] = wt[:, 0:1] * expert(wg1_ref, wu1_ref, wd1_ref) + wt[:, 1:2] * expert(wg2_ref, wu2_ref, wd2_ref)

    @pl.when(s >= nused)
    def _():
        o_ref[...] = jnp.zeros_like(o_ref)


def _experts(tile_e1, tile_e2, nused, src3, x1, wg, wu, wd):
    ntiles = src3.shape[0]
    wspec_up = lambda which: pl.BlockSpec((1, D_MODEL, D_EXPERT), lambda s, e1, e2, nu: ((e1, e2)[which][s], 0, 0))
    wspec_dn = lambda which: pl.BlockSpec((1, D_EXPERT, D_MODEL), lambda s, e1, e2, nu: ((e1, e2)[which][s], 0, 0))
    grid_spec = pltpu.PrefetchScalarGridSpec(
        num_scalar_prefetch=3,
        grid=(ntiles,),
        in_specs=[
            pl.BlockSpec((1, 1, MOE_TILE), lambda s, e1, e2, nu: (s, 0, 0), memory_space=pltpu.SMEM),
            pl.BlockSpec((1, 1, MOE_TILE), lambda s, e1, e2, nu: (jnp.minimum(s + 1, ntiles - 1), 0, 0),
                         memory_space=pltpu.SMEM),
            pl.BlockSpec(memory_space=pl.ANY),
            wspec_up(0), wspec_up(0), wspec_dn(0), wspec_up(1), wspec_up(1), wspec_dn(1),
        ],
        out_specs=pl.BlockSpec((MOE_TILE, D_MODEL), lambda s, e1, e2, nu: (s, 0)),
        scratch_shapes=[pltpu.VMEM((2, MOE_TILE, X1_COLS), F32), pltpu.SemaphoreType.DMA((2,))],
    )
    return pl.pallas_call(
        _experts_body,
        grid_spec=grid_spec,
        out_shape=jax.ShapeDtypeStruct((ntiles * MOE_TILE, D_MODEL), F32),
        compiler_params=_cparams(1),
        name="moe_experts",
    )(tile_e1, tile_e2, nused, src3, src3, x1, wg, wu, wd, wg, wu, wd)


def _final_body(pos_ref, pos_next_ref, y_hbm, x_ref, g_ref, b_ref, o_ref, ob_ref, ybuf, sem, *, tm):
    s = pl.program_id(0)
    slot = lax.rem(s, 2)

    @pl.when(s == 0)
    def _():
        _gather_start(pos_ref, y_hbm, ybuf.at[0], sem.at[0], tm)

    @pl.when(s + 1 < pl.num_programs(0))
    def _():
        _gather_start(pos_next_ref, y_hbm, ybuf.at[1 - slot], sem.at[1 - slot], tm)

    _gather_wait(y_hbm, ybuf.at[slot], sem.at[slot], tm)
    x2 = _layer_norm(DN_ALPHA * x_ref[...] + ybuf[slot], g_ref[...], b_ref[...])
    o_ref[...] = x2
    ob_ref[...] = x2.astype(BF16)


def _final(pos3, y_sorted, x1, ln_g, ln_b, *, tm):
    t_tokens = x1.shape[0]
    nsteps = t_tokens // tm
    return pl.pallas_call(
        functools.partial(_final_body, tm=tm),
        grid=(nsteps,),
        in_specs=[
            pl.BlockSpec((1, 1, tm), lambda i: (i, 0, 0), memory_space=pltpu.SMEM),
            pl.BlockSpec((1, 1, tm), lambda i: (jnp.minimum(i + 1, nsteps - 1), 0, 0), memory_space=pltpu.SMEM),
            pl.BlockSpec(memory_space=pl.ANY),
            pl.BlockSpec((tm, D_MODEL), lambda i: (i, 0)),
            pl.BlockSpec((1, D_MODEL), lambda i: (0, 0)),
            pl.BlockSpec((1, D_MODEL), lambda i: (0, 0)),
        ],
        out_specs=[pl.BlockSpec((tm, D_MODEL), lambda i: (i, 0)), pl.BlockSpec((tm, D_MODEL), lambda i: (i, 0))],
        out_shape=[jax.ShapeDtypeStruct((t_tokens, D_MODEL), F32), jax.ShapeDtypeStruct((t_tokens, D_MODEL), BF16)],
        scratch_shapes=[pltpu.VMEM((2, tm, D_MODEL), F32), pltpu.SemaphoreType.DMA((2,))],
        compiler_params=_cparams(1),
        name="moe_combine_ln",
    )(pos3, pos3, y_sorted, x1, ln_g, ln_b)


def _rope_tables(seq):
    inv = 1.0 / (ROPE_THETA ** (jnp.arange(0, HEAD_DIM, 2, dtype=F32) / HEAD_DIM))
    ang = jnp.arange(seq, dtype=F32)[:, None] * inv[None, :]
    return jnp.cos(ang), jnp.sin(ang)


def _halves_layout(w_heads):
    d, n = w_heads.shape
    t = w_heads.reshape(d, n // PROJ_TN, HEADS_PER_TILE, 2, HEAD_DIM // 2)
    return t.transpose(0, 1, 3, 2, 4).reshape(d, n)


def _reorder_w_in(w):
    pts = np.cumsum((0,) + IN_SIZES)
    sec = [w[:, pts[k]:pts[k + 1]] for k in range(len(IN_SIZES))]
    a_q, a_c, a_iq, a_ik, a_iw, b_q, b_k, b_v, c_q, c_k, c_v, gates = sec
    zeros = lambda n: jnp.zeros((w.shape[0], n), w.dtype)
    w_rope = _halves_layout(jnp.concatenate([a_q, a_iq, b_q, b_k, c_q, c_k, a_ik, zeros(HEAD_DIM)], axis=1))
    w_plain = jnp.concatenate([gates, b_v, a_c, a_iw, zeros(LANES - A_IDX_HEADS), c_v, zeros(LANES)], axis=1)
    assert w_rope.shape[1] == N_ROPE and w_plain.shape[1] == N_PLAIN
    return w_rope.astype(BF16), w_plain.astype(BF16)


def _kv_weights(w_uk, w_uv):
    half = HEAD_DIM // 2
    zeros = lambda n: jnp.zeros((w_uk.shape[0], n), w_uk.dtype)
    return jnp.concatenate([w_uk[:, :half], w_uv, zeros(HALF_TN - half - HEAD_DIM),
                            w_uk[:, half:], zeros(HALF_TN - half)], axis=1).astype(BF16)


def _moe_plan(cls, n_tokens):
    ntiles = n_tokens // MOE_TILE + N_CLASSES
    onehot = (cls[:, None] == jnp.arange(N_CLASSES, dtype=jnp.int32)[None, :]).astype(jnp.int32)
    csum = jnp.cumsum(onehot, axis=0)
    rank = jnp.sum(onehot * csum, axis=1) - 1
    counts = csum[-1]
    ptiles = (counts + MOE_TILE - 1) // MOE_TILE
    tile_end = jnp.cumsum(ptiles)
    tile_start = tile_end - ptiles
    pos = (tile_start[cls] * MOE_TILE + rank).astype(jnp.int32)
    nused = tile_end[-1:].astype(jnp.int32)
    tile_ids = jnp.arange(ntiles, dtype=jnp.int32)
    tile_cls = jnp.minimum(jnp.sum((tile_ids[:, None] >= tile_end[None, :]).astype(jnp.int32), axis=1),
                           N_CLASSES - 1).astype(jnp.int32)
    pair = np.array(PAIRS, dtype=np.int32)
    e_lo = jnp.asarray(np.repeat(np.arange(N_GROUPS), len(PAIRS)) * EXPERTS_PER_GROUP + np.tile(pair[:, 0], N_GROUPS), jnp.int32)
    e_hi = jnp.asarray(np.repeat(np.arange(N_GROUPS), len(PAIRS)) * EXPERTS_PER_GROUP + np.tile(pair[:, 1], N_GROUPS), jnp.int32)
    src = jnp.zeros((ntiles * MOE_TILE,), jnp.int32).at[pos].set(jnp.arange(n_tokens, dtype=jnp.int32))
    return e_lo[tile_cls], e_hi[tile_cls], nused, src.reshape(ntiles, 1, MOE_TILE), pos


def kernel(x, w_in, a_w_uk, a_w_uv, c_sinks, w_branch, w_o, ln1_g, ln1_b, router_w, router_b,
           moe_w_gate, moe_w_up, moe_w_down, ln2_g, ln2_b):
    bsz, seq, _ = x.shape
    n_tokens = bsz * seq
    topk = min(A_TOPK_MAX, seq // 4)
    nkb = seq // B_BLOCK
    tq = 256
    kc = min(DSA_KC, seq)
    tm_proj = min(1024, seq)
    tm_row = 512
    half = HEAD_DIM // 2

    cos, sin = _rope_tables(seq)
    cos_in = jnp.tile(cos, (1, HEADS_PER_TILE))
    sin_in = jnp.tile(sin, (1, HEADS_PER_TILE))
    cos_kv = jnp.concatenate([cos, jnp.ones((seq, HALF_TN - half), F32)], axis=1)
    sin_kv = jnp.concatenate([sin, jnp.zeros((seq, HALF_TN - half), F32)], axis=1)
    rwT = router_w.T
    rb = router_b.reshape(N_EXPERTS, 1)

    xf = x.reshape(n_tokens, D_MODEL)
    xb = xf.astype(BF16)
    for l in range(DEPTH):
        w_rope, w_plain = _reorder_w_in(w_in[l])
        h_rope = _proj_rope(xb, 0, D_MODEL, w_rope, cos_in, sin_in, tm=tm_proj, seq=seq)
        h_plain = _proj_plain(xb, w_plain, tm=tm_proj)
        kv = _proj_rope(h_plain, COL_AC // A_KV_RANK, A_KV_RANK, _kv_weights(a_w_uk[l], a_w_uv[l]),
                        cos_kv, sin_kv, tm=tm_proj, seq=seq)

        vaT4 = kv[:, KV_VA0:KV_VA0 + HEAD_DIM].reshape(bsz, seq // kc, kc, HEAD_DIM).transpose(0, 1, 3, 2)
        bvT5 = (h_plain[:, COL_BV:COL_BV + B_W].reshape(bsz, nkb, B_BLOCK, B_HEADS, HEAD_DIM)
                .transpose(0, 3, 1, 4, 2))
        cvT = (h_plain[:, COL_CV:COL_CV + C_KV_HEADS * HEAD_DIM].reshape(bsz, seq, C_KV_HEADS, HEAD_DIM)
               .transpose(0, 2, 3, 1))

        o_a = _dsa(h_rope, h_plain, kv, vaT4, bsz=bsz, seq=seq, tq=min(DSA_TQ, seq), kc=kc, topk=topk)
        kmean = _moba_kmean(h_rope, n_tokens=n_tokens).reshape(bsz, nkb, B_W)
        sel = _moba_gate(h_rope, kmean, bsz=bsz, seq=seq, tq=tq)
        o_b = _moba(h_rope, bvT5, sel, bsz=bsz, seq=seq)
        o_c = _swa(c_sinks[l], h_rope, cvT, bsz=bsz, seq=seq, tq=tq)

        wb_all = w_branch[l].astype(BF16)
        x1, cls = _merge(o_a, o_b, o_c, h_plain, xf,
                              wb_all[:A_W], wb_all[A_W:A_W + B_W], wb_all[A_W + B_W:], w_o[l].astype(BF16),
                              ln1_g[l].reshape(1, D_MODEL), ln1_b[l].reshape(1, D_MODEL), rwT, rb, tm=tm_row)

        e1, e2, nused, src3, pos = _moe_plan(cls[0], n_tokens)
        y_sorted = _experts(e1, e2, nused, src3, x1,
                            moe_w_gate[l].astype(BF16), moe_w_up[l].astype(BF16), moe_w_down[l].astype(BF16))
        xf, xb = _final(pos.reshape(n_tokens // tm_row, 1, tm_row), y_sorted, x1,
                        ln2_g[l].reshape(1, D_MODEL), ln2_b[l].reshape(1, D_MODEL), tm=tm_row)
    return xf.reshape(bsz, seq, D_MODEL)
```

```python
import functools
import math

import jax
import jax.numpy as jnp
import numpy as np
from jax import lax
from jax.experimental import pallas as pl
from jax.experimental.pallas import tpu as pltpu

D_MODEL = 1024
DEPTH = 2
HEAD_DIM = 64
ROPE_THETA = 10000.0
LN_EPS = 1e-5
A_HEADS = 4
A_KV_RANK = 128
A_IDX_HEADS = 4
A_IDX_DIM = 64
A_TOPK_MAX = 256
B_HEADS = 4
B_BLOCK = 256
B_TOPK_BLOCKS = 3
C_HEADS = 8
C_KV_HEADS = 2
C_WINDOW = 128
N_BRANCH = 3
A_W = A_HEADS * HEAD_DIM
B_W = B_HEADS * HEAD_DIM
C_W = C_HEADS * HEAD_DIM
IN_SIZES = (A_W, A_KV_RANK, A_IDX_HEADS * A_IDX_DIM, A_IDX_DIM, A_IDX_HEADS,
            B_W, B_W, B_W, C_W, C_KV_HEADS * HEAD_DIM, C_KV_HEADS * HEAD_DIM,
            N_BRANCH * D_MODEL)
N_EXPERTS = 16
N_GROUPS = 4
EXPERTS_PER_GROUP = 4
D_EXPERT = 512
DN_ALPHA = (2 * DEPTH) ** 0.25
ATTN_SCALE = HEAD_DIM ** -0.5
LOG2E = math.log2(math.e)
Q_SCALE = ATTN_SCALE * LOG2E

LANES = 128
SUBLANES = 8
VMEM_LIMIT_BYTES = 56 * 1024 * 1024

NEG_BIG = -1e30
BF16 = jnp.bfloat16
F32 = jnp.float32

PROJ_TN = 256
HALF_TN = PROJ_TN // 2
HEADS_PER_TILE = PROJ_TN // HEAD_DIM
TILE_AQ = 0
TILE_AIQ = 1
TILE_BQ = 2
TILE_BK = 3
TILE_CQ = 4
TILE_CK_AIK = 6
N_ROPE = 7 * PROJ_TN
COL_GATES = 0
COL_BV = 3072
COL_AC = 3328
COL_AIW = 3456
COL_CV = 3584
N_PLAIN = 3840
PLAIN_TN = 768
KV_VA0 = HEAD_DIM // 2

PAIRS = ((0, 1), (0, 2), (0, 3), (1, 2), (1, 3), (2, 3))
N_CLASSES = N_GROUPS * len(PAIRS)
MOE_TILE = 256
X1_COLS = D_MODEL + LANES

DSA_TQ = 512
DSA_KC = 256

VALUE_MID_PASSES = 16
PEEL_MAX = 2.0
MIN_NORMAL_KEY = 0x00800000
MAX_SELECT_PASSES = 96


def _cparams(n_axes):
    return pltpu.CompilerParams(dimension_semantics=("arbitrary",) * n_axes,
                                vmem_limit_bytes=VMEM_LIMIT_BYTES)


def _proj_rope_body(x_ref, w_ref, cos_ref, sin_ref, o_ref, *, n_sub):
    sub = x_ref.shape[0] // n_sub
    for mi in range(n_sub):
        rows = slice(mi * sub, (mi + 1) * sub)
        acc = jnp.dot(x_ref[rows, :], w_ref[0], preferred_element_type=F32)
        a1, a2 = acc[:, :HALF_TN], acc[:, HALF_TN:]
        c, sn = cos_ref[rows, :], sin_ref[rows, :]
        o_ref[rows, :HALF_TN] = (a1 * c - a2 * sn).astype(o_ref.dtype)
        o_ref[rows, HALF_TN:] = (a2 * c + a1 * sn).astype(o_ref.dtype)


def _proj_plain_body(x_ref, w_ref, o_ref, *, n_sub):
    sub = x_ref.shape[0] // n_sub
    for mi in range(n_sub):
        rows = slice(mi * sub, (mi + 1) * sub)
        o_ref[rows, :] = jnp.dot(x_ref[rows, :], w_ref[0], preferred_element_type=F32).astype(o_ref.dtype)


def _column_tiles(w, tn=PROJ_TN):
    k_dim, n = w.shape
    return w.reshape(k_dim, n // tn, tn).transpose(1, 0, 2)


def _proj_rope(x, x_col_block, k_dim, w, cos_t, sin_t, *, tm, seq):
    t_tokens = x.shape[0]
    n = w.shape[1]
    pos_blocks = seq // tm
    return pl.pallas_call(
        functools.partial(_proj_rope_body, n_sub=max(tm // 256, 1)),
        grid=(t_tokens // tm, n // PROJ_TN),
        in_specs=[
            pl.BlockSpec((tm, k_dim), lambda i, j: (i, x_col_block)),
            pl.BlockSpec((1, k_dim, PROJ_TN), lambda i, j: (j, 0, 0)),
            pl.BlockSpec((tm, HALF_TN), lambda i, j: (i % pos_blocks, 0)),
            pl.BlockSpec((tm, HALF_TN), lambda i, j: (i % pos_blocks, 0)),
        ],
        out_specs=pl.BlockSpec((tm, PROJ_TN), lambda i, j: (i, j)),
        out_shape=jax.ShapeDtypeStruct((t_tokens, n), BF16),
        compiler_params=_cparams(2),
        name="proj_rope",
    )(x, _column_tiles(w), cos_t, sin_t)


def _proj_plain(x, w, *, tm):
    t_tokens, k_dim = x.shape
    n = w.shape[1]
    return pl.pallas_call(
        functools.partial(_proj_plain_body, n_sub=max(tm // 256, 1)),
        grid=(t_tokens // tm, n // PLAIN_TN),
        in_specs=[
            pl.BlockSpec((tm, k_dim), lambda i, j: (i, 0)),
            pl.BlockSpec((1, k_dim, PLAIN_TN), lambda i, j: (j, 0, 0)),
        ],
        out_specs=pl.BlockSpec((tm, PLAIN_TN), lambda i, j: (i, j)),
        out_shape=jax.ShapeDtypeStruct((t_tokens, n), BF16),
        compiler_params=_cparams(2),
        name="proj_plain",
    )(x, _column_tiles(w, PLAIN_TN))


def _f32_to_key(x):
    b = lax.bitcast_convert_type(x, jnp.int32)
    return jnp.where(b < 0, b ^ jnp.int32(0x7FFFFFFF), b)


def _key_to_f32(k):
    b = jnp.where(k < 0, k ^ jnp.int32(0x7FFFFFFF), k)
    return lax.bitcast_convert_type(b, F32)


def _fold8(x, op):
    return op(x.reshape(x.shape[0] // SUBLANES, SUBLANES, x.shape[1]), axis=0)


def _tile_T(x_ref):
    return x_ref[...].astype(F32).T


def _head_rows(xT, h):
    half = HEAD_DIM // 2
    return xT[h * half:(h + 1) * half], xT[HALF_TN + h * half:HALF_TN + (h + 1) * half]


def _place_head(first, second, slot):
    half = HEAD_DIM // 2
    w = first.shape[1]
    before, after = slot * half, HALF_TN - (slot + 1) * half
    pieces = []
    for part in (first, second):
        pieces += [jnp.zeros((before, w), part.dtype)] * (before > 0) + [part] + [jnp.zeros((after, w), part.dtype)] * (after > 0)
    return jnp.concatenate(pieces, axis=0)


def _store_rows(o_ref, oT_heads):
    o_ref[...] = jnp.concatenate(oT_heads, axis=0).T.astype(o_ref.dtype)


IK_SLOT = 2


def _dsa_body(q_ref, iq_ref, iw_ref, ik_ref, kv_ref, vaT_ref, o_ref, sc_ref, acc_ref, sbuf_ref, *, tq, kc, topk):
    i = pl.program_id(1)
    q0 = i * tq
    nch = (q0 + tq + kc - 1) // kc
    qpos = q0 + lax.broadcasted_iota(jnp.int32, (1, tq), 1)
    iqT = _tile_T(iq_ref)
    iq_ops = [_place_head(*[p.astype(BF16) for p in _head_rows(iqT, h)], IK_SLOT) for h in range(A_IDX_HEADS)]
    w_idx = iw_ref[...].astype(F32).T[:A_IDX_HEADS]

    def score_chunk(c, carry, diagonal):
        mn, mx = carry
        k0 = pl.multiple_of(c * kc, kc)
        ikc = ik_ref[pl.ds(k0, kc), :]
        acc = jnp.zeros((kc, tq), F32)
        for h in range(A_IDX_HEADS):
            d = jnp.dot(ikc, iq_ops[h], preferred_element_type=F32)
            acc = acc + w_idx[h:h + 1, :] * jnp.maximum(d, 0.0)
        if diagonal:
            causal = (k0 + lax.broadcasted_iota(jnp.int32, (kc, 1), 0)) <= qpos
            lo_part, hi_part = jnp.where(causal, acc, jnp.inf), jnp.where(causal, acc, -jnp.inf)
        else:
            lo_part = hi_part = acc
        sc_ref[pl.ds(k0, kc), :] = hi_part
        return jnp.minimum(mn, _fold8(lo_part, jnp.min)), jnp.maximum(mx, _fold8(hi_part, jnp.max))

    n_full = (q0 + 1) // kc
    carry = lax.fori_loop(0, n_full, functools.partial(score_chunk, diagonal=False),
                          (jnp.full((SUBLANES, tq), jnp.inf, F32), jnp.full((SUBLANES, tq), -jnp.inf, F32)))
    mn8, mx8 = lax.fori_loop(n_full, nch, functools.partial(score_chunk, diagonal=True), carry)
    row_min = jnp.min(mn8, axis=0, keepdims=True)
    row_max = jnp.max(mx8, axis=0, keepdims=True)

    pc = 2 * kc
    npair = (nch + 1) // 2
    n_chunks_total = vaT_ref.shape[1]

    @pl.when(nch % 2 == 1)
    def _():
        sc_ref[pl.ds(pl.multiple_of(nch * kc, kc), kc), :] = jnp.full((kc, tq), -jnp.inf, F32)

    def count_ge(thr):
        def body(j, cnt):
            s = sc_ref[pl.ds(pl.multiple_of(j * pc, pc), pc), :]
            return cnt + _fold8(jnp.where(s >= thr, 1.0, 0.0), jnp.sum)
        cnt8 = lax.fori_loop(0, npair, body, jnp.zeros((SUBLANES, tq), F32))
        return jnp.sum(cnt8, axis=0, keepdims=True)

    def min_ge(thr):
        def body(j, mn):
            s = sc_ref[pl.ds(pl.multiple_of(j * pc, pc), pc), :]
            return jnp.minimum(mn, _fold8(jnp.where(s >= thr, s, jnp.inf), jnp.min))
        mn8 = lax.fori_loop(0, npair, body, jnp.full((SUBLANES, tq), jnp.inf, F32))
        return jnp.min(mn8, axis=0, keepdims=True)

    def count_zero():
        def body(j, carry):
            c0, cp = carry
            s = sc_ref[pl.ds(pl.multiple_of(j * pc, pc), pc), :]
            return (c0 + _fold8(jnp.where(s >= 0.0, 1.0, 0.0), jnp.sum),
                    cp + _fold8(jnp.where(s > 0.0, 1.0, 0.0), jnp.sum))
        z8 = jnp.zeros((SUBLANES, tq), F32)
        c0, cp = lax.fori_loop(0, npair, body, (z8, z8))
        return jnp.sum(c0, axis=0, keepdims=True), jnp.sum(cp, axis=0, keepdims=True)

    kf = float(topk)
    n_valid = (qpos + 1).astype(F32)
    few = n_valid <= kf
    cnt_nonneg, cnt_pos = count_zero()
    is_zero = jnp.logical_and(cnt_pos < kf, cnt_nonneg >= kf)
    is_pos = cnt_pos >= kf
    lo0 = jnp.where(is_zero, 0, jnp.where(is_pos, MIN_NORMAL_KEY, _f32_to_key(row_min)))
    hi0 = jnp.where(is_zero, 1, jnp.where(is_pos, _f32_to_key(row_max) + 1, -1))
    cnt0 = jnp.where(is_zero, cnt_nonneg, jnp.where(is_pos, cnt_pos, n_valid))
    lo0 = jnp.where(few, _f32_to_key(row_min), lo0)
    hi0 = jnp.where(few, lo0 + 1, hi0)

    def unfinished(lo, hi, cnt_lo):
        return jnp.logical_and(lo + 1 != hi, cnt_lo != kf)

    def bis_cond(st):
        lo, hi, cnt_lo, it = st
        return jnp.logical_and(jnp.max(jnp.where(unfinished(lo, hi, cnt_lo), 1.0, 0.0)) > 0.0, it < MAX_SELECT_PASSES)

    def bisect(st):
        lo, hi, cnt_lo, it = st
        nd = unfinished(lo, hi, cnt_lo)
        mid_val = _f32_to_key(0.5 * _key_to_f32(lo) + 0.5 * _key_to_f32(hi))
        mid_key = lo + lax.shift_right_logical(hi - lo, 1)
        mid = jnp.where(it < VALUE_MID_PASSES, jnp.clip(mid_val, lo + 1, hi - 1), mid_key)
        cnt = count_ge(_key_to_f32(mid))
        ge = cnt >= kf
        up = jnp.logical_and(nd, ge)
        down = jnp.logical_and(nd, jnp.logical_not(ge))
        return jnp.where(up, mid, lo), jnp.where(down, mid, hi), jnp.where(up, cnt, cnt_lo), it + 1

    def peel(st):
        lo, hi, cnt_lo, it = st
        nd = unfinished(lo, hi, cnt_lo)
        k1 = jnp.maximum(_f32_to_key(min_ge(_key_to_f32(lo))), lo)
        cnt = count_ge(_key_to_f32(k1 + 1))
        ge = cnt >= kf
        up = jnp.logical_and(nd, ge)
        down = jnp.logical_and(nd, jnp.logical_not(ge))
        new_lo = jnp.where(nd, jnp.where(ge, k1 + 1, k1), lo)
        return new_lo, jnp.where(down, k1 + 1, hi), jnp.where(up, cnt, cnt_lo), it + 2

    def bis_body(st):
        lo, hi, cnt_lo, it = st
        excess = jnp.max(jnp.where(unfinished(lo, hi, cnt_lo), cnt_lo - kf, 0.0))
        return lax.cond(excess > PEEL_MAX, bisect, peel, st)

    lo, _, cnt_lo, _ = lax.while_loop(bis_cond, bis_body, (lo0, hi0, cnt0, jnp.int32(0)))
    thr = _key_to_f32(lo)

    tie = jnp.logical_and(cnt_lo > kf, jnp.logical_not(few))

    @pl.when(jnp.max(jnp.where(tie, 1.0, 0.0)) > 0.0)
    def _():
        def count_gt():
            def gt_body(c, cnt):
                k0 = pl.multiple_of(c * kc, kc)
                s = sc_ref[pl.ds(k0, kc), :]
                return cnt + _fold8(jnp.where(s > thr, 1.0, 0.0), jnp.sum)
            gt8 = lax.fori_loop(0, nch, gt_body, jnp.zeros((SUBLANES, tq), F32))
            return jnp.sum(gt8, axis=0, keepdims=True)

        zero_thr = thr == 0.0
        other = jnp.max(jnp.where(jnp.logical_and(tie, jnp.logical_not(zero_thr)), 1.0, 0.0)) > 0.0
        gt = lax.cond(other, count_gt, lambda: cnt_pos)
        need = kf - jnp.where(zero_thr, cnt_pos, gt)
        r = lax.broadcasted_iota(jnp.int32, (kc, kc), 0)
        cidx = lax.broadcasted_iota(jnp.int32, (kc, kc), 1)
        tri = jnp.where(cidx <= r, 1.0, 0.0).astype(BF16)

        thr_tie = jnp.where(tie, thr, jnp.nan)
        keep = jnp.where(tie, need, jnp.inf)

        def tie_body(c, seen):
            k0 = pl.multiple_of(c * kc, kc)
            s = sc_ref[pl.ds(k0, kc), :]
            eqf = jnp.where(s == thr_tie, 1.0, 0.0)
            rank = jnp.dot(tri, eqf.astype(BF16), preferred_element_type=F32) + seen
            sc_ref[pl.ds(k0, kc), :] = jnp.where(eqf * rank > keep, -jnp.inf, s)
            return seen + jnp.sum(eqf, axis=0, keepdims=True)

        lax.fori_loop(0, nch, tie_body, jnp.zeros((1, tq), F32))

    qT_all = (_tile_T(q_ref) * Q_SCALE).astype(BF16)
    qT = jnp.concatenate([_place_head(*_head_rows(qT_all, h), 0) for h in range(A_HEADS)], axis=1)
    acc_ref[...] = jnp.zeros_like(acc_ref)
    ones_rows = jnp.ones((SUBLANES, kc), BF16)

    def issue_scores(c, slot):
        k0 = pl.multiple_of(jnp.minimum(c, n_chunks_total - 1) * kc, kc)
        sbuf_ref[slot] = jnp.dot(kv_ref[pl.ds(k0, kc), :], qT, preferred_element_type=F32)

    def softmax_pv(c, slot, m):
        k0 = pl.multiple_of(c * kc, kc)
        v1 = jnp.concatenate([vaT_ref[0, c], ones_rows], axis=0)
        sel = sc_ref[pl.ds(k0, kc), :] >= thr
        s = jnp.concatenate([jnp.where(sel, sbuf_ref[slot, :, h * tq:(h + 1) * tq], NEG_BIG)
                             for h in range(A_HEADS)], axis=1)
        m_new = jnp.maximum(m, jnp.max(s, axis=0, keepdims=True))
        alpha = jnp.exp2(m - m_new)
        p = jnp.exp2((s - m_new).astype(BF16))
        acc_ref[...] = acc_ref[...] * alpha + jnp.dot(v1, p, preferred_element_type=F32)
        return m_new

    def chunk_pair(j, m):
        c0 = 2 * j
        issue_scores(c0 + 1, 1)
        m = softmax_pv(c0, 0, m)
        issue_scores(c0 + 2, 0)
        return softmax_pv(c0 + 1, 1, m)

    issue_scores(0, 0)
    lax.fori_loop(0, npair, chunk_pair, jnp.full((1, A_HEADS * tq), NEG_BIG, F32))
    a = acc_ref[...]
    o = a[:HEAD_DIM] / a[HEAD_DIM:HEAD_DIM + 1]
    _store_rows(o_ref, [o[:, h * tq:(h + 1) * tq] for h in range(A_HEADS)])


def _dsa(h_rope, h_plain, kv, vaT4, *, bsz, seq, tq, kc, topk):
    nq = seq // tq
    return pl.pallas_call(
        functools.partial(_dsa_body, tq=tq, kc=kc, topk=topk),
        grid=(bsz, nq),
        in_specs=[
            pl.BlockSpec((tq, PROJ_TN), lambda b, i: (b * nq + i, TILE_AQ)),
            pl.BlockSpec((tq, PROJ_TN), lambda b, i: (b * nq + i, TILE_AIQ)),
            pl.BlockSpec((tq, LANES), lambda b, i: (b * nq + i, COL_AIW // LANES)),
            pl.BlockSpec((seq, PROJ_TN), lambda b, i: (b, TILE_CK_AIK)),
            pl.BlockSpec((seq, PROJ_TN), lambda b, i: (b, 0)),
            pl.BlockSpec((1, seq // kc, HEAD_DIM, kc), lambda b, i: (b, 0, 0, 0)),
        ],
        out_specs=pl.BlockSpec((tq, A_W), lambda b, i: (b * nq + i, 0)),
        out_shape=jax.ShapeDtypeStruct((bsz * seq, A_W), BF16),
        scratch_shapes=[pltpu.VMEM((seq + kc, tq), F32),
                        pltpu.VMEM((HEAD_DIM + SUBLANES, A_HEADS * tq), F32),
                        pltpu.VMEM((2, kc, A_HEADS * tq), F32)],
        compiler_params=_cparams(2),
        name="dsa_attention",
    )(h_rope, h_rope, h_plain, h_rope, kv, vaT4)


def _kmean_body(k_ref, o_ref):
    o_ref[0] = jnp.mean(k_ref[...].astype(F32), axis=0, keepdims=True)


def _moba_kmean(h_rope, *, n_tokens):
    nblk = n_tokens // B_BLOCK
    return pl.pallas_call(
        _kmean_body,
        grid=(nblk,),
        in_specs=[pl.BlockSpec((B_BLOCK, B_W), lambda i: (i, TILE_BK))],
        out_specs=pl.BlockSpec((1, 1, B_W), lambda i: (i, 0, 0)),
        out_shape=jax.ShapeDtypeStruct((nblk, 1, B_W), F32),
        compiler_params=_cparams(1),
        name="moba_kmean",
    )(h_rope)


def _moba_gate_body(q_ref, km_ref, sel_ref, *, tq, nkb, nsel):
    i = pl.program_id(1)
    qpos = i * tq + lax.broadcasted_iota(jnp.int32, (1, tq), 1)
    cur = qpos // B_BLOCK
    row = lax.broadcasted_iota(jnp.int32, (nkb, tq), 0)
    past = row < cur
    qT = _tile_T(q_ref).astype(BF16)
    km = km_ref[0]
    col_head = (lax.broadcasted_iota(jnp.int32, km.shape, 1) % HALF_TN) // (HEAD_DIM // 2)
    for h in range(B_HEADS):
        km_h = jnp.where(col_head == h, km, 0.0).astype(BF16)
        g = jnp.dot(km_h, qT, preferred_element_type=F32)
        g = jnp.where(past, g, -jnp.inf)
        sel = jnp.zeros((nkb, tq), F32)
        for _ in range(nsel):
            mx = jnp.max(g, axis=0, keepdims=True)
            idx = jnp.min(jnp.where(g == mx, row, nkb), axis=0, keepdims=True)
            pick = row == idx
            sel = jnp.where(pick, 1.0, sel)
            g = jnp.where(pick, -jnp.inf, g)
        sel_ref[0, h] = jnp.where(past, sel, 0.0)


def _moba_gate(h_rope, kmean, *, bsz, seq, tq):
    nkb = seq // B_BLOCK
    nsel = min(B_TOPK_BLOCKS, nkb)
    nq = seq // tq
    return pl.pallas_call(
        functools.partial(_moba_gate_body, tq=tq, nkb=nkb, nsel=nsel),
        grid=(bsz, nq),
        in_specs=[
            pl.BlockSpec((tq, PROJ_TN), lambda b, i: (b * nq + i, TILE_BQ)),
            pl.BlockSpec((1, nkb, B_W), lambda b, i: (b, 0, 0)),
        ],
        out_specs=pl.BlockSpec((1, B_HEADS, nkb, tq), lambda b, i: (b, 0, 0, i)),
        out_shape=jax.ShapeDtypeStruct((bsz, B_HEADS, nkb, seq), F32),
        compiler_params=_cparams(2),
        name="moba_gate",
    )(h_rope, kmean)


def _moba_body(q_ref, k_ref, vT_ref, sel_ref, o_ref, acc_ref, sbuf_ref, *, tq):
    i = pl.program_id(1)
    qT_all = (_tile_T(q_ref) * Q_SCALE).astype(BF16)
    qT = [_place_head(*_head_rows(qT_all, h), h) for h in range(B_HEADS)]
    ones_rows = jnp.ones((SUBLANES, B_BLOCK), BF16)
    n_past = i

    def values(h, j):
        return jnp.concatenate([vT_ref[0, h, j], ones_rows], axis=0)

    def issue_scores(j, slot):
        k0 = pl.multiple_of(jnp.minimum(j, i) * B_BLOCK, B_BLOCK)
        for h in range(B_HEADS):
            sbuf_ref[slot, h] = jnp.dot(k_ref[pl.ds(k0, B_BLOCK), :], qT[h], preferred_element_type=F32)

    def softmax_pv(j, slot, ms):
        ps, alphas, new_ms = [], [], []
        for h in range(B_HEADS):
            s = jnp.where(sel_ref[0, h, pl.ds(j, 1), :] > 0.0, sbuf_ref[slot, h], NEG_BIG)
            m_new = jnp.maximum(ms[h], jnp.max(s, axis=0, keepdims=True))
            alphas.append(jnp.exp2(ms[h] - m_new))
            ps.append(jnp.exp2((s - m_new).astype(BF16)))
            new_ms.append(m_new)
        for h in range(B_HEADS):
            acc_ref[h] = acc_ref[h] * alphas[h] + jnp.dot(values(h, j), ps[h], preferred_element_type=F32)
        return tuple(new_ms)

    own0 = pl.multiple_of(i * B_BLOCK, B_BLOCK)
    s_own = [jnp.dot(k_ref[pl.ds(own0, B_BLOCK), :], qT[h], preferred_element_type=F32) for h in range(B_HEADS)]
    issue_scores(0, 0)
    causal = lax.broadcasted_iota(jnp.int32, (B_BLOCK, tq), 0) <= lax.broadcasted_iota(jnp.int32, (B_BLOCK, tq), 1)
    ms = []
    for h in range(B_HEADS):
        s = jnp.where(causal, s_own[h], NEG_BIG)
        m0 = jnp.max(s, axis=0, keepdims=True)
        acc_ref[h] = jnp.dot(values(h, i), jnp.exp2((s - m0).astype(BF16)), preferred_element_type=F32)
        ms.append(m0)

    def block_pair(jj, ms):
        j0 = 2 * jj
        issue_scores(j0 + 1, 1)
        ms = softmax_pv(j0, 0, ms)
        issue_scores(j0 + 2, 0)
        return softmax_pv(j0 + 1, 1, ms)

    lax.fori_loop(0, (n_past + 1) // 2, block_pair, tuple(ms))
    outs = []
    for h in range(B_HEADS):
        a = acc_ref[h]
        outs.append(a[:HEAD_DIM] / a[HEAD_DIM:HEAD_DIM + 1])
    _store_rows(o_ref, outs)


def _moba(h_rope, bvT5, sel, *, bsz, seq):
    tq = B_BLOCK
    nkb = seq // B_BLOCK
    nq = seq // tq
    return pl.pallas_call(
        functools.partial(_moba_body, tq=tq),
        grid=(bsz, nq),
        in_specs=[
            pl.BlockSpec((tq, PROJ_TN), lambda b, i: (b * nq + i, TILE_BQ)),
            pl.BlockSpec((seq, PROJ_TN), lambda b, i: (b, TILE_BK)),
            pl.BlockSpec((1, B_HEADS, nkb, HEAD_DIM, B_BLOCK), lambda b, i: (b, 0, 0, 0, 0)),
            pl.BlockSpec((1, B_HEADS, nkb, tq), lambda b, i: (b, 0, 0, i)),
        ],
        out_specs=pl.BlockSpec((tq, B_W), lambda b, i: (b * nq + i, 0)),
        out_shape=jax.ShapeDtypeStruct((bsz * seq, B_W), BF16),
        scratch_shapes=[pltpu.VMEM((B_HEADS, HEAD_DIM + SUBLANES, tq), F32),
                        pltpu.VMEM((2, B_HEADS, B_BLOCK, tq), F32)],
        compiler_params=_cparams(2),
        name="moba_attention",
    )(h_rope, h_rope, bvT5, sel)


def _swa_body(sink_ref, q0_ref, q1_ref, k0_ref, k1_ref, k2_ref, v0_ref, v1_ref, v2_ref, o_ref, *, tq):
    i = pl.program_id(1)
    q0 = i * tq
    qpos = q0 + lax.broadcasted_iota(jnp.int32, (1, tq), 1)
    nk = tq + C_WINDOW
    kpos = q0 - C_WINDOW + lax.broadcasted_iota(jnp.int32, (nk, 1), 0)
    diff = qpos - kpos
    ok = jnp.where(diff >= 0, jnp.where(diff < C_WINDOW, jnp.where(kpos >= 0, 1.0, 0.0), 0.0), 0.0) > 0.0
    ones_rows = jnp.ones((SUBLANES, nk), BF16)
    group = C_HEADS // C_KV_HEADS
    assert group == HEADS_PER_TILE
    kk = jnp.concatenate([k0_ref[...], k1_ref[...], k2_ref[...]], axis=0)
    outs = []
    scores = []
    for kv, q_ref in enumerate((q0_ref, q1_ref)):
        qT_all = (_tile_T(q_ref) * Q_SCALE).astype(BF16)
        qT = jnp.concatenate([_place_head(*_head_rows(qT_all, g), kv) for g in range(group)], axis=1)
        scores.append(jnp.dot(kk, qT, preferred_element_type=F32))
    for kv in range(C_KV_HEADS):
        vv = jnp.concatenate([v0_ref[0, kv], v1_ref[0, kv], v2_ref[0, kv]], axis=1)
        v1 = jnp.concatenate([vv, ones_rows], axis=0)
        heads = [kv * group + g for g in range(group)]
        sink = jnp.concatenate([jnp.full((1, tq), sink_ref[hd] * LOG2E, F32) for hd in heads], axis=1)
        s = jnp.concatenate([jnp.where(ok, scores[kv][:, g * tq:(g + 1) * tq], NEG_BIG) for g in range(group)], axis=1)
        m = jnp.maximum(jnp.max(s, axis=0, keepdims=True), sink)
        p = jnp.exp2((s - m).astype(BF16))
        a = jnp.dot(v1, p, preferred_element_type=F32)
        o = a[:HEAD_DIM] / (a[HEAD_DIM:HEAD_DIM + 1] + jnp.exp2(sink - m))
        outs += [o[:, g * tq:(g + 1) * tq] for g in range(group)]
    _store_rows(o_ref, outs)


def _swa(sinks, h_rope, cvT, *, bsz, seq, tq):
    r = tq // C_WINDOW
    assert r == 2
    nq = seq // tq
    nwb = seq // C_WINDOW

    def kspec(off):
        return pl.BlockSpec((C_WINDOW, PROJ_TN),
                            lambda b, i, s: (b * nwb + jnp.maximum(i * r + off, 0), TILE_CK_AIK))

    def vspec(off):
        return pl.BlockSpec((1, C_KV_HEADS, HEAD_DIM, C_WINDOW),
                            lambda b, i, s: (b, 0, 0, jnp.maximum(i * r + off, 0)))

    grid_spec = pltpu.PrefetchScalarGridSpec(
        num_scalar_prefetch=1,
        grid=(bsz, nq),
        in_specs=[pl.BlockSpec((tq, PROJ_TN), lambda b, i, s: (b * nq + i, TILE_CQ)),
                  pl.BlockSpec((tq, PROJ_TN), lambda b, i, s: (b * nq + i, TILE_CQ + 1)),
                  kspec(-1), kspec(0), kspec(1), vspec(-1), vspec(0), vspec(1)],
        out_specs=pl.BlockSpec((tq, C_W), lambda b, i, s: (b * nq + i, 0)),
    )
    return pl.pallas_call(
        functools.partial(_swa_body, tq=tq),
        grid_spec=grid_spec,
        out_shape=jax.ShapeDtypeStruct((bsz * seq, C_W), BF16),
        compiler_params=_cparams(2),
        name="swa_attention",
    )(sinks, h_rope, h_rope, h_rope, h_rope, h_rope, cvT, cvT, cvT)


def _layer_norm(z, g, b):
    mu = jnp.mean(z, axis=-1, keepdims=True)
    zc = z - mu
    var = jnp.mean(zc * zc, axis=-1, keepdims=True)
    return zc * lax.rsqrt(var + LN_EPS) * g + b


def _split_bf16(a):
    hi = a.astype(BF16)
    lo = (a - hi.astype(F32)).astype(BF16)
    return hi, lo


def _router(x1, rw_ref, rb_ref, cls_ref):
    nt = (((1,), (1,)), ((), ()))
    xh, xl = _split_bf16(x1)
    wh, wl = _split_bf16(rw_ref[...])
    logits = (lax.dot_general(wh, xh, nt, preferred_element_type=F32)
              + lax.dot_general(wh, xl, nt, preferred_element_type=F32)
              + lax.dot_general(wl, xh, nt, preferred_element_type=F32))
    aff = jax.nn.sigmoid(logits)
    score = aff + rb_ref[...]
    tm = x1.shape[0]
    sc = [score[e:e + 1, :] for e in range(N_EXPERTS)]
    af = [aff[e:e + 1, :] for e in range(N_EXPERTS)]
    in_top = []
    grp_score = []
    for gq in range(N_GROUPS):
        gs = jnp.zeros((1, tm), F32)
        for a in range(EXPERTS_PER_GROUP):
            ea = gq * EXPERTS_PER_GROUP + a
            rank = jnp.zeros((1, tm), F32)
            for b in range(EXPERTS_PER_GROUP):
                if b == a:
                    continue
                eb = gq * EXPERTS_PER_GROUP + b
                beats = (sc[eb] >= sc[ea]) if b < a else (sc[eb] > sc[ea])
                rank = rank + jnp.where(beats, 1.0, 0.0)
            top = rank < 2.0
            in_top.append(top)
            gs = gs + jnp.where(top, sc[ea], 0.0)
        grp_score.append(gs)
    best = grp_score[0]
    gstar = jnp.zeros((1, tm), jnp.int32)
    for gq in range(1, N_GROUPS):
        better = grp_score[gq] > best
        best = jnp.where(better, grp_score[gq], best)
        gstar = jnp.where(better, gq, gstar)
    cls = jnp.zeros((1, tm), jnp.int32)
    w_lo = jnp.zeros((1, tm), F32)
    w_hi = jnp.zeros((1, tm), F32)
    for gq in range(N_GROUPS):
        is_g = gstar == gq
        for pi, (a, b) in enumerate(PAIRS):
            ea, eb = gq * EXPERTS_PER_GROUP + a, gq * EXPERTS_PER_GROUP + b
            hit = jnp.where(is_g, jnp.where(in_top[ea], jnp.where(in_top[eb], 1.0, 0.0), 0.0), 0.0) > 0.0
            cls = jnp.where(hit, gq * len(PAIRS) + pi, cls)
            tot = af[ea] + af[eb]
            w_lo = jnp.where(hit, af[ea] / tot, w_lo)
            w_hi = jnp.where(hit, af[eb] / tot, w_hi)
    cls_ref[...] = cls
    return w_lo, w_hi


def _merge_body(oa_ref, ob_ref, oc_ref, ga_ref, gb_ref, gc_ref, x_ref, wa_ref, wb_ref, wc_ref, wo_ref,
                g_ref, b_ref, rw_ref, rb_ref, x1_ref, cls_ref):
    def branch(o_ref, w_ref, gate_ref):
        y = jnp.dot(o_ref[...], w_ref[...], preferred_element_type=F32)
        return jax.nn.sigmoid(gate_ref[...].astype(F32)) * y

    merged = branch(oa_ref, wa_ref, ga_ref) + branch(ob_ref, wb_ref, gb_ref) + branch(oc_ref, wc_ref, gc_ref)
    y = jnp.dot(merged.astype(BF16), wo_ref[...], preferred_element_type=F32)
    x1 = _layer_norm(DN_ALPHA * x_ref[...] + y, g_ref[...], b_ref[...])
    w_lo, w_hi = _router(x1, rw_ref, rb_ref, cls_ref)
    x1_ref[:, :D_MODEL] = x1
    x1_ref[:, D_MODEL:] = jnp.concatenate([w_lo, w_hi, jnp.zeros((LANES - 2, x1.shape[0]), F32)], axis=0).T


def _merge(oa, ob, oc, h_plain, x, wa, wb, wc, wo, ln_g, ln_b, rwT, rb, *, tm):
    t_tokens = x.shape[0]
    gate_blk = COL_GATES // D_MODEL
    full = lambda shape: pl.BlockSpec(shape, lambda i: (0,) * len(shape))
    return pl.pallas_call(
        _merge_body,
        grid=(t_tokens // tm,),
        in_specs=[
            pl.BlockSpec((tm, A_W), lambda i: (i, 0)),
            pl.BlockSpec((tm, B_W), lambda i: (i, 0)),
            pl.BlockSpec((tm, C_W), lambda i: (i, 0)),
            pl.BlockSpec((tm, D_MODEL), lambda i: (i, gate_blk)),
            pl.BlockSpec((tm, D_MODEL), lambda i: (i, gate_blk + 1)),
            pl.BlockSpec((tm, D_MODEL), lambda i: (i, gate_blk + 2)),
            pl.BlockSpec((tm, D_MODEL), lambda i: (i, 0)),
            full((A_W, D_MODEL)), full((B_W, D_MODEL)), full((C_W, D_MODEL)), full((D_MODEL, D_MODEL)),
            full((1, D_MODEL)), full((1, D_MODEL)), full((N_EXPERTS, D_MODEL)), full((N_EXPERTS, 1)),
        ],
        out_specs=[
            pl.BlockSpec((tm, X1_COLS), lambda i: (i, 0)),
            pl.BlockSpec((1, tm), lambda i: (0, i)),
        ],
        out_shape=[
            jax.ShapeDtypeStruct((t_tokens, X1_COLS), F32),
            jax.ShapeDtypeStruct((1, t_tokens), jnp.int32),
        ],
        compiler_params=_cparams(1),
        name="merge_ln_router",
    )(oa, ob, oc, h_plain, h_plain, h_plain, x, wa, wb, wc, wo, ln_g, ln_b, rwT, rb)


def _gather_start(idx_ref, src_hbm, dst_ref, sem, n_rows):
    def start(r, _):
        pltpu.make_async_copy(src_hbm.at[pl.ds(idx_ref[0, 0, r], 1)], dst_ref.at[pl.ds(r, 1)], sem).start()
        return 0

    lax.fori_loop(0, n_rows, start, 0, unroll=8)


def _gather_wait(src_hbm, dst_ref, sem, n_rows):
    pltpu.make_async_copy(src_hbm.at[pl.ds(0, n_rows)], dst_ref, sem).wait()


def _experts_body(e1_ref, e2_ref, nused_ref, src_ref, src_next_ref, x_hbm,
                  wg1_ref, wu1_ref, wd1_ref, wg2_ref, wu2_ref, wd2_ref, o_ref, xbuf, sem):
    s = pl.program_id(0)
    nused = nused_ref[0]
    slot = lax.rem(s, 2)

    @pl.when(s == 0)
    def _():
        _gather_start(src_ref, x_hbm, xbuf.at[0], sem.at[0], MOE_TILE)

    @pl.when(s + 1 < nused)
    def _():
        _gather_start(src_next_ref, x_hbm, xbuf.at[1 - slot], sem.at[1 - slot], MOE_TILE)

    @pl.when(s < nused)
    def _():
        _gather_wait(x_hbm, xbuf.at[slot], sem.at[slot], MOE_TILE)
        xb = xbuf[slot, :, :D_MODEL].astype(BF16)
        wt = xbuf[slot, :, D_MODEL:D_MODEL + 2]

        def expert(wg_ref, wu_ref, wd_ref):
            g = jnp.dot(xb, wg_ref[0], preferred_element_type=F32)
            u = jnp.dot(xb, wu_ref[0], preferred_element_type=F32)
            he = (g * jax.nn.sigmoid(g) * u).astype(BF16)
            return jnp.dot(he, wd_ref[0], preferred_element_type=F32)

        o_ref[...] = wt[:, 0:1] * expert(wg1_ref, wu1_ref, wd1_ref) + wt[:, 1:2] * expert(wg2_ref, wu2_ref, wd2_ref)

    @pl.when(s >= nused)
    def _():
        o_ref[...] = jnp.zeros_like(o_ref)


def _experts(tile_e1, tile_e2, nused, src3, x1, wg, wu, wd):
    ntiles = src3.shape[0]
    wspec_up = lambda which: pl.BlockSpec((1, D_MODEL, D_EXPERT), lambda s, e1, e2, nu: ((e1, e2)[which][s], 0, 0))
    wspec_dn = lambda which: pl.BlockSpec((1, D_EXPERT, D_MODEL), lambda s, e1, e2, nu: ((e1, e2)[which][s], 0, 0))
    grid_spec = pltpu.PrefetchScalarGridSpec(
        num_scalar_prefetch=3,
        grid=(ntiles,),
        in_specs=[
            pl.BlockSpec((1, 1, MOE_TILE), lambda s, e1, e2, nu: (s, 0, 0), memory_space=pltpu.SMEM),
            pl.BlockSpec((1, 1, MOE_TILE), lambda s, e1, e2, nu: (jnp.minimum(s + 1, ntiles - 1), 0, 0),
                         memory_space=pltpu.SMEM),
            pl.BlockSpec(memory_space=pl.ANY),
            wspec_up(0), wspec_up(0), wspec_dn(0), wspec_up(1), wspec_up(1), wspec_dn(1),
        ],
        out_specs=pl.BlockSpec((MOE_TILE, D_MODEL), lambda s, e1, e2, nu: (s, 0)),
        scratch_shapes=[pltpu.VMEM((2, MOE_TILE, X1_COLS), F32), pltpu.SemaphoreType.DMA((2,))],
    )
    return pl.pallas_call(
        _experts_body,
        grid_spec=grid_spec,
        out_shape=jax.ShapeDtypeStruct((ntiles * MOE_TILE, D_MODEL), F32),
        compiler_params=_cparams(1),
        name="moe_experts",
    )(tile_e1, tile_e2, nused, src3, src3, x1, wg, wu, wd, wg, wu, wd)


def _final_body(pos_ref, pos_next_ref, y_hbm, x_ref, g_ref, b_ref, o_ref, ob_ref, ybuf, sem, *, tm):
    s = pl.program_id(0)
    slot = lax.rem(s, 2)

    @pl.when(s == 0)
    def _():
        _gather_start(pos_ref, y_hbm, ybuf.at[0], sem.at[0], tm)

    @pl.when(s + 1 < pl.num_programs(0))
    def _():
        _gather_start(pos_next_ref, y_hbm, ybuf.at[1 - slot], sem.at[1 - slot], tm)

    _gather_wait(y_hbm, ybuf.at[slot], sem.at[slot], tm)
    x2 = _layer_norm(DN_ALPHA * x_ref[...] + ybuf[slot], g_ref[...], b_ref[...])
    o_ref[...] = x2
    ob_ref[...] = x2.astype(BF16)


def _final(pos3, y_sorted, x1, ln_g, ln_b, *, tm):
    t_tokens = x1.shape[0]
    nsteps = t_tokens // tm
    return pl.pallas_call(
        functools.partial(_final_body, tm=tm),
        grid=(nsteps,),
        in_specs=[
            pl.BlockSpec((1, 1, tm), lambda i: (i, 0, 0), memory_space=pltpu.SMEM),
            pl.BlockSpec((1, 1, tm), lambda i: (jnp.minimum(i + 1, nsteps - 1), 0, 0), memory_space=pltpu.SMEM),
            pl.BlockSpec(memory_space=pl.ANY),
            pl.BlockSpec((tm, D_MODEL), lambda i: (i, 0)),
            pl.BlockSpec((1, D_MODEL), lambda i: (0, 0)),
            pl.BlockSpec((1, D_MODEL), lambda i: (0, 0)),
        ],
        out_specs=[pl.BlockSpec((tm, D_MODEL), lambda i: (i, 0)), pl.BlockSpec((tm, D_MODEL), lambda i: (i, 0))],
        out_shape=[jax.ShapeDtypeStruct((t_tokens, D_MODEL), F32), jax.ShapeDtypeStruct((t_tokens, D_MODEL), BF16)],
        scratch_shapes=[pltpu.VMEM((2, tm, D_MODEL), F32), pltpu.SemaphoreType.DMA((2,))],
        compiler_params=_cparams(1),
        name="moe_combine_ln",
    )(pos3, pos3, y_sorted, x1, ln_g, ln_b)


def _rope_tables(seq):
    inv = 1.0 / (ROPE_THETA ** (jnp.arange(0, HEAD_DIM, 2, dtype=F32) / HEAD_DIM))
    ang = jnp.arange(seq, dtype=F32)[:, None] * inv[None, :]
    return jnp.cos(ang), jnp.sin(ang)


def _halves_layout(w_heads):
    d, n = w_heads.shape
    t = w_heads.reshape(d, n // PROJ_TN, HEADS_PER_TILE, 2, HEAD_DIM // 2)
    return t.transpose(0, 1, 3, 2, 4).reshape(d, n)


def _reorder_w_in(w):
    pts = np.cumsum((0,) + IN_SIZES)
    sec = [w[:, pts[k]:pts[k + 1]] for k in range(len(IN_SIZES))]
    a_q, a_c, a_iq, a_ik, a_iw, b_q, b_k, b_v, c_q, c_k, c_v, gates = sec
    zeros = lambda n: jnp.zeros((w.shape[0], n), w.dtype)
    w_rope = _halves_layout(jnp.concatenate([a_q, a_iq, b_q, b_k, c_q, c_k, a_ik, zeros(HEAD_DIM)], axis=1))
    w_plain = jnp.concatenate([gates, b_v, a_c, a_iw, zeros(LANES - A_IDX_HEADS), c_v, zeros(LANES)], axis=1)
    assert w_rope.shape[1] == N_ROPE and w_plain.shape[1] == N_PLAIN
    return w_rope.astype(BF16), w_plain.astype(BF16)


def _kv_weights(w_uk, w_uv):
    half = HEAD_DIM // 2
    zeros = lambda n: jnp.zeros((w_uk.shape[0], n), w_uk.dtype)
    return jnp.concatenate([w_uk[:, :half], w_uv, zeros(HALF_TN - half - HEAD_DIM),
                            w_uk[:, half:], zeros(HALF_TN - half)], axis=1).astype(BF16)


def _moe_plan(cls, n_tokens):
    ntiles = n_tokens // MOE_TILE + N_CLASSES
    onehot = (cls[:, None] == jnp.arange(N_CLASSES, dtype=jnp.int32)[None, :]).astype(jnp.int32)
    csum = jnp.cumsum(onehot, axis=0)
    rank = jnp.sum(onehot * csum, axis=1) - 1
    counts = csum[-1]
    ptiles = (counts + MOE_TILE - 1) // MOE_TILE
    tile_end = jnp.cumsum(ptiles)
    tile_start = tile_end - ptiles
    pos = (tile_start[cls] * MOE_TILE + rank).astype(jnp.int32)
    nused = tile_end[-1:].astype(jnp.int32)
    tile_ids = jnp.arange(ntiles, dtype=jnp.int32)
    tile_cls = jnp.minimum(jnp.sum((tile_ids[:, None] >= tile_end[None, :]).astype(jnp.int32), axis=1),
                           N_CLASSES - 1).astype(jnp.int32)
    pair = np.array(PAIRS, dtype=np.int32)
    e_lo = jnp.asarray(np.repeat(np.arange(N_GROUPS), len(PAIRS)) * EXPERTS_PER_GROUP + np.tile(pair[:, 0], N_GROUPS), jnp.int32)
    e_hi = jnp.asarray(np.repeat(np.arange(N_GROUPS), len(PAIRS)) * EXPERTS_PER_GROUP + np.tile(pair[:, 1], N_GROUPS), jnp.int32)
    src = jnp.zeros((ntiles * MOE_TILE,), jnp.int32).at[pos].set(jnp.arange(n_tokens, dtype=jnp.int32))
    return e_lo[tile_cls], e_hi[tile_cls], nused, src.reshape(ntiles, 1, MOE_TILE), pos


def kernel(x, w_in, a_w_uk, a_w_uv, c_sinks, w_branch, w_o, ln1_g, ln1_b, router_w, router_b,
           moe_w_gate, moe_w_up, moe_w_down, ln2_g, ln2_b):
    bsz, seq, _ = x.shape
    n_tokens = bsz * seq
    topk = min(A_TOPK_MAX, seq // 4)
    nkb = seq // B_BLOCK
    tq = 256
    kc = min(DSA_KC, seq)
    tm_proj = min(1024, seq)
    tm_row = 512
    half = HEAD_DIM // 2

    cos, sin = _rope_tables(seq)
    cos_in = jnp.tile(cos, (1, HEADS_PER_TILE))
    sin_in = jnp.tile(sin, (1, HEADS_PER_TILE))
    cos_kv = jnp.concatenate([cos, jnp.ones((seq, HALF_TN - half), F32)], axis=1)
    sin_kv = jnp.concatenate([sin, jnp.zeros((seq, HALF_TN - half), F32)], axis=1)
    rwT = router_w.T
    rb = router_b.reshape(N_EXPERTS, 1)

    xf = x.reshape(n_tokens, D_MODEL)
    xb = xf.astype(BF16)
    for l in range(DEPTH):
        w_rope, w_plain = _reorder_w_in(w_in[l])
        h_rope = _proj_rope(xb, 0, D_MODEL, w_rope, cos_in, sin_in, tm=tm_proj, seq=seq)
        h_plain = _proj_plain(xb, w_plain, tm=tm_proj)
        kv = _proj_rope(h_plain, COL_AC // A_KV_RANK, A_KV_RANK, _kv_weights(a_w_uk[l], a_w_uv[l]),
                        cos_kv, sin_kv, tm=tm_proj, seq=seq)

        vaT4 = kv[:, KV_VA0:KV_VA0 + HEAD_DIM].reshape(bsz, seq // kc, kc, HEAD_DIM).transpose(0, 1, 3, 2)
        bvT5 = (h_plain[:, COL_BV:COL_BV + B_W].reshape(bsz, nkb, B_BLOCK, B_HEADS, HEAD_DIM)
                .transpose(0, 3, 1, 4, 2))
        cvT = (h_plain[:, COL_CV:COL_CV + C_KV_HEADS * HEAD_DIM].reshape(bsz, seq, C_KV_HEADS, HEAD_DIM)
               .transpose(0, 2, 3, 1))

        o_a = _dsa(h_rope, h_plain, kv, vaT4, bsz=bsz, seq=seq, tq=min(DSA_TQ, seq), kc=kc, topk=topk)
        kmean = _moba_kmean(h_rope, n_tokens=n_tokens).reshape(bsz, nkb, B_W)
        sel = _moba_gate(h_rope, kmean, bsz=bsz, seq=seq, tq=tq)
        o_b = _moba(h_rope, bvT5, sel, bsz=bsz, seq=seq)
        o_c = _swa(c_sinks[l], h_rope, cvT, bsz=bsz, seq=seq, tq=tq)

        wb_all = w_branch[l].astype(BF16)
        x1, cls = _merge(o_a, o_b, o_c, h_plain, xf,
                              wb_all[:A_W], wb_all[A_W:A_W + B_W], wb_all[A_W + B_W:], w_o[l].astype(BF16),
                              ln1_g[l].reshape(1, D_MODEL), ln1_b[l].reshape(1, D_MODEL), rwT, rb, tm=tm_row)

        e1, e2, nused, src3, pos = _moe_plan(cls[0], n_tokens)
        y_sorted = _experts(e1, e2, nused, src3, x1,
                            moe_w_gate[l].astype(BF16), moe_w_up[l].astype(BF16), moe_w_down[l].astype(BF16))
        xf, xb = _final(pos.reshape(n_tokens // tm_row, 1, tm_row), y_sorted, x1,
                        ln2_g[l].reshape(1, D_MODEL), ln2_b[l].reshape(1, D_MODEL), tm=tm_row)
    return xf.reshape(bsz, seq, D_MODEL)
```

```python
import functools
import math

import jax
import jax.numpy as jnp
import numpy as np
from jax import lax
from jax.experimental import pallas as pl
from jax.experimental.pallas import tpu as pltpu

D_MODEL = 1024
DEPTH = 2
HEAD_DIM = 64
ROPE_THETA = 10000.0
LN_EPS = 1e-5
A_HEADS = 4
A_KV_RANK = 128
A_IDX_HEADS = 4
A_IDX_DIM = 64
A_TOPK_MAX = 256
B_HEADS = 4
B_BLOCK = 256
B_TOPK_BLOCKS = 3
C_HEADS = 8
C_KV_HEADS = 2
C_WINDOW = 128
N_BRANCH = 3
A_W = A_HEADS * HEAD_DIM
B_W = B_HEADS * HEAD_DIM
C_W = C_HEADS * HEAD_DIM
IN_SIZES = (A_W, A_KV_RANK, A_IDX_HEADS * A_IDX_DIM, A_IDX_DIM, A_IDX_HEADS,
            B_W, B_W, B_W, C_W, C_KV_HEADS * HEAD_DIM, C_KV_HEADS * HEAD_DIM,
            N_BRANCH * D_MODEL)
N_EXPERTS = 16
N_GROUPS = 4
EXPERTS_PER_GROUP = 4
D_EXPERT = 512
DN_ALPHA = (2 * DEPTH) ** 0.25
ATTN_SCALE = HEAD_DIM ** -0.5
LOG2E = math.log2(math.e)
Q_SCALE = ATTN_SCALE * LOG2E

LANES = 128
SUBLANES = 8
VMEM_LIMIT_BYTES = 56 * 1024 * 1024

NEG_BIG = -1e30
BF16 = jnp.bfloat16
F32 = jnp.float32

PROJ_TN = 256
HALF_TN = PROJ_TN // 2
HEADS_PER_TILE = PROJ_TN // HEAD_DIM
TILE_AQ = 0
TILE_AIQ = 1
TILE_BQ = 2
TILE_BK = 3
TILE_CQ = 4
TILE_CK_AIK = 6
N_ROPE = 7 * PROJ_TN
COL_GATES = 0
COL_BV = 3072
COL_AC = 3328
COL_AIW = 3456
COL_CV = 3584
N_PLAIN = 3840
PLAIN_TN = 1920
KV_VA0 = HEAD_DIM // 2

PAIRS = ((0, 1), (0, 2), (0, 3), (1, 2), (1, 3), (2, 3))
N_CLASSES = N_GROUPS * len(PAIRS)
MOE_TILE = 256
X1_COLS = D_MODEL + LANES

DSA_TQ = 512
DSA_KC = 256

VALUE_MID_PASSES = 16
PEEL_MAX = 2.0
MIN_NORMAL_KEY = 0x00800000
MAX_SELECT_PASSES = 96


def _cparams(n_axes):
    return pltpu.CompilerParams(dimension_semantics=("arbitrary",) * n_axes,
                                vmem_limit_bytes=VMEM_LIMIT_BYTES)


def _proj_rope_body(x_ref, w_ref, cos_ref, sin_ref, o_ref, *, n_sub):
    sub = x_ref.shape[0] // n_sub
    for t in range(w_ref.shape[0]):
        for mi in range(n_sub):
            rows = slice(mi * sub, (mi + 1) * sub)
            acc = jnp.dot(x_ref[rows, :], w_ref[t], preferred_element_type=F32)
            a1, a2 = acc[:, :HALF_TN], acc[:, HALF_TN:]
            c, sn = cos_ref[rows, :], sin_ref[rows, :]
            c0 = t * PROJ_TN
            o_ref[rows, c0:c0 + HALF_TN] = (a1 * c - a2 * sn).astype(o_ref.dtype)
            o_ref[rows, c0 + HALF_TN:c0 + PROJ_TN] = (a2 * c + a1 * sn).astype(o_ref.dtype)


def _proj_plain_body(x_ref, w_ref, o_ref, *, n_sub):
    sub = x_ref.shape[0] // n_sub
    for mi in range(n_sub):
        rows = slice(mi * sub, (mi + 1) * sub)
        o_ref[rows, :] = jnp.dot(x_ref[rows, :], w_ref[0], preferred_element_type=F32).astype(o_ref.dtype)


def _column_tiles(w, tn=PROJ_TN):
    k_dim, n = w.shape
    return w.reshape(k_dim, n // tn, tn).transpose(1, 0, 2)


def _proj_rope(x, x_col_block, k_dim, w, cos_t, sin_t, *, tm, seq):
    t_tokens = x.shape[0]
    n = w.shape[1]
    pos_blocks = seq // tm
    return pl.pallas_call(
        functools.partial(_proj_rope_body, n_sub=max(tm // 256, 1)),
        grid=(t_tokens // tm,),
        in_specs=[
            pl.BlockSpec((tm, k_dim), lambda i: (i, x_col_block)),
            pl.BlockSpec((n // PROJ_TN, k_dim, PROJ_TN), lambda i: (0, 0, 0)),
            pl.BlockSpec((tm, HALF_TN), lambda i: (i % pos_blocks, 0)),
            pl.BlockSpec((tm, HALF_TN), lambda i: (i % pos_blocks, 0)),
        ],
        out_specs=pl.BlockSpec((tm, n), lambda i: (i, 0)),
        out_shape=jax.ShapeDtypeStruct((t_tokens, n), BF16),
        compiler_params=_cparams(1),
        name="proj_rope",
    )(x, _column_tiles(w), cos_t, sin_t)


def _proj_plain(x, w, *, tm):
    t_tokens, k_dim = x.shape
    n = w.shape[1]
    return pl.pallas_call(
        functools.partial(_proj_plain_body, n_sub=max(tm // 256, 1)),
        grid=(t_tokens // tm, n // PLAIN_TN),
        in_specs=[
            pl.BlockSpec((tm, k_dim), lambda i, j: (i, 0)),
            pl.BlockSpec((1, k_dim, PLAIN_TN), lambda i, j: (j, 0, 0)),
        ],
        out_specs=pl.BlockSpec((tm, PLAIN_TN), lambda i, j: (i, j)),
        out_shape=jax.ShapeDtypeStruct((t_tokens, n), BF16),
        compiler_params=_cparams(2),
        name="proj_plain",
    )(x, _column_tiles(w, PLAIN_TN))


def _f32_to_key(x):
    b = lax.bitcast_convert_type(x, jnp.int32)
    return jnp.where(b < 0, b ^ jnp.int32(0x7FFFFFFF), b)


def _key_to_f32(k):
    b = jnp.where(k < 0, k ^ jnp.int32(0x7FFFFFFF), k)
    return lax.bitcast_convert_type(b, F32)


def _fold8(x, op):
    return op(x.reshape(x.shape[0] // SUBLANES, SUBLANES, x.shape[1]), axis=0)


def _tile_T(x_ref):
    return x_ref[...].astype(F32).T


def _head_rows(xT, h):
    half = HEAD_DIM // 2
    return xT[h * half:(h + 1) * half], xT[HALF_TN + h * half:HALF_TN + (h + 1) * half]


def _place_head(first, second, slot):
    half = HEAD_DIM // 2
    w = first.shape[1]
    before, after = slot * half, HALF_TN - (slot + 1) * half
    pieces = []
    for part in (first, second):
        pieces += [jnp.zeros((before, w), part.dtype)] * (before > 0) + [part] + [jnp.zeros((after, w), part.dtype)] * (after > 0)
    return jnp.concatenate(pieces, axis=0)


def _store_rows(o_ref, oT_heads):
    o_ref[...] = jnp.concatenate(oT_heads, axis=0).T.astype(o_ref.dtype)


IK_SLOT = 2


def _dsa_body(q_ref, iq_ref, iw_ref, ik_ref, kv_ref, vaT_ref, o_ref, sc_ref, acc_ref, sbuf_ref, *, tq, kc, topk):
    i = pl.program_id(1)
    q0 = i * tq
    nch = (q0 + tq + kc - 1) // kc
    qpos = q0 + lax.broadcasted_iota(jnp.int32, (1, tq), 1)
    iqT = _tile_T(iq_ref)
    iq_ops = [_place_head(*[p.astype(BF16) for p in _head_rows(iqT, h)], IK_SLOT) for h in range(A_IDX_HEADS)]
    w_idx = iw_ref[...].astype(F32).T[:A_IDX_HEADS]

    def score_chunk(c, carry, diagonal):
        mn, mx = carry
        k0 = pl.multiple_of(c * kc, kc)
        ikc = ik_ref[pl.ds(k0, kc), :]
        acc = jnp.zeros((kc, tq), F32)
        for h in range(A_IDX_HEADS):
            d = jnp.dot(ikc, iq_ops[h], preferred_element_type=F32)
            acc = acc + w_idx[h:h + 1, :] * jnp.maximum(d, 0.0)
        if diagonal:
            causal = (k0 + lax.broadcasted_iota(jnp.int32, (kc, 1), 0)) <= qpos
            lo_part, hi_part = jnp.where(causal, acc, jnp.inf), jnp.where(causal, acc, -jnp.inf)
        else:
            lo_part = hi_part = acc
        sc_ref[pl.ds(k0, kc), :] = hi_part
        return jnp.minimum(mn, _fold8(lo_part, jnp.min)), jnp.maximum(mx, _fold8(hi_part, jnp.max))

    n_full = (q0 + 1) // kc
    carry = lax.fori_loop(0, n_full, functools.partial(score_chunk, diagonal=False),
                          (jnp.full((SUBLANES, tq), jnp.inf, F32), jnp.full((SUBLANES, tq), -jnp.inf, F32)))
    mn8, mx8 = lax.fori_loop(n_full, nch, functools.partial(score_chunk, diagonal=True), carry)
    row_min = jnp.min(mn8, axis=0, keepdims=True)
    row_max = jnp.max(mx8, axis=0, keepdims=True)

    pc = 2 * kc
    npair = (nch + 1) // 2
    n_chunks_total = vaT_ref.shape[1]

    @pl.when(nch % 2 == 1)
    def _():
        sc_ref[pl.ds(pl.multiple_of(nch * kc, kc), kc), :] = jnp.full((kc, tq), -jnp.inf, F32)

    def count_ge(thr):
        def body(j, cnt):
            s = sc_ref[pl.ds(pl.multiple_of(j * pc, pc), pc), :]
            return cnt + _fold8(jnp.where(s >= thr, 1.0, 0.0), jnp.sum)
        cnt8 = lax.fori_loop(0, npair, body, jnp.zeros((SUBLANES, tq), F32))
        return jnp.sum(cnt8, axis=0, keepdims=True)

    def min_ge(thr):
        def body(j, mn):
            s = sc_ref[pl.ds(pl.multiple_of(j * pc, pc), pc), :]
            return jnp.minimum(mn, _fold8(jnp.where(s >= thr, s, jnp.inf), jnp.min))
        mn8 = lax.fori_loop(0, npair, body, jnp.full((SUBLANES, tq), jnp.inf, F32))
        return jnp.min(mn8, axis=0, keepdims=True)

    def count_zero():
        def body(j, carry):
            c0, cp = carry
            s = sc_ref[pl.ds(pl.multiple_of(j * pc, pc), pc), :]
            return (c0 + _fold8(jnp.where(s >= 0.0, 1.0, 0.0), jnp.sum),
                    cp + _fold8(jnp.where(s > 0.0, 1.0, 0.0), jnp.sum))
        z8 = jnp.zeros((SUBLANES, tq), F32)
        c0, cp = lax.fori_loop(0, npair, body, (z8, z8))
        return jnp.sum(c0, axis=0, keepdims=True), jnp.sum(cp, axis=0, keepdims=True)

    kf = float(topk)
    n_valid = (qpos + 1).astype(F32)
    few = n_valid <= kf
    cnt_nonneg, cnt_pos = count_zero()
    is_zero = jnp.logical_and(cnt_pos < kf, cnt_nonneg >= kf)
    is_pos = cnt_pos >= kf
    lo0 = jnp.where(is_zero, 0, jnp.where(is_pos, MIN_NORMAL_KEY, _f32_to_key(row_min)))
    hi0 = jnp.where(is_zero, 1, jnp.where(is_pos, _f32_to_key(row_max) + 1, -1))
    cnt0 = jnp.where(is_zero, cnt_nonneg, jnp.where(is_pos, cnt_pos, n_valid))
    lo0 = jnp.where(few, _f32_to_key(row_min), lo0)
    hi0 = jnp.where(few, lo0 + 1, hi0)

    def unfinished(lo, hi, cnt_lo):
        return jnp.logical_and(lo + 1 != hi, cnt_lo != kf)

    def bis_cond(st):
        lo, hi, cnt_lo, it = st
        return jnp.logical_and(jnp.max(jnp.where(unfinished(lo, hi, cnt_lo), 1.0, 0.0)) > 0.0, it < MAX_SELECT_PASSES)

    def bisect(st):
        lo, hi, cnt_lo, it = st
        nd = unfinished(lo, hi, cnt_lo)
        mid_val = _f32_to_key(0.5 * _key_to_f32(lo) + 0.5 * _key_to_f32(hi))
        mid_key = lo + lax.shift_right_logical(hi - lo, 1)
        mid = jnp.where(it < VALUE_MID_PASSES, jnp.clip(mid_val, lo + 1, hi - 1), mid_key)
        cnt = count_ge(_key_to_f32(mid))
        ge = cnt >= kf
        up = jnp.logical_and(nd, ge)
        down = jnp.logical_and(nd, jnp.logical_not(ge))
        return jnp.where(up, mid, lo), jnp.where(down, mid, hi), jnp.where(up, cnt, cnt_lo), it + 1

    def peel(st):
        lo, hi, cnt_lo, it = st
        nd = unfinished(lo, hi, cnt_lo)
        k1 = jnp.maximum(_f32_to_key(min_ge(_key_to_f32(lo))), lo)
        cnt = count_ge(_key_to_f32(k1 + 1))
        ge = cnt >= kf
        up = jnp.logical_and(nd, ge)
        down = jnp.logical_and(nd, jnp.logical_not(ge))
        new_lo = jnp.where(nd, jnp.where(ge, k1 + 1, k1), lo)
        return new_lo, jnp.where(down, k1 + 1, hi), jnp.where(up, cnt, cnt_lo), it + 2

    def bis_body(st):
        lo, hi, cnt_lo, it = st
        excess = jnp.max(jnp.where(unfinished(lo, hi, cnt_lo), cnt_lo - kf, 0.0))
        return lax.cond(excess > PEEL_MAX, bisect, peel, st)

    lo, _, cnt_lo, _ = lax.while_loop(bis_cond, bis_body, (lo0, hi0, cnt0, jnp.int32(0)))
    thr = _key_to_f32(lo)

    tie = jnp.logical_and(cnt_lo > kf, jnp.logical_not(few))

    @pl.when(jnp.max(jnp.where(tie, 1.0, 0.0)) > 0.0)
    def _():
        def count_gt():
            def gt_body(c, cnt):
                k0 = pl.multiple_of(c * kc, kc)
                s = sc_ref[pl.ds(k0, kc), :]
                return cnt + _fold8(jnp.where(s > thr, 1.0, 0.0), jnp.sum)
            gt8 = lax.fori_loop(0, nch, gt_body, jnp.zeros((SUBLANES, tq), F32))
            return jnp.sum(gt8, axis=0, keepdims=True)

        zero_thr = thr == 0.0
        other = jnp.max(jnp.where(jnp.logical_and(tie, jnp.logical_not(zero_thr)), 1.0, 0.0)) > 0.0
        gt = lax.cond(other, count_gt, lambda: cnt_pos)
        need = kf - jnp.where(zero_thr, cnt_pos, gt)
        r = lax.broadcasted_iota(jnp.int32, (kc, kc), 0)
        cidx = lax.broadcasted_iota(jnp.int32, (kc, kc), 1)
        tri = jnp.where(cidx <= r, 1.0, 0.0).astype(BF16)

        thr_tie = jnp.where(tie, thr, jnp.nan)
        keep = jnp.where(tie, need, jnp.inf)

        def tie_body(c, seen):
            k0 = pl.multiple_of(c * kc, kc)
            s = sc_ref[pl.ds(k0, kc), :]
            eqf = jnp.where(s == thr_tie, 1.0, 0.0)
            rank = jnp.dot(tri, eqf.astype(BF16), preferred_element_type=F32) + seen
            sc_ref[pl.ds(k0, kc), :] = jnp.where(eqf * rank > keep, -jnp.inf, s)
            return seen + jnp.sum(eqf, axis=0, keepdims=True)

        lax.fori_loop(0, nch, tie_body, jnp.zeros((1, tq), F32))

    qT_all = (_tile_T(q_ref) * Q_SCALE).astype(BF16)
    qT = jnp.concatenate([_place_head(*_head_rows(qT_all, h), 0) for h in range(A_HEADS)], axis=1)
    acc_ref[...] = jnp.zeros_like(acc_ref)
    ones_rows = jnp.ones((SUBLANES, kc), BF16)

    def issue_scores(c, slot):
        k0 = pl.multiple_of(jnp.minimum(c, n_chunks_total - 1) * kc, kc)
        sbuf_ref[slot] = jnp.dot(kv_ref[pl.ds(k0, kc), :], qT, preferred_element_type=F32)

    def softmax_pv(c, slot, m):
        k0 = pl.multiple_of(c * kc, kc)
        v1 = jnp.concatenate([vaT_ref[0, c], ones_rows], axis=0)
        sel = sc_ref[pl.ds(k0, kc), :] >= thr
        s = jnp.concatenate([jnp.where(sel, sbuf_ref[slot, :, h * tq:(h + 1) * tq], NEG_BIG)
                             for h in range(A_HEADS)], axis=1)
        m_new = jnp.maximum(m, jnp.max(s, axis=0, keepdims=True))
        alpha = jnp.exp2(m - m_new)
        p = jnp.exp2((s - m_new).astype(BF16))
        acc_ref[...] = acc_ref[...] * alpha + jnp.dot(v1, p, preferred_element_type=F32)
        return m_new

    def chunk_pair(j, m):
        c0 = 2 * j
        issue_scores(c0 + 1, 1)
        m = softmax_pv(c0, 0, m)
        issue_scores(c0 + 2, 0)
        return softmax_pv(c0 + 1, 1, m)

    issue_scores(0, 0)
    lax.fori_loop(0, npair, chunk_pair, jnp.full((1, A_HEADS * tq), NEG_BIG, F32))
    a = acc_ref[...]
    o = a[:HEAD_DIM] / a[HEAD_DIM:HEAD_DIM + 1]
    _store_rows(o_ref, [o[:, h * tq:(h + 1) * tq] for h in range(A_HEADS)])


def _dsa(h_rope, h_plain, kv, vaT4, *, bsz, seq, tq, kc, topk):
    nq = seq // tq
    return pl.pallas_call(
        functools.partial(_dsa_body, tq=tq, kc=kc, topk=topk),
        grid=(bsz, nq),
        in_specs=[
            pl.BlockSpec((tq, PROJ_TN), lambda b, i: (b * nq + i, TILE_AQ)),
            pl.BlockSpec((tq, PROJ_TN), lambda b, i: (b * nq + i, TILE_AIQ)),
            pl.BlockSpec((tq, LANES), lambda b, i: (b * nq + i, COL_AIW // LANES)),
            pl.BlockSpec((seq, PROJ_TN), lambda b, i: (b, TILE_CK_AIK)),
            pl.BlockSpec((seq, PROJ_TN), lambda b, i: (b, 0)),
            pl.BlockSpec((1, seq // kc, HEAD_DIM, kc), lambda b, i: (b, 0, 0, 0)),
        ],
        out_specs=pl.BlockSpec((tq, A_W), lambda b, i: (b * nq + i, 0)),
        out_shape=jax.ShapeDtypeStruct((bsz * seq, A_W), BF16),
        scratch_shapes=[pltpu.VMEM((seq + kc, tq), F32),
                        pltpu.VMEM((HEAD_DIM + SUBLANES, A_HEADS * tq), F32),
                        pltpu.VMEM((2, kc, A_HEADS * tq), F32)],
        compiler_params=_cparams(2),
        name="dsa_attention",
    )(h_rope, h_rope, h_plain, h_rope, kv, vaT4)


def _kmean_body(k_ref, o_ref):
    o_ref[0] = jnp.mean(k_ref[...].astype(F32), axis=0, keepdims=True)


def _moba_kmean(h_rope, *, n_tokens):
    nblk = n_tokens // B_BLOCK
    return pl.pallas_call(
        _kmean_body,
        grid=(nblk,),
        in_specs=[pl.BlockSpec((B_BLOCK, B_W), lambda i: (i, TILE_BK))],
        out_specs=pl.BlockSpec((1, 1, B_W), lambda i: (i, 0, 0)),
        out_shape=jax.ShapeDtypeStruct((nblk, 1, B_W), F32),
        compiler_params=_cparams(1),
        name="moba_kmean",
    )(h_rope)


def _moba_gate_body(q_ref, km_ref, sel_ref, *, tq, nkb, nsel):
    i = pl.program_id(1)
    qpos = i * tq + lax.broadcasted_iota(jnp.int32, (1, tq), 1)
    cur = qpos // B_BLOCK
    row = lax.broadcasted_iota(jnp.int32, (nkb, tq), 0)
    past = row < cur
    qT = _tile_T(q_ref).astype(BF16)
    km = km_ref[0]
    col_head = (lax.broadcasted_iota(jnp.int32, km.shape, 1) % HALF_TN) // (HEAD_DIM // 2)
    for h in range(B_HEADS):
        km_h = jnp.where(col_head == h, km, 0.0).astype(BF16)
        g = jnp.dot(km_h, qT, preferred_element_type=F32)
        g = jnp.where(past, g, -jnp.inf)
        sel = jnp.zeros((nkb, tq), F32)
        for _ in range(nsel):
            mx = jnp.max(g, axis=0, keepdims=True)
            idx = jnp.min(jnp.where(g == mx, row, nkb), axis=0, keepdims=True)
            pick = row == idx
            sel = jnp.where(pick, 1.0, sel)
            g = jnp.where(pick, -jnp.inf, g)
        sel_ref[0, h] = jnp.where(past, sel, 0.0)


def _moba_gate(h_rope, kmean, *, bsz, seq, tq):
    nkb = seq // B_BLOCK
    nsel = min(B_TOPK_BLOCKS, nkb)
    nq = seq // tq
    return pl.pallas_call(
        functools.partial(_moba_gate_body, tq=tq, nkb=nkb, nsel=nsel),
        grid=(bsz, nq),
        in_specs=[
            pl.BlockSpec((tq, PROJ_TN), lambda b, i: (b * nq + i, TILE_BQ)),
            pl.BlockSpec((1, nkb, B_W), lambda b, i: (b, 0, 0)),
        ],
        out_specs=pl.BlockSpec((1, B_HEADS, nkb, tq), lambda b, i: (b, 0, 0, i)),
        out_shape=jax.ShapeDtypeStruct((bsz, B_HEADS, nkb, seq), F32),
        compiler_params=_cparams(2),
        name="moba_gate",
    )(h_rope, kmean)


def _moba_body(q_ref, k_ref, vT_ref, sel_ref, o_ref, acc_ref, sbuf_ref, *, tq):
    i = pl.program_id(1)
    qT_all = (_tile_T(q_ref) * Q_SCALE).astype(BF16)
    qT = [_place_head(*_head_rows(qT_all, h), h) for h in range(B_HEADS)]
    ones_rows = jnp.ones((SUBLANES, B_BLOCK), BF16)
    n_past = i

    def values(h, j):
        return jnp.concatenate([vT_ref[0, h, j], ones_rows], axis=0)

    def issue_scores(j, slot):
        k0 = pl.multiple_of(jnp.minimum(j, i) * B_BLOCK, B_BLOCK)
        for h in range(B_HEADS):
            sbuf_ref[slot, h] = jnp.dot(k_ref[pl.ds(k0, B_BLOCK), :], qT[h], preferred_element_type=F32)

    def softmax_pv(j, slot, ms):
        ps, alphas, new_ms = [], [], []
        for h in range(B_HEADS):
            s = jnp.where(sel_ref[0, h, pl.ds(j, 1), :] > 0.0, sbuf_ref[slot, h], NEG_BIG)
            m_new = jnp.maximum(ms[h], jnp.max(s, axis=0, keepdims=True))
            alphas.append(jnp.exp2(ms[h] - m_new))
            ps.append(jnp.exp2((s - m_new).astype(BF16)))
            new_ms.append(m_new)
        for h in range(B_HEADS):
            acc_ref[h] = acc_ref[h] * alphas[h] + jnp.dot(values(h, j), ps[h], preferred_element_type=F32)
        return tuple(new_ms)

    own0 = pl.multiple_of(i * B_BLOCK, B_BLOCK)
    s_own = [jnp.dot(k_ref[pl.ds(own0, B_BLOCK), :], qT[h], preferred_element_type=F32) for h in range(B_HEADS)]
    issue_scores(0, 0)
    causal = lax.broadcasted_iota(jnp.int32, (B_BLOCK, tq), 0) <= lax.broadcasted_iota(jnp.int32, (B_BLOCK, tq), 1)
    ms = []
    for h in range(B_HEADS):
        s = jnp.where(causal, s_own[h], NEG_BIG)
        m0 = jnp.max(s, axis=0, keepdims=True)
        acc_ref[h] = jnp.dot(values(h, i), jnp.exp2((s - m0).astype(BF16)), preferred_element_type=F32)
        ms.append(m0)

    def block_pair(jj, ms):
        j0 = 2 * jj
        issue_scores(j0 + 1, 1)
        ms = softmax_pv(j0, 0, ms)
        issue_scores(j0 + 2, 0)
        return softmax_pv(j0 + 1, 1, ms)

    lax.fori_loop(0, (n_past + 1) // 2, block_pair, tuple(ms))
    outs = []
    for h in range(B_HEADS):
        a = acc_ref[h]
        outs.append(a[:HEAD_DIM] / a[HEAD_DIM:HEAD_DIM + 1])
    _store_rows(o_ref, outs)


def _moba(h_rope, bvT5, sel, *, bsz, seq):
    tq = B_BLOCK
    nkb = seq // B_BLOCK
    nq = seq // tq
    return pl.pallas_call(
        functools.partial(_moba_body, tq=tq),
        grid=(bsz, nq),
        in_specs=[
            pl.BlockSpec((tq, PROJ_TN), lambda b, i: (b * nq + i, TILE_BQ)),
            pl.BlockSpec((seq, PROJ_TN), lambda b, i: (b, TILE_BK)),
            pl.BlockSpec((1, B_HEADS, nkb, HEAD_DIM, B_BLOCK), lambda b, i: (b, 0, 0, 0, 0)),
            pl.BlockSpec((1, B_HEADS, nkb, tq), lambda b, i: (b, 0, 0, i)),
        ],
        out_specs=pl.BlockSpec((tq, B_W), lambda b, i: (b * nq + i, 0)),
        out_shape=jax.ShapeDtypeStruct((bsz * seq, B_W), BF16),
        scratch_shapes=[pltpu.VMEM((B_HEADS, HEAD_DIM + SUBLANES, tq), F32),
                        pltpu.VMEM((2, B_HEADS, B_BLOCK, tq), F32)],
        compiler_params=_cparams(2),
        name="moba_attention",
    )(h_rope, h_rope, bvT5, sel)


def _swa_body(sink_ref, q0_ref, q1_ref, k0_ref, k1_ref, k2_ref, v0_ref, v1_ref, v2_ref, o_ref, *, tq):
    i = pl.program_id(1)
    q0 = i * tq
    qpos = q0 + lax.broadcasted_iota(jnp.int32, (1, tq), 1)
    nk = tq + C_WINDOW
    kpos = q0 - C_WINDOW + lax.broadcasted_iota(jnp.int32, (nk, 1), 0)
    diff = qpos - kpos
    ok = jnp.where(diff >= 0, jnp.where(diff < C_WINDOW, jnp.where(kpos >= 0, 1.0, 0.0), 0.0), 0.0) > 0.0
    ones_rows = jnp.ones((SUBLANES, nk), BF16)
    group = C_HEADS // C_KV_HEADS
    assert group == HEADS_PER_TILE
    kk = jnp.concatenate([k0_ref[...], k1_ref[...], k2_ref[...]], axis=0)
    outs = []
    scores = []
    for kv, q_ref in enumerate((q0_ref, q1_ref)):
        qT_all = (_tile_T(q_ref) * Q_SCALE).astype(BF16)
        qT = jnp.concatenate([_place_head(*_head_rows(qT_all, g), kv) for g in range(group)], axis=1)
        scores.append(jnp.dot(kk, qT, preferred_element_type=F32))
    for kv in range(C_KV_HEADS):
        vv = jnp.concatenate([v0_ref[0, kv], v1_ref[0, kv], v2_ref[0, kv]], axis=1)
        v1 = jnp.concatenate([vv, ones_rows], axis=0)
        heads = [kv * group + g for g in range(group)]
        sink = jnp.concatenate([jnp.full((1, tq), sink_ref[hd] * LOG2E, F32) for hd in heads], axis=1)
        s = jnp.concatenate([jnp.where(ok, scores[kv][:, g * tq:(g + 1) * tq], NEG_BIG) for g in range(group)], axis=1)
        m = jnp.maximum(jnp.max(s, axis=0, keepdims=True), sink)
        p = jnp.exp2((s - m).astype(BF16))
        a = jnp.dot(v1, p, preferred_element_type=F32)
        o = a[:HEAD_DIM] / (a[HEAD_DIM:HEAD_DIM + 1] + jnp.exp2(sink - m))
        outs += [o[:, g * tq:(g + 1) * tq] for g in range(group)]
    _store_rows(o_ref, outs)


def _swa(sinks, h_rope, cvT, *, bsz, seq, tq):
    r = tq // C_WINDOW
    assert r == 2
    nq = seq // tq
    nwb = seq // C_WINDOW

    def kspec(off):
        return pl.BlockSpec((C_WINDOW, PROJ_TN),
                            lambda b, i, s: (b * nwb + jnp.maximum(i * r + off, 0), TILE_CK_AIK))

    def vspec(off):
        return pl.BlockSpec((1, C_KV_HEADS, HEAD_DIM, C_WINDOW),
                            lambda b, i, s: (b, 0, 0, jnp.maximum(i * r + off, 0)))

    grid_spec = pltpu.PrefetchScalarGridSpec(
        num_scalar_prefetch=1,
        grid=(bsz, nq),
        in_specs=[pl.BlockSpec((tq, PROJ_TN), lambda b, i, s: (b * nq + i, TILE_CQ)),
                  pl.BlockSpec((tq, PROJ_TN), lambda b, i, s: (b * nq + i, TILE_CQ + 1)),
                  kspec(-1), kspec(0), kspec(1), vspec(-1), vspec(0), vspec(1)],
        out_specs=pl.BlockSpec((tq, C_W), lambda b, i, s: (b * nq + i, 0)),
    )
    return pl.pallas_call(
        functools.partial(_swa_body, tq=tq),
        grid_spec=grid_spec,
        out_shape=jax.ShapeDtypeStruct((bsz * seq, C_W), BF16),
        compiler_params=_cparams(2),
        name="swa_attention",
    )(sinks, h_rope, h_rope, h_rope, h_rope, h_rope, cvT, cvT, cvT)


def _layer_norm(z, g, b):
    mu = jnp.mean(z, axis=-1, keepdims=True)
    zc = z - mu
    var = jnp.mean(zc * zc, axis=-1, keepdims=True)
    return zc * lax.rsqrt(var + LN_EPS) * g + b


def _split_bf16(a):
    hi = a.astype(BF16)
    lo = (a - hi.astype(F32)).astype(BF16)
    return hi, lo


def _router(x1, rw_ref, rb_ref, cls_ref):
    nt = (((1,), (1,)), ((), ()))
    xh, xl = _split_bf16(x1)
    wh, wl = _split_bf16(rw_ref[...])
    logits = (lax.dot_general(wh, xh, nt, preferred_element_type=F32)
              + lax.dot_general(wh, xl, nt, preferred_element_type=F32)
              + lax.dot_general(wl, xh, nt, preferred_element_type=F32))
    aff = jax.nn.sigmoid(logits)
    score = aff + rb_ref[...]
    tm = x1.shape[0]
    sc = [score[e:e + 1, :] for e in range(N_EXPERTS)]
    af = [aff[e:e + 1, :] for e in range(N_EXPERTS)]
    in_top = []
    grp_score = []
    for gq in range(N_GROUPS):
        gs = jnp.zeros((1, tm), F32)
        for a in range(EXPERTS_PER_GROUP):
            ea = gq * EXPERTS_PER_GROUP + a
            rank = jnp.zeros((1, tm), F32)
            for b in range(EXPERTS_PER_GROUP):
                if b == a:
                    continue
                eb = gq * EXPERTS_PER_GROUP + b
                beats = (sc[eb] >= sc[ea]) if b < a else (sc[eb] > sc[ea])
                rank = rank + jnp.where(beats, 1.0, 0.0)
            top = rank < 2.0
            in_top.append(top)
            gs = gs + jnp.where(top, sc[ea], 0.0)
        grp_score.append(gs)
    best = grp_score[0]
    gstar = jnp.zeros((1, tm), jnp.int32)
    for gq in range(1, N_GROUPS):
        better = grp_score[gq] > best
        best = jnp.where(better, grp_score[gq], best)
        gstar = jnp.where(better, gq, gstar)
    cls = jnp.zeros((1, tm), jnp.int32)
    w_lo = jnp.zeros((1, tm), F32)
    w_hi = jnp.zeros((1, tm), F32)
    for gq in range(N_GROUPS):
        is_g = gstar == gq
        for pi, (a, b) in enumerate(PAIRS):
            ea, eb = gq * EXPERTS_PER_GROUP + a, gq * EXPERTS_PER_GROUP + b
            hit = jnp.where(is_g, jnp.where(in_top[ea], jnp.where(in_top[eb], 1.0, 0.0), 0.0), 0.0) > 0.0
            cls = jnp.where(hit, gq * len(PAIRS) + pi, cls)
            tot = af[ea] + af[eb]
            w_lo = jnp.where(hit, af[ea] / tot, w_lo)
            w_hi = jnp.where(hit, af[eb] / tot, w_hi)
    cls_ref[...] = cls
    return w_lo, w_hi


def _merge_body(oa_ref, ob_ref, oc_ref, ga_ref, gb_ref, gc_ref, x_ref, wa_ref, wb_ref, wc_ref, wo_ref,
                g_ref, b_ref, rw_ref, rb_ref, x1_ref, cls_ref):
    def branch(o_ref, w_ref, gate_ref):
        y = jnp.dot(o_ref[...], w_ref[...], preferred_element_type=F32)
        return jax.nn.sigmoid(gate_ref[...].astype(F32)) * y

    merged = branch(oa_ref, wa_ref, ga_ref) + branch(ob_ref, wb_ref, gb_ref) + branch(oc_ref, wc_ref, gc_ref)
    y = jnp.dot(merged.astype(BF16), wo_ref[...], preferred_element_type=F32)
    x1 = _layer_norm(DN_ALPHA * x_ref[...] + y, g_ref[...], b_ref[...])
    w_lo, w_hi = _router(x1, rw_ref, rb_ref, cls_ref)
    x1_ref[:, :D_MODEL] = x1
    x1_ref[:, D_MODEL:] = jnp.concatenate([w_lo, w_hi, jnp.zeros((LANES - 2, x1.shape[0]), F32)], axis=0).T


def _merge(oa, ob, oc, h_plain, x, wa, wb, wc, wo, ln_g, ln_b, rwT, rb, *, tm):
    t_tokens = x.shape[0]
    gate_blk = COL_GATES // D_MODEL
    full = lambda shape: pl.BlockSpec(shape, lambda i: (0,) * len(shape))
    return pl.pallas_call(
        _merge_body,
        grid=(t_tokens // tm,),
        in_specs=[
            pl.BlockSpec((tm, A_W), lambda i: (i, 0)),
            pl.BlockSpec((tm, B_W), lambda i: (i, 0)),
            pl.BlockSpec((tm, C_W), lambda i: (i, 0)),
            pl.BlockSpec((tm, D_MODEL), lambda i: (i, gate_blk)),
            pl.BlockSpec((tm, D_MODEL), lambda i: (i, gate_blk + 1)),
            pl.BlockSpec((tm, D_MODEL), lambda i: (i, gate_blk + 2)),
            pl.BlockSpec((tm, D_MODEL), lambda i: (i, 0)),
            full((A_W, D_MODEL)), full((B_W, D_MODEL)), full((C_W, D_MODEL)), full((D_MODEL, D_MODEL)),
            full((1, D_MODEL)), full((1, D_MODEL)), full((N_EXPERTS, D_MODEL)), full((N_EXPERTS, 1)),
        ],
        out_specs=[
            pl.BlockSpec((tm, X1_COLS), lambda i: (i, 0)),
            pl.BlockSpec((1, tm), lambda i: (0, i)),
        ],
        out_shape=[
            jax.ShapeDtypeStruct((t_tokens, X1_COLS), F32),
            jax.ShapeDtypeStruct((1, t_tokens), jnp.int32),
        ],
        compiler_params=_cparams(1),
        name="merge_ln_router",
    )(oa, ob, oc, h_plain, h_plain, h_plain, x, wa, wb, wc, wo, ln_g, ln_b, rwT, rb)


def _gather_start(idx_ref, src_hbm, dst_ref, sem, n_rows):
    def start(r, _):
        pltpu.make_async_copy(src_hbm.at[pl.ds(idx_ref[0, 0, r], 1)], dst_ref.at[pl.ds(r, 1)], sem).start()
        return 0

    lax.fori_loop(0, n_rows, start, 0, unroll=8)


def _gather_wait(src_hbm, dst_ref, sem, n_rows):
    pltpu.make_async_copy(src_hbm.at[pl.ds(0, n_rows)], dst_ref, sem).wait()


def _experts_body(e1_ref, e2_ref, nused_ref, src_ref, src_next_ref, x_hbm,
                  wg1_ref, wu1_ref, wd1_ref, wg2_ref, wu2_ref, wd2_ref, o_ref, xbuf, sem):
    s = pl.program_id(0)
    nused = nused_ref[0]
    slot = lax.rem(s, 2)

    @pl.when(s == 0)
    def _():
        _gather_start(src_ref, x_hbm, xbuf.at[0], sem.at[0], MOE_TILE)

    @pl.when(s + 1 < nused)
    def _():
        _gather_start(src_next_ref, x_hbm, xbuf.at[1 - slot], sem.at[1 - slot], MOE_TILE)

    @pl.when(s < nused)
    def _():
        _gather_wait(x_hbm, xbuf.at[slot], sem.at[slot], MOE_TILE)
        xb = xbuf[slot, :, :D_MODEL].astype(BF16)
        wt = xbuf[slot, :, D_MODEL:D_MODEL + 2]

        def expert(wg_ref, wu_ref, wd_ref):
            g = jnp.dot(xb, wg_ref[0], preferred_element_type=F32)
            u = jnp.dot(xb, wu_ref[0], preferred_element_type=F32)
            he = (g * jax.nn.sigmoid(g) * u).astype(BF16)
            return jnp.dot(he, wd_ref[0], preferred_element_type=F32)

        o_ref[...] = wt[:, 0:1] * expert(wg1_ref, wu1_ref, wd1_ref) + wt[:, 1:2] * expert(wg2_ref, wu2_ref, wd2_ref)

    @pl.when(s >= nused)
    def _():
        o_ref[...] = jnp.zeros_like(o_ref)


def _experts(tile_e1, tile_e2, nused, src3, x1, wg, wu, wd):
    ntiles = src3.shape[0]
    wspec_up = lambda which: pl.BlockSpec((1, D_MODEL, D_EXPERT), lambda s, e1, e2, nu: ((e1, e2)[which][s], 0, 0))
    wspec_dn = lambda which: pl.BlockSpec((1, D_EXPERT, D_MODEL), lambda s, e1, e2, nu: ((e1, e2)[which][s], 0, 0))
    grid_spec = pltpu.PrefetchScalarGridSpec(
        num_scalar_prefetch=3,
        grid=(ntiles,),
        in_specs=[
            pl.BlockSpec((1, 1, MOE_TILE), lambda s, e1, e2, nu: (s, 0, 0), memory_space=pltpu.SMEM),
            pl.BlockSpec((1, 1, MOE_TILE), lambda s, e1, e2, nu: (jnp.minimum(s + 1, ntiles - 1), 0, 0),
                         memory_space=pltpu.SMEM),
            pl.BlockSpec(memory_space=pl.ANY),
            wspec_up(0), wspec_up(0), wspec_dn(0), wspec_up(1), wspec_up(1), wspec_dn(1),
        ],
        out_specs=pl.BlockSpec((MOE_TILE, D_MODEL), lambda s, e1, e2, nu: (s, 0)),
        scratch_shapes=[pltpu.VMEM((2, MOE_TILE, X1_COLS), F32), pltpu.SemaphoreType.DMA((2,))],
    )
    return pl.pallas_call(
        _experts_body,
        grid_spec=grid_spec,
        out_shape=jax.ShapeDtypeStruct((ntiles * MOE_TILE, D_MODEL), F32),
        compiler_params=_cparams(1),
        name="moe_experts",
    )(tile_e1, tile_e2, nused, src3, src3, x1, wg, wu, wd, wg, wu, wd)


def _final_body(pos_ref, pos_next_ref, y_hbm, x_ref, g_ref, b_ref, o_ref, ob_ref, ybuf, sem, *, tm):
    s = pl.program_id(0)
    slot = lax.rem(s, 2)

    @pl.when(s == 0)
    def _():
        _gather_start(pos_ref, y_hbm, ybuf.at[0], sem.at[0], tm)

    @pl.when(s + 1 < pl.num_programs(0))
    def _():
        _gather_start(pos_next_ref, y_hbm, ybuf.at[1 - slot], sem.at[1 - slot], tm)

    _gather_wait(y_hbm, ybuf.at[slot], sem.at[slot], tm)
    x2 = _layer_norm(DN_ALPHA * x_ref[...] + ybuf[slot], g_ref[...], b_ref[...])
    o_ref[...] = x2
    ob_ref[...] = x2.astype(BF16)


def _final(pos3, y_sorted, x1, ln_g, ln_b, *, tm):
    t_tokens = x1.shape[0]
    nsteps = t_tokens // tm
    return pl.pallas_call(
        functools.partial(_final_body, tm=tm),
        grid=(nsteps,),
        in_specs=[
            pl.BlockSpec((1, 1, tm), lambda i: (i, 0, 0), memory_space=pltpu.SMEM),
            pl.BlockSpec((1, 1, tm), lambda i: (jnp.minimum(i + 1, nsteps - 1), 0, 0), memory_space=pltpu.SMEM),
            pl.BlockSpec(memory_space=pl.ANY),
            pl.BlockSpec((tm, D_MODEL), lambda i: (i, 0)),
            pl.BlockSpec((1, D_MODEL), lambda i: (0, 0)),
            pl.BlockSpec((1, D_MODEL), lambda i: (0, 0)),
        ],
        out_specs=[pl.BlockSpec((tm, D_MODEL), lambda i: (i, 0)), pl.BlockSpec((tm, D_MODEL), lambda i: (i, 0))],
        out_shape=[jax.ShapeDtypeStruct((t_tokens, D_MODEL), F32), jax.ShapeDtypeStruct((t_tokens, D_MODEL), BF16)],
        scratch_shapes=[pltpu.VMEM((2, tm, D_MODEL), F32), pltpu.SemaphoreType.DMA((2,))],
        compiler_params=_cparams(1),
        name="moe_combine_ln",
    )(pos3, pos3, y_sorted, x1, ln_g, ln_b)


def _rope_tables(seq):
    inv = 1.0 / (ROPE_THETA ** (jnp.arange(0, HEAD_DIM, 2, dtype=F32) / HEAD_DIM))
    ang = jnp.arange(seq, dtype=F32)[:, None] * inv[None, :]
    return jnp.cos(ang), jnp.sin(ang)


def _halves_layout(w_heads):
    d, n = w_heads.shape
    t = w_heads.reshape(d, n // PROJ_TN, HEADS_PER_TILE, 2, HEAD_DIM // 2)
    return t.transpose(0, 1, 3, 2, 4).reshape(d, n)


def _reorder_w_in(w):
    pts = np.cumsum((0,) + IN_SIZES)
    sec = [w[:, pts[k]:pts[k + 1]] for k in range(len(IN_SIZES))]
    a_q, a_c, a_iq, a_ik, a_iw, b_q, b_k, b_v, c_q, c_k, c_v, gates = sec
    zeros = lambda n: jnp.zeros((w.shape[0], n), w.dtype)
    w_rope = _halves_layout(jnp.concatenate([a_q, a_iq, b_q, b_k, c_q, c_k, a_ik, zeros(HEAD_DIM)], axis=1))
    w_plain = jnp.concatenate([gates, b_v, a_c, a_iw, zeros(LANES - A_IDX_HEADS), c_v, zeros(LANES)], axis=1)
    assert w_rope.shape[1] == N_ROPE and w_plain.shape[1] == N_PLAIN
    return w_rope.astype(BF16), w_plain.astype(BF16)


def _kv_weights(w_uk, w_uv):
    half = HEAD_DIM // 2
    zeros = lambda n: jnp.zeros((w_uk.shape[0], n), w_uk.dtype)
    return jnp.concatenate([w_uk[:, :half], w_uv, zeros(HALF_TN - half - HEAD_DIM),
                            w_uk[:, half:], zeros(HALF_TN - half)], axis=1).astype(BF16)


def _moe_plan(cls, n_tokens):
    ntiles = n_tokens // MOE_TILE + N_CLASSES
    onehot = (cls[:, None] == jnp.arange(N_CLASSES, dtype=jnp.int32)[None, :]).astype(jnp.int32)
    csum = jnp.cumsum(onehot, axis=0)
    rank = jnp.sum(onehot * csum, axis=1) - 1
    counts = csum[-1]
    ptiles = (counts + MOE_TILE - 1) // MOE_TILE
    tile_end = jnp.cumsum(ptiles)
    tile_start = tile_end - ptiles
    pos = (tile_start[cls] * MOE_TILE + rank).astype(jnp.int32)
    nused = tile_end[-1:].astype(jnp.int32)
    tile_ids = jnp.arange(ntiles, dtype=jnp.int32)
    tile_cls = jnp.minimum(jnp.sum((tile_ids[:, None] >= tile_end[None, :]).astype(jnp.int32), axis=1),
                           N_CLASSES - 1).astype(jnp.int32)
    pair = np.array(PAIRS, dtype=np.int32)
    e_lo = jnp.asarray(np.repeat(np.arange(N_GROUPS), len(PAIRS)) * EXPERTS_PER_GROUP + np.tile(pair[:, 0], N_GROUPS), jnp.int32)
    e_hi = jnp.asarray(np.repeat(np.arange(N_GROUPS), len(PAIRS)) * EXPERTS_PER_GROUP + np.tile(pair[:, 1], N_GROUPS), jnp.int32)
    src = jnp.zeros((ntiles * MOE_TILE,), jnp.int32).at[pos].set(jnp.arange(n_tokens, dtype=jnp.int32))
    return e_lo[tile_cls], e_hi[tile_cls], nused, src.reshape(ntiles, 1, MOE_TILE), pos


def kernel(x, w_in, a_w_uk, a_w_uv, c_sinks, w_branch, w_o, ln1_g, ln1_b, router_w, router_b,
           moe_w_gate, moe_w_up, moe_w_down, ln2_g, ln2_b):
    bsz, seq, _ = x.shape
    n_tokens = bsz * seq
    topk = min(A_TOPK_MAX, seq // 4)
    nkb = seq // B_BLOCK
    tq = 256
    kc = min(DSA_KC, seq)
    tm_proj = min(1024, seq)
    tm_row = 512
    half = HEAD_DIM // 2

    cos, sin = _rope_tables(seq)
    cos_in = jnp.tile(cos, (1, HEADS_PER_TILE))
    sin_in = jnp.tile(sin, (1, HEADS_PER_TILE))
    cos_kv = jnp.concatenate([cos, jnp.ones((seq, HALF_TN - half), F32)], axis=1)
    sin_kv = jnp.concatenate([sin, jnp.zeros((seq, HALF_TN - half), F32)], axis=1)
    rwT = router_w.T
    rb = router_b.reshape(N_EXPERTS, 1)

    xf = x.reshape(n_tokens, D_MODEL)
    xb = xf.astype(BF16)
    for l in range(DEPTH):
        w_rope, w_plain = _reorder_w_in(w_in[l])
        h_rope = _proj_rope(xb, 0, D_MODEL, w_rope, cos_in, sin_in, tm=tm_proj, seq=seq)
        h_plain = _proj_plain(xb, w_plain, tm=tm_proj)
        kv = _proj_rope(h_plain, COL_AC // A_KV_RANK, A_KV_RANK, _kv_weights(a_w_uk[l], a_w_uv[l]),
                        cos_kv, sin_kv, tm=tm_proj, seq=seq)

        vaT4 = kv[:, KV_VA0:KV_VA0 + HEAD_DIM].reshape(bsz, seq // kc, kc, HEAD_DIM).transpose(0, 1, 3, 2)
        bvT5 = (h_plain[:, COL_BV:COL_BV + B_W].reshape(bsz, nkb, B_BLOCK, B_HEADS, HEAD_DIM)
                .transpose(0, 3, 1, 4, 2))
        cvT = (h_plain[:, COL_CV:COL_CV + C_KV_HEADS * HEAD_DIM].reshape(bsz, seq, C_KV_HEADS, HEAD_DIM)
               .transpose(0, 2, 3, 1))

        o_a = _dsa(h_rope, h_plain, kv, vaT4, bsz=bsz, seq=seq, tq=min(DSA_TQ, seq), kc=kc, topk=topk)
        kmean = _moba_kmean(h_rope, n_tokens=n_tokens).reshape(bsz, nkb, B_W)
        sel = _moba_gate(h_rope, kmean, bsz=bsz, seq=seq, tq=tq)
        o_b = _moba(h_rope, bvT5, sel, bsz=bsz, seq=seq)
        o_c = _swa(c_sinks[l], h_rope, cvT, bsz=bsz, seq=seq, tq=tq)

        wb_all = w_branch[l].astype(BF16)
        x1, cls = _merge(o_a, o_b, o_c, h_plain, xf,
                              wb_all[:A_W], wb_all[A_W:A_W + B_W], wb_all[A_W + B_W:], w_o[l].astype(BF16),
                              ln1_g[l].reshape(1, D_MODEL), ln1_b[l].reshape(1, D_MODEL), rwT, rb, tm=tm_row)

        e1, e2, nused, src3, pos = _moe_plan(cls[0], n_tokens)
        y_sorted = _experts(e1, e2, nused, src3, x1,
                            moe_w_gate[l].astype(BF16), moe_w_up[l].astype(BF16), moe_w_down[l].astype(BF16))
        xf, xb = _final(pos.reshape(n_tokens // tm_row, 1, tm_row), y_sorted, x1,
                        ln2_g[l].reshape(1, D_MODEL), ln2_b[l].reshape(1, D_MODEL), tm=tm_row)
    return xf.reshape(bsz, seq, D_MODEL)
```

```python
import functools
import math

import jax
import jax.numpy as jnp
import numpy as np
from jax import lax
from jax.experimental import pallas as pl
from jax.experimental.pallas import tpu as pltpu

D_MODEL = 1024
DEPTH = 2
HEAD_DIM = 64
ROPE_THETA = 10000.0
LN_EPS = 1e-5
A_HEADS = 4
A_KV_RANK = 128
A_IDX_HEADS = 4
A_IDX_DIM = 64
A_TOPK_MAX = 256
B_HEADS = 4
B_BLOCK = 256
B_TOPK_BLOCKS = 3
C_HEADS = 8
C_KV_HEADS = 2
C_WINDOW = 128
N_BRANCH = 3
A_W = A_HEADS * HEAD_DIM
B_W = B_HEADS * HEAD_DIM
C_W = C_HEADS * HEAD_DIM
IN_SIZES = (A_W, A_KV_RANK, A_IDX_HEADS * A_IDX_DIM, A_IDX_DIM, A_IDX_HEADS,
            B_W, B_W, B_W, C_W, C_KV_HEADS * HEAD_DIM, C_KV_HEADS * HEAD_DIM,
            N_BRANCH * D_MODEL)
N_EXPERTS = 16
N_GROUPS = 4
EXPERTS_PER_GROUP = 4
D_EXPERT = 512
DN_ALPHA = (2 * DEPTH) ** 0.25
ATTN_SCALE = HEAD_DIM ** -0.5
LOG2E = math.log2(math.e)
Q_SCALE = ATTN_SCALE * LOG2E

LANES = 128
SUBLANES = 8
VMEM_LIMIT_BYTES = 56 * 1024 * 1024

NEG_BIG = -1e30
BF16 = jnp.bfloat16
F32 = jnp.float32

PROJ_TN = 256
HALF_TN = PROJ_TN // 2
HEADS_PER_TILE = PROJ_TN // HEAD_DIM
TILE_AQ = 0
TILE_AIQ = 1
TILE_BQ = 2
TILE_BK = 3
TILE_CQ = 4
TILE_CK_AIK = 6
N_ROPE = 7 * PROJ_TN
COL_GATES = 0
COL_BV = 3072
COL_AC = 3328
COL_AIW = 3456
COL_CV = 3584
N_PLAIN = 3840
PLAIN_TN = 1920
KV_VA0 = HEAD_DIM // 2

PAIRS = ((0, 1), (0, 2), (0, 3), (1, 2), (1, 3), (2, 3))
N_CLASSES = N_GROUPS * len(PAIRS)
MOE_TILE = 256
X1_COLS = D_MODEL + LANES

DSA_TQ = 512
DSA_KC = 256
GATE_TQ = 1024

VALUE_MID_PASSES = 16
PEEL_MAX = 2.0
MIN_NORMAL_KEY = 0x00800000
MAX_SELECT_PASSES = 96


def _cparams(n_axes):
    return pltpu.CompilerParams(dimension_semantics=("arbitrary",) * n_axes,
                                vmem_limit_bytes=VMEM_LIMIT_BYTES)


def _proj_rope_body(x_ref, w_ref, cos_ref, sin_ref, o_ref, *, n_sub):
    sub = x_ref.shape[0] // n_sub
    for t in range(w_ref.shape[0]):
        for mi in range(n_sub):
            rows = slice(mi * sub, (mi + 1) * sub)
            acc = jnp.dot(x_ref[rows, :], w_ref[t], preferred_element_type=F32)
            a1, a2 = acc[:, :HALF_TN], acc[:, HALF_TN:]
            c, sn = cos_ref[rows, :], sin_ref[rows, :]
            c0 = t * PROJ_TN
            o_ref[rows, c0:c0 + HALF_TN] = (a1 * c - a2 * sn).astype(o_ref.dtype)
            o_ref[rows, c0 + HALF_TN:c0 + PROJ_TN] = (a2 * c + a1 * sn).astype(o_ref.dtype)


def _proj_plain_body(x_ref, w_ref, o_ref, *, n_sub):
    sub = x_ref.shape[0] // n_sub
    for mi in range(n_sub):
        rows = slice(mi * sub, (mi + 1) * sub)
        o_ref[rows, :] = jnp.dot(x_ref[rows, :], w_ref[0], preferred_element_type=F32).astype(o_ref.dtype)


def _column_tiles(w, tn=PROJ_TN):
    k_dim, n = w.shape
    return w.reshape(k_dim, n // tn, tn).transpose(1, 0, 2)


def _proj_rope(x, x_col_block, k_dim, w, cos_t, sin_t, *, tm, seq):
    t_tokens = x.shape[0]
    n = w.shape[1]
    pos_blocks = seq // tm
    return pl.pallas_call(
        functools.partial(_proj_rope_body, n_sub=max(tm // 256, 1)),
        grid=(t_tokens // tm,),
        in_specs=[
            pl.BlockSpec((tm, k_dim), lambda i: (i, x_col_block)),
            pl.BlockSpec((n // PROJ_TN, k_dim, PROJ_TN), lambda i: (0, 0, 0)),
            pl.BlockSpec((tm, HALF_TN), lambda i: (i % pos_blocks, 0)),
            pl.BlockSpec((tm, HALF_TN), lambda i: (i % pos_blocks, 0)),
        ],
        out_specs=pl.BlockSpec((tm, n), lambda i: (i, 0)),
        out_shape=jax.ShapeDtypeStruct((t_tokens, n), BF16),
        compiler_params=_cparams(1),
        name="proj_rope",
    )(x, _column_tiles(w), cos_t, sin_t)


def _proj_plain(x, w, *, tm):
    t_tokens, k_dim = x.shape
    n = w.shape[1]
    return pl.pallas_call(
        functools.partial(_proj_plain_body, n_sub=max(tm // 256, 1)),
        grid=(t_tokens // tm, n // PLAIN_TN),
        in_specs=[
            pl.BlockSpec((tm, k_dim), lambda i, j: (i, 0)),
            pl.BlockSpec((1, k_dim, PLAIN_TN), lambda i, j: (j, 0, 0)),
        ],
        out_specs=pl.BlockSpec((tm, PLAIN_TN), lambda i, j: (i, j)),
        out_shape=jax.ShapeDtypeStruct((t_tokens, n), BF16),
        compiler_params=_cparams(2),
        name="proj_plain",
    )(x, _column_tiles(w, PLAIN_TN))


def _f32_to_key(x):
    b = lax.bitcast_convert_type(x, jnp.int32)
    return jnp.where(b < 0, b ^ jnp.int32(0x7FFFFFFF), b)


def _key_to_f32(k):
    b = jnp.where(k < 0, k ^ jnp.int32(0x7FFFFFFF), k)
    return lax.bitcast_convert_type(b, F32)


def _fold8(x, op):
    return op(x.reshape(x.shape[0] // SUBLANES, SUBLANES, x.shape[1]), axis=0)


def _tile_T(x_ref):
    return x_ref[...].astype(F32).T


def _head_rows(xT, h):
    half = HEAD_DIM // 2
    return xT[h * half:(h + 1) * half], xT[HALF_TN + h * half:HALF_TN + (h + 1) * half]


def _place_head(first, second, slot):
    half = HEAD_DIM // 2
    w = first.shape[1]
    before, after = slot * half, HALF_TN - (slot + 1) * half
    pieces = []
    for part in (first, second):
        pieces += [jnp.zeros((before, w), part.dtype)] * (before > 0) + [part] + [jnp.zeros((after, w), part.dtype)] * (after > 0)
    return jnp.concatenate(pieces, axis=0)


def _store_rows(o_ref, oT_heads):
    o_ref[...] = jnp.concatenate(oT_heads, axis=0).T.astype(o_ref.dtype)


IK_SLOT = 2


def _dsa_body(q_ref, iq_ref, iw_ref, ik_ref, kv_ref, vaT_ref, o_ref, sc_ref, acc_ref, sbuf_ref, *, tq, kc, topk):
    i = pl.program_id(1)
    q0 = i * tq
    nch = (q0 + tq + kc - 1) // kc
    qpos = q0 + lax.broadcasted_iota(jnp.int32, (1, tq), 1)
    iqT = _tile_T(iq_ref)
    iq_ops = [_place_head(*[p.astype(BF16) for p in _head_rows(iqT, h)], IK_SLOT) for h in range(A_IDX_HEADS)]
    w_idx = iw_ref[...].astype(F32).T[:A_IDX_HEADS]

    def score_chunk(c, carry, diagonal):
        mn, mx = carry
        k0 = pl.multiple_of(c * kc, kc)
        ikc = ik_ref[pl.ds(k0, kc), :]
        acc = jnp.zeros((kc, tq), F32)
        for h in range(A_IDX_HEADS):
            d = jnp.dot(ikc, iq_ops[h], preferred_element_type=F32)
            acc = acc + w_idx[h:h + 1, :] * jnp.maximum(d, 0.0)
        if diagonal:
            causal = (k0 + lax.broadcasted_iota(jnp.int32, (kc, 1), 0)) <= qpos
            lo_part, hi_part = jnp.where(causal, acc, jnp.inf), jnp.where(causal, acc, -jnp.inf)
        else:
            lo_part = hi_part = acc
        sc_ref[pl.ds(k0, kc), :] = hi_part
        return jnp.minimum(mn, _fold8(lo_part, jnp.min)), jnp.maximum(mx, _fold8(hi_part, jnp.max))

    n_full = (q0 + 1) // kc
    carry = lax.fori_loop(0, n_full, functools.partial(score_chunk, diagonal=False),
                          (jnp.full((SUBLANES, tq), jnp.inf, F32), jnp.full((SUBLANES, tq), -jnp.inf, F32)))
    mn8, mx8 = lax.fori_loop(n_full, nch, functools.partial(score_chunk, diagonal=True), carry)
    row_min = jnp.min(mn8, axis=0, keepdims=True)
    row_max = jnp.max(mx8, axis=0, keepdims=True)

    pc = 2 * kc
    npair = (nch + 1) // 2
    n_chunks_total = vaT_ref.shape[1]

    @pl.when(nch % 2 == 1)
    def _():
        sc_ref[pl.ds(pl.multiple_of(nch * kc, kc), kc), :] = jnp.full((kc, tq), -jnp.inf, F32)

    def count_ge(thr):
        def body(j, cnt):
            s = sc_ref[pl.ds(pl.multiple_of(j * pc, pc), pc), :]
            return cnt + _fold8(jnp.where(s >= thr, 1.0, 0.0), jnp.sum)
        cnt8 = lax.fori_loop(0, npair, body, jnp.zeros((SUBLANES, tq), F32))
        return jnp.sum(cnt8, axis=0, keepdims=True)

    def min_ge(thr):
        def body(j, mn):
            s = sc_ref[pl.ds(pl.multiple_of(j * pc, pc), pc), :]
            return jnp.minimum(mn, _fold8(jnp.where(s >= thr, s, jnp.inf), jnp.min))
        mn8 = lax.fori_loop(0, npair, body, jnp.full((SUBLANES, tq), jnp.inf, F32))
        return jnp.min(mn8, axis=0, keepdims=True)

    def count_zero():
        def body(j, carry):
            c0, cp = carry
            s = sc_ref[pl.ds(pl.multiple_of(j * pc, pc), pc), :]
            return (c0 + _fold8(jnp.where(s >= 0.0, 1.0, 0.0), jnp.sum),
                    cp + _fold8(jnp.where(s > 0.0, 1.0, 0.0), jnp.sum))
        z8 = jnp.zeros((SUBLANES, tq), F32)
        c0, cp = lax.fori_loop(0, npair, body, (z8, z8))
        return jnp.sum(c0, axis=0, keepdims=True), jnp.sum(cp, axis=0, keepdims=True)

    kf = float(topk)
    n_valid = (qpos + 1).astype(F32)
    few = n_valid <= kf
    cnt_nonneg, cnt_pos = count_zero()
    is_zero = jnp.logical_and(cnt_pos < kf, cnt_nonneg >= kf)
    is_pos = cnt_pos >= kf
    lo0 = jnp.where(is_zero, 0, jnp.where(is_pos, MIN_NORMAL_KEY, _f32_to_key(row_min)))
    hi0 = jnp.where(is_zero, 1, jnp.where(is_pos, _f32_to_key(row_max) + 1, -1))
    cnt0 = jnp.where(is_zero, cnt_nonneg, jnp.where(is_pos, cnt_pos, n_valid))
    lo0 = jnp.where(few, _f32_to_key(row_min), lo0)
    hi0 = jnp.where(few, lo0 + 1, hi0)

    def unfinished(lo, hi, cnt_lo):
        return jnp.logical_and(lo + 1 != hi, cnt_lo != kf)

    def bis_cond(st):
        lo, hi, cnt_lo, it = st
        return jnp.logical_and(jnp.max(jnp.where(unfinished(lo, hi, cnt_lo), 1.0, 0.0)) > 0.0, it < MAX_SELECT_PASSES)

    def bisect(st):
        lo, hi, cnt_lo, it = st
        nd = unfinished(lo, hi, cnt_lo)
        mid_val = _f32_to_key(0.5 * _key_to_f32(lo) + 0.5 * _key_to_f32(hi))
        mid_key = lo + lax.shift_right_logical(hi - lo, 1)
        mid = jnp.where(it < VALUE_MID_PASSES, jnp.clip(mid_val, lo + 1, hi - 1), mid_key)
        cnt = count_ge(_key_to_f32(mid))
        ge = cnt >= kf
        up = jnp.logical_and(nd, ge)
        down = jnp.logical_and(nd, jnp.logical_not(ge))
        return jnp.where(up, mid, lo), jnp.where(down, mid, hi), jnp.where(up, cnt, cnt_lo), it + 1

    def peel(st):
        lo, hi, cnt_lo, it = st
        nd = unfinished(lo, hi, cnt_lo)
        k1 = jnp.maximum(_f32_to_key(min_ge(_key_to_f32(lo))), lo)
        cnt = count_ge(_key_to_f32(k1 + 1))
        ge = cnt >= kf
        up = jnp.logical_and(nd, ge)
        down = jnp.logical_and(nd, jnp.logical_not(ge))
        new_lo = jnp.where(nd, jnp.where(ge, k1 + 1, k1), lo)
        return new_lo, jnp.where(down, k1 + 1, hi), jnp.where(up, cnt, cnt_lo), it + 2

    def bis_body(st):
        lo, hi, cnt_lo, it = st
        excess = jnp.max(jnp.where(unfinished(lo, hi, cnt_lo), cnt_lo - kf, 0.0))
        return lax.cond(excess > PEEL_MAX, bisect, peel, st)

    lo, _, cnt_lo, _ = lax.while_loop(bis_cond, bis_body, (lo0, hi0, cnt0, jnp.int32(0)))
    thr = _key_to_f32(lo)

    tie = jnp.logical_and(cnt_lo > kf, jnp.logical_not(few))

    @pl.when(jnp.max(jnp.where(tie, 1.0, 0.0)) > 0.0)
    def _():
        def count_gt():
            def gt_body(c, cnt):
                k0 = pl.multiple_of(c * kc, kc)
                s = sc_ref[pl.ds(k0, kc), :]
                return cnt + _fold8(jnp.where(s > thr, 1.0, 0.0), jnp.sum)
            gt8 = lax.fori_loop(0, nch, gt_body, jnp.zeros((SUBLANES, tq), F32))
            return jnp.sum(gt8, axis=0, keepdims=True)

        zero_thr = thr == 0.0
        other = jnp.max(jnp.where(jnp.logical_and(tie, jnp.logical_not(zero_thr)), 1.0, 0.0)) > 0.0
        gt = lax.cond(other, count_gt, lambda: cnt_pos)
        need = kf - jnp.where(zero_thr, cnt_pos, gt)
        r = lax.broadcasted_iota(jnp.int32, (kc, kc), 0)
        cidx = lax.broadcasted_iota(jnp.int32, (kc, kc), 1)
        tri = jnp.where(cidx <= r, 1.0, 0.0).astype(BF16)

        thr_tie = jnp.where(tie, thr, jnp.nan)
        keep = jnp.where(tie, need, jnp.inf)

        def tie_body(c, seen):
            k0 = pl.multiple_of(c * kc, kc)
            s = sc_ref[pl.ds(k0, kc), :]
            eqf = jnp.where(s == thr_tie, 1.0, 0.0)
            rank = jnp.dot(tri, eqf.astype(BF16), preferred_element_type=F32) + seen
            sc_ref[pl.ds(k0, kc), :] = jnp.where(eqf * rank > keep, -jnp.inf, s)
            return seen + jnp.sum(eqf, axis=0, keepdims=True)

        lax.fori_loop(0, nch, tie_body, jnp.zeros((1, tq), F32))

    qT_all = (_tile_T(q_ref) * Q_SCALE).astype(BF16)
    qT = jnp.concatenate([_place_head(*_head_rows(qT_all, h), 0) for h in range(A_HEADS)], axis=1)
    acc_ref[...] = jnp.zeros_like(acc_ref)
    ones_rows = jnp.ones((SUBLANES, kc), BF16)

    def issue_scores(c, slot):
        k0 = pl.multiple_of(jnp.minimum(c, n_chunks_total - 1) * kc, kc)
        sbuf_ref[slot] = jnp.dot(kv_ref[pl.ds(k0, kc), :], qT, preferred_element_type=F32)

    def softmax_pv(c, slot, m):
        k0 = pl.multiple_of(c * kc, kc)
        v1 = jnp.concatenate([vaT_ref[0, c], ones_rows], axis=0)
        sel = sc_ref[pl.ds(k0, kc), :] >= thr
        s = jnp.concatenate([jnp.where(sel, sbuf_ref[slot, :, h * tq:(h + 1) * tq], NEG_BIG)
                             for h in range(A_HEADS)], axis=1)
        m_new = jnp.maximum(m, jnp.max(s, axis=0, keepdims=True))
        alpha = jnp.exp2(m - m_new)
        p = jnp.exp2((s - m_new).astype(BF16))
        acc_ref[...] = acc_ref[...] * alpha + jnp.dot(v1, p, preferred_element_type=F32)
        return m_new

    def chunk_pair(j, m):
        c0 = 2 * j
        issue_scores(c0 + 1, 1)
        m = softmax_pv(c0, 0, m)
        issue_scores(c0 + 2, 0)
        return softmax_pv(c0 + 1, 1, m)

    issue_scores(0, 0)
    lax.fori_loop(0, npair, chunk_pair, jnp.full((1, A_HEADS * tq), NEG_BIG, F32))
    a = acc_ref[...]
    o = a[:HEAD_DIM] / a[HEAD_DIM:HEAD_DIM + 1]
    _store_rows(o_ref, [o[:, h * tq:(h + 1) * tq] for h in range(A_HEADS)])


def _dsa(h_rope, h_plain, kv, vaT4, *, bsz, seq, tq, kc, topk):
    nq = seq // tq
    return pl.pallas_call(
        functools.partial(_dsa_body, tq=tq, kc=kc, topk=topk),
        grid=(bsz, nq),
        in_specs=[
            pl.BlockSpec((tq, PROJ_TN), lambda b, i: (b * nq + i, TILE_AQ)),
            pl.BlockSpec((tq, PROJ_TN), lambda b, i: (b * nq + i, TILE_AIQ)),
            pl.BlockSpec((tq, LANES), lambda b, i: (b * nq + i, COL_AIW // LANES)),
            pl.BlockSpec((seq, PROJ_TN), lambda b, i: (b, TILE_CK_AIK)),
            pl.BlockSpec((seq, PROJ_TN), lambda b, i: (b, 0)),
            pl.BlockSpec((1, seq // kc, HEAD_DIM, kc), lambda b, i: (b, 0, 0, 0)),
        ],
        out_specs=pl.BlockSpec((tq, A_W), lambda b, i: (b * nq + i, 0)),
        out_shape=jax.ShapeDtypeStruct((bsz * seq, A_W), BF16),
        scratch_shapes=[pltpu.VMEM((seq + kc, tq), F32),
                        pltpu.VMEM((HEAD_DIM + SUBLANES, A_HEADS * tq), F32),
                        pltpu.VMEM((2, kc, A_HEADS * tq), F32)],
        compiler_params=_cparams(2),
        name="dsa_attention",
    )(h_rope, h_rope, h_plain, h_rope, kv, vaT4)


def _kmean_body(k_ref, o_ref):
    gb = o_ref.shape[0]
    o_ref[...] = jnp.mean(k_ref[...].astype(F32).reshape(gb, B_BLOCK, B_W), axis=1)


def _moba_kmean(h_rope, *, n_tokens):
    nblk = n_tokens // B_BLOCK
    gb = math.gcd(nblk, SUBLANES)
    return pl.pallas_call(
        _kmean_body,
        grid=(nblk // gb,),
        in_specs=[pl.BlockSpec((gb * B_BLOCK, B_W), lambda i: (i, TILE_BK))],
        out_specs=pl.BlockSpec((gb, B_W), lambda i: (i, 0)),
        out_shape=jax.ShapeDtypeStruct((nblk, B_W), F32),
        compiler_params=_cparams(1),
        name="moba_kmean",
    )(h_rope)


def _moba_gate_body(q_ref, km_ref, sel_ref, *, tq, nkb, nsel):
    i = pl.program_id(1)
    qpos = i * tq + lax.broadcasted_iota(jnp.int32, (1, tq), 1)
    cur = qpos // B_BLOCK
    row = lax.broadcasted_iota(jnp.int32, (nkb, tq), 0)
    past = row < cur
    qT = _tile_T(q_ref).astype(BF16)
    km = km_ref[0]
    col_head = (lax.broadcasted_iota(jnp.int32, km.shape, 1) % HALF_TN) // (HEAD_DIM // 2)
    for h in range(B_HEADS):
        km_h = jnp.where(col_head == h, km, 0.0).astype(BF16)
        g = jnp.dot(km_h, qT, preferred_element_type=F32)
        g = jnp.where(past, g, -jnp.inf)
        sel = jnp.zeros((nkb, tq), F32)
        for _ in range(nsel):
            mx = jnp.max(g, axis=0, keepdims=True)
            idx = jnp.min(jnp.where(g == mx, row, nkb), axis=0, keepdims=True)
            pick = row == idx
            sel = jnp.where(pick, 1.0, sel)
            g = jnp.where(pick, -jnp.inf, g)
        sel_ref[0, h] = jnp.where(past, sel, 0.0)


def _moba_gate(h_rope, kmean, *, bsz, seq, tq):
    nkb = seq // B_BLOCK
    nsel = min(B_TOPK_BLOCKS, nkb)
    nq = seq // tq
    return pl.pallas_call(
        functools.partial(_moba_gate_body, tq=tq, nkb=nkb, nsel=nsel),
        grid=(bsz, nq),
        in_specs=[
            pl.BlockSpec((tq, PROJ_TN), lambda b, i: (b * nq + i, TILE_BQ)),
            pl.BlockSpec((1, nkb, B_W), lambda b, i: (b, 0, 0)),
        ],
        out_specs=pl.BlockSpec((1, B_HEADS, nkb, tq), lambda b, i: (b, 0, 0, i)),
        out_shape=jax.ShapeDtypeStruct((bsz, B_HEADS, nkb, seq), F32),
        compiler_params=_cparams(2),
        name="moba_gate",
    )(h_rope, kmean)


def _moba_body(q_ref, k_ref, vT_ref, sel_ref, o_ref, acc_ref, sbuf_ref, *, tq):
    i = pl.program_id(1)
    qT_all = (_tile_T(q_ref) * Q_SCALE).astype(BF16)
    qT = [_place_head(*_head_rows(qT_all, h), h) for h in range(B_HEADS)]
    ones_rows = jnp.ones((SUBLANES, B_BLOCK), BF16)
    n_past = i

    def values(h, j):
        return jnp.concatenate([vT_ref[0, h, j], ones_rows], axis=0)

    def issue_scores(j, slot):
        k0 = pl.multiple_of(jnp.minimum(j, i) * B_BLOCK, B_BLOCK)
        for h in range(B_HEADS):
            sbuf_ref[slot, h] = jnp.dot(k_ref[pl.ds(k0, B_BLOCK), :], qT[h], preferred_element_type=F32)

    def softmax_pv(j, slot, ms):
        ps, alphas, new_ms = [], [], []
        for h in range(B_HEADS):
            s = jnp.where(sel_ref[0, h, pl.ds(j, 1), :] > 0.0, sbuf_ref[slot, h], NEG_BIG)
            m_new = jnp.maximum(ms[h], jnp.max(s, axis=0, keepdims=True))
            alphas.append(jnp.exp2(ms[h] - m_new))
            ps.append(jnp.exp2((s - m_new).astype(BF16)))
            new_ms.append(m_new)
        for h in range(B_HEADS):
            acc_ref[h] = acc_ref[h] * alphas[h] + jnp.dot(values(h, j), ps[h], preferred_element_type=F32)
        return tuple(new_ms)

    own0 = pl.multiple_of(i * B_BLOCK, B_BLOCK)
    s_own = [jnp.dot(k_ref[pl.ds(own0, B_BLOCK), :], qT[h], preferred_element_type=F32) for h in range(B_HEADS)]
    issue_scores(0, 0)
    causal = lax.broadcasted_iota(jnp.int32, (B_BLOCK, tq), 0) <= lax.broadcasted_iota(jnp.int32, (B_BLOCK, tq), 1)
    ms = []
    for h in range(B_HEADS):
        s = jnp.where(causal, s_own[h], NEG_BIG)
        m0 = jnp.max(s, axis=0, keepdims=True)
        acc_ref[h] = jnp.dot(values(h, i), jnp.exp2((s - m0).astype(BF16)), preferred_element_type=F32)
        ms.append(m0)

    def block_pair(jj, ms):
        j0 = 2 * jj
        issue_scores(j0 + 1, 1)
        ms = softmax_pv(j0, 0, ms)
        issue_scores(j0 + 2, 0)
        return softmax_pv(j0 + 1, 1, ms)

    lax.fori_loop(0, (n_past + 1) // 2, block_pair, tuple(ms))
    outs = []
    for h in range(B_HEADS):
        a = acc_ref[h]
        outs.append(a[:HEAD_DIM] / a[HEAD_DIM:HEAD_DIM + 1])
    _store_rows(o_ref, outs)


def _moba(h_rope, bvT5, sel, *, bsz, seq):
    tq = B_BLOCK
    nkb = seq // B_BLOCK
    nq = seq // tq
    return pl.pallas_call(
        functools.partial(_moba_body, tq=tq),
        grid=(bsz, nq),
        in_specs=[
            pl.BlockSpec((tq, PROJ_TN), lambda b, i: (b * nq + i, TILE_BQ)),
            pl.BlockSpec((seq, PROJ_TN), lambda b, i: (b, TILE_BK)),
            pl.BlockSpec((1, B_HEADS, nkb, HEAD_DIM, B_BLOCK), lambda b, i: (b, 0, 0, 0, 0)),
            pl.BlockSpec((1, B_HEADS, nkb, tq), lambda b, i: (b, 0, 0, i)),
        ],
        out_specs=pl.BlockSpec((tq, B_W), lambda b, i: (b * nq + i, 0)),
        out_shape=jax.ShapeDtypeStruct((bsz * seq, B_W), BF16),
        scratch_shapes=[pltpu.VMEM((B_HEADS, HEAD_DIM + SUBLANES, tq), F32),
                        pltpu.VMEM((2, B_HEADS, B_BLOCK, tq), F32)],
        compiler_params=_cparams(2),
        name="moba_attention",
    )(h_rope, h_rope, bvT5, sel)


def _swa_body(sink_ref, q0_ref, q1_ref, k0_ref, k1_ref, k2_ref, v0_ref, v1_ref, v2_ref, o_ref, *, tq):
    i = pl.program_id(1)
    q0 = i * tq
    qpos = q0 + lax.broadcasted_iota(jnp.int32, (1, tq), 1)
    nk = tq + C_WINDOW
    kpos = q0 - C_WINDOW + lax.broadcasted_iota(jnp.int32, (nk, 1), 0)
    diff = qpos - kpos
    ok = jnp.where(diff >= 0, jnp.where(diff < C_WINDOW, jnp.where(kpos >= 0, 1.0, 0.0), 0.0), 0.0) > 0.0
    ones_rows = jnp.ones((SUBLANES, nk), BF16)
    group = C_HEADS // C_KV_HEADS
    assert group == HEADS_PER_TILE
    kk = jnp.concatenate([k0_ref[...], k1_ref[...], k2_ref[...]], axis=0)
    outs = []
    scores = []
    for kv, q_ref in enumerate((q0_ref, q1_ref)):
        qT_all = (_tile_T(q_ref) * Q_SCALE).astype(BF16)
        qT = jnp.concatenate([_place_head(*_head_rows(qT_all, g), kv) for g in range(group)], axis=1)
        scores.append(jnp.dot(kk, qT, preferred_element_type=F32))
    for kv in range(C_KV_HEADS):
        vv = jnp.concatenate([v0_ref[0, kv], v1_ref[0, kv], v2_ref[0, kv]], axis=1)
        v1 = jnp.concatenate([vv, ones_rows], axis=0)
        heads = [kv * group + g for g in range(group)]
        sink = jnp.concatenate([jnp.full((1, tq), sink_ref[hd] * LOG2E, F32) for hd in heads], axis=1)
        s = jnp.concatenate([jnp.where(ok, scores[kv][:, g * tq:(g + 1) * tq], NEG_BIG) for g in range(group)], axis=1)
        m = jnp.maximum(jnp.max(s, axis=0, keepdims=True), sink)
        p = jnp.exp2((s - m).astype(BF16))
        a = jnp.dot(v1, p, preferred_element_type=F32)
        o = a[:HEAD_DIM] / (a[HEAD_DIM:HEAD_DIM + 1] + jnp.exp2(sink - m))
        outs += [o[:, g * tq:(g + 1) * tq] for g in range(group)]
    _store_rows(o_ref, outs)


def _swa(sinks, h_rope, cvT, *, bsz, seq, tq):
    r = tq // C_WINDOW
    assert r == 2
    nq = seq // tq
    nwb = seq // C_WINDOW

    def kspec(off):
        return pl.BlockSpec((C_WINDOW, PROJ_TN),
                            lambda b, i, s: (b * nwb + jnp.maximum(i * r + off, 0), TILE_CK_AIK))

    def vspec(off):
        return pl.BlockSpec((1, C_KV_HEADS, HEAD_DIM, C_WINDOW),
                            lambda b, i, s: (b, 0, 0, jnp.maximum(i * r + off, 0)))

    grid_spec = pltpu.PrefetchScalarGridSpec(
        num_scalar_prefetch=1,
        grid=(bsz, nq),
        in_specs=[pl.BlockSpec((tq, PROJ_TN), lambda b, i, s: (b * nq + i, TILE_CQ)),
                  pl.BlockSpec((tq, PROJ_TN), lambda b, i, s: (b * nq + i, TILE_CQ + 1)),
                  kspec(-1), kspec(0), kspec(1), vspec(-1), vspec(0), vspec(1)],
        out_specs=pl.BlockSpec((tq, C_W), lambda b, i, s: (b * nq + i, 0)),
    )
    return pl.pallas_call(
        functools.partial(_swa_body, tq=tq),
        grid_spec=grid_spec,
        out_shape=jax.ShapeDtypeStruct((bsz * seq, C_W), BF16),
        compiler_params=_cparams(2),
        name="swa_attention",
    )(sinks, h_rope, h_rope, h_rope, h_rope, h_rope, cvT, cvT, cvT)


def _layer_norm(z, g, b):
    mu = jnp.mean(z, axis=-1, keepdims=True)
    zc = z - mu
    var = jnp.mean(zc * zc, axis=-1, keepdims=True)
    return zc * lax.rsqrt(var + LN_EPS) * g + b


def _split_bf16(a):
    hi = a.astype(BF16)
    lo = (a - hi.astype(F32)).astype(BF16)
    return hi, lo


def _router(x1, rw_ref, rb_ref, cls_ref):
    nt = (((1,), (1,)), ((), ()))
    xh, xl = _split_bf16(x1)
    wh, wl = _split_bf16(rw_ref[...])
    logits = (lax.dot_general(wh, xh, nt, preferred_element_type=F32)
              + lax.dot_general(wh, xl, nt, preferred_element_type=F32)
              + lax.dot_general(wl, xh, nt, preferred_element_type=F32))
    aff = jax.nn.sigmoid(logits)
    score = aff + rb_ref[...]
    tm = x1.shape[0]
    sc = [score[e:e + 1, :] for e in range(N_EXPERTS)]
    af = [aff[e:e + 1, :] for e in range(N_EXPERTS)]
    in_top = []
    grp_score = []
    for gq in range(N_GROUPS):
        gs = jnp.zeros((1, tm), F32)
        for a in range(EXPERTS_PER_GROUP):
            ea = gq * EXPERTS_PER_GROUP + a
            rank = jnp.zeros((1, tm), F32)
            for b in range(EXPERTS_PER_GROUP):
                if b == a:
                    continue
                eb = gq * EXPERTS_PER_GROUP + b
                beats = (sc[eb] >= sc[ea]) if b < a else (sc[eb] > sc[ea])
                rank = rank + jnp.where(beats, 1.0, 0.0)
            top = rank < 2.0
            in_top.append(top)
            gs = gs + jnp.where(top, sc[ea], 0.0)
        grp_score.append(gs)
    best = grp_score[0]
    gstar = jnp.zeros((1, tm), jnp.int32)
    for gq in range(1, N_GROUPS):
        better = grp_score[gq] > best
        best = jnp.where(better, grp_score[gq], best)
        gstar = jnp.where(better, gq, gstar)
    cls = jnp.zeros((1, tm), jnp.int32)
    w_lo = jnp.zeros((1, tm), F32)
    w_hi = jnp.zeros((1, tm), F32)
    for gq in range(N_GROUPS):
        is_g = gstar == gq
        for pi, (a, b) in enumerate(PAIRS):
            ea, eb = gq * EXPERTS_PER_GROUP + a, gq * EXPERTS_PER_GROUP + b
            hit = jnp.where(is_g, jnp.where(in_top[ea], jnp.where(in_top[eb], 1.0, 0.0), 0.0), 0.0) > 0.0
            cls = jnp.where(hit, gq * len(PAIRS) + pi, cls)
            tot = af[ea] + af[eb]
            w_lo = jnp.where(hit, af[ea] / tot, w_lo)
            w_hi = jnp.where(hit, af[eb] / tot, w_hi)
    cls_ref[...] = cls
    return w_lo, w_hi


def _merge_body(oa_ref, ob_ref, oc_ref, ga_ref, gb_ref, gc_ref, x_ref, wa_ref, wb_ref, wc_ref, wo_ref,
                g_ref, b_ref, rw_ref, rb_ref, x1_ref, cls_ref):
    def branch(o_ref, w_ref, gate_ref):
        y = jnp.dot(o_ref[...], w_ref[...], preferred_element_type=F32)
        return jax.nn.sigmoid(gate_ref[...].astype(F32)) * y

    merged = branch(oa_ref, wa_ref, ga_ref) + branch(ob_ref, wb_ref, gb_ref) + branch(oc_ref, wc_ref, gc_ref)
    y = jnp.dot(merged.astype(BF16), wo_ref[...], preferred_element_type=F32)
    x1 = _layer_norm(DN_ALPHA * x_ref[...] + y, g_ref[...], b_ref[...])
    w_lo, w_hi = _router(x1, rw_ref, rb_ref, cls_ref)
    x1_ref[:, :D_MODEL] = x1
    x1_ref[:, D_MODEL:] = jnp.concatenate([w_lo, w_hi, jnp.zeros((LANES - 2, x1.shape[0]), F32)], axis=0).T


def _merge(oa, ob, oc, h_plain, x, wa, wb, wc, wo, ln_g, ln_b, rwT, rb, *, tm):
    t_tokens = x.shape[0]
    gate_blk = COL_GATES // D_MODEL
    full = lambda shape: pl.BlockSpec(shape, lambda i: (0,) * len(shape))
    return pl.pallas_call(
        _merge_body,
        grid=(t_tokens // tm,),
        in_specs=[
            pl.BlockSpec((tm, A_W), lambda i: (i, 0)),
            pl.BlockSpec((tm, B_W), lambda i: (i, 0)),
            pl.BlockSpec((tm, C_W), lambda i: (i, 0)),
            pl.BlockSpec((tm, D_MODEL), lambda i: (i, gate_blk)),
            pl.BlockSpec((tm, D_MODEL), lambda i: (i, gate_blk + 1)),
            pl.BlockSpec((tm, D_MODEL), lambda i: (i, gate_blk + 2)),
            pl.BlockSpec((tm, D_MODEL), lambda i: (i, 0)),
            full((A_W, D_MODEL)), full((B_W, D_MODEL)), full((C_W, D_MODEL)), full((D_MODEL, D_MODEL)),
            full((1, D_MODEL)), full((1, D_MODEL)), full((N_EXPERTS, D_MODEL)), full((N_EXPERTS, 1)),
        ],
        out_specs=[
            pl.BlockSpec((tm, X1_COLS), lambda i: (i, 0)),
            pl.BlockSpec((1, tm), lambda i: (0, i)),
        ],
        out_shape=[
            jax.ShapeDtypeStruct((t_tokens, X1_COLS), F32),
            jax.ShapeDtypeStruct((1, t_tokens), jnp.int32),
        ],
        compiler_params=_cparams(1),
        name="merge_ln_router",
    )(oa, ob, oc, h_plain, h_plain, h_plain, x, wa, wb, wc, wo, ln_g, ln_b, rwT, rb)


def _gather_start(idx_ref, src_hbm, dst_ref, sem, n_rows):
    def start(r, _):
        pltpu.make_async_copy(src_hbm.at[pl.ds(idx_ref[0, 0, r], 1)], dst_ref.at[pl.ds(r, 1)], sem).start()
        return 0

    lax.fori_loop(0, n_rows, start, 0, unroll=8)


def _gather_wait(src_hbm, dst_ref, sem, n_rows):
    pltpu.make_async_copy(src_hbm.at[pl.ds(0, n_rows)], dst_ref, sem).wait()


def _experts_body(e1_ref, e2_ref, nused_ref, src_ref, src_next_ref, x_hbm,
                  wg1_ref, wu1_ref, wd1_ref, wg2_ref, wu2_ref, wd2_ref, o_ref, xbuf, sem):
    s = pl.program_id(0)
    nused = nused_ref[0]
    slot = lax.rem(s, 2)

    @pl.when(s == 0)
    def _():
        _gather_start(src_ref, x_hbm, xbuf.at[0], sem.at[0], MOE_TILE)

    @pl.when(s + 1 < nused)
    def _():
        _gather_start(src_next_ref, x_hbm, xbuf.at[1 - slot], sem.at[1 - slot], MOE_TILE)

    @pl.when(s < nused)
    def _():
        _gather_wait(x_hbm, xbuf.at[slot], sem.at[slot], MOE_TILE)
        xb = xbuf[slot, :, :D_MODEL].astype(BF16)
        wt = xbuf[slot, :, D_MODEL:D_MODEL + 2]

        def expert(wg_ref, wu_ref, wd_ref):
            g = jnp.dot(xb, wg_ref[0], preferred_element_type=F32)
            u = jnp.dot(xb, wu_ref[0], preferred_element_type=F32)
            he = (g * jax.nn.sigmoid(g) * u).astype(BF16)
            return jnp.dot(he, wd_ref[0], preferred_element_type=F32)

        o_ref[...] = wt[:, 0:1] * expert(wg1_ref, wu1_ref, wd1_ref) + wt[:, 1:2] * expert(wg2_ref, wu2_ref, wd2_ref)

    @pl.when(s >= nused)
    def _():
        o_ref[...] = jnp.zeros_like(o_ref)


def _experts(tile_e1, tile_e2, nused, src3, x1, wg, wu, wd):
    ntiles = src3.shape[0]
    wspec_up = lambda which: pl.BlockSpec((1, D_MODEL, D_EXPERT), lambda s, e1, e2, nu: ((e1, e2)[which][s], 0, 0))
    wspec_dn = lambda which: pl.BlockSpec((1, D_EXPERT, D_MODEL), lambda s, e1, e2, nu: ((e1, e2)[which][s], 0, 0))
    grid_spec = pltpu.PrefetchScalarGridSpec(
        num_scalar_prefetch=3,
        grid=(ntiles,),
        in_specs=[
            pl.BlockSpec((1, 1, MOE_TILE), lambda s, e1, e2, nu: (s, 0, 0), memory_space=pltpu.SMEM),
            pl.BlockSpec((1, 1, MOE_TILE), lambda s, e1, e2, nu: (jnp.minimum(s + 1, ntiles - 1), 0, 0),
                         memory_space=pltpu.SMEM),
            pl.BlockSpec(memory_space=pl.ANY),
            wspec_up(0), wspec_up(0), wspec_dn(0), wspec_up(1), wspec_up(1), wspec_dn(1),
        ],
        out_specs=pl.BlockSpec((MOE_TILE, D_MODEL), lambda s, e1, e2, nu: (s, 0)),
        scratch_shapes=[pltpu.VMEM((2, MOE_TILE, X1_COLS), F32), pltpu.SemaphoreType.DMA((2,))],
    )
    return pl.pallas_call(
        _experts_body,
        grid_spec=grid_spec,
        out_shape=jax.ShapeDtypeStruct((ntiles * MOE_TILE, D_MODEL), F32),
        compiler_params=_cparams(1),
        name="moe_experts",
    )(tile_e1, tile_e2, nused, src3, src3, x1, wg, wu, wd, wg, wu, wd)


def _final_body(pos_ref, pos_next_ref, y_hbm, x_ref, g_ref, b_ref, o_ref, ob_ref, ybuf, sem, *, tm):
    s = pl.program_id(0)
    slot = lax.rem(s, 2)

    @pl.when(s == 0)
    def _():
        _gather_start(pos_ref, y_hbm, ybuf.at[0], sem.at[0], tm)

    @pl.when(s + 1 < pl.num_programs(0))
    def _():
        _gather_start(pos_next_ref, y_hbm, ybuf.at[1 - slot], sem.at[1 - slot], tm)

    _gather_wait(y_hbm, ybuf.at[slot], sem.at[slot], tm)
    x2 = _layer_norm(DN_ALPHA * x_ref[...] + ybuf[slot], g_ref[...], b_ref[...])
    o_ref[...] = x2
    ob_ref[...] = x2.astype(BF16)


def _final(pos3, y_sorted, x1, ln_g, ln_b, *, tm):
    t_tokens = x1.shape[0]
    nsteps = t_tokens // tm
    return pl.pallas_call(
        functools.partial(_final_body, tm=tm),
        grid=(nsteps,),
        in_specs=[
            pl.BlockSpec((1, 1, tm), lambda i: (i, 0, 0), memory_space=pltpu.SMEM),
            pl.BlockSpec((1, 1, tm), lambda i: (jnp.minimum(i + 1, nsteps - 1), 0, 0), memory_space=pltpu.SMEM),
            pl.BlockSpec(memory_space=pl.ANY),
            pl.BlockSpec((tm, D_MODEL), lambda i: (i, 0)),
            pl.BlockSpec((1, D_MODEL), lambda i: (0, 0)),
            pl.BlockSpec((1, D_MODEL), lambda i: (0, 0)),
        ],
        out_specs=[pl.BlockSpec((tm, D_MODEL), lambda i: (i, 0)), pl.BlockSpec((tm, D_MODEL), lambda i: (i, 0))],
        out_shape=[jax.ShapeDtypeStruct((t_tokens, D_MODEL), F32), jax.ShapeDtypeStruct((t_tokens, D_MODEL), BF16)],
        scratch_shapes=[pltpu.VMEM((2, tm, D_MODEL), F32), pltpu.SemaphoreType.DMA((2,))],
        compiler_params=_cparams(1),
        name="moe_combine_ln",
    )(pos3, pos3, y_sorted, x1, ln_g, ln_b)


def _rope_tables(seq):
    inv = 1.0 / (ROPE_THETA ** (jnp.arange(0, HEAD_DIM, 2, dtype=F32) / HEAD_DIM))
    ang = jnp.arange(seq, dtype=F32)[:, None] * inv[None, :]
    return jnp.cos(ang), jnp.sin(ang)


def _halves_layout(w_heads):
    d, n = w_heads.shape
    t = w_heads.reshape(d, n // PROJ_TN, HEADS_PER_TILE, 2, HEAD_DIM // 2)
    return t.transpose(0, 1, 3, 2, 4).reshape(d, n)


def _reorder_w_in(w):
    pts = np.cumsum((0,) + IN_SIZES)
    sec = [w[:, pts[k]:pts[k + 1]] for k in range(len(IN_SIZES))]
    a_q, a_c, a_iq, a_ik, a_iw, b_q, b_k, b_v, c_q, c_k, c_v, gates = sec
    zeros = lambda n: jnp.zeros((w.shape[0], n), w.dtype)
    w_rope = _halves_layout(jnp.concatenate([a_q, a_iq, b_q, b_k, c_q, c_k, a_ik, zeros(HEAD_DIM)], axis=1))
    w_plain = jnp.concatenate([gates, b_v, a_c, a_iw, zeros(LANES - A_IDX_HEADS), c_v, zeros(LANES)], axis=1)
    assert w_rope.shape[1] == N_ROPE and w_plain.shape[1] == N_PLAIN
    return w_rope.astype(BF16), w_plain.astype(BF16)


def _kv_weights(w_uk, w_uv):
    half = HEAD_DIM // 2
    zeros = lambda n: jnp.zeros((w_uk.shape[0], n), w_uk.dtype)
    return jnp.concatenate([w_uk[:, :half], w_uv, zeros(HALF_TN - half - HEAD_DIM),
                            w_uk[:, half:], zeros(HALF_TN - half)], axis=1).astype(BF16)


def _moe_plan(cls, n_tokens):
    ntiles = n_tokens // MOE_TILE + N_CLASSES
    onehot = (cls[:, None] == jnp.arange(N_CLASSES, dtype=jnp.int32)[None, :]).astype(jnp.int32)
    csum = jnp.cumsum(onehot, axis=0)
    rank = jnp.sum(onehot * csum, axis=1) - 1
    counts = csum[-1]
    ptiles = (counts + MOE_TILE - 1) // MOE_TILE
    tile_end = jnp.cumsum(ptiles)
    tile_start = tile_end - ptiles
    pos = (tile_start[cls] * MOE_TILE + rank).astype(jnp.int32)
    nused = tile_end[-1:].astype(jnp.int32)
    tile_ids = jnp.arange(ntiles, dtype=jnp.int32)
    tile_cls = jnp.minimum(jnp.sum((tile_ids[:, None] >= tile_end[None, :]).astype(jnp.int32), axis=1),
                           N_CLASSES - 1).astype(jnp.int32)
    pair = np.array(PAIRS, dtype=np.int32)
    e_lo = jnp.asarray(np.repeat(np.arange(N_GROUPS), len(PAIRS)) * EXPERTS_PER_GROUP + np.tile(pair[:, 0], N_GROUPS), jnp.int32)
    e_hi = jnp.asarray(np.repeat(np.arange(N_GROUPS), len(PAIRS)) * EXPERTS_PER_GROUP + np.tile(pair[:, 1], N_GROUPS), jnp.int32)
    src = jnp.zeros((ntiles * MOE_TILE,), jnp.int32).at[pos].set(jnp.arange(n_tokens, dtype=jnp.int32))
    return e_lo[tile_cls], e_hi[tile_cls], nused, src.reshape(ntiles, 1, MOE_TILE), pos


def kernel(x, w_in, a_w_uk, a_w_uv, c_sinks, w_branch, w_o, ln1_g, ln1_b, router_w, router_b,
           moe_w_gate, moe_w_up, moe_w_down, ln2_g, ln2_b):
    bsz, seq, _ = x.shape
    n_tokens = bsz * seq
    topk = min(A_TOPK_MAX, seq // 4)
    nkb = seq // B_BLOCK
    tq = 256
    kc = min(DSA_KC, seq)
    tm_proj = min(1024, seq)
    tm_row = 512
    half = HEAD_DIM // 2

    cos, sin = _rope_tables(seq)
    cos_in = jnp.tile(cos, (1, HEADS_PER_TILE))
    sin_in = jnp.tile(sin, (1, HEADS_PER_TILE))
    cos_kv = jnp.concatenate([cos, jnp.ones((seq, HALF_TN - half), F32)], axis=1)
    sin_kv = jnp.concatenate([sin, jnp.zeros((seq, HALF_TN - half), F32)], axis=1)
    rwT = router_w.T
    rb = router_b.reshape(N_EXPERTS, 1)

    xf = x.reshape(n_tokens, D_MODEL)
    xb = xf.astype(BF16)
    for l in range(DEPTH):
        w_rope, w_plain = _reorder_w_in(w_in[l])
        h_rope = _proj_rope(xb, 0, D_MODEL, w_rope, cos_in, sin_in, tm=tm_proj, seq=seq)
        h_plain = _proj_plain(xb, w_plain, tm=tm_proj)
        kv = _proj_rope(h_plain, COL_AC // A_KV_RANK, A_KV_RANK, _kv_weights(a_w_uk[l], a_w_uv[l]),
                        cos_kv, sin_kv, tm=tm_proj, seq=seq)

        vaT4 = kv[:, KV_VA0:KV_VA0 + HEAD_DIM].reshape(bsz, seq // kc, kc, HEAD_DIM).transpose(0, 1, 3, 2)
        bvT5 = (h_plain[:, COL_BV:COL_BV + B_W].reshape(bsz, nkb, B_BLOCK, B_HEADS, HEAD_DIM)
                .transpose(0, 3, 1, 4, 2))
        cvT = (h_plain[:, COL_CV:COL_CV + C_KV_HEADS * HEAD_DIM].reshape(bsz, seq, C_KV_HEADS, HEAD_DIM)
               .transpose(0, 2, 3, 1))

        o_a = _dsa(h_rope, h_plain, kv, vaT4, bsz=bsz, seq=seq, tq=min(DSA_TQ, seq), kc=kc, topk=topk)
        kmean = _moba_kmean(h_rope, n_tokens=n_tokens).reshape(bsz, nkb, B_W)
        sel = _moba_gate(h_rope, kmean, bsz=bsz, seq=seq, tq=min(GATE_TQ, seq))
        o_b = _moba(h_rope, bvT5, sel, bsz=bsz, seq=seq)
        o_c = _swa(c_sinks[l], h_rope, cvT, bsz=bsz, seq=seq, tq=tq)

        wb_all = w_branch[l].astype(BF16)
        x1, cls = _merge(o_a, o_b, o_c, h_plain, xf,
                              wb_all[:A_W], wb_all[A_W:A_W + B_W], wb_all[A_W + B_W:], w_o[l].astype(BF16),
                              ln1_g[l].reshape(1, D_MODEL), ln1_b[l].reshape(1, D_MODEL), rwT, rb, tm=tm_row)

        e1, e2, nused, src3, pos = _moe_plan(cls[0], n_tokens)
        y_sorted = _experts(e1, e2, nused, src3, x1,
                            moe_w_gate[l].astype(BF16), moe_w_up[l].astype(BF16), moe_w_down[l].astype(BF16))
        xf, xb = _final(pos.reshape(n_tokens // tm_row, 1, tm_row), y_sorted, x1,
                        ln2_g[l].reshape(1, D_MODEL), ln2_b[l].reshape(1, D_MODEL), tm=tm_row)
    return xf.reshape(bsz, seq, D_MODEL)
```

```python
import functools
import math

import jax
import jax.numpy as jnp
import numpy as np
from jax import lax
from jax.experimental import pallas as pl
from jax.experimental.pallas import tpu as pltpu

D_MODEL = 1024
DEPTH = 2
HEAD_DIM = 64
ROPE_THETA = 10000.0
LN_EPS = 1e-5
A_HEADS = 4
A_KV_RANK = 128
A_IDX_HEADS = 4
A_IDX_DIM = 64
A_TOPK_MAX = 256
B_HEADS = 4
B_BLOCK = 256
B_TOPK_BLOCKS = 3
C_HEADS = 8
C_KV_HEADS = 2
C_WINDOW = 128
N_BRANCH = 3
A_W = A_HEADS * HEAD_DIM
B_W = B_HEADS * HEAD_DIM
C_W = C_HEADS * HEAD_DIM
IN_SIZES = (A_W, A_KV_RANK, A_IDX_HEADS * A_IDX_DIM, A_IDX_DIM, A_IDX_HEADS,
            B_W, B_W, B_W, C_W, C_KV_HEADS * HEAD_DIM, C_KV_HEADS * HEAD_DIM,
            N_BRANCH * D_MODEL)
N_EXPERTS = 16
N_GROUPS = 4
EXPERTS_PER_GROUP = 4
D_EXPERT = 512
DN_ALPHA = (2 * DEPTH) ** 0.25
ATTN_SCALE = HEAD_DIM ** -0.5
LOG2E = math.log2(math.e)
Q_SCALE = ATTN_SCALE * LOG2E

LANES = 128
SUBLANES = 8
VMEM_LIMIT_BYTES = 56 * 1024 * 1024

NEG_BIG = -1e30
BF16 = jnp.bfloat16
F32 = jnp.float32

PROJ_TN = 256
HALF_TN = PROJ_TN // 2
HEADS_PER_TILE = PROJ_TN // HEAD_DIM
TILE_AQ = 0
TILE_AIQ = 1
TILE_BQ = 2
TILE_BK = 3
TILE_CQ = 4
TILE_CK_AIK = 6
N_ROPE = 7 * PROJ_TN
COL_GATES = 0
COL_BV = 3072
COL_AC = 3328
COL_AIW = 3456
COL_CV = 3584
N_PLAIN = 3840
PLAIN_TN = 1920
KV_VA0 = HEAD_DIM // 2

PAIRS = ((0, 1), (0, 2), (0, 3), (1, 2), (1, 3), (2, 3))
N_CLASSES = N_GROUPS * len(PAIRS)
MOE_TILE = 256
X1_COLS = D_MODEL + LANES

DSA_TQ = 512
DSA_KC = 256
GATE_TQ = 1024

VALUE_MID_PASSES = 16
PEEL_MAX = 2.0
MIN_NORMAL_KEY = 0x00800000
MAX_SELECT_PASSES = 96


def _cparams(n_axes):
    return pltpu.CompilerParams(dimension_semantics=("arbitrary",) * n_axes,
                                vmem_limit_bytes=VMEM_LIMIT_BYTES)


def _proj_rope_body(x_ref, w_ref, cos_ref, sin_ref, o_ref, *, n_sub):
    sub = x_ref.shape[0] // n_sub
    for t in range(w_ref.shape[0]):
        for mi in range(n_sub):
            rows = slice(mi * sub, (mi + 1) * sub)
            acc = jnp.dot(x_ref[rows, :], w_ref[t], preferred_element_type=F32)
            a1, a2 = acc[:, :HALF_TN], acc[:, HALF_TN:]
            c, sn = cos_ref[rows, :], sin_ref[rows, :]
            c0 = t * PROJ_TN
            o_ref[rows, c0:c0 + HALF_TN] = (a1 * c - a2 * sn).astype(o_ref.dtype)
            o_ref[rows, c0 + HALF_TN:c0 + PROJ_TN] = (a2 * c + a1 * sn).astype(o_ref.dtype)


def _proj_plain_body(x_ref, w_ref, o_ref, *, n_sub):
    sub = x_ref.shape[0] // n_sub
    tn = o_ref.shape[1]
    is_gate = (pl.program_id(1) * tn + lax.broadcasted_iota(jnp.int32, (1, tn), 1)) < COL_GATES + N_BRANCH * D_MODEL
    for mi in range(n_sub):
        rows = slice(mi * sub, (mi + 1) * sub)
        acc = jnp.dot(x_ref[rows, :], w_ref[0], preferred_element_type=F32)
        o_ref[rows, :] = jnp.where(is_gate, jax.nn.sigmoid(acc), acc).astype(o_ref.dtype)


def _column_tiles(w, tn=PROJ_TN):
    k_dim, n = w.shape
    return w.reshape(k_dim, n // tn, tn).transpose(1, 0, 2)


def _proj_rope(x, x_col_block, k_dim, w, cos_t, sin_t, *, tm, seq):
    t_tokens = x.shape[0]
    n = w.shape[1]
    pos_blocks = seq // tm
    return pl.pallas_call(
        functools.partial(_proj_rope_body, n_sub=max(tm // 256, 1)),
        grid=(t_tokens // tm,),
        in_specs=[
            pl.BlockSpec((tm, k_dim), lambda i: (i, x_col_block)),
            pl.BlockSpec((n // PROJ_TN, k_dim, PROJ_TN), lambda i: (0, 0, 0)),
            pl.BlockSpec((tm, HALF_TN), lambda i: (i % pos_blocks, 0)),
            pl.BlockSpec((tm, HALF_TN), lambda i: (i % pos_blocks, 0)),
        ],
        out_specs=pl.BlockSpec((tm, n), lambda i: (i, 0)),
        out_shape=jax.ShapeDtypeStruct((t_tokens, n), BF16),
        compiler_params=_cparams(1),
        name="proj_rope",
    )(x, _column_tiles(w), cos_t, sin_t)


def _proj_plain(x, w, *, tm):
    t_tokens, k_dim = x.shape
    n = w.shape[1]
    return pl.pallas_call(
        functools.partial(_proj_plain_body, n_sub=max(tm // 256, 1)),
        grid=(t_tokens // tm, n // PLAIN_TN),
        in_specs=[
            pl.BlockSpec((tm, k_dim), lambda i, j: (i, 0)),
            pl.BlockSpec((1, k_dim, PLAIN_TN), lambda i, j: (j, 0, 0)),
        ],
        out_specs=pl.BlockSpec((tm, PLAIN_TN), lambda i, j: (i, j)),
        out_shape=jax.ShapeDtypeStruct((t_tokens, n), BF16),
        compiler_params=_cparams(2),
        name="proj_plain",
    )(x, _column_tiles(w, PLAIN_TN))


def _values_T_body(bv_ref, cv_ref, kv_ref, bvT_ref, cvT_ref, kvT_ref):
    for src, dst in ((bv_ref, bvT_ref), (cv_ref, cvT_ref), (kv_ref, kvT_ref)):
        for g in range(dst.shape[0]):
            dst[g] = src[g * B_BLOCK:(g + 1) * B_BLOCK, :].astype(F32).T.astype(dst.dtype)


def _values_T(h_plain, kv, *, n_tokens):
    nblk = n_tokens // B_BLOCK
    gb = math.gcd(nblk, SUBLANES)
    in_tile = lambda col: pl.BlockSpec((gb * B_BLOCK, PROJ_TN), lambda i: (i, col // PROJ_TN))
    out_tile = pl.BlockSpec((gb, PROJ_TN, B_BLOCK), lambda i: (i, 0, 0))
    shape = jax.ShapeDtypeStruct((nblk, PROJ_TN, B_BLOCK), BF16)
    return pl.pallas_call(
        _values_T_body,
        grid=(nblk // gb,),
        in_specs=[in_tile(COL_BV), in_tile(COL_CV), in_tile(0)],
        out_specs=[out_tile, out_tile, out_tile],
        out_shape=[shape, shape, shape],
        compiler_params=_cparams(1),
        name="values_transpose",
    )(h_plain, h_plain, kv)


def _f32_to_key(x):
    b = lax.bitcast_convert_type(x, jnp.int32)
    return jnp.where(b < 0, b ^ jnp.int32(0x7FFFFFFF), b)


def _key_to_f32(k):
    b = jnp.where(k < 0, k ^ jnp.int32(0x7FFFFFFF), k)
    return lax.bitcast_convert_type(b, F32)


def _fold8(x, op):
    return op(x.reshape(x.shape[0] // SUBLANES, SUBLANES, x.shape[1]), axis=0)


def _tile_T(x_ref):
    return x_ref[...].astype(F32).T


def _head_rows(xT, h):
    half = HEAD_DIM // 2
    return xT[h * half:(h + 1) * half], xT[HALF_TN + h * half:HALF_TN + (h + 1) * half]


def _place_head(first, second, slot):
    half = HEAD_DIM // 2
    w = first.shape[1]
    before, after = slot * half, HALF_TN - (slot + 1) * half
    pieces = []
    for part in (first, second):
        pieces += [jnp.zeros((before, w), part.dtype)] * (before > 0) + [part] + [jnp.zeros((after, w), part.dtype)] * (after > 0)
    return jnp.concatenate(pieces, axis=0)


def _store_rows(o_ref, oT_heads):
    o_ref[...] = jnp.concatenate(oT_heads, axis=0).T.astype(o_ref.dtype)


IK_SLOT = 2


def _dsa_body(q_ref, iq_ref, iw_ref, ik_ref, kv_ref, vT_ref, o_ref, sc_ref, acc_ref, sbuf_ref, *, tq, kc, topk):
    i = pl.program_id(1)
    q0 = i * tq
    nch = (q0 + tq + kc - 1) // kc
    qpos = q0 + lax.broadcasted_iota(jnp.int32, (1, tq), 1)
    iqT = _tile_T(iq_ref)
    iq_ops = [_place_head(*[p.astype(BF16) for p in _head_rows(iqT, h)], IK_SLOT) for h in range(A_IDX_HEADS)]
    w_idx = iw_ref[...].astype(F32).T[:A_IDX_HEADS]

    def score_chunk(c, carry, diagonal):
        mn, mx = carry
        k0 = pl.multiple_of(c * kc, kc)
        ikc = ik_ref[pl.ds(k0, kc), :]
        acc = jnp.zeros((kc, tq), F32)
        for h in range(A_IDX_HEADS):
            d = jnp.dot(ikc, iq_ops[h], preferred_element_type=F32)
            acc = acc + w_idx[h:h + 1, :] * jnp.maximum(d, 0.0)
        if diagonal:
            causal = (k0 + lax.broadcasted_iota(jnp.int32, (kc, 1), 0)) <= qpos
            lo_part, hi_part = jnp.where(causal, acc, jnp.inf), jnp.where(causal, acc, -jnp.inf)
        else:
            lo_part = hi_part = acc
        sc_ref[pl.ds(k0, kc), :] = hi_part
        return jnp.minimum(mn, _fold8(lo_part, jnp.min)), jnp.maximum(mx, _fold8(hi_part, jnp.max))

    n_full = (q0 + 1) // kc
    carry = lax.fori_loop(0, n_full, functools.partial(score_chunk, diagonal=False),
                          (jnp.full((SUBLANES, tq), jnp.inf, F32), jnp.full((SUBLANES, tq), -jnp.inf, F32)))
    mn8, mx8 = lax.fori_loop(n_full, nch, functools.partial(score_chunk, diagonal=True), carry)
    row_min = jnp.min(mn8, axis=0, keepdims=True)
    row_max = jnp.max(mx8, axis=0, keepdims=True)

    pc = 2 * kc
    npair = (nch + 1) // 2
    n_chunks_total = vT_ref.shape[0]

    @pl.when(nch % 2 == 1)
    def _():
        sc_ref[pl.ds(pl.multiple_of(nch * kc, kc), kc), :] = jnp.full((kc, tq), -jnp.inf, F32)

    def count_ge(thr):
        def body(j, cnt):
            s = sc_ref[pl.ds(pl.multiple_of(j * pc, pc), pc), :]
            return cnt + _fold8(jnp.where(s >= thr, 1.0, 0.0), jnp.sum)
        cnt8 = lax.fori_loop(0, npair, body, jnp.zeros((SUBLANES, tq), F32))
        return jnp.sum(cnt8, axis=0, keepdims=True)

    def min_ge(thr):
        def body(j, mn):
            s = sc_ref[pl.ds(pl.multiple_of(j * pc, pc), pc), :]
            return jnp.minimum(mn, _fold8(jnp.where(s >= thr, s, jnp.inf), jnp.min))
        mn8 = lax.fori_loop(0, npair, body, jnp.full((SUBLANES, tq), jnp.inf, F32))
        return jnp.min(mn8, axis=0, keepdims=True)

    def count_zero():
        def body(j, carry):
            c0, cp = carry
            s = sc_ref[pl.ds(pl.multiple_of(j * pc, pc), pc), :]
            return (c0 + _fold8(jnp.where(s >= 0.0, 1.0, 0.0), jnp.sum),
                    cp + _fold8(jnp.where(s > 0.0, 1.0, 0.0), jnp.sum))
        z8 = jnp.zeros((SUBLANES, tq), F32)
        c0, cp = lax.fori_loop(0, npair, body, (z8, z8))
        return jnp.sum(c0, axis=0, keepdims=True), jnp.sum(cp, axis=0, keepdims=True)

    kf = float(topk)
    n_valid = (qpos + 1).astype(F32)
    few = n_valid <= kf
    cnt_nonneg, cnt_pos = count_zero()
    is_zero = jnp.logical_and(cnt_pos < kf, cnt_nonneg >= kf)
    is_pos = cnt_pos >= kf
    lo0 = jnp.where(is_zero, 0, jnp.where(is_pos, MIN_NORMAL_KEY, _f32_to_key(row_min)))
    hi0 = jnp.where(is_zero, 1, jnp.where(is_pos, _f32_to_key(row_max) + 1, -1))
    cnt0 = jnp.where(is_zero, cnt_nonneg, jnp.where(is_pos, cnt_pos, n_valid))
    lo0 = jnp.where(few, _f32_to_key(row_min), lo0)
    hi0 = jnp.where(few, lo0 + 1, hi0)

    def unfinished(lo, hi, cnt_lo):
        return jnp.logical_and(lo + 1 != hi, cnt_lo != kf)

    def bis_cond(st):
        lo, hi, cnt_lo, it = st
        return jnp.logical_and(jnp.max(jnp.where(unfinished(lo, hi, cnt_lo), 1.0, 0.0)) > 0.0, it < MAX_SELECT_PASSES)

    def bisect(st):
        lo, hi, cnt_lo, it = st
        nd = unfinished(lo, hi, cnt_lo)
        mid_val = _f32_to_key(0.5 * _key_to_f32(lo) + 0.5 * _key_to_f32(hi))
        mid_key = lo + lax.shift_right_logical(hi - lo, 1)
        mid = jnp.where(it < VALUE_MID_PASSES, jnp.clip(mid_val, lo + 1, hi - 1), mid_key)
        cnt = count_ge(_key_to_f32(mid))
        ge = cnt >= kf
        up = jnp.logical_and(nd, ge)
        down = jnp.logical_and(nd, jnp.logical_not(ge))
        return jnp.where(up, mid, lo), jnp.where(down, mid, hi), jnp.where(up, cnt, cnt_lo), it + 1

    def peel(st):
        lo, hi, cnt_lo, it = st
        nd = unfinished(lo, hi, cnt_lo)
        k1 = jnp.maximum(_f32_to_key(min_ge(_key_to_f32(lo))), lo)
        cnt = count_ge(_key_to_f32(k1 + 1))
        ge = cnt >= kf
        up = jnp.logical_and(nd, ge)
        down = jnp.logical_and(nd, jnp.logical_not(ge))
        new_lo = jnp.where(nd, jnp.where(ge, k1 + 1, k1), lo)
        return new_lo, jnp.where(down, k1 + 1, hi), jnp.where(up, cnt, cnt_lo), it + 2

    def bis_body(st):
        lo, hi, cnt_lo, it = st
        excess = jnp.max(jnp.where(unfinished(lo, hi, cnt_lo), cnt_lo - kf, 0.0))
        return lax.cond(excess > PEEL_MAX, bisect, peel, st)

    lo, _, cnt_lo, _ = lax.while_loop(bis_cond, bis_body, (lo0, hi0, cnt0, jnp.int32(0)))
    thr = _key_to_f32(lo)

    tie = jnp.logical_and(cnt_lo > kf, jnp.logical_not(few))

    @pl.when(jnp.max(jnp.where(tie, 1.0, 0.0)) > 0.0)
    def _():
        def count_gt():
            def gt_body(c, cnt):
                k0 = pl.multiple_of(c * kc, kc)
                s = sc_ref[pl.ds(k0, kc), :]
                return cnt + _fold8(jnp.where(s > thr, 1.0, 0.0), jnp.sum)
            gt8 = lax.fori_loop(0, nch, gt_body, jnp.zeros((SUBLANES, tq), F32))
            return jnp.sum(gt8, axis=0, keepdims=True)

        zero_thr = thr == 0.0
        other = jnp.max(jnp.where(jnp.logical_and(tie, jnp.logical_not(zero_thr)), 1.0, 0.0)) > 0.0
        gt = lax.cond(other, count_gt, lambda: cnt_pos)
        need = kf - jnp.where(zero_thr, cnt_pos, gt)
        r = lax.broadcasted_iota(jnp.int32, (kc, kc), 0)
        cidx = lax.broadcasted_iota(jnp.int32, (kc, kc), 1)
        tri = jnp.where(cidx <= r, 1.0, 0.0).astype(BF16)

        thr_tie = jnp.where(tie, thr, jnp.nan)
        keep = jnp.where(tie, need, jnp.inf)

        def tie_body(c, seen):
            k0 = pl.multiple_of(c * kc, kc)
            s = sc_ref[pl.ds(k0, kc), :]
            eqf = jnp.where(s == thr_tie, 1.0, 0.0)
            rank = jnp.dot(tri, eqf.astype(BF16), preferred_element_type=F32) + seen
            sc_ref[pl.ds(k0, kc), :] = jnp.where(eqf * rank > keep, -jnp.inf, s)
            return seen + jnp.sum(eqf, axis=0, keepdims=True)

        lax.fori_loop(0, nch, tie_body, jnp.zeros((1, tq), F32))

    qT_all = (_tile_T(q_ref) * Q_SCALE).astype(BF16)
    qT = jnp.concatenate([_place_head(*_head_rows(qT_all, h), 0) for h in range(A_HEADS)], axis=1)
    acc_ref[...] = jnp.zeros_like(acc_ref)
    ones_rows = jnp.ones((SUBLANES, kc), BF16)

    def issue_scores(c, slot):
        k0 = pl.multiple_of(jnp.minimum(c, n_chunks_total - 1) * kc, kc)
        sbuf_ref[slot] = jnp.dot(kv_ref[pl.ds(k0, kc), :], qT, preferred_element_type=F32)

    def softmax_pv(c, slot, m):
        k0 = pl.multiple_of(c * kc, kc)
        v1 = jnp.concatenate([vT_ref[c, KV_VA0:KV_VA0 + HEAD_DIM, :], ones_rows], axis=0)
        sel = sc_ref[pl.ds(k0, kc), :] >= thr
        s = jnp.concatenate([jnp.where(sel, sbuf_ref[slot, :, h * tq:(h + 1) * tq], NEG_BIG)
                             for h in range(A_HEADS)], axis=1)
        m_new = jnp.maximum(m, jnp.max(s, axis=0, keepdims=True))
        alpha = jnp.exp2(m - m_new)
        p = jnp.exp2((s - m_new).astype(BF16))
        acc_ref[...] = acc_ref[...] * alpha + jnp.dot(v1, p, preferred_element_type=F32)
        return m_new

    def chunk_pair(j, m):
        c0 = 2 * j
        issue_scores(c0 + 1, 1)
        m = softmax_pv(c0, 0, m)
        issue_scores(c0 + 2, 0)
        return softmax_pv(c0 + 1, 1, m)

    issue_scores(0, 0)
    lax.fori_loop(0, npair, chunk_pair, jnp.full((1, A_HEADS * tq), NEG_BIG, F32))
    a = acc_ref[...]
    o = a[:HEAD_DIM] / a[HEAD_DIM:HEAD_DIM + 1]
    _store_rows(o_ref, [o[:, h * tq:(h + 1) * tq] for h in range(A_HEADS)])


def _dsa(h_rope, h_plain, kv, kvT, *, bsz, seq, tq, kc, topk):
    nq = seq // tq
    assert kc == B_BLOCK
    return pl.pallas_call(
        functools.partial(_dsa_body, tq=tq, kc=kc, topk=topk),
        grid=(bsz, nq),
        in_specs=[
            pl.BlockSpec((tq, PROJ_TN), lambda b, i: (b * nq + i, TILE_AQ)),
            pl.BlockSpec((tq, PROJ_TN), lambda b, i: (b * nq + i, TILE_AIQ)),
            pl.BlockSpec((tq, LANES), lambda b, i: (b * nq + i, COL_AIW // LANES)),
            pl.BlockSpec((seq, PROJ_TN), lambda b, i: (b, TILE_CK_AIK)),
            pl.BlockSpec((seq, PROJ_TN), lambda b, i: (b, 0)),
            pl.BlockSpec((seq // kc, PROJ_TN, kc), lambda b, i: (b, 0, 0)),
        ],
        out_specs=pl.BlockSpec((tq, A_W), lambda b, i: (b * nq + i, 0)),
        out_shape=jax.ShapeDtypeStruct((bsz * seq, A_W), BF16),
        scratch_shapes=[pltpu.VMEM((seq + kc, tq), F32),
                        pltpu.VMEM((HEAD_DIM + SUBLANES, A_HEADS * tq), F32),
                        pltpu.VMEM((2, kc, A_HEADS * tq), F32)],
        compiler_params=_cparams(2),
        name="dsa_attention",
    )(h_rope, h_rope, h_plain, h_rope, kv, kvT)


def _kmean_body(k_ref, o_ref):
    gb = o_ref.shape[0]
    o_ref[...] = jnp.mean(k_ref[...].astype(F32).reshape(gb, B_BLOCK, B_W), axis=1)


def _moba_kmean(h_rope, *, n_tokens):
    nblk = n_tokens // B_BLOCK
    gb = math.gcd(nblk, SUBLANES)
    return pl.pallas_call(
        _kmean_body,
        grid=(nblk // gb,),
        in_specs=[pl.BlockSpec((gb * B_BLOCK, B_W), lambda i: (i, TILE_BK))],
        out_specs=pl.BlockSpec((gb, B_W), lambda i: (i, 0)),
        out_shape=jax.ShapeDtypeStruct((nblk, B_W), F32),
        compiler_params=_cparams(1),
        name="moba_kmean",
    )(h_rope)


def _moba_gate_body(q_ref, km_ref, sel_ref, *, tq, nkb, nsel):
    i = pl.program_id(1)
    qpos = i * tq + lax.broadcasted_iota(jnp.int32, (1, tq), 1)
    cur = qpos // B_BLOCK
    row = lax.broadcasted_iota(jnp.int32, (nkb, tq), 0)
    past = row < cur
    qT = _tile_T(q_ref).astype(BF16)
    km = km_ref[0]
    col_head = (lax.broadcasted_iota(jnp.int32, km.shape, 1) % HALF_TN) // (HEAD_DIM // 2)
    for h in range(B_HEADS):
        km_h = jnp.where(col_head == h, km, 0.0).astype(BF16)
        g = jnp.dot(km_h, qT, preferred_element_type=F32)
        g = jnp.where(past, g, -jnp.inf)
        sel = jnp.zeros((nkb, tq), F32)
        for _ in range(nsel):
            mx = jnp.max(g, axis=0, keepdims=True)
            idx = jnp.min(jnp.where(g == mx, row, nkb), axis=0, keepdims=True)
            pick = row == idx
            sel = jnp.where(pick, 1.0, sel)
            g = jnp.where(pick, -jnp.inf, g)
        sel_ref[0, h] = jnp.where(past, sel, 0.0)


def _moba_gate(h_rope, kmean, *, bsz, seq, tq):
    nkb = seq // B_BLOCK
    nsel = min(B_TOPK_BLOCKS, nkb)
    nq = seq // tq
    return pl.pallas_call(
        functools.partial(_moba_gate_body, tq=tq, nkb=nkb, nsel=nsel),
        grid=(bsz, nq),
        in_specs=[
            pl.BlockSpec((tq, PROJ_TN), lambda b, i: (b * nq + i, TILE_BQ)),
            pl.BlockSpec((1, nkb, B_W), lambda b, i: (b, 0, 0)),
        ],
        out_specs=pl.BlockSpec((1, B_HEADS, nkb, tq), lambda b, i: (b, 0, 0, i)),
        out_shape=jax.ShapeDtypeStruct((bsz, B_HEADS, nkb, seq), F32),
        compiler_params=_cparams(2),
        name="moba_gate",
    )(h_rope, kmean)


def _moba_body(q_ref, k_ref, vT_ref, sel_ref, o_ref, acc_ref, sbuf_ref, *, tq):
    i = pl.program_id(1)
    qT_all = (_tile_T(q_ref) * Q_SCALE).astype(BF16)
    qT = [_place_head(*_head_rows(qT_all, h), h) for h in range(B_HEADS)]
    ones_rows = jnp.ones((SUBLANES, B_BLOCK), BF16)
    n_past = i

    def values(h, j):
        return jnp.concatenate([vT_ref[j, h * HEAD_DIM:(h + 1) * HEAD_DIM, :], ones_rows], axis=0)

    def issue_scores(j, slot):
        k0 = pl.multiple_of(jnp.minimum(j, i) * B_BLOCK, B_BLOCK)
        for h in range(B_HEADS):
            sbuf_ref[slot, h] = jnp.dot(k_ref[pl.ds(k0, B_BLOCK), :], qT[h], preferred_element_type=F32)

    def softmax_pv(j, slot, ms):
        ps, alphas, new_ms = [], [], []
        for h in range(B_HEADS):
            s = jnp.where(sel_ref[0, h, pl.ds(j, 1), :] > 0.0, sbuf_ref[slot, h], NEG_BIG)
            m_new = jnp.maximum(ms[h], jnp.max(s, axis=0, keepdims=True))
            alphas.append(jnp.exp2(ms[h] - m_new))
            ps.append(jnp.exp2((s - m_new).astype(BF16)))
            new_ms.append(m_new)
        for h in range(B_HEADS):
            acc_ref[h] = acc_ref[h] * alphas[h] + jnp.dot(values(h, j), ps[h], preferred_element_type=F32)
        return tuple(new_ms)

    own0 = pl.multiple_of(i * B_BLOCK, B_BLOCK)
    s_own = [jnp.dot(k_ref[pl.ds(own0, B_BLOCK), :], qT[h], preferred_element_type=F32) for h in range(B_HEADS)]
    issue_scores(0, 0)
    causal = lax.broadcasted_iota(jnp.int32, (B_BLOCK, tq), 0) <= lax.broadcasted_iota(jnp.int32, (B_BLOCK, tq), 1)
    ms = []
    for h in range(B_HEADS):
        s = jnp.where(causal, s_own[h], NEG_BIG)
        m0 = jnp.max(s, axis=0, keepdims=True)
        acc_ref[h] = jnp.dot(values(h, i), jnp.exp2((s - m0).astype(BF16)), preferred_element_type=F32)
        ms.append(m0)

    def block_pair(jj, ms):
        j0 = 2 * jj
        issue_scores(j0 + 1, 1)
        ms = softmax_pv(j0, 0, ms)
        issue_scores(j0 + 2, 0)
        return softmax_pv(j0 + 1, 1, ms)

    lax.fori_loop(0, (n_past + 1) // 2, block_pair, tuple(ms))
    outs = []
    for h in range(B_HEADS):
        a = acc_ref[h]
        outs.append(a[:HEAD_DIM] / a[HEAD_DIM:HEAD_DIM + 1])
    _store_rows(o_ref, outs)


def _moba(h_rope, bvT, sel, *, bsz, seq):
    tq = B_BLOCK
    nkb = seq // B_BLOCK
    nq = seq // tq
    return pl.pallas_call(
        functools.partial(_moba_body, tq=tq),
        grid=(bsz, nq),
        in_specs=[
            pl.BlockSpec((tq, PROJ_TN), lambda b, i: (b * nq + i, TILE_BQ)),
            pl.BlockSpec((seq, PROJ_TN), lambda b, i: (b, TILE_BK)),
            pl.BlockSpec((nkb, B_W, B_BLOCK), lambda b, i: (b, 0, 0)),
            pl.BlockSpec((1, B_HEADS, nkb, tq), lambda b, i: (b, 0, 0, i)),
        ],
        out_specs=pl.BlockSpec((tq, B_W), lambda b, i: (b * nq + i, 0)),
        out_shape=jax.ShapeDtypeStruct((bsz * seq, B_W), BF16),
        scratch_shapes=[pltpu.VMEM((B_HEADS, HEAD_DIM + SUBLANES, tq), F32),
                        pltpu.VMEM((2, B_HEADS, B_BLOCK, tq), F32)],
        compiler_params=_cparams(2),
        name="moba_attention",
    )(h_rope, h_rope, bvT, sel)


def _swa_body(sink_ref, q0_ref, q1_ref, k0_ref, k1_ref, k2_ref, vprev_ref, vcur_ref, o_ref, *, tq):
    i = pl.program_id(1)
    q0 = i * tq
    qpos = q0 + lax.broadcasted_iota(jnp.int32, (1, tq), 1)
    nk = tq + C_WINDOW
    kpos = q0 - C_WINDOW + lax.broadcasted_iota(jnp.int32, (nk, 1), 0)
    diff = qpos - kpos
    ok = jnp.where(diff >= 0, jnp.where(diff < C_WINDOW, jnp.where(kpos >= 0, 1.0, 0.0), 0.0), 0.0) > 0.0
    ones_rows = jnp.ones((SUBLANES, nk), BF16)
    group = C_HEADS // C_KV_HEADS
    assert group == HEADS_PER_TILE
    kk = jnp.concatenate([k0_ref[...], k1_ref[...], k2_ref[...]], axis=0)
    outs = []
    scores = []
    for kv, q_ref in enumerate((q0_ref, q1_ref)):
        qT_all = (_tile_T(q_ref) * Q_SCALE).astype(BF16)
        qT = jnp.concatenate([_place_head(*_head_rows(qT_all, g), kv) for g in range(group)], axis=1)
        scores.append(jnp.dot(kk, qT, preferred_element_type=F32))
    for kv in range(C_KV_HEADS):
        vrows = slice(kv * HEAD_DIM, (kv + 1) * HEAD_DIM)
        vv = jnp.concatenate([vprev_ref[0, vrows, tq - C_WINDOW:], vcur_ref[0, vrows, :]], axis=1)
        v1 = jnp.concatenate([vv, ones_rows], axis=0)
        heads = [kv * group + g for g in range(group)]
        sink = jnp.concatenate([jnp.full((1, tq), sink_ref[hd] * LOG2E, F32) for hd in heads], axis=1)
        s = jnp.concatenate([jnp.where(ok, scores[kv][:, g * tq:(g + 1) * tq], NEG_BIG) for g in range(group)], axis=1)
        m = jnp.maximum(jnp.max(s, axis=0, keepdims=True), sink)
        p = jnp.exp2((s - m).astype(BF16))
        a = jnp.dot(v1, p, preferred_element_type=F32)
        o = a[:HEAD_DIM] / (a[HEAD_DIM:HEAD_DIM + 1] + jnp.exp2(sink - m))
        outs += [o[:, g * tq:(g + 1) * tq] for g in range(group)]
    _store_rows(o_ref, outs)


def _swa(sinks, h_rope, cvT, *, bsz, seq, tq):
    r = tq // C_WINDOW
    assert r == 2 and tq == B_BLOCK
    nq = seq // tq
    nwb = seq // C_WINDOW

    def kspec(off):
        return pl.BlockSpec((C_WINDOW, PROJ_TN),
                            lambda b, i, s: (b * nwb + jnp.maximum(i * r + off, 0), TILE_CK_AIK))

    def vspec(off):
        return pl.BlockSpec((1, PROJ_TN, B_BLOCK), lambda b, i, s: (b * nq + jnp.maximum(i + off, 0), 0, 0))

    grid_spec = pltpu.PrefetchScalarGridSpec(
        num_scalar_prefetch=1,
        grid=(bsz, nq),
        in_specs=[pl.BlockSpec((tq, PROJ_TN), lambda b, i, s: (b * nq + i, TILE_CQ)),
                  pl.BlockSpec((tq, PROJ_TN), lambda b, i, s: (b * nq + i, TILE_CQ + 1)),
                  kspec(-1), kspec(0), kspec(1), vspec(-1), vspec(0)],
        out_specs=pl.BlockSpec((tq, C_W), lambda b, i, s: (b * nq + i, 0)),
    )
    return pl.pallas_call(
        functools.partial(_swa_body, tq=tq),
        grid_spec=grid_spec,
        out_shape=jax.ShapeDtypeStruct((bsz * seq, C_W), BF16),
        compiler_params=_cparams(2),
        name="swa_attention",
    )(sinks, h_rope, h_rope, h_rope, h_rope, h_rope, cvT, cvT)


def _layer_norm(z, g, b):
    mu = jnp.mean(z, axis=-1, keepdims=True)
    zc = z - mu
    var = jnp.mean(zc * zc, axis=-1, keepdims=True)
    return zc * lax.rsqrt(var + LN_EPS) * g + b


def _split_bf16(a):
    hi = a.astype(BF16)
    lo = (a - hi.astype(F32)).astype(BF16)
    return hi, lo


def _router(x1, rw_ref, rb_ref, cls_ref):
    nt = (((1,), (1,)), ((), ()))
    xh, xl = _split_bf16(x1)
    wh, wl = _split_bf16(rw_ref[...])
    logits = (lax.dot_general(wh, xh, nt, preferred_element_type=F32)
              + lax.dot_general(wh, xl, nt, preferred_element_type=F32)
              + lax.dot_general(wl, xh, nt, preferred_element_type=F32))
    aff = jax.nn.sigmoid(logits)
    score = aff + rb_ref[...]
    tm = x1.shape[0]
    sc = [score[e:e + 1, :] for e in range(N_EXPERTS)]
    af = [aff[e:e + 1, :] for e in range(N_EXPERTS)]
    in_top = []
    grp_score = []
    for gq in range(N_GROUPS):
        gs = jnp.zeros((1, tm), F32)
        for a in range(EXPERTS_PER_GROUP):
            ea = gq * EXPERTS_PER_GROUP + a
            rank = jnp.zeros((1, tm), F32)
            for b in range(EXPERTS_PER_GROUP):
                if b == a:
                    continue
                eb = gq * EXPERTS_PER_GROUP + b
                beats = (sc[eb] >= sc[ea]) if b < a else (sc[eb] > sc[ea])
                rank = rank + jnp.where(beats, 1.0, 0.0)
            top = rank < 2.0
            in_top.append(top)
            gs = gs + jnp.where(top, sc[ea], 0.0)
        grp_score.append(gs)
    best = grp_score[0]
    gstar = jnp.zeros((1, tm), jnp.int32)
    for gq in range(1, N_GROUPS):
        better = grp_score[gq] > best
        best = jnp.where(better, grp_score[gq], best)
        gstar = jnp.where(better, gq, gstar)
    cls = jnp.zeros((1, tm), jnp.int32)
    w_lo = jnp.zeros((1, tm), F32)
    w_hi = jnp.zeros((1, tm), F32)
    for gq in range(N_GROUPS):
        is_g = gstar == gq
        for pi, (a, b) in enumerate(PAIRS):
            ea, eb = gq * EXPERTS_PER_GROUP + a, gq * EXPERTS_PER_GROUP + b
            hit = jnp.where(is_g, jnp.where(in_top[ea], jnp.where(in_top[eb], 1.0, 0.0), 0.0), 0.0) > 0.0
            cls = jnp.where(hit, gq * len(PAIRS) + pi, cls)
            tot = af[ea] + af[eb]
            w_lo = jnp.where(hit, af[ea] / tot, w_lo)
            w_hi = jnp.where(hit, af[eb] / tot, w_hi)
    cls_ref[...] = cls
    return w_lo, w_hi


def _merge_body(oa_ref, ob_ref, oc_ref, ga_ref, gb_ref, gc_ref, x_ref, wa_ref, wb_ref, wc_ref, wo_ref,
                g_ref, b_ref, rw_ref, rb_ref, x1_ref, cls_ref):
    def branch(o_ref, w_ref, gate_ref):
        y = jnp.dot(o_ref[...], w_ref[...], preferred_element_type=F32)
        return gate_ref[...].astype(F32) * y

    merged = branch(oa_ref, wa_ref, ga_ref) + branch(ob_ref, wb_ref, gb_ref) + branch(oc_ref, wc_ref, gc_ref)
    y = jnp.dot(merged.astype(BF16), wo_ref[...], preferred_element_type=F32)
    x1 = _layer_norm(DN_ALPHA * x_ref[...] + y, g_ref[...], b_ref[...])
    w_lo, w_hi = _router(x1, rw_ref, rb_ref, cls_ref)
    x1_ref[:, :D_MODEL] = x1
    x1_ref[:, D_MODEL:] = jnp.concatenate([w_lo, w_hi, jnp.zeros((LANES - 2, x1.shape[0]), F32)], axis=0).T


def _merge(oa, ob, oc, h_plain, x, wa, wb, wc, wo, ln_g, ln_b, rwT, rb, *, tm):
    t_tokens = x.shape[0]
    gate_blk = COL_GATES // D_MODEL
    full = lambda shape: pl.BlockSpec(shape, lambda i: (0,) * len(shape))
    return pl.pallas_call(
        _merge_body,
        grid=(t_tokens // tm,),
        in_specs=[
            pl.BlockSpec((tm, A_W), lambda i: (i, 0)),
            pl.BlockSpec((tm, B_W), lambda i: (i, 0)),
            pl.BlockSpec((tm, C_W), lambda i: (i, 0)),
            pl.BlockSpec((tm, D_MODEL), lambda i: (i, gate_blk)),
            pl.BlockSpec((tm, D_MODEL), lambda i: (i, gate_blk + 1)),
            pl.BlockSpec((tm, D_MODEL), lambda i: (i, gate_blk + 2)),
            pl.BlockSpec((tm, D_MODEL), lambda i: (i, 0)),
            full((A_W, D_MODEL)), full((B_W, D_MODEL)), full((C_W, D_MODEL)), full((D_MODEL, D_MODEL)),
            full((1, D_MODEL)), full((1, D_MODEL)), full((N_EXPERTS, D_MODEL)), full((N_EXPERTS, 1)),
        ],
        out_specs=[
            pl.BlockSpec((tm, X1_COLS), lambda i: (i, 0)),
            pl.BlockSpec((1, tm), lambda i: (0, i)),
        ],
        out_shape=[
            jax.ShapeDtypeStruct((t_tokens, X1_COLS), F32),
            jax.ShapeDtypeStruct((1, t_tokens), jnp.int32),
        ],
        compiler_params=_cparams(1),
        name="merge_ln_router",
    )(oa, ob, oc, h_plain, h_plain, h_plain, x, wa, wb, wc, wo, ln_g, ln_b, rwT, rb)


def _gather_start(idx_ref, src_hbm, dst_ref, sem, n_rows):
    def start(r, _):
        pltpu.make_async_copy(src_hbm.at[pl.ds(idx_ref[0, 0, r], 1)], dst_ref.at[pl.ds(r, 1)], sem).start()
        return 0

    lax.fori_loop(0, n_rows, start, 0, unroll=8)


def _gather_wait(src_hbm, dst_ref, sem, n_rows):
    pltpu.make_async_copy(src_hbm.at[pl.ds(0, n_rows)], dst_ref, sem).wait()


def _experts_body(e1_ref, e2_ref, nused_ref, src_ref, src_next_ref, x_hbm,
                  wg1_ref, wu1_ref, wd1_ref, wg2_ref, wu2_ref, wd2_ref, o_ref, xbuf, sem):
    s = pl.program_id(0)
    nused = nused_ref[0]
    slot = lax.rem(s, 2)

    @pl.when(s == 0)
    def _():
        _gather_start(src_ref, x_hbm, xbuf.at[0], sem.at[0], MOE_TILE)

    @pl.when(s + 1 < nused)
    def _():
        _gather_start(src_next_ref, x_hbm, xbuf.at[1 - slot], sem.at[1 - slot], MOE_TILE)

    @pl.when(s < nused)
    def _():
        _gather_wait(x_hbm, xbuf.at[slot], sem.at[slot], MOE_TILE)
        xb = xbuf[slot, :, :D_MODEL].astype(BF16)
        wt = xbuf[slot, :, D_MODEL:D_MODEL + 2]

        def expert(wg_ref, wu_ref, wd_ref):
            g = jnp.dot(xb, wg_ref[0], preferred_element_type=F32)
            u = jnp.dot(xb, wu_ref[0], preferred_element_type=F32)
            he = (g * jax.nn.sigmoid(g) * u).astype(BF16)
            return jnp.dot(he, wd_ref[0], preferred_element_type=F32)

        o_ref[...] = wt[:, 0:1] * expert(wg1_ref, wu1_ref, wd1_ref) + wt[:, 1:2] * expert(wg2_ref, wu2_ref, wd2_ref)

    @pl.when(s >= nused)
    def _():
        o_ref[...] = jnp.zeros_like(o_ref)


def _experts(tile_e1, tile_e2, nused, src3, x1, wg, wu, wd):
    ntiles = src3.shape[0]
    wspec_up = lambda which: pl.BlockSpec((1, D_MODEL, D_EXPERT), lambda s, e1, e2, nu: ((e1, e2)[which][s], 0, 0))
    wspec_dn = lambda which: pl.BlockSpec((1, D_EXPERT, D_MODEL), lambda s, e1, e2, nu: ((e1, e2)[which][s], 0, 0))
    grid_spec = pltpu.PrefetchScalarGridSpec(
        num_scalar_prefetch=3,
        grid=(ntiles,),
        in_specs=[
            pl.BlockSpec((1, 1, MOE_TILE), lambda s, e1, e2, nu: (s, 0, 0), memory_space=pltpu.SMEM),
            pl.BlockSpec((1, 1, MOE_TILE), lambda s, e1, e2, nu: (jnp.minimum(s + 1, ntiles - 1), 0, 0),
                         memory_space=pltpu.SMEM),
            pl.BlockSpec(memory_space=pl.ANY),
            wspec_up(0), wspec_up(0), wspec_dn(0), wspec_up(1), wspec_up(1), wspec_dn(1),
        ],
        out_specs=pl.BlockSpec((MOE_TILE, D_MODEL), lambda s, e1, e2, nu: (s, 0)),
        scratch_shapes=[pltpu.VMEM((2, MOE_TILE, X1_COLS), F32), pltpu.SemaphoreType.DMA((2,))],
    )
    return pl.pallas_call(
        _experts_body,
        grid_spec=grid_spec,
        out_shape=jax.ShapeDtypeStruct((ntiles * MOE_TILE, D_MODEL), F32),
        compiler_params=_cparams(1),
        name="moe_experts",
    )(tile_e1, tile_e2, nused, src3, src3, x1, wg, wu, wd, wg, wu, wd)


def _final_body(pos_ref, pos_next_ref, y_hbm, x_ref, g_ref, b_ref, o_ref, ob_ref, ybuf, sem, *, tm):
    s = pl.program_id(0)
    slot = lax.rem(s, 2)

    @pl.when(s == 0)
    def _():
        _gather_start(pos_ref, y_hbm, ybuf.at[0], sem.at[0], tm)

    @pl.when(s + 1 < pl.num_programs(0))
    def _():
        _gather_start(pos_next_ref, y_hbm, ybuf.at[1 - slot], sem.at[1 - slot], tm)

    _gather_wait(y_hbm, ybuf.at[slot], sem.at[slot], tm)
    x2 = _layer_norm(DN_ALPHA * x_ref[...] + ybuf[slot], g_ref[...], b_ref[...])
    o_ref[...] = x2
    ob_ref[...] = x2.astype(BF16)


def _final(pos3, y_sorted, x1, ln_g, ln_b, *, tm):
    t_tokens = x1.shape[0]
    nsteps = t_tokens // tm
    return pl.pallas_call(
        functools.partial(_final_body, tm=tm),
        grid=(nsteps,),
        in_specs=[
            pl.BlockSpec((1, 1, tm), lambda i: (i, 0, 0), memory_space=pltpu.SMEM),
            pl.BlockSpec((1, 1, tm), lambda i: (jnp.minimum(i + 1, nsteps - 1), 0, 0), memory_space=pltpu.SMEM),
            pl.BlockSpec(memory_space=pl.ANY),
            pl.BlockSpec((tm, D_MODEL), lambda i: (i, 0)),
            pl.BlockSpec((1, D_MODEL), lambda i: (0, 0)),
            pl.BlockSpec((1, D_MODEL), lambda i: (0, 0)),
        ],
        out_specs=[pl.BlockSpec((tm, D_MODEL), lambda i: (i, 0)), pl.BlockSpec((tm, D_MODEL), lambda i: (i, 0))],
        out_shape=[jax.ShapeDtypeStruct((t_tokens, D_MODEL), F32), jax.ShapeDtypeStruct((t_tokens, D_MODEL), BF16)],
        scratch_shapes=[pltpu.VMEM((2, tm, D_MODEL), F32), pltpu.SemaphoreType.DMA((2,))],
        compiler_params=_cparams(1),
        name="moe_combine_ln",
    )(pos3, pos3, y_sorted, x1, ln_g, ln_b)


def _rope_tables(seq):
    inv = 1.0 / (ROPE_THETA ** (jnp.arange(0, HEAD_DIM, 2, dtype=F32) / HEAD_DIM))
    ang = jnp.arange(seq, dtype=F32)[:, None] * inv[None, :]
    return jnp.cos(ang), jnp.sin(ang)


def _halves_layout(w_heads):
    d, n = w_heads.shape
    t = w_heads.reshape(d, n // PROJ_TN, HEADS_PER_TILE, 2, HEAD_DIM // 2)
    return t.transpose(0, 1, 3, 2, 4).reshape(d, n)


def _reorder_w_in(w):
    pts = np.cumsum((0,) + IN_SIZES)
    sec = [w[:, pts[k]:pts[k + 1]] for k in range(len(IN_SIZES))]
    a_q, a_c, a_iq, a_ik, a_iw, b_q, b_k, b_v, c_q, c_k, c_v, gates = sec
    zeros = lambda n: jnp.zeros((w.shape[0], n), w.dtype)
    w_rope = _halves_layout(jnp.concatenate([a_q, a_iq, b_q, b_k, c_q, c_k, a_ik, zeros(HEAD_DIM)], axis=1))
    w_plain = jnp.concatenate([gates, b_v, a_c, a_iw, zeros(LANES - A_IDX_HEADS), c_v, zeros(LANES)], axis=1)
    assert w_rope.shape[1] == N_ROPE and w_plain.shape[1] == N_PLAIN
    return w_rope.astype(BF16), w_plain.astype(BF16)


def _kv_weights(w_uk, w_uv):
    half = HEAD_DIM // 2
    zeros = lambda n: jnp.zeros((w_uk.shape[0], n), w_uk.dtype)
    return jnp.concatenate([w_uk[:, :half], w_uv, zeros(HALF_TN - half - HEAD_DIM),
                            w_uk[:, half:], zeros(HALF_TN - half)], axis=1).astype(BF16)


def _moe_plan(cls, n_tokens):
    ntiles = n_tokens // MOE_TILE + N_CLASSES
    onehot = (cls[:, None] == jnp.arange(N_CLASSES, dtype=jnp.int32)[None, :]).astype(jnp.int32)
    csum = jnp.cumsum(onehot, axis=0)
    rank = jnp.sum(onehot * csum, axis=1) - 1
    counts = csum[-1]
    ptiles = (counts + MOE_TILE - 1) // MOE_TILE
    tile_end = jnp.cumsum(ptiles)
    tile_start = tile_end - ptiles
    pos = (tile_start[cls] * MOE_TILE + rank).astype(jnp.int32)
    nused = tile_end[-1:].astype(jnp.int32)
    tile_ids = jnp.arange(ntiles, dtype=jnp.int32)
    tile_cls = jnp.minimum(jnp.sum((tile_ids[:, None] >= tile_end[None, :]).astype(jnp.int32), axis=1),
                           N_CLASSES - 1).astype(jnp.int32)
    pair = np.array(PAIRS, dtype=np.int32)
    e_lo = jnp.asarray(np.repeat(np.arange(N_GROUPS), len(PAIRS)) * EXPERTS_PER_GROUP + np.tile(pair[:, 0], N_GROUPS), jnp.int32)
    e_hi = jnp.asarray(np.repeat(np.arange(N_GROUPS), len(PAIRS)) * EXPERTS_PER_GROUP + np.tile(pair[:, 1], N_GROUPS), jnp.int32)
    src = jnp.zeros((ntiles * MOE_TILE,), jnp.int32).at[pos].set(jnp.arange(n_tokens, dtype=jnp.int32))
    return e_lo[tile_cls], e_hi[tile_cls], nused, src.reshape(ntiles, 1, MOE_TILE), pos


def kernel(x, w_in, a_w_uk, a_w_uv, c_sinks, w_branch, w_o, ln1_g, ln1_b, router_w, router_b,
           moe_w_gate, moe_w_up, moe_w_down, ln2_g, ln2_b):
    bsz, seq, _ = x.shape
    n_tokens = bsz * seq
    topk = min(A_TOPK_MAX, seq // 4)
    nkb = seq // B_BLOCK
    tq = 256
    kc = min(DSA_KC, seq)
    tm_proj = min(1024, seq)
    tm_row = 512
    half = HEAD_DIM // 2

    cos, sin = _rope_tables(seq)
    cos_in = jnp.tile(cos, (1, HEADS_PER_TILE))
    sin_in = jnp.tile(sin, (1, HEADS_PER_TILE))
    cos_kv = jnp.concatenate([cos, jnp.ones((seq, HALF_TN - half), F32)], axis=1)
    sin_kv = jnp.concatenate([sin, jnp.zeros((seq, HALF_TN - half), F32)], axis=1)
    rwT = router_w.T
    rb = router_b.reshape(N_EXPERTS, 1)

    xf = x.reshape(n_tokens, D_MODEL)
    xb = xf.astype(BF16)
    for l in range(DEPTH):
        w_rope, w_plain = _reorder_w_in(w_in[l])
        h_rope = _proj_rope(xb, 0, D_MODEL, w_rope, cos_in, sin_in, tm=tm_proj, seq=seq)
        h_plain = _proj_plain(xb, w_plain, tm=tm_proj)
        kv = _proj_rope(h_plain, COL_AC // A_KV_RANK, A_KV_RANK, _kv_weights(a_w_uk[l], a_w_uv[l]),
                        cos_kv, sin_kv, tm=tm_proj, seq=seq)

        bvT, cvT, kvT = _values_T(h_plain, kv, n_tokens=n_tokens)

        o_a = _dsa(h_rope, h_plain, kv, kvT, bsz=bsz, seq=seq, tq=min(DSA_TQ, seq), kc=kc, topk=topk)
        kmean = _moba_kmean(h_rope, n_tokens=n_tokens).reshape(bsz, nkb, B_W)
        sel = _moba_gate(h_rope, kmean, bsz=bsz, seq=seq, tq=min(GATE_TQ, seq))
        o_b = _moba(h_rope, bvT, sel, bsz=bsz, seq=seq)
        o_c = _swa(c_sinks[l], h_rope, cvT, bsz=bsz, seq=seq, tq=tq)

        wb_all = w_branch[l].astype(BF16)
        x1, cls = _merge(o_a, o_b, o_c, h_plain, xf,
                              wb_all[:A_W], wb_all[A_W:A_W + B_W], wb_all[A_W + B_W:], w_o[l].astype(BF16),
                              ln1_g[l].reshape(1, D_MODEL), ln1_b[l].reshape(1, D_MODEL), rwT, rb, tm=tm_row)

        e1, e2, nused, src3, pos = _moe_plan(cls[0], n_tokens)
        y_sorted = _experts(e1, e2, nused, src3, x1,
                            moe_w_gate[l].astype(BF16), moe_w_up[l].astype(BF16), moe_w_down[l].astype(BF16))
        xf, xb = _final(pos.reshape(n_tokens // tm_row, 1, tm_row), y_sorted, x1,
                        ln2_g[l].reshape(1, D_MODEL), ln2_b[l].reshape(1, D_MODEL), tm=tm_row)
    return xf.reshape(bsz, seq, D_MODEL)
```

```python
import functools
import math

import jax
import jax.numpy as jnp
import numpy as np
from jax import lax
from jax.experimental import pallas as pl
from jax.experimental.pallas import tpu as pltpu

D_MODEL = 1024
DEPTH = 2
HEAD_DIM = 64
ROPE_THETA = 10000.0
LN_EPS = 1e-5
A_HEADS = 4
A_KV_RANK = 128
A_IDX_HEADS = 4
A_IDX_DIM = 64
A_TOPK_MAX = 256
B_HEADS = 4
B_BLOCK = 256
B_TOPK_BLOCKS = 3
C_HEADS = 8
C_KV_HEADS = 2
C_WINDOW = 128
N_BRANCH = 3
A_W = A_HEADS * HEAD_DIM
B_W = B_HEADS * HEAD_DIM
C_W = C_HEADS * HEAD_DIM
IN_SIZES = (A_W, A_KV_RANK, A_IDX_HEADS * A_IDX_DIM, A_IDX_DIM, A_IDX_HEADS,
            B_W, B_W, B_W, C_W, C_KV_HEADS * HEAD_DIM, C_KV_HEADS * HEAD_DIM,
            N_BRANCH * D_MODEL)
N_EXPERTS = 16
N_GROUPS = 4
EXPERTS_PER_GROUP = 4
D_EXPERT = 512
DN_ALPHA = (2 * DEPTH) ** 0.25
ATTN_SCALE = HEAD_DIM ** -0.5
LOG2E = math.log2(math.e)
Q_SCALE = ATTN_SCALE * LOG2E

LANES = 128
SUBLANES = 8
VMEM_LIMIT_BYTES = 56 * 1024 * 1024

NEG_BIG = -1e30
BF16 = jnp.bfloat16
F32 = jnp.float32

PROJ_TN = 256
HALF_TN = PROJ_TN // 2
HEADS_PER_TILE = PROJ_TN // HEAD_DIM
TILE_AQ = 0
TILE_AIQ = 1
TILE_BQ = 2
TILE_BK = 3
TILE_CQ = 4
TILE_CK_AIK = 6
N_ROPE = 7 * PROJ_TN
COL_GATES = 0
COL_BV = 3072
COL_AC = 3328
COL_AIW = 3456
COL_CV = 3584
N_PLAIN = 3840
PLAIN_TN = 1920
KV_VA0 = HEAD_DIM // 2

PAIRS = ((0, 1), (0, 2), (0, 3), (1, 2), (1, 3), (2, 3))
N_CLASSES = N_GROUPS * len(PAIRS)
MOE_TILE = 256
X1_COLS = D_MODEL + LANES

PROJ_TM = 1024
PROJ_SUB = 256
ROW_TM = 512
ATTN_TQ = 256
DSA_TQ = 512
DSA_KC = 256
GATE_TQ = 1024
assert COL_GATES == 0

VALUE_MID_PASSES = 16
PEEL_MAX = 2.0
MIN_NORMAL_KEY = 0x00800000
MAX_SELECT_PASSES = 96


def _cparams(n_axes):
    return pltpu.CompilerParams(dimension_semantics=("arbitrary",) * n_axes,
                                vmem_limit_bytes=VMEM_LIMIT_BYTES)


def _proj_rope_body(x_ref, w_ref, cos_ref, sin_ref, o_ref, *, n_sub):
    sub = x_ref.shape[0] // n_sub
    for t in range(w_ref.shape[0]):
        for mi in range(n_sub):
            rows = slice(mi * sub, (mi + 1) * sub)
            acc = jnp.dot(x_ref[rows, :], w_ref[t], preferred_element_type=F32)
            a1, a2 = acc[:, :HALF_TN], acc[:, HALF_TN:]
            c, sn = cos_ref[rows, :], sin_ref[rows, :]
            c0 = t * PROJ_TN
            o_ref[rows, c0:c0 + HALF_TN] = (a1 * c - a2 * sn).astype(o_ref.dtype)
            o_ref[rows, c0 + HALF_TN:c0 + PROJ_TN] = (a2 * c + a1 * sn).astype(o_ref.dtype)


def _proj_plain_body(x_ref, w_ref, o_ref, *, n_sub):
    sub = x_ref.shape[0] // n_sub
    tn = o_ref.shape[1]
    is_gate = (pl.program_id(1) * tn + lax.broadcasted_iota(jnp.int32, (1, tn), 1)) < COL_GATES + N_BRANCH * D_MODEL
    for mi in range(n_sub):
        rows = slice(mi * sub, (mi + 1) * sub)
        acc = jnp.dot(x_ref[rows, :], w_ref[0], preferred_element_type=F32)
        o_ref[rows, :] = jnp.where(is_gate, jax.nn.sigmoid(acc), acc).astype(o_ref.dtype)


def _column_tiles(w, tn=PROJ_TN):
    k_dim, n = w.shape
    return w.reshape(k_dim, n // tn, tn).transpose(1, 0, 2)


def _proj_rope(x, x_col_block, k_dim, w, cos_t, sin_t, *, tm, seq):
    t_tokens = x.shape[0]
    n = w.shape[1]
    pos_blocks = seq // tm
    return pl.pallas_call(
        functools.partial(_proj_rope_body, n_sub=max(tm // PROJ_SUB, 1)),
        grid=(t_tokens // tm,),
        in_specs=[
            pl.BlockSpec((tm, k_dim), lambda i: (i, x_col_block)),
            pl.BlockSpec((n // PROJ_TN, k_dim, PROJ_TN), lambda i: (0, 0, 0)),
            pl.BlockSpec((tm, HALF_TN), lambda i: (i % pos_blocks, 0)),
            pl.BlockSpec((tm, HALF_TN), lambda i: (i % pos_blocks, 0)),
        ],
        out_specs=pl.BlockSpec((tm, n), lambda i: (i, 0)),
        out_shape=jax.ShapeDtypeStruct((t_tokens, n), BF16),
        compiler_params=_cparams(1),
        name="proj_rope",
    )(x, _column_tiles(w), cos_t, sin_t)


def _proj_plain(x, w, *, tm):
    t_tokens, k_dim = x.shape
    n = w.shape[1]
    return pl.pallas_call(
        functools.partial(_proj_plain_body, n_sub=max(tm // PROJ_SUB, 1)),
        grid=(t_tokens // tm, n // PLAIN_TN),
        in_specs=[
            pl.BlockSpec((tm, k_dim), lambda i, j: (i, 0)),
            pl.BlockSpec((1, k_dim, PLAIN_TN), lambda i, j: (j, 0, 0)),
        ],
        out_specs=pl.BlockSpec((tm, PLAIN_TN), lambda i, j: (i, j)),
        out_shape=jax.ShapeDtypeStruct((t_tokens, n), BF16),
        compiler_params=_cparams(2),
        name="proj_plain",
    )(x, _column_tiles(w, PLAIN_TN))


def _values_T_body(bv_ref, cv_ref, kv_ref, bvT_ref, cvT_ref, kvT_ref):
    for src, dst in ((bv_ref, bvT_ref), (cv_ref, cvT_ref), (kv_ref, kvT_ref)):
        for g in range(dst.shape[0]):
            dst[g] = src[g * B_BLOCK:(g + 1) * B_BLOCK, :].astype(F32).T.astype(dst.dtype)


def _values_T(h_plain, kv, *, n_tokens):
    nblk = n_tokens // B_BLOCK
    gb = math.gcd(nblk, SUBLANES)
    in_tile = lambda col: pl.BlockSpec((gb * B_BLOCK, PROJ_TN), lambda i: (i, col // PROJ_TN))
    out_tile = pl.BlockSpec((gb, PROJ_TN, B_BLOCK), lambda i: (i, 0, 0))
    shape = jax.ShapeDtypeStruct((nblk, PROJ_TN, B_BLOCK), BF16)
    return pl.pallas_call(
        _values_T_body,
        grid=(nblk // gb,),
        in_specs=[in_tile(COL_BV), in_tile(COL_CV), in_tile(0)],
        out_specs=[out_tile, out_tile, out_tile],
        out_shape=[shape, shape, shape],
        compiler_params=_cparams(1),
        name="values_transpose",
    )(h_plain, h_plain, kv)


def _f32_to_key(x):
    b = lax.bitcast_convert_type(x, jnp.int32)
    return jnp.where(b < 0, b ^ jnp.int32(0x7FFFFFFF), b)


def _key_to_f32(k):
    b = jnp.where(k < 0, k ^ jnp.int32(0x7FFFFFFF), k)
    return lax.bitcast_convert_type(b, F32)


def _fold8(x, op):
    return op(x.reshape(x.shape[0] // SUBLANES, SUBLANES, x.shape[1]), axis=0)


def _tile_T(x_ref):
    return x_ref[...].astype(F32).T


def _head_rows(xT, h):
    half = HEAD_DIM // 2
    return xT[h * half:(h + 1) * half], xT[HALF_TN + h * half:HALF_TN + (h + 1) * half]


def _place_head(first, second, slot):
    half = HEAD_DIM // 2
    w = first.shape[1]
    before, after = slot * half, HALF_TN - (slot + 1) * half
    pieces = []
    for part in (first, second):
        pieces += [jnp.zeros((before, w), part.dtype)] * (before > 0) + [part] + [jnp.zeros((after, w), part.dtype)] * (after > 0)
    return jnp.concatenate(pieces, axis=0)


def _store_rows(o_ref, oT_heads):
    o_ref[...] = jnp.concatenate(oT_heads, axis=0).T.astype(o_ref.dtype)


IK_SLOT = 2


def _dsa_body(q_ref, iq_ref, iw_ref, ik_ref, kv_ref, vT_ref, o_ref, sc_ref, acc_ref, sbuf_ref, *, tq, kc, topk):
    i = pl.program_id(1)
    q0 = i * tq
    nch = (q0 + tq + kc - 1) // kc
    qpos = q0 + lax.broadcasted_iota(jnp.int32, (1, tq), 1)
    iqT = _tile_T(iq_ref)
    iq_ops = [_place_head(*[p.astype(BF16) for p in _head_rows(iqT, h)], IK_SLOT) for h in range(A_IDX_HEADS)]
    w_idx = iw_ref[...].astype(F32).T[:A_IDX_HEADS]

    def score_chunk(c, carry, diagonal):
        mn, mx = carry
        k0 = pl.multiple_of(c * kc, kc)
        ikc = ik_ref[pl.ds(k0, kc), :]
        acc = jnp.zeros((kc, tq), F32)
        for h in range(A_IDX_HEADS):
            d = jnp.dot(ikc, iq_ops[h], preferred_element_type=F32)
            acc = acc + w_idx[h:h + 1, :] * jnp.maximum(d, 0.0)
        if diagonal:
            causal = (k0 + lax.broadcasted_iota(jnp.int32, (kc, 1), 0)) <= qpos
            lo_part, hi_part = jnp.where(causal, acc, jnp.inf), jnp.where(causal, acc, -jnp.inf)
        else:
            lo_part = hi_part = acc
        sc_ref[pl.ds(k0, kc), :] = hi_part
        return jnp.minimum(mn, _fold8(lo_part, jnp.min)), jnp.maximum(mx, _fold8(hi_part, jnp.max))

    n_full = (q0 + 1) // kc
    carry = lax.fori_loop(0, n_full, functools.partial(score_chunk, diagonal=False),
                          (jnp.full((SUBLANES, tq), jnp.inf, F32), jnp.full((SUBLANES, tq), -jnp.inf, F32)))
    mn8, mx8 = lax.fori_loop(n_full, nch, functools.partial(score_chunk, diagonal=True), carry)
    row_min = jnp.min(mn8, axis=0, keepdims=True)
    row_max = jnp.max(mx8, axis=0, keepdims=True)

    pc = 2 * kc
    npair = (nch + 1) // 2
    n_chunks_total = vT_ref.shape[0]

    @pl.when(nch % 2 == 1)
    def _():
        sc_ref[pl.ds(pl.multiple_of(nch * kc, kc), kc), :] = jnp.full((kc, tq), -jnp.inf, F32)

    def count_ge(thr):
        def body(j, cnt):
            s = sc_ref[pl.ds(pl.multiple_of(j * pc, pc), pc), :]
            return cnt + _fold8(jnp.where(s >= thr, 1.0, 0.0), jnp.sum)
        cnt8 = lax.fori_loop(0, npair, body, jnp.zeros((SUBLANES, tq), F32))
        return jnp.sum(cnt8, axis=0, keepdims=True)

    def min_ge(thr):
        def body(j, mn):
            s = sc_ref[pl.ds(pl.multiple_of(j * pc, pc), pc), :]
            return jnp.minimum(mn, _fold8(jnp.where(s >= thr, s, jnp.inf), jnp.min))
        mn8 = lax.fori_loop(0, npair, body, jnp.full((SUBLANES, tq), jnp.inf, F32))
        return jnp.min(mn8, axis=0, keepdims=True)

    def count_zero():
        def body(j, carry):
            c0, cp = carry
            s = sc_ref[pl.ds(pl.multiple_of(j * pc, pc), pc), :]
            return (c0 + _fold8(jnp.where(s >= 0.0, 1.0, 0.0), jnp.sum),
                    cp + _fold8(jnp.where(s > 0.0, 1.0, 0.0), jnp.sum))
        z8 = jnp.zeros((SUBLANES, tq), F32)
        c0, cp = lax.fori_loop(0, npair, body, (z8, z8))
        return jnp.sum(c0, axis=0, keepdims=True), jnp.sum(cp, axis=0, keepdims=True)

    kf = float(topk)
    n_valid = (qpos + 1).astype(F32)
    few = n_valid <= kf
    cnt_nonneg, cnt_pos = count_zero()
    is_zero = jnp.logical_and(cnt_pos < kf, cnt_nonneg >= kf)
    is_pos = cnt_pos >= kf
    lo0 = jnp.where(is_zero, 0, jnp.where(is_pos, MIN_NORMAL_KEY, _f32_to_key(row_min)))
    hi0 = jnp.where(is_zero, 1, jnp.where(is_pos, _f32_to_key(row_max) + 1, -1))
    cnt0 = jnp.where(is_zero, cnt_nonneg, jnp.where(is_pos, cnt_pos, n_valid))
    lo0 = jnp.where(few, _f32_to_key(row_min), lo0)
    hi0 = jnp.where(few, lo0 + 1, hi0)

    def unfinished(lo, hi, cnt_lo):
        return jnp.logical_and(lo + 1 != hi, cnt_lo != kf)

    def bis_cond(st):
        lo, hi, cnt_lo, it = st
        return jnp.logical_and(jnp.max(jnp.where(unfinished(lo, hi, cnt_lo), 1.0, 0.0)) > 0.0, it < MAX_SELECT_PASSES)

    def bisect(st):
        lo, hi, cnt_lo, it = st
        nd = unfinished(lo, hi, cnt_lo)
        mid_val = _f32_to_key(0.5 * _key_to_f32(lo) + 0.5 * _key_to_f32(hi))
        mid_key = lo + lax.shift_right_logical(hi - lo, 1)
        mid = jnp.where(it < VALUE_MID_PASSES, jnp.clip(mid_val, lo + 1, hi - 1), mid_key)
        cnt = count_ge(_key_to_f32(mid))
        ge = cnt >= kf
        up = jnp.logical_and(nd, ge)
        down = jnp.logical_and(nd, jnp.logical_not(ge))
        return jnp.where(up, mid, lo), jnp.where(down, mid, hi), jnp.where(up, cnt, cnt_lo), it + 1

    def peel(st):
        lo, hi, cnt_lo, it = st
        nd = unfinished(lo, hi, cnt_lo)
        k1 = jnp.maximum(_f32_to_key(min_ge(_key_to_f32(lo))), lo)
        cnt = count_ge(_key_to_f32(k1 + 1))
        ge = cnt >= kf
        up = jnp.logical_and(nd, ge)
        down = jnp.logical_and(nd, jnp.logical_not(ge))
        new_lo = jnp.where(nd, jnp.where(ge, k1 + 1, k1), lo)
        return new_lo, jnp.where(down, k1 + 1, hi), jnp.where(up, cnt, cnt_lo), it + 2

    def bis_body(st):
        lo, hi, cnt_lo, it = st
        excess = jnp.max(jnp.where(unfinished(lo, hi, cnt_lo), cnt_lo - kf, 0.0))
        return lax.cond(excess > PEEL_MAX, bisect, peel, st)

    lo, _, cnt_lo, _ = lax.while_loop(bis_cond, bis_body, (lo0, hi0, cnt0, jnp.int32(0)))
    thr = _key_to_f32(lo)

    tie = jnp.logical_and(cnt_lo > kf, jnp.logical_not(few))

    @pl.when(jnp.max(jnp.where(tie, 1.0, 0.0)) > 0.0)
    def _():
        def count_gt():
            def gt_body(c, cnt):
                k0 = pl.multiple_of(c * kc, kc)
                s = sc_ref[pl.ds(k0, kc), :]
                return cnt + _fold8(jnp.where(s > thr, 1.0, 0.0), jnp.sum)
            gt8 = lax.fori_loop(0, nch, gt_body, jnp.zeros((SUBLANES, tq), F32))
            return jnp.sum(gt8, axis=0, keepdims=True)

        zero_thr = thr == 0.0
        other = jnp.max(jnp.where(jnp.logical_and(tie, jnp.logical_not(zero_thr)), 1.0, 0.0)) > 0.0
        gt = lax.cond(other, count_gt, lambda: cnt_pos)
        need = kf - jnp.where(zero_thr, cnt_pos, gt)
        r = lax.broadcasted_iota(jnp.int32, (kc, kc), 0)
        cidx = lax.broadcasted_iota(jnp.int32, (kc, kc), 1)
        tri = jnp.where(cidx <= r, 1.0, 0.0).astype(BF16)

        thr_tie = jnp.where(tie, thr, jnp.nan)
        keep = jnp.where(tie, need, jnp.inf)

        def tie_body(c, seen):
            k0 = pl.multiple_of(c * kc, kc)
            s = sc_ref[pl.ds(k0, kc), :]
            eqf = jnp.where(s == thr_tie, 1.0, 0.0)
            rank = jnp.dot(tri, eqf.astype(BF16), preferred_element_type=F32) + seen
            sc_ref[pl.ds(k0, kc), :] = jnp.where(eqf * rank > keep, -jnp.inf, s)
            return seen + jnp.sum(eqf, axis=0, keepdims=True)

        lax.fori_loop(0, nch, tie_body, jnp.zeros((1, tq), F32))

    qT_all = (_tile_T(q_ref) * Q_SCALE).astype(BF16)
    qT = jnp.concatenate([_place_head(*_head_rows(qT_all, h), 0) for h in range(A_HEADS)], axis=1)
    acc_ref[...] = jnp.zeros_like(acc_ref)
    ones_rows = jnp.ones((SUBLANES, kc), BF16)

    def issue_scores(c, slot):
        k0 = pl.multiple_of(jnp.minimum(c, n_chunks_total - 1) * kc, kc)
        sbuf_ref[slot] = jnp.dot(kv_ref[pl.ds(k0, kc), :], qT, preferred_element_type=F32)

    def softmax_pv(c, slot, m):
        k0 = pl.multiple_of(c * kc, kc)
        v1 = jnp.concatenate([vT_ref[c, KV_VA0:KV_VA0 + HEAD_DIM, :], ones_rows], axis=0)
        sel = sc_ref[pl.ds(k0, kc), :] >= thr
        s = jnp.concatenate([jnp.where(sel, sbuf_ref[slot, :, h * tq:(h + 1) * tq], NEG_BIG)
                             for h in range(A_HEADS)], axis=1)
        m_new = jnp.maximum(m, jnp.max(s, axis=0, keepdims=True))
        alpha = jnp.exp2(m - m_new)
        p = jnp.exp2((s - m_new).astype(BF16))
        acc_ref[...] = acc_ref[...] * alpha + jnp.dot(v1, p, preferred_element_type=F32)
        return m_new

    def chunk_pair(j, m):
        c0 = 2 * j
        issue_scores(c0 + 1, 1)
        m = softmax_pv(c0, 0, m)
        issue_scores(c0 + 2, 0)
        return softmax_pv(c0 + 1, 1, m)

    issue_scores(0, 0)
    lax.fori_loop(0, npair, chunk_pair, jnp.full((1, A_HEADS * tq), NEG_BIG, F32))
    a = acc_ref[...]
    o = a[:HEAD_DIM] / a[HEAD_DIM:HEAD_DIM + 1]
    _store_rows(o_ref, [o[:, h * tq:(h + 1) * tq] for h in range(A_HEADS)])


def _dsa(h_rope, h_plain, kv, kvT, *, bsz, seq, tq, kc, topk):
    nq = seq // tq
    assert kc == B_BLOCK
    return pl.pallas_call(
        functools.partial(_dsa_body, tq=tq, kc=kc, topk=topk),
        grid=(bsz, nq),
        in_specs=[
            pl.BlockSpec((tq, PROJ_TN), lambda b, i: (b * nq + i, TILE_AQ)),
            pl.BlockSpec((tq, PROJ_TN), lambda b, i: (b * nq + i, TILE_AIQ)),
            pl.BlockSpec((tq, LANES), lambda b, i: (b * nq + i, COL_AIW // LANES)),
            pl.BlockSpec((seq, PROJ_TN), lambda b, i: (b, TILE_CK_AIK)),
            pl.BlockSpec((seq, PROJ_TN), lambda b, i: (b, 0)),
            pl.BlockSpec((seq // kc, PROJ_TN, kc), lambda b, i: (b, 0, 0)),
        ],
        out_specs=pl.BlockSpec((tq, A_W), lambda b, i: (b * nq + i, 0)),
        out_shape=jax.ShapeDtypeStruct((bsz * seq, A_W), BF16),
        scratch_shapes=[pltpu.VMEM((seq + kc, tq), F32),
                        pltpu.VMEM((HEAD_DIM + SUBLANES, A_HEADS * tq), F32),
                        pltpu.VMEM((2, kc, A_HEADS * tq), F32)],
        compiler_params=_cparams(2),
        name="dsa_attention",
    )(h_rope, h_rope, h_plain, h_rope, kv, kvT)


def _kmean_body(k_ref, o_ref):
    gb = o_ref.shape[0]
    o_ref[...] = jnp.mean(k_ref[...].astype(F32).reshape(gb, B_BLOCK, B_W), axis=1)


def _moba_kmean(h_rope, *, n_tokens):
    nblk = n_tokens // B_BLOCK
    gb = math.gcd(nblk, SUBLANES)
    return pl.pallas_call(
        _kmean_body,
        grid=(nblk // gb,),
        in_specs=[pl.BlockSpec((gb * B_BLOCK, B_W), lambda i: (i, TILE_BK))],
        out_specs=pl.BlockSpec((gb, B_W), lambda i: (i, 0)),
        out_shape=jax.ShapeDtypeStruct((nblk, B_W), F32),
        compiler_params=_cparams(1),
        name="moba_kmean",
    )(h_rope)


def _moba_gate_body(q_ref, km_ref, sel_ref, *, tq, nkb, nsel):
    i = pl.program_id(1)
    qpos = i * tq + lax.broadcasted_iota(jnp.int32, (1, tq), 1)
    cur = qpos // B_BLOCK
    row = lax.broadcasted_iota(jnp.int32, (nkb, tq), 0)
    past = row < cur
    qT = _tile_T(q_ref).astype(BF16)
    km = km_ref[0]
    col_head = (lax.broadcasted_iota(jnp.int32, km.shape, 1) % HALF_TN) // (HEAD_DIM // 2)
    for h in range(B_HEADS):
        km_h = jnp.where(col_head == h, km, 0.0).astype(BF16)
        g = jnp.dot(km_h, qT, preferred_element_type=F32)
        g = jnp.where(past, g, -jnp.inf)
        sel = jnp.zeros((nkb, tq), F32)
        for _ in range(nsel):
            mx = jnp.max(g, axis=0, keepdims=True)
            idx = jnp.min(jnp.where(g == mx, row, nkb), axis=0, keepdims=True)
            pick = row == idx
            sel = jnp.where(pick, 1.0, sel)
            g = jnp.where(pick, -jnp.inf, g)
        sel_ref[0, h] = jnp.where(past, sel, 0.0)


def _moba_gate(h_rope, kmean, *, bsz, seq, tq):
    nkb = seq // B_BLOCK
    nsel = min(B_TOPK_BLOCKS, nkb)
    nq = seq // tq
    return pl.pallas_call(
        functools.partial(_moba_gate_body, tq=tq, nkb=nkb, nsel=nsel),
        grid=(bsz, nq),
        in_specs=[
            pl.BlockSpec((tq, PROJ_TN), lambda b, i: (b * nq + i, TILE_BQ)),
            pl.BlockSpec((1, nkb, B_W), lambda b, i: (b, 0, 0)),
        ],
        out_specs=pl.BlockSpec((1, B_HEADS, nkb, tq), lambda b, i: (b, 0, 0, i)),
        out_shape=jax.ShapeDtypeStruct((bsz, B_HEADS, nkb, seq), F32),
        compiler_params=_cparams(2),
        name="moba_gate",
    )(h_rope, kmean)


def _moba_body(q_ref, k_ref, vT_ref, sel_ref, o_ref, acc_ref, sbuf_ref, *, tq):
    i = pl.program_id(1)
    qT_all = (_tile_T(q_ref) * Q_SCALE).astype(BF16)
    qT = [_place_head(*_head_rows(qT_all, h), h) for h in range(B_HEADS)]
    ones_rows = jnp.ones((SUBLANES, B_BLOCK), BF16)
    n_past = i

    def values(h, j):
        return jnp.concatenate([vT_ref[j, h * HEAD_DIM:(h + 1) * HEAD_DIM, :], ones_rows], axis=0)

    def issue_scores(j, slot):
        k0 = pl.multiple_of(jnp.minimum(j, i) * B_BLOCK, B_BLOCK)
        for h in range(B_HEADS):
            sbuf_ref[slot, h] = jnp.dot(k_ref[pl.ds(k0, B_BLOCK), :], qT[h], preferred_element_type=F32)

    def softmax_pv(j, slot, ms):
        ps, alphas, new_ms = [], [], []
        for h in range(B_HEADS):
            s = jnp.where(sel_ref[0, h, pl.ds(j, 1), :] > 0.0, sbuf_ref[slot, h], NEG_BIG)
            m_new = jnp.maximum(ms[h], jnp.max(s, axis=0, keepdims=True))
            alphas.append(jnp.exp2(ms[h] - m_new))
            ps.append(jnp.exp2((s - m_new).astype(BF16)))
            new_ms.append(m_new)
        for h in range(B_HEADS):
            acc_ref[h] = acc_ref[h] * alphas[h] + jnp.dot(values(h, j), ps[h], preferred_element_type=F32)
        return tuple(new_ms)

    own0 = pl.multiple_of(i * B_BLOCK, B_BLOCK)
    s_own = [jnp.dot(k_ref[pl.ds(own0, B_BLOCK), :], qT[h], preferred_element_type=F32) for h in range(B_HEADS)]
    issue_scores(0, 0)
    causal = lax.broadcasted_iota(jnp.int32, (B_BLOCK, tq), 0) <= lax.broadcasted_iota(jnp.int32, (B_BLOCK, tq), 1)
    ms = []
    for h in range(B_HEADS):
        s = jnp.where(causal, s_own[h], NEG_BIG)
        m0 = jnp.max(s, axis=0, keepdims=True)
        acc_ref[h] = jnp.dot(values(h, i), jnp.exp2((s - m0).astype(BF16)), preferred_element_type=F32)
        ms.append(m0)

    def block_pair(jj, ms):
        j0 = 2 * jj
        issue_scores(j0 + 1, 1)
        ms = softmax_pv(j0, 0, ms)
        issue_scores(j0 + 2, 0)
        return softmax_pv(j0 + 1, 1, ms)

    lax.fori_loop(0, (n_past + 1) // 2, block_pair, tuple(ms))
    outs = []
    for h in range(B_HEADS):
        a = acc_ref[h]
        outs.append(a[:HEAD_DIM] / a[HEAD_DIM:HEAD_DIM + 1])
    _store_rows(o_ref, outs)


def _moba(h_rope, bvT, sel, *, bsz, seq):
    tq = B_BLOCK
    nkb = seq // B_BLOCK
    nq = seq // tq
    return pl.pallas_call(
        functools.partial(_moba_body, tq=tq),
        grid=(bsz, nq),
        in_specs=[
            pl.BlockSpec((tq, PROJ_TN), lambda b, i: (b * nq + i, TILE_BQ)),
            pl.BlockSpec((seq, PROJ_TN), lambda b, i: (b, TILE_BK)),
            pl.BlockSpec((nkb, B_W, B_BLOCK), lambda b, i: (b, 0, 0)),
            pl.BlockSpec((1, B_HEADS, nkb, tq), lambda b, i: (b, 0, 0, i)),
        ],
        out_specs=pl.BlockSpec((tq, B_W), lambda b, i: (b * nq + i, 0)),
        out_shape=jax.ShapeDtypeStruct((bsz * seq, B_W), BF16),
        scratch_shapes=[pltpu.VMEM((B_HEADS, HEAD_DIM + SUBLANES, tq), F32),
                        pltpu.VMEM((2, B_HEADS, B_BLOCK, tq), F32)],
        compiler_params=_cparams(2),
        name="moba_attention",
    )(h_rope, h_rope, bvT, sel)


def _swa_body(sink_ref, q0_ref, q1_ref, k0_ref, k1_ref, k2_ref, vprev_ref, vcur_ref, o_ref, *, tq):
    i = pl.program_id(1)
    q0 = i * tq
    qpos = q0 + lax.broadcasted_iota(jnp.int32, (1, tq), 1)
    nk = tq + C_WINDOW
    kpos = q0 - C_WINDOW + lax.broadcasted_iota(jnp.int32, (nk, 1), 0)
    diff = qpos - kpos
    ok = jnp.where(diff >= 0, jnp.where(diff < C_WINDOW, jnp.where(kpos >= 0, 1.0, 0.0), 0.0), 0.0) > 0.0
    ones_rows = jnp.ones((SUBLANES, nk), BF16)
    group = C_HEADS // C_KV_HEADS
    assert group == HEADS_PER_TILE
    kk = jnp.concatenate([k0_ref[...], k1_ref[...], k2_ref[...]], axis=0)
    outs = []
    scores = []
    for kv, q_ref in enumerate((q0_ref, q1_ref)):
        qT_all = (_tile_T(q_ref) * Q_SCALE).astype(BF16)
        qT = jnp.concatenate([_place_head(*_head_rows(qT_all, g), kv) for g in range(group)], axis=1)
        scores.append(jnp.dot(kk, qT, preferred_element_type=F32))
    for kv in range(C_KV_HEADS):
        vrows = slice(kv * HEAD_DIM, (kv + 1) * HEAD_DIM)
        vv = jnp.concatenate([vprev_ref[0, vrows, tq - C_WINDOW:], vcur_ref[0, vrows, :]], axis=1)
        v1 = jnp.concatenate([vv, ones_rows], axis=0)
        heads = [kv * group + g for g in range(group)]
        sink = jnp.concatenate([jnp.full((1, tq), sink_ref[hd] * LOG2E, F32) for hd in heads], axis=1)
        s = jnp.concatenate([jnp.where(ok, scores[kv][:, g * tq:(g + 1) * tq], NEG_BIG) for g in range(group)], axis=1)
        m = jnp.maximum(jnp.max(s, axis=0, keepdims=True), sink)
        p = jnp.exp2((s - m).astype(BF16))
        a = jnp.dot(v1, p, preferred_element_type=F32)
        o = a[:HEAD_DIM] / (a[HEAD_DIM:HEAD_DIM + 1] + jnp.exp2(sink - m))
        outs += [o[:, g * tq:(g + 1) * tq] for g in range(group)]
    _store_rows(o_ref, outs)


def _swa(sinks, h_rope, cvT, *, bsz, seq, tq):
    r = tq // C_WINDOW
    assert r == 2 and tq == B_BLOCK
    nq = seq // tq
    nwb = seq // C_WINDOW

    def kspec(off):
        return pl.BlockSpec((C_WINDOW, PROJ_TN),
                            lambda b, i, s: (b * nwb + jnp.maximum(i * r + off, 0), TILE_CK_AIK))

    def vspec(off):
        return pl.BlockSpec((1, PROJ_TN, B_BLOCK), lambda b, i, s: (b * nq + jnp.maximum(i + off, 0), 0, 0))

    grid_spec = pltpu.PrefetchScalarGridSpec(
        num_scalar_prefetch=1,
        grid=(bsz, nq),
        in_specs=[pl.BlockSpec((tq, PROJ_TN), lambda b, i, s: (b * nq + i, TILE_CQ)),
                  pl.BlockSpec((tq, PROJ_TN), lambda b, i, s: (b * nq + i, TILE_CQ + 1)),
                  kspec(-1), kspec(0), kspec(1), vspec(-1), vspec(0)],
        out_specs=pl.BlockSpec((tq, C_W), lambda b, i, s: (b * nq + i, 0)),
    )
    return pl.pallas_call(
        functools.partial(_swa_body, tq=tq),
        grid_spec=grid_spec,
        out_shape=jax.ShapeDtypeStruct((bsz * seq, C_W), BF16),
        compiler_params=_cparams(2),
        name="swa_attention",
    )(sinks, h_rope, h_rope, h_rope, h_rope, h_rope, cvT, cvT)


def _layer_norm(z, g, b):
    mu = jnp.mean(z, axis=-1, keepdims=True)
    zc = z - mu
    var = jnp.mean(zc * zc, axis=-1, keepdims=True)
    return zc * lax.rsqrt(var + LN_EPS) * g + b


def _split_bf16(a):
    hi = a.astype(BF16)
    lo = (a - hi.astype(F32)).astype(BF16)
    return hi, lo


def _router(x1, rw_ref, rb_ref, cls_ref):
    nt = (((1,), (1,)), ((), ()))
    xh, xl = _split_bf16(x1)
    wh, wl = _split_bf16(rw_ref[...])
    logits = (lax.dot_general(wh, xh, nt, preferred_element_type=F32)
              + lax.dot_general(wh, xl, nt, preferred_element_type=F32)
              + lax.dot_general(wl, xh, nt, preferred_element_type=F32))
    aff = jax.nn.sigmoid(logits)
    score = aff + rb_ref[...]
    tm = x1.shape[0]
    sc = [score[e:e + 1, :] for e in range(N_EXPERTS)]
    af = [aff[e:e + 1, :] for e in range(N_EXPERTS)]
    in_top = []
    grp_score = []
    for gq in range(N_GROUPS):
        gs = jnp.zeros((1, tm), F32)
        for a in range(EXPERTS_PER_GROUP):
            ea = gq * EXPERTS_PER_GROUP + a
            rank = jnp.zeros((1, tm), F32)
            for b in range(EXPERTS_PER_GROUP):
                if b == a:
                    continue
                eb = gq * EXPERTS_PER_GROUP + b
                beats = (sc[eb] >= sc[ea]) if b < a else (sc[eb] > sc[ea])
                rank = rank + jnp.where(beats, 1.0, 0.0)
            top = rank < 2.0
            in_top.append(top)
            gs = gs + jnp.where(top, sc[ea], 0.0)
        grp_score.append(gs)
    best = grp_score[0]
    gstar = jnp.zeros((1, tm), jnp.int32)
    for gq in range(1, N_GROUPS):
        better = grp_score[gq] > best
        best = jnp.where(better, grp_score[gq], best)
        gstar = jnp.where(better, gq, gstar)
    cls = jnp.zeros((1, tm), jnp.int32)
    w_lo = jnp.zeros((1, tm), F32)
    w_hi = jnp.zeros((1, tm), F32)
    for gq in range(N_GROUPS):
        is_g = gstar == gq
        for pi, (a, b) in enumerate(PAIRS):
            ea, eb = gq * EXPERTS_PER_GROUP + a, gq * EXPERTS_PER_GROUP + b
            hit = jnp.where(is_g, jnp.where(in_top[ea], jnp.where(in_top[eb], 1.0, 0.0), 0.0), 0.0) > 0.0
            cls = jnp.where(hit, gq * len(PAIRS) + pi, cls)
            tot = af[ea] + af[eb]
            w_lo = jnp.where(hit, af[ea] / tot, w_lo)
            w_hi = jnp.where(hit, af[eb] / tot, w_hi)
    cls_ref[...] = cls
    return w_lo, w_hi


def _merge_body(oa_ref, ob_ref, oc_ref, ga_ref, gb_ref, gc_ref, x_ref, wa_ref, wb_ref, wc_ref, wo_ref,
                g_ref, b_ref, rw_ref, rb_ref, x1_ref, cls_ref):
    def branch(o_ref, w_ref, gate_ref):
        y = jnp.dot(o_ref[...], w_ref[...], preferred_element_type=F32)
        return gate_ref[...].astype(F32) * y

    merged = branch(oa_ref, wa_ref, ga_ref) + branch(ob_ref, wb_ref, gb_ref) + branch(oc_ref, wc_ref, gc_ref)
    y = jnp.dot(merged.astype(BF16), wo_ref[...], preferred_element_type=F32)
    x1 = _layer_norm(DN_ALPHA * x_ref[...] + y, g_ref[...], b_ref[...])
    w_lo, w_hi = _router(x1, rw_ref, rb_ref, cls_ref)
    x1_ref[:, :D_MODEL] = x1
    x1_ref[:, D_MODEL:] = jnp.concatenate([w_lo, w_hi, jnp.zeros((LANES - 2, x1.shape[0]), F32)], axis=0).T


def _merge(oa, ob, oc, h_plain, x, wa, wb, wc, wo, ln_g, ln_b, rwT, rb, *, tm):
    t_tokens = x.shape[0]
    gate_blk = COL_GATES // D_MODEL
    full = lambda shape: pl.BlockSpec(shape, lambda i: (0,) * len(shape))
    return pl.pallas_call(
        _merge_body,
        grid=(t_tokens // tm,),
        in_specs=[
            pl.BlockSpec((tm, A_W), lambda i: (i, 0)),
            pl.BlockSpec((tm, B_W), lambda i: (i, 0)),
            pl.BlockSpec((tm, C_W), lambda i: (i, 0)),
            pl.BlockSpec((tm, D_MODEL), lambda i: (i, gate_blk)),
            pl.BlockSpec((tm, D_MODEL), lambda i: (i, gate_blk + 1)),
            pl.BlockSpec((tm, D_MODEL), lambda i: (i, gate_blk + 2)),
            pl.BlockSpec((tm, D_MODEL), lambda i: (i, 0)),
            full((A_W, D_MODEL)), full((B_W, D_MODEL)), full((C_W, D_MODEL)), full((D_MODEL, D_MODEL)),
            full((1, D_MODEL)), full((1, D_MODEL)), full((N_EXPERTS, D_MODEL)), full((N_EXPERTS, 1)),
        ],
        out_specs=[
            pl.BlockSpec((tm, X1_COLS), lambda i: (i, 0)),
            pl.BlockSpec((1, tm), lambda i: (0, i)),
        ],
        out_shape=[
            jax.ShapeDtypeStruct((t_tokens, X1_COLS), F32),
            jax.ShapeDtypeStruct((1, t_tokens), jnp.int32),
        ],
        compiler_params=_cparams(1),
        name="merge_ln_router",
    )(oa, ob, oc, h_plain, h_plain, h_plain, x, wa, wb, wc, wo, ln_g, ln_b, rwT, rb)


def _gather_start(idx_ref, src_hbm, dst_ref, sem, n_rows):
    def start(r, _):
        pltpu.make_async_copy(src_hbm.at[pl.ds(idx_ref[0, 0, r], 1)], dst_ref.at[pl.ds(r, 1)], sem).start()
        return 0

    lax.fori_loop(0, n_rows, start, 0, unroll=8)


def _gather_wait(src_hbm, dst_ref, sem, n_rows):
    pltpu.make_async_copy(src_hbm.at[pl.ds(0, n_rows)], dst_ref, sem).wait()


def _experts_body(e1_ref, e2_ref, nused_ref, src_ref, src_next_ref, x_hbm,
                  wg1_ref, wu1_ref, wd1_ref, wg2_ref, wu2_ref, wd2_ref, o_ref, xbuf, sem):
    s = pl.program_id(0)
    nused = nused_ref[0]
    slot = lax.rem(s, 2)

    @pl.when(s == 0)
    def _():
        _gather_start(src_ref, x_hbm, xbuf.at[0], sem.at[0], MOE_TILE)

    @pl.when(s + 1 < nused)
    def _():
        _gather_start(src_next_ref, x_hbm, xbuf.at[1 - slot], sem.at[1 - slot], MOE_TILE)

    @pl.when(s < nused)
    def _():
        _gather_wait(x_hbm, xbuf.at[slot], sem.at[slot], MOE_TILE)
        xb = xbuf[slot, :, :D_MODEL].astype(BF16)
        wt = xbuf[slot, :, D_MODEL:D_MODEL + 2]

        def expert(wg_ref, wu_ref, wd_ref):
            g = jnp.dot(xb, wg_ref[0], preferred_element_type=F32)
            u = jnp.dot(xb, wu_ref[0], preferred_element_type=F32)
            he = (g * jax.nn.sigmoid(g) * u).astype(BF16)
            return jnp.dot(he, wd_ref[0], preferred_element_type=F32)

        o_ref[...] = wt[:, 0:1] * expert(wg1_ref, wu1_ref, wd1_ref) + wt[:, 1:2] * expert(wg2_ref, wu2_ref, wd2_ref)

    @pl.when(s >= nused)
    def _():
        o_ref[...] = jnp.zeros_like(o_ref)


def _experts(tile_e1, tile_e2, nused, src3, x1, wg, wu, wd):
    ntiles = src3.shape[0]
    wspec_up = lambda which: pl.BlockSpec((1, D_MODEL, D_EXPERT), lambda s, e1, e2, nu: ((e1, e2)[which][s], 0, 0))
    wspec_dn = lambda which: pl.BlockSpec((1, D_EXPERT, D_MODEL), lambda s, e1, e2, nu: ((e1, e2)[which][s], 0, 0))
    grid_spec = pltpu.PrefetchScalarGridSpec(
        num_scalar_prefetch=3,
        grid=(ntiles,),
        in_specs=[
            pl.BlockSpec((1, 1, MOE_TILE), lambda s, e1, e2, nu: (s, 0, 0), memory_space=pltpu.SMEM),
            pl.BlockSpec((1, 1, MOE_TILE), lambda s, e1, e2, nu: (jnp.minimum(s + 1, ntiles - 1), 0, 0),
                         memory_space=pltpu.SMEM),
            pl.BlockSpec(memory_space=pl.ANY),
            wspec_up(0), wspec_up(0), wspec_dn(0), wspec_up(1), wspec_up(1), wspec_dn(1),
        ],
        out_specs=pl.BlockSpec((MOE_TILE, D_MODEL), lambda s, e1, e2, nu: (s, 0)),
        scratch_shapes=[pltpu.VMEM((2, MOE_TILE, X1_COLS), F32), pltpu.SemaphoreType.DMA((2,))],
    )
    return pl.pallas_call(
        _experts_body,
        grid_spec=grid_spec,
        out_shape=jax.ShapeDtypeStruct((ntiles * MOE_TILE, D_MODEL), F32),
        compiler_params=_cparams(1),
        name="moe_experts",
    )(tile_e1, tile_e2, nused, src3, src3, x1, wg, wu, wd, wg, wu, wd)


def _final_body(pos_ref, pos_next_ref, y_hbm, x_ref, g_ref, b_ref, o_ref, ob_ref, ybuf, sem, *, tm):
    s = pl.program_id(0)
    slot = lax.rem(s, 2)

    @pl.when(s == 0)
    def _():
        _gather_start(pos_ref, y_hbm, ybuf.at[0], sem.at[0], tm)

    @pl.when(s + 1 < pl.num_programs(0))
    def _():
        _gather_start(pos_next_ref, y_hbm, ybuf.at[1 - slot], sem.at[1 - slot], tm)

    _gather_wait(y_hbm, ybuf.at[slot], sem.at[slot], tm)
    x2 = _layer_norm(DN_ALPHA * x_ref[...] + ybuf[slot], g_ref[...], b_ref[...])
    o_ref[...] = x2
    ob_ref[...] = x2.astype(BF16)


def _final(pos3, y_sorted, x1, ln_g, ln_b, *, tm):
    t_tokens = x1.shape[0]
    nsteps = t_tokens // tm
    return pl.pallas_call(
        functools.partial(_final_body, tm=tm),
        grid=(nsteps,),
        in_specs=[
            pl.BlockSpec((1, 1, tm), lambda i: (i, 0, 0), memory_space=pltpu.SMEM),
            pl.BlockSpec((1, 1, tm), lambda i: (jnp.minimum(i + 1, nsteps - 1), 0, 0), memory_space=pltpu.SMEM),
            pl.BlockSpec(memory_space=pl.ANY),
            pl.BlockSpec((tm, D_MODEL), lambda i: (i, 0)),
            pl.BlockSpec((1, D_MODEL), lambda i: (0, 0)),
            pl.BlockSpec((1, D_MODEL), lambda i: (0, 0)),
        ],
        out_specs=[pl.BlockSpec((tm, D_MODEL), lambda i: (i, 0)), pl.BlockSpec((tm, D_MODEL), lambda i: (i, 0))],
        out_shape=[jax.ShapeDtypeStruct((t_tokens, D_MODEL), F32), jax.ShapeDtypeStruct((t_tokens, D_MODEL), BF16)],
        scratch_shapes=[pltpu.VMEM((2, tm, D_MODEL), F32), pltpu.SemaphoreType.DMA((2,))],
        compiler_params=_cparams(1),
        name="moe_combine_ln",
    )(pos3, pos3, y_sorted, x1, ln_g, ln_b)


def _rope_tables(seq):
    inv = 1.0 / (ROPE_THETA ** (jnp.arange(0, HEAD_DIM, 2, dtype=F32) / HEAD_DIM))
    ang = jnp.arange(seq, dtype=F32)[:, None] * inv[None, :]
    return jnp.cos(ang), jnp.sin(ang)


def _halves_layout(w_heads):
    d, n = w_heads.shape
    t = w_heads.reshape(d, n // PROJ_TN, HEADS_PER_TILE, 2, HEAD_DIM // 2)
    return t.transpose(0, 1, 3, 2, 4).reshape(d, n)


def _reorder_w_in(w):
    pts = np.cumsum((0,) + IN_SIZES)
    sec = [w[:, pts[k]:pts[k + 1]] for k in range(len(IN_SIZES))]
    a_q, a_c, a_iq, a_ik, a_iw, b_q, b_k, b_v, c_q, c_k, c_v, gates = sec
    zeros = lambda n: jnp.zeros((w.shape[0], n), w.dtype)
    w_rope = _halves_layout(jnp.concatenate([a_q, a_iq, b_q, b_k, c_q, c_k, a_ik, zeros(HEAD_DIM)], axis=1))
    w_plain = jnp.concatenate([gates, b_v, a_c, a_iw, zeros(LANES - A_IDX_HEADS), c_v, zeros(LANES)], axis=1)
    assert w_rope.shape[1] == N_ROPE and w_plain.shape[1] == N_PLAIN
    return w_rope.astype(BF16), w_plain.astype(BF16)


def _kv_weights(w_uk, w_uv):
    half = HEAD_DIM // 2
    zeros = lambda n: jnp.zeros((w_uk.shape[0], n), w_uk.dtype)
    return jnp.concatenate([w_uk[:, :half], w_uv, zeros(HALF_TN - half - HEAD_DIM),
                            w_uk[:, half:], zeros(HALF_TN - half)], axis=1).astype(BF16)


def _moe_plan(cls, n_tokens):
    ntiles = n_tokens // MOE_TILE + N_CLASSES
    onehot = (cls[:, None] == jnp.arange(N_CLASSES, dtype=jnp.int32)[None, :]).astype(jnp.int32)
    csum = jnp.cumsum(onehot, axis=0)
    rank = jnp.sum(onehot * csum, axis=1) - 1
    counts = csum[-1]
    ptiles = (counts + MOE_TILE - 1) // MOE_TILE
    tile_end = jnp.cumsum(ptiles)
    tile_start = tile_end - ptiles
    pos = (tile_start[cls] * MOE_TILE + rank).astype(jnp.int32)
    nused = tile_end[-1:].astype(jnp.int32)
    tile_ids = jnp.arange(ntiles, dtype=jnp.int32)
    tile_cls = jnp.minimum(jnp.sum((tile_ids[:, None] >= tile_end[None, :]).astype(jnp.int32), axis=1),
                           N_CLASSES - 1).astype(jnp.int32)
    pair = np.array(PAIRS, dtype=np.int32)
    e_lo = jnp.asarray(np.repeat(np.arange(N_GROUPS), len(PAIRS)) * EXPERTS_PER_GROUP + np.tile(pair[:, 0], N_GROUPS), jnp.int32)
    e_hi = jnp.asarray(np.repeat(np.arange(N_GROUPS), len(PAIRS)) * EXPERTS_PER_GROUP + np.tile(pair[:, 1], N_GROUPS), jnp.int32)
    src = jnp.zeros((ntiles * MOE_TILE,), jnp.int32).at[pos].set(jnp.arange(n_tokens, dtype=jnp.int32))
    return e_lo[tile_cls], e_hi[tile_cls], nused, src.reshape(ntiles, 1, MOE_TILE), pos


def kernel(x, w_in, a_w_uk, a_w_uv, c_sinks, w_branch, w_o, ln1_g, ln1_b, router_w, router_b,
           moe_w_gate, moe_w_up, moe_w_down, ln2_g, ln2_b):
    bsz, seq, _ = x.shape
    n_tokens = bsz * seq
    topk = min(A_TOPK_MAX, seq // 4)
    nkb = seq // B_BLOCK
    tq = ATTN_TQ
    kc = min(DSA_KC, seq)
    tm_proj = min(PROJ_TM, seq)
    tm_row = ROW_TM
    half = HEAD_DIM // 2

    cos, sin = _rope_tables(seq)
    cos_in = jnp.tile(cos, (1, HEADS_PER_TILE))
    sin_in = jnp.tile(sin, (1, HEADS_PER_TILE))
    cos_kv = jnp.concatenate([cos, jnp.ones((seq, HALF_TN - half), F32)], axis=1)
    sin_kv = jnp.concatenate([sin, jnp.zeros((seq, HALF_TN - half), F32)], axis=1)
    rwT = router_w.T
    rb = router_b.reshape(N_EXPERTS, 1)

    xf = x.reshape(n_tokens, D_MODEL)
    xb = xf.astype(BF16)
    for l in range(DEPTH):
        w_rope, w_plain = _reorder_w_in(w_in[l])
        h_rope = _proj_rope(xb, 0, D_MODEL, w_rope, cos_in, sin_in, tm=tm_proj, seq=seq)
        h_plain = _proj_plain(xb, w_plain, tm=tm_proj)
        kv = _proj_rope(h_plain, COL_AC // A_KV_RANK, A_KV_RANK, _kv_weights(a_w_uk[l], a_w_uv[l]),
                        cos_kv, sin_kv, tm=tm_proj, seq=seq)

        bvT, cvT, kvT = _values_T(h_plain, kv, n_tokens=n_tokens)

        o_a = _dsa(h_rope, h_plain, kv, kvT, bsz=bsz, seq=seq, tq=min(DSA_TQ, seq), kc=kc, topk=topk)
        kmean = _moba_kmean(h_rope, n_tokens=n_tokens).reshape(bsz, nkb, B_W)
        sel = _moba_gate(h_rope, kmean, bsz=bsz, seq=seq, tq=min(GATE_TQ, seq))
        o_b = _moba(h_rope, bvT, sel, bsz=bsz, seq=seq)
        o_c = _swa(c_sinks[l], h_rope, cvT, bsz=bsz, seq=seq, tq=tq)

        wb_all = w_branch[l].astype(BF16)
        x1, cls = _merge(o_a, o_b, o_c, h_plain, xf,
                              wb_all[:A_W], wb_all[A_W:A_W + B_W], wb_all[A_W + B_W:], w_o[l].astype(BF16),
                              ln1_g[l].reshape(1, D_MODEL), ln1_b[l].reshape(1, D_MODEL), rwT, rb, tm=tm_row)

        e1, e2, nused, src3, pos = _moe_plan(cls[0], n_tokens)
        y_sorted = _experts(e1, e2, nused, src3, x1,
                            moe_w_gate[l].astype(BF16), moe_w_up[l].astype(BF16), moe_w_down[l].astype(BF16))
        xf, xb = _final(pos.reshape(n_tokens // tm_row, 1, tm_row), y_sorted, x1,
                        ln2_g[l].reshape(1, D_MODEL), ln2_b[l].reshape(1, D_MODEL), tm=tm_row)
    return xf.reshape(bsz, seq, D_MODEL)
```

```python
import functools
import math

import jax
import jax.numpy as jnp
import numpy as np
from jax import lax
from jax.experimental import pallas as pl
from jax.experimental.pallas import tpu as pltpu

D_MODEL = 1024
DEPTH = 2
HEAD_DIM = 64
ROPE_THETA = 10000.0
LN_EPS = 1e-5
A_HEADS = 4
A_KV_RANK = 128
A_IDX_HEADS = 4
A_IDX_DIM = 64
A_TOPK_MAX = 256
B_HEADS = 4
B_BLOCK = 256
B_TOPK_BLOCKS = 3
C_HEADS = 8
C_KV_HEADS = 2
C_WINDOW = 128
N_BRANCH = 3
A_W = A_HEADS * HEAD_DIM
B_W = B_HEADS * HEAD_DIM
C_W = C_HEADS * HEAD_DIM
IN_SIZES = (A_W, A_KV_RANK, A_IDX_HEADS * A_IDX_DIM, A_IDX_DIM, A_IDX_HEADS,
            B_W, B_W, B_W, C_W, C_KV_HEADS * HEAD_DIM, C_KV_HEADS * HEAD_DIM,
            N_BRANCH * D_MODEL)
N_EXPERTS = 16
N_GROUPS = 4
EXPERTS_PER_GROUP = 4
D_EXPERT = 512
DN_ALPHA = (2 * DEPTH) ** 0.25
ATTN_SCALE = HEAD_DIM ** -0.5
LOG2E = math.log2(math.e)
Q_SCALE = ATTN_SCALE * LOG2E

LANES = 128
SUBLANES = 8
VMEM_LIMIT_BYTES = 56 * 1024 * 1024

NEG_BIG = -1e30
BF16 = jnp.bfloat16
F32 = jnp.float32

PROJ_TN = 256
HALF_TN = PROJ_TN // 2
HEADS_PER_TILE = PROJ_TN // HEAD_DIM
TILE_AQ = 0
TILE_AIQ = 1
TILE_BQ = 2
TILE_BK = 3
TILE_CQ = 4
TILE_CK_AIK = 6
N_ROPE = 7 * PROJ_TN
COL_GATES = 0
COL_BV = 3072
COL_AC = 3328
COL_AIW = 3456
COL_CV = 3584
N_PLAIN = 3840
PLAIN_TN = 1920
KV_VA0 = HEAD_DIM // 2

PAIRS = ((0, 1), (0, 2), (0, 3), (1, 2), (1, 3), (2, 3))
N_CLASSES = N_GROUPS * len(PAIRS)
MOE_TILE = 256
X1_COLS = D_MODEL + LANES

PROJ_TM = 1024
PROJ_SUB = 256
ROW_TM = 512
ATTN_TQ = 256
DSA_TQ = 512
DSA_KC = 256
GATE_TQ = 1024
assert COL_GATES == 0

VALUE_MID_PASSES = 16
PEEL_MAX = 2.0
MIN_NORMAL_KEY = 0x00800000
MAX_SELECT_PASSES = 96


def _cparams(n_axes):
    return pltpu.CompilerParams(dimension_semantics=("arbitrary",) * n_axes,
                                vmem_limit_bytes=VMEM_LIMIT_BYTES)


def _proj_rope_body(x_ref, w_ref, cos_ref, sin_ref, o_ref, *, n_sub):
    sub = x_ref.shape[0] // n_sub
    for t in range(w_ref.shape[0]):
        for mi in range(n_sub):
            rows = slice(mi * sub, (mi + 1) * sub)
            acc = jnp.dot(x_ref[rows, :], w_ref[t], preferred_element_type=F32)
            a1, a2 = acc[:, :HALF_TN], acc[:, HALF_TN:]
            c, sn = cos_ref[rows, :], sin_ref[rows, :]
            c0 = t * PROJ_TN
            o_ref[rows, c0:c0 + HALF_TN] = (a1 * c - a2 * sn).astype(o_ref.dtype)
            o_ref[rows, c0 + HALF_TN:c0 + PROJ_TN] = (a2 * c + a1 * sn).astype(o_ref.dtype)


def _proj_plain_body(x_ref, w_ref, o_ref, *, n_sub):
    sub = x_ref.shape[0] // n_sub
    tn = o_ref.shape[1]
    is_gate = (pl.program_id(1) * tn + lax.broadcasted_iota(jnp.int32, (1, tn), 1)) < COL_GATES + N_BRANCH * D_MODEL
    for mi in range(n_sub):
        rows = slice(mi * sub, (mi + 1) * sub)
        acc = jnp.dot(x_ref[rows, :], w_ref[0], preferred_element_type=F32)
        o_ref[rows, :] = jnp.where(is_gate, jax.nn.sigmoid(acc), acc).astype(o_ref.dtype)


def _column_tiles(w, tn=PROJ_TN):
    k_dim, n = w.shape
    return w.reshape(k_dim, n // tn, tn).transpose(1, 0, 2)


def _proj_rope(x, x_col_block, k_dim, w, cos_t, sin_t, *, tm, seq):
    t_tokens = x.shape[0]
    n = w.shape[1]
    pos_blocks = seq // tm
    return pl.pallas_call(
        functools.partial(_proj_rope_body, n_sub=max(tm // PROJ_SUB, 1)),
        grid=(t_tokens // tm,),
        in_specs=[
            pl.BlockSpec((tm, k_dim), lambda i: (i, x_col_block)),
            pl.BlockSpec((n // PROJ_TN, k_dim, PROJ_TN), lambda i: (0, 0, 0)),
            pl.BlockSpec((tm, HALF_TN), lambda i: (i % pos_blocks, 0)),
            pl.BlockSpec((tm, HALF_TN), lambda i: (i % pos_blocks, 0)),
        ],
        out_specs=pl.BlockSpec((tm, n), lambda i: (i, 0)),
        out_shape=jax.ShapeDtypeStruct((t_tokens, n), BF16),
        compiler_params=_cparams(1),
        name="proj_rope",
    )(x, _column_tiles(w), cos_t, sin_t)


def _proj_plain(x, w, *, tm):
    t_tokens, k_dim = x.shape
    n = w.shape[1]
    return pl.pallas_call(
        functools.partial(_proj_plain_body, n_sub=max(tm // PROJ_SUB, 1)),
        grid=(t_tokens // tm, n // PLAIN_TN),
        in_specs=[
            pl.BlockSpec((tm, k_dim), lambda i, j: (i, 0)),
            pl.BlockSpec((1, k_dim, PLAIN_TN), lambda i, j: (j, 0, 0)),
        ],
        out_specs=pl.BlockSpec((tm, PLAIN_TN), lambda i, j: (i, j)),
        out_shape=jax.ShapeDtypeStruct((t_tokens, n), BF16),
        compiler_params=_cparams(2),
        name="proj_plain",
    )(x, _column_tiles(w, PLAIN_TN))


def _values_T_body(bv_ref, cv_ref, kv_ref, bvT_ref, cvT_ref, kvT_ref):
    for src, dst in ((bv_ref, bvT_ref), (cv_ref, cvT_ref), (kv_ref, kvT_ref)):
        for g in range(dst.shape[0]):
            dst[g] = src[g * B_BLOCK:(g + 1) * B_BLOCK, :].astype(F32).T.astype(dst.dtype)


def _values_T(h_plain, kv, *, n_tokens):
    nblk = n_tokens // B_BLOCK
    gb = math.gcd(nblk, SUBLANES)
    in_tile = lambda col: pl.BlockSpec((gb * B_BLOCK, PROJ_TN), lambda i: (i, col // PROJ_TN))
    out_tile = pl.BlockSpec((gb, PROJ_TN, B_BLOCK), lambda i: (i, 0, 0))
    shape = jax.ShapeDtypeStruct((nblk, PROJ_TN, B_BLOCK), BF16)
    return pl.pallas_call(
        _values_T_body,
        grid=(nblk // gb,),
        in_specs=[in_tile(COL_BV), in_tile(COL_CV), in_tile(0)],
        out_specs=[out_tile, out_tile, out_tile],
        out_shape=[shape, shape, shape],
        compiler_params=_cparams(1),
        name="values_transpose",
    )(h_plain, h_plain, kv)


def _f32_to_key(x):
    b = lax.bitcast_convert_type(x, jnp.int32)
    return jnp.where(b < 0, b ^ jnp.int32(0x7FFFFFFF), b)


def _key_to_f32(k):
    b = jnp.where(k < 0, k ^ jnp.int32(0x7FFFFFFF), k)
    return lax.bitcast_convert_type(b, F32)


def _fold8(x, op):
    return op(x.reshape(x.shape[0] // SUBLANES, SUBLANES, x.shape[1]), axis=0)


def _tile_T(x_ref):
    return x_ref[...].astype(F32).T


def _head_rows(xT, h):
    half = HEAD_DIM // 2
    return xT[h * half:(h + 1) * half], xT[HALF_TN + h * half:HALF_TN + (h + 1) * half]


def _place_head(first, second, slot):
    half = HEAD_DIM // 2
    w = first.shape[1]
    before, after = slot * half, HALF_TN - (slot + 1) * half
    pieces = []
    for part in (first, second):
        pieces += [jnp.zeros((before, w), part.dtype)] * (before > 0) + [part] + [jnp.zeros((after, w), part.dtype)] * (after > 0)
    return jnp.concatenate(pieces, axis=0)


def _store_rows(o_ref, oT_heads):
    o_ref[...] = jnp.concatenate(oT_heads, axis=0).T.astype(o_ref.dtype)


IK_SLOT = 2


def _dsa_body(q_ref, iq_ref, iw_ref, ik_ref, kv_ref, vT_ref, o_ref, sc_ref, acc_ref, sbuf_ref, *, tq, kc, topk):
    i = pl.program_id(1)
    q0 = i * tq
    nch = (q0 + tq + kc - 1) // kc
    qpos = q0 + lax.broadcasted_iota(jnp.int32, (1, tq), 1)
    iqT = _tile_T(iq_ref)
    iq_ops = [_place_head(*[p.astype(BF16) for p in _head_rows(iqT, h)], IK_SLOT) for h in range(A_IDX_HEADS)]
    w_idx = iw_ref[...].astype(F32).T[:A_IDX_HEADS]

    def score_chunk(c, carry, diagonal):
        mn, mx = carry
        k0 = pl.multiple_of(c * kc, kc)
        ikc = ik_ref[pl.ds(k0, kc), :]
        acc = jnp.zeros((kc, tq), F32)
        for h in range(A_IDX_HEADS):
            d = jnp.dot(ikc, iq_ops[h], preferred_element_type=F32)
            acc = acc + w_idx[h:h + 1, :] * jnp.maximum(d, 0.0)
        if diagonal:
            causal = (k0 + lax.broadcasted_iota(jnp.int32, (kc, 1), 0)) <= qpos
            lo_part, hi_part = jnp.where(causal, acc, jnp.inf), jnp.where(causal, acc, -jnp.inf)
        else:
            lo_part = hi_part = acc
        sc_ref[pl.ds(k0, kc), :] = hi_part
        return jnp.minimum(mn, _fold8(lo_part, jnp.min)), jnp.maximum(mx, _fold8(hi_part, jnp.max))

    n_full = (q0 + 1) // kc
    carry = lax.fori_loop(0, n_full, functools.partial(score_chunk, diagonal=False),
                          (jnp.full((SUBLANES, tq), jnp.inf, F32), jnp.full((SUBLANES, tq), -jnp.inf, F32)))
    mn8, mx8 = lax.fori_loop(n_full, nch, functools.partial(score_chunk, diagonal=True), carry)
    row_min = jnp.min(mn8, axis=0, keepdims=True)
    row_max = jnp.max(mx8, axis=0, keepdims=True)

    pc = 2 * kc
    npair = (nch + 1) // 2
    n_chunks_total = vT_ref.shape[0]

    @pl.when(nch % 2 == 1)
    def _():
        sc_ref[pl.ds(pl.multiple_of(nch * kc, kc), kc), :] = jnp.full((kc, tq), -jnp.inf, F32)

    def count_ge(thr):
        def body(j, cnt):
            s = sc_ref[pl.ds(pl.multiple_of(j * pc, pc), pc), :]
            return cnt + _fold8(jnp.where(s >= thr, 1.0, 0.0), jnp.sum)
        cnt8 = lax.fori_loop(0, npair, body, jnp.zeros((SUBLANES, tq), F32))
        return jnp.sum(cnt8, axis=0, keepdims=True)

    def min_ge(thr):
        def body(j, mn):
            s = sc_ref[pl.ds(pl.multiple_of(j * pc, pc), pc), :]
            return jnp.minimum(mn, _fold8(jnp.where(s >= thr, s, jnp.inf), jnp.min))
        mn8 = lax.fori_loop(0, npair, body, jnp.full((SUBLANES, tq), jnp.inf, F32))
        return jnp.min(mn8, axis=0, keepdims=True)

    def count_zero():
        def body(j, carry):
            c0, cp = carry
            s = sc_ref[pl.ds(pl.multiple_of(j * pc, pc), pc), :]
            return (c0 + _fold8(jnp.where(s >= 0.0, 1.0, 0.0), jnp.sum),
                    cp + _fold8(jnp.where(s > 0.0, 1.0, 0.0), jnp.sum))
        z8 = jnp.zeros((SUBLANES, tq), F32)
        c0, cp = lax.fori_loop(0, npair, body, (z8, z8))
        return jnp.sum(c0, axis=0, keepdims=True), jnp.sum(cp, axis=0, keepdims=True)

    kf = float(topk)
    n_valid = (qpos + 1).astype(F32)
    few = n_valid <= kf
    cnt_nonneg, cnt_pos = count_zero()
    is_zero = jnp.logical_and(cnt_pos < kf, cnt_nonneg >= kf)
    is_pos = cnt_pos >= kf
    lo0 = jnp.where(is_zero, 0, jnp.where(is_pos, MIN_NORMAL_KEY, _f32_to_key(row_min)))
    hi0 = jnp.where(is_zero, 1, jnp.where(is_pos, _f32_to_key(row_max) + 1, -1))
    cnt0 = jnp.where(is_zero, cnt_nonneg, jnp.where(is_pos, cnt_pos, n_valid))
    lo0 = jnp.where(few, _f32_to_key(row_min), lo0)
    hi0 = jnp.where(few, lo0 + 1, hi0)

    def unfinished(lo, hi, cnt_lo):
        return jnp.logical_and(lo + 1 != hi, cnt_lo != kf)

    def surplus(lo, hi, cnt_lo):
        return jnp.max(jnp.where(unfinished(lo, hi, cnt_lo), cnt_lo - kf, 0.0))

    def bis_cond(st):
        _, it, excess = st
        return jnp.logical_and(excess > 0.0, it < MAX_SELECT_PASSES)

    def bisect(st):
        lo, hi, cnt_lo, it = st
        nd = unfinished(lo, hi, cnt_lo)
        mid_val = _f32_to_key(0.5 * _key_to_f32(lo) + 0.5 * _key_to_f32(hi))
        mid_key = lo + lax.shift_right_logical(hi - lo, 1)
        mid = jnp.where(it < VALUE_MID_PASSES, jnp.clip(mid_val, lo + 1, hi - 1), mid_key)
        cnt = count_ge(_key_to_f32(mid))
        ge = cnt >= kf
        up = jnp.logical_and(nd, ge)
        down = jnp.logical_and(nd, jnp.logical_not(ge))
        return jnp.where(up, mid, lo), jnp.where(down, mid, hi), jnp.where(up, cnt, cnt_lo), it + 1

    def peel(st):
        lo, hi, cnt_lo, it = st
        nd = unfinished(lo, hi, cnt_lo)
        k1 = jnp.maximum(_f32_to_key(min_ge(_key_to_f32(lo))), lo)
        cnt = count_ge(_key_to_f32(k1 + 1))
        ge = cnt >= kf
        up = jnp.logical_and(nd, ge)
        down = jnp.logical_and(nd, jnp.logical_not(ge))
        new_lo = jnp.where(nd, jnp.where(ge, k1 + 1, k1), lo)
        return new_lo, jnp.where(down, k1 + 1, hi), jnp.where(up, cnt, cnt_lo), it + 2

    def bis_body(st):
        (lo, hi, cnt_lo), it, excess = st
        lo, hi, cnt_lo, it = lax.cond(excess > PEEL_MAX, bisect, peel, (lo, hi, cnt_lo, it))
        return (lo, hi, cnt_lo), it, surplus(lo, hi, cnt_lo)

    (lo, _, cnt_lo), _, _ = lax.while_loop(bis_cond, bis_body,
                                           ((lo0, hi0, cnt0), jnp.int32(0), surplus(lo0, hi0, cnt0)))
    thr = _key_to_f32(lo)

    tie = jnp.logical_and(cnt_lo > kf, jnp.logical_not(few))

    @pl.when(jnp.max(jnp.where(tie, 1.0, 0.0)) > 0.0)
    def _():
        def count_gt():
            def gt_body(c, cnt):
                k0 = pl.multiple_of(c * kc, kc)
                s = sc_ref[pl.ds(k0, kc), :]
                return cnt + _fold8(jnp.where(s > thr, 1.0, 0.0), jnp.sum)
            gt8 = lax.fori_loop(0, nch, gt_body, jnp.zeros((SUBLANES, tq), F32))
            return jnp.sum(gt8, axis=0, keepdims=True)

        zero_thr = thr == 0.0
        other = jnp.max(jnp.where(jnp.logical_and(tie, jnp.logical_not(zero_thr)), 1.0, 0.0)) > 0.0
        gt = lax.cond(other, count_gt, lambda: cnt_pos)
        need = kf - jnp.where(zero_thr, cnt_pos, gt)
        r = lax.broadcasted_iota(jnp.int32, (kc, kc), 0)
        cidx = lax.broadcasted_iota(jnp.int32, (kc, kc), 1)
        tri = jnp.where(cidx <= r, 1.0, 0.0).astype(BF16)

        thr_tie = jnp.where(tie, thr, jnp.nan)
        keep = jnp.where(tie, need, jnp.inf)

        def tie_body(c, seen):
            k0 = pl.multiple_of(c * kc, kc)
            s = sc_ref[pl.ds(k0, kc), :]
            eqf = jnp.where(s == thr_tie, 1.0, 0.0)
            rank = jnp.dot(tri, eqf.astype(BF16), preferred_element_type=F32) + seen
            sc_ref[pl.ds(k0, kc), :] = jnp.where(eqf * rank > keep, -jnp.inf, s)
            return seen + jnp.sum(eqf, axis=0, keepdims=True)

        lax.fori_loop(0, nch, tie_body, jnp.zeros((1, tq), F32))

    qT_all = (_tile_T(q_ref) * Q_SCALE).astype(BF16)
    qT = jnp.concatenate([_place_head(*_head_rows(qT_all, h), 0) for h in range(A_HEADS)], axis=1)
    acc_ref[...] = jnp.zeros_like(acc_ref)
    ones_rows = jnp.ones((SUBLANES, kc), BF16)

    def issue_scores(c, slot):
        k0 = pl.multiple_of(jnp.minimum(c, n_chunks_total - 1) * kc, kc)
        sbuf_ref[slot] = jnp.dot(kv_ref[pl.ds(k0, kc), :], qT, preferred_element_type=F32)

    def softmax_pv(c, slot, m):
        k0 = pl.multiple_of(c * kc, kc)
        v1 = jnp.concatenate([vT_ref[c, KV_VA0:KV_VA0 + HEAD_DIM, :], ones_rows], axis=0)
        sel = sc_ref[pl.ds(k0, kc), :] >= thr
        s = jnp.concatenate([jnp.where(sel, sbuf_ref[slot, :, h * tq:(h + 1) * tq], NEG_BIG)
                             for h in range(A_HEADS)], axis=1)
        m_new = jnp.maximum(m, jnp.max(s, axis=0, keepdims=True))
        alpha = jnp.exp2(m - m_new)
        p = jnp.exp2((s - m_new).astype(BF16))
        acc_ref[...] = acc_ref[...] * alpha + jnp.dot(v1, p, preferred_element_type=F32)
        return m_new

    def chunk_pair(j, m):
        c0 = 2 * j
        issue_scores(c0 + 1, 1)
        m = softmax_pv(c0, 0, m)
        issue_scores(c0 + 2, 0)
        return softmax_pv(c0 + 1, 1, m)

    issue_scores(0, 0)
    lax.fori_loop(0, npair, chunk_pair, jnp.full((1, A_HEADS * tq), NEG_BIG, F32))
    a = acc_ref[...]
    o = a[:HEAD_DIM] / a[HEAD_DIM:HEAD_DIM + 1]
    _store_rows(o_ref, [o[:, h * tq:(h + 1) * tq] for h in range(A_HEADS)])


def _dsa(h_rope, h_plain, kv, kvT, *, bsz, seq, tq, kc, topk):
    nq = seq // tq
    assert kc == B_BLOCK
    return pl.pallas_call(
        functools.partial(_dsa_body, tq=tq, kc=kc, topk=topk),
        grid=(bsz, nq),
        in_specs=[
            pl.BlockSpec((tq, PROJ_TN), lambda b, i: (b * nq + i, TILE_AQ)),
            pl.BlockSpec((tq, PROJ_TN), lambda b, i: (b * nq + i, TILE_AIQ)),
            pl.BlockSpec((tq, LANES), lambda b, i: (b * nq + i, COL_AIW // LANES)),
            pl.BlockSpec((seq, PROJ_TN), lambda b, i: (b, TILE_CK_AIK)),
            pl.BlockSpec((seq, PROJ_TN), lambda b, i: (b, 0)),
            pl.BlockSpec((seq // kc, PROJ_TN, kc), lambda b, i: (b, 0, 0)),
        ],
        out_specs=pl.BlockSpec((tq, A_W), lambda b, i: (b * nq + i, 0)),
        out_shape=jax.ShapeDtypeStruct((bsz * seq, A_W), BF16),
        scratch_shapes=[pltpu.VMEM((seq + kc, tq), F32),
                        pltpu.VMEM((HEAD_DIM + SUBLANES, A_HEADS * tq), F32),
                        pltpu.VMEM((2, kc, A_HEADS * tq), F32)],
        compiler_params=_cparams(2),
        name="dsa_attention",
    )(h_rope, h_rope, h_plain, h_rope, kv, kvT)


def _kmean_body(k_ref, o_ref):
    gb = o_ref.shape[0]
    o_ref[...] = jnp.mean(k_ref[...].astype(F32).reshape(gb, B_BLOCK, B_W), axis=1)


def _moba_kmean(h_rope, *, n_tokens):
    nblk = n_tokens // B_BLOCK
    gb = math.gcd(nblk, SUBLANES)
    return pl.pallas_call(
        _kmean_body,
        grid=(nblk // gb,),
        in_specs=[pl.BlockSpec((gb * B_BLOCK, B_W), lambda i: (i, TILE_BK))],
        out_specs=pl.BlockSpec((gb, B_W), lambda i: (i, 0)),
        out_shape=jax.ShapeDtypeStruct((nblk, B_W), F32),
        compiler_params=_cparams(1),
        name="moba_kmean",
    )(h_rope)


def _moba_gate_body(q_ref, km_ref, sel_ref, *, tq, nkb, nsel):
    i = pl.program_id(1)
    qpos = i * tq + lax.broadcasted_iota(jnp.int32, (1, tq), 1)
    cur = qpos // B_BLOCK
    row = lax.broadcasted_iota(jnp.int32, (nkb, tq), 0)
    past = row < cur
    qT = _tile_T(q_ref).astype(BF16)
    km = km_ref[0]
    col_head = (lax.broadcasted_iota(jnp.int32, km.shape, 1) % HALF_TN) // (HEAD_DIM // 2)
    for h in range(B_HEADS):
        km_h = jnp.where(col_head == h, km, 0.0).astype(BF16)
        g = jnp.dot(km_h, qT, preferred_element_type=F32)
        g = jnp.where(past, g, -jnp.inf)
        sel = jnp.zeros((nkb, tq), F32)
        for _ in range(nsel):
            mx = jnp.max(g, axis=0, keepdims=True)
            idx = jnp.min(jnp.where(g == mx, row, nkb), axis=0, keepdims=True)
            pick = row == idx
            sel = jnp.where(pick, 1.0, sel)
            g = jnp.where(pick, -jnp.inf, g)
        sel_ref[0, h] = jnp.where(past, sel, 0.0)


def _moba_gate(h_rope, kmean, *, bsz, seq, tq):
    nkb = seq // B_BLOCK
    nsel = min(B_TOPK_BLOCKS, nkb)
    nq = seq // tq
    return pl.pallas_call(
        functools.partial(_moba_gate_body, tq=tq, nkb=nkb, nsel=nsel),
        grid=(bsz, nq),
        in_specs=[
            pl.BlockSpec((tq, PROJ_TN), lambda b, i: (b * nq + i, TILE_BQ)),
            pl.BlockSpec((1, nkb, B_W), lambda b, i: (b, 0, 0)),
        ],
        out_specs=pl.BlockSpec((1, B_HEADS, nkb, tq), lambda b, i: (b, 0, 0, i)),
        out_shape=jax.ShapeDtypeStruct((bsz, B_HEADS, nkb, seq), F32),
        compiler_params=_cparams(2),
        name="moba_gate",
    )(h_rope, kmean)


def _moba_body(q_ref, k_ref, vT_ref, sel_ref, o_ref, acc_ref, sbuf_ref, *, tq):
    i = pl.program_id(1)
    qT_all = (_tile_T(q_ref) * Q_SCALE).astype(BF16)
    qT = [_place_head(*_head_rows(qT_all, h), h) for h in range(B_HEADS)]
    ones_rows = jnp.ones((SUBLANES, B_BLOCK), BF16)
    n_past = i

    def values(h, j):
        return jnp.concatenate([vT_ref[j, h * HEAD_DIM:(h + 1) * HEAD_DIM, :], ones_rows], axis=0)

    def issue_scores(j, slot):
        k0 = pl.multiple_of(jnp.minimum(j, i) * B_BLOCK, B_BLOCK)
        for h in range(B_HEADS):
            sbuf_ref[slot, h] = jnp.dot(k_ref[pl.ds(k0, B_BLOCK), :], qT[h], preferred_element_type=F32)

    def softmax_pv(j, slot, ms):
        ps, alphas, new_ms = [], [], []
        for h in range(B_HEADS):
            s = jnp.where(sel_ref[0, h, pl.ds(j, 1), :] > 0.0, sbuf_ref[slot, h], NEG_BIG)
            m_new = jnp.maximum(ms[h], jnp.max(s, axis=0, keepdims=True))
            alphas.append(jnp.exp2(ms[h] - m_new))
            ps.append(jnp.exp2((s - m_new).astype(BF16)))
            new_ms.append(m_new)
        for h in range(B_HEADS):
            acc_ref[h] = acc_ref[h] * alphas[h] + jnp.dot(values(h, j), ps[h], preferred_element_type=F32)
        return tuple(new_ms)

    own0 = pl.multiple_of(i * B_BLOCK, B_BLOCK)
    s_own = [jnp.dot(k_ref[pl.ds(own0, B_BLOCK), :], qT[h], preferred_element_type=F32) for h in range(B_HEADS)]
    issue_scores(0, 0)
    causal = lax.broadcasted_iota(jnp.int32, (B_BLOCK, tq), 0) <= lax.broadcasted_iota(jnp.int32, (B_BLOCK, tq), 1)
    ms = []
    for h in range(B_HEADS):
        s = jnp.where(causal, s_own[h], NEG_BIG)
        m0 = jnp.max(s, axis=0, keepdims=True)
        acc_ref[h] = jnp.dot(values(h, i), jnp.exp2((s - m0).astype(BF16)), preferred_element_type=F32)
        ms.append(m0)

    def block_pair(jj, ms):
        j0 = 2 * jj
        issue_scores(j0 + 1, 1)
        ms = softmax_pv(j0, 0, ms)
        issue_scores(j0 + 2, 0)
        return softmax_pv(j0 + 1, 1, ms)

    lax.fori_loop(0, (n_past + 1) // 2, block_pair, tuple(ms))
    outs = []
    for h in range(B_HEADS):
        a = acc_ref[h]
        outs.append(a[:HEAD_DIM] / a[HEAD_DIM:HEAD_DIM + 1])
    _store_rows(o_ref, outs)


def _moba(h_rope, bvT, sel, *, bsz, seq):
    tq = B_BLOCK
    nkb = seq // B_BLOCK
    nq = seq // tq
    return pl.pallas_call(
        functools.partial(_moba_body, tq=tq),
        grid=(bsz, nq),
        in_specs=[
            pl.BlockSpec((tq, PROJ_TN), lambda b, i: (b * nq + i, TILE_BQ)),
            pl.BlockSpec((seq, PROJ_TN), lambda b, i: (b, TILE_BK)),
            pl.BlockSpec((nkb, B_W, B_BLOCK), lambda b, i: (b, 0, 0)),
            pl.BlockSpec((1, B_HEADS, nkb, tq), lambda b, i: (b, 0, 0, i)),
        ],
        out_specs=pl.BlockSpec((tq, B_W), lambda b, i: (b * nq + i, 0)),
        out_shape=jax.ShapeDtypeStruct((bsz * seq, B_W), BF16),
        scratch_shapes=[pltpu.VMEM((B_HEADS, HEAD_DIM + SUBLANES, tq), F32),
                        pltpu.VMEM((2, B_HEADS, B_BLOCK, tq), F32)],
        compiler_params=_cparams(2),
        name="moba_attention",
    )(h_rope, h_rope, bvT, sel)


def _swa_body(sink_ref, q0_ref, q1_ref, k0_ref, k1_ref, k2_ref, vprev_ref, vcur_ref, o_ref, *, tq):
    i = pl.program_id(1)
    q0 = i * tq
    qpos = q0 + lax.broadcasted_iota(jnp.int32, (1, tq), 1)
    nk = tq + C_WINDOW
    kpos = q0 - C_WINDOW + lax.broadcasted_iota(jnp.int32, (nk, 1), 0)
    diff = qpos - kpos
    ok = jnp.where(diff >= 0, jnp.where(diff < C_WINDOW, jnp.where(kpos >= 0, 1.0, 0.0), 0.0), 0.0) > 0.0
    ones_rows = jnp.ones((SUBLANES, nk), BF16)
    group = C_HEADS // C_KV_HEADS
    assert group == HEADS_PER_TILE
    kk = jnp.concatenate([k0_ref[...], k1_ref[...], k2_ref[...]], axis=0)
    outs = []
    scores = []
    for kv, q_ref in enumerate((q0_ref, q1_ref)):
        qT_all = (_tile_T(q_ref) * Q_SCALE).astype(BF16)
        qT = jnp.concatenate([_place_head(*_head_rows(qT_all, g), kv) for g in range(group)], axis=1)
        scores.append(jnp.dot(kk, qT, preferred_element_type=F32))
    for kv in range(C_KV_HEADS):
        vrows = slice(kv * HEAD_DIM, (kv + 1) * HEAD_DIM)
        vv = jnp.concatenate([vprev_ref[0, vrows, tq - C_WINDOW:], vcur_ref[0, vrows, :]], axis=1)
        v1 = jnp.concatenate([vv, ones_rows], axis=0)
        heads = [kv * group + g for g in range(group)]
        sink = jnp.concatenate([jnp.full((1, tq), sink_ref[hd] * LOG2E, F32) for hd in heads], axis=1)
        s = jnp.concatenate([jnp.where(ok, scores[kv][:, g * tq:(g + 1) * tq], NEG_BIG) for g in range(group)], axis=1)
        m = jnp.maximum(jnp.max(s, axis=0, keepdims=True), sink)
        p = jnp.exp2((s - m).astype(BF16))
        a = jnp.dot(v1, p, preferred_element_type=F32)
        o = a[:HEAD_DIM] / (a[HEAD_DIM:HEAD_DIM + 1] + jnp.exp2(sink - m))
        outs += [o[:, g * tq:(g + 1) * tq] for g in range(group)]
    _store_rows(o_ref, outs)


def _swa(sinks, h_rope, cvT, *, bsz, seq, tq):
    r = tq // C_WINDOW
    assert r == 2 and tq == B_BLOCK
    nq = seq // tq
    nwb = seq // C_WINDOW

    def kspec(off):
        return pl.BlockSpec((C_WINDOW, PROJ_TN),
                            lambda b, i, s: (b * nwb + jnp.maximum(i * r + off, 0), TILE_CK_AIK))

    def vspec(off):
        return pl.BlockSpec((1, PROJ_TN, B_BLOCK), lambda b, i, s: (b * nq + jnp.maximum(i + off, 0), 0, 0))

    grid_spec = pltpu.PrefetchScalarGridSpec(
        num_scalar_prefetch=1,
        grid=(bsz, nq),
        in_specs=[pl.BlockSpec((tq, PROJ_TN), lambda b, i, s: (b * nq + i, TILE_CQ)),
                  pl.BlockSpec((tq, PROJ_TN), lambda b, i, s: (b * nq + i, TILE_CQ + 1)),
                  kspec(-1), kspec(0), kspec(1), vspec(-1), vspec(0)],
        out_specs=pl.BlockSpec((tq, C_W), lambda b, i, s: (b * nq + i, 0)),
    )
    return pl.pallas_call(
        functools.partial(_swa_body, tq=tq),
        grid_spec=grid_spec,
        out_shape=jax.ShapeDtypeStruct((bsz * seq, C_W), BF16),
        compiler_params=_cparams(2),
        name="swa_attention",
    )(sinks, h_rope, h_rope, h_rope, h_rope, h_rope, cvT, cvT)


def _layer_norm(z, g, b):
    mu = jnp.mean(z, axis=-1, keepdims=True)
    zc = z - mu
    var = jnp.mean(zc * zc, axis=-1, keepdims=True)
    return zc * lax.rsqrt(var + LN_EPS) * g + b


def _split_bf16(a):
    hi = a.astype(BF16)
    lo = (a - hi.astype(F32)).astype(BF16)
    return hi, lo


def _router(x1, rw_ref, rb_ref, cls_ref):
    nt = (((1,), (1,)), ((), ()))
    xh, xl = _split_bf16(x1)
    wh, wl = _split_bf16(rw_ref[...])
    logits = (lax.dot_general(wh, xh, nt, preferred_element_type=F32)
              + lax.dot_general(wh, xl, nt, preferred_element_type=F32)
              + lax.dot_general(wl, xh, nt, preferred_element_type=F32))
    aff = jax.nn.sigmoid(logits)
    score = aff + rb_ref[...]
    tm = x1.shape[0]
    sc = [score[e:e + 1, :] for e in range(N_EXPERTS)]
    af = [aff[e:e + 1, :] for e in range(N_EXPERTS)]
    in_top = []
    grp_score = []
    for gq in range(N_GROUPS):
        gs = jnp.zeros((1, tm), F32)
        for a in range(EXPERTS_PER_GROUP):
            ea = gq * EXPERTS_PER_GROUP + a
            rank = jnp.zeros((1, tm), F32)
            for b in range(EXPERTS_PER_GROUP):
                if b == a:
                    continue
                eb = gq * EXPERTS_PER_GROUP + b
                beats = (sc[eb] >= sc[ea]) if b < a else (sc[eb] > sc[ea])
                rank = rank + jnp.where(beats, 1.0, 0.0)
            top = rank < 2.0
            in_top.append(top)
            gs = gs + jnp.where(top, sc[ea], 0.0)
        grp_score.append(gs)
    best = grp_score[0]
    gstar = jnp.zeros((1, tm), jnp.int32)
    for gq in range(1, N_GROUPS):
        better = grp_score[gq] > best
        best = jnp.where(better, grp_score[gq], best)
        gstar = jnp.where(better, gq, gstar)
    cls = jnp.zeros((1, tm), jnp.int32)
    w_lo = jnp.zeros((1, tm), F32)
    w_hi = jnp.zeros((1, tm), F32)
    for gq in range(N_GROUPS):
        is_g = gstar == gq
        for pi, (a, b) in enumerate(PAIRS):
            ea, eb = gq * EXPERTS_PER_GROUP + a, gq * EXPERTS_PER_GROUP + b
            hit = jnp.where(is_g, jnp.where(in_top[ea], jnp.where(in_top[eb], 1.0, 0.0), 0.0), 0.0) > 0.0
            cls = jnp.where(hit, gq * len(PAIRS) + pi, cls)
            tot = af[ea] + af[eb]
            w_lo = jnp.where(hit, af[ea] / tot, w_lo)
            w_hi = jnp.where(hit, af[eb] / tot, w_hi)
    cls_ref[...] = cls
    return w_lo, w_hi


def _merge_body(oa_ref, ob_ref, oc_ref, ga_ref, gb_ref, gc_ref, x_ref, wa_ref, wb_ref, wc_ref, wo_ref,
                g_ref, b_ref, rw_ref, rb_ref, x1_ref, cls_ref):
    def branch(o_ref, w_ref, gate_ref):
        y = jnp.dot(o_ref[...], w_ref[...], preferred_element_type=F32)
        return gate_ref[...].astype(F32) * y

    merged = branch(oa_ref, wa_ref, ga_ref) + branch(ob_ref, wb_ref, gb_ref) + branch(oc_ref, wc_ref, gc_ref)
    y = jnp.dot(merged.astype(BF16), wo_ref[...], preferred_element_type=F32)
    x1 = _layer_norm(DN_ALPHA * x_ref[...] + y, g_ref[...], b_ref[...])
    w_lo, w_hi = _router(x1, rw_ref, rb_ref, cls_ref)
    x1_ref[:, :D_MODEL] = x1
    x1_ref[:, D_MODEL:] = jnp.concatenate([w_lo, w_hi, jnp.zeros((LANES - 2, x1.shape[0]), F32)], axis=0).T


def _merge(oa, ob, oc, h_plain, x, wa, wb, wc, wo, ln_g, ln_b, rwT, rb, *, tm):
    t_tokens = x.shape[0]
    gate_blk = COL_GATES // D_MODEL
    full = lambda shape: pl.BlockSpec(shape, lambda i: (0,) * len(shape))
    return pl.pallas_call(
        _merge_body,
        grid=(t_tokens // tm,),
        in_specs=[
            pl.BlockSpec((tm, A_W), lambda i: (i, 0)),
            pl.BlockSpec((tm, B_W), lambda i: (i, 0)),
            pl.BlockSpec((tm, C_W), lambda i: (i, 0)),
            pl.BlockSpec((tm, D_MODEL), lambda i: (i, gate_blk)),
            pl.BlockSpec((tm, D_MODEL), lambda i: (i, gate_blk + 1)),
            pl.BlockSpec((tm, D_MODEL), lambda i: (i, gate_blk + 2)),
            pl.BlockSpec((tm, D_MODEL), lambda i: (i, 0)),
            full((A_W, D_MODEL)), full((B_W, D_MODEL)), full((C_W, D_MODEL)), full((D_MODEL, D_MODEL)),
            full((1, D_MODEL)), full((1, D_MODEL)), full((N_EXPERTS, D_MODEL)), full((N_EXPERTS, 1)),
        ],
        out_specs=[
            pl.BlockSpec((tm, X1_COLS), lambda i: (i, 0)),
            pl.BlockSpec((1, tm), lambda i: (0, i)),
        ],
        out_shape=[
            jax.ShapeDtypeStruct((t_tokens, X1_COLS), F32),
            jax.ShapeDtypeStruct((1, t_tokens), jnp.int32),
        ],
        compiler_params=_cparams(1),
        name="merge_ln_router",
    )(oa, ob, oc, h_plain, h_plain, h_plain, x, wa, wb, wc, wo, ln_g, ln_b, rwT, rb)


def _gather_start(idx_ref, src_hbm, dst_ref, sem, n_rows):
    def start(r, _):
        pltpu.make_async_copy(src_hbm.at[pl.ds(idx_ref[0, 0, r], 1)], dst_ref.at[pl.ds(r, 1)], sem).start()
        return 0

    lax.fori_loop(0, n_rows, start, 0, unroll=8)


def _gather_wait(src_hbm, dst_ref, sem, n_rows):
    pltpu.make_async_copy(src_hbm.at[pl.ds(0, n_rows)], dst_ref, sem).wait()


def _experts_body(e1_ref, e2_ref, nused_ref, src_ref, src_next_ref, x_hbm,
                  wg1_ref, wu1_ref, wd1_ref, wg2_ref, wu2_ref, wd2_ref, o_ref, xbuf, sem):
    s = pl.program_id(0)
    nused = nused_ref[0]
    slot = lax.rem(s, 2)

    @pl.when(s == 0)
    def _():
        _gather_start(src_ref, x_hbm, xbuf.at[0], sem.at[0], MOE_TILE)

    @pl.when(s + 1 < nused)
    def _():
        _gather_start(src_next_ref, x_hbm, xbuf.at[1 - slot], sem.at[1 - slot], MOE_TILE)

    @pl.when(s < nused)
    def _():
        _gather_wait(x_hbm, xbuf.at[slot], sem.at[slot], MOE_TILE)
        xb = xbuf[slot, :, :D_MODEL].astype(BF16)
        wt = xbuf[slot, :, D_MODEL:D_MODEL + 2]

        def expert(wg_ref, wu_ref, wd_ref):
            g = jnp.dot(xb, wg_ref[0], preferred_element_type=F32)
            u = jnp.dot(xb, wu_ref[0], preferred_element_type=F32)
            he = (g * jax.nn.sigmoid(g) * u).astype(BF16)
            return jnp.dot(he, wd_ref[0], preferred_element_type=F32)

        o_ref[...] = wt[:, 0:1] * expert(wg1_ref, wu1_ref, wd1_ref) + wt[:, 1:2] * expert(wg2_ref, wu2_ref, wd2_ref)

    @pl.when(s >= nused)
    def _():
        o_ref[...] = jnp.zeros_like(o_ref)


def _experts(tile_e1, tile_e2, nused, src3, x1, wg, wu, wd):
    ntiles = src3.shape[0]
    wspec_up = lambda which: pl.BlockSpec((1, D_MODEL, D_EXPERT), lambda s, e1, e2, nu: ((e1, e2)[which][s], 0, 0))
    wspec_dn = lambda which: pl.BlockSpec((1, D_EXPERT, D_MODEL), lambda s, e1, e2, nu: ((e1, e2)[which][s], 0, 0))
    grid_spec = pltpu.PrefetchScalarGridSpec(
        num_scalar_prefetch=3,
        grid=(ntiles,),
        in_specs=[
            pl.BlockSpec((1, 1, MOE_TILE), lambda s, e1, e2, nu: (s, 0, 0), memory_space=pltpu.SMEM),
            pl.BlockSpec((1, 1, MOE_TILE), lambda s, e1, e2, nu: (jnp.minimum(s + 1, ntiles - 1), 0, 0),
                         memory_space=pltpu.SMEM),
            pl.BlockSpec(memory_space=pl.ANY),
            wspec_up(0), wspec_up(0), wspec_dn(0), wspec_up(1), wspec_up(1), wspec_dn(1),
        ],
        out_specs=pl.BlockSpec((MOE_TILE, D_MODEL), lambda s, e1, e2, nu: (s, 0)),
        scratch_shapes=[pltpu.VMEM((2, MOE_TILE, X1_COLS), F32), pltpu.SemaphoreType.DMA((2,))],
    )
    return pl.pallas_call(
        _experts_body,
        grid_spec=grid_spec,
        out_shape=jax.ShapeDtypeStruct((ntiles * MOE_TILE, D_MODEL), F32),
        compiler_params=_cparams(1),
        name="moe_experts",
    )(tile_e1, tile_e2, nused, src3, src3, x1, wg, wu, wd, wg, wu, wd)


def _final_body(pos_ref, pos_next_ref, y_hbm, x_ref, g_ref, b_ref, o_ref, ob_ref, ybuf, sem, *, tm):
    s = pl.program_id(0)
    slot = lax.rem(s, 2)

    @pl.when(s == 0)
    def _():
        _gather_start(pos_ref, y_hbm, ybuf.at[0], sem.at[0], tm)

    @pl.when(s + 1 < pl.num_programs(0))
    def _():
        _gather_start(pos_next_ref, y_hbm, ybuf.at[1 - slot], sem.at[1 - slot], tm)

    _gather_wait(y_hbm, ybuf.at[slot], sem.at[slot], tm)
    x2 = _layer_norm(DN_ALPHA * x_ref[...] + ybuf[slot], g_ref[...], b_ref[...])
    o_ref[...] = x2
    ob_ref[...] = x2.astype(BF16)


def _final(pos3, y_sorted, x1, ln_g, ln_b, *, tm):
    t_tokens = x1.shape[0]
    nsteps = t_tokens // tm
    return pl.pallas_call(
        functools.partial(_final_body, tm=tm),
        grid=(nsteps,),
        in_specs=[
            pl.BlockSpec((1, 1, tm), lambda i: (i, 0, 0), memory_space=pltpu.SMEM),
            pl.BlockSpec((1, 1, tm), lambda i: (jnp.minimum(i + 1, nsteps - 1), 0, 0), memory_space=pltpu.SMEM),
            pl.BlockSpec(memory_space=pl.ANY),
            pl.BlockSpec((tm, D_MODEL), lambda i: (i, 0)),
            pl.BlockSpec((1, D_MODEL), lambda i: (0, 0)),
            pl.BlockSpec((1, D_MODEL), lambda i: (0, 0)),
        ],
        out_specs=[pl.BlockSpec((tm, D_MODEL), lambda i: (i, 0)), pl.BlockSpec((tm, D_MODEL), lambda i: (i, 0))],
        out_shape=[jax.ShapeDtypeStruct((t_tokens, D_MODEL), F32), jax.ShapeDtypeStruct((t_tokens, D_MODEL), BF16)],
        scratch_shapes=[pltpu.VMEM((2, tm, D_MODEL), F32), pltpu.SemaphoreType.DMA((2,))],
        compiler_params=_cparams(1),
        name="moe_combine_ln",
    )(pos3, pos3, y_sorted, x1, ln_g, ln_b)


def _rope_tables(seq):
    inv = 1.0 / (ROPE_THETA ** (jnp.arange(0, HEAD_DIM, 2, dtype=F32) / HEAD_DIM))
    ang = jnp.arange(seq, dtype=F32)[:, None] * inv[None, :]
    return jnp.cos(ang), jnp.sin(ang)


def _halves_layout(w_heads):
    d, n = w_heads.shape
    t = w_heads.reshape(d, n // PROJ_TN, HEADS_PER_TILE, 2, HEAD_DIM // 2)
    return t.transpose(0, 1, 3, 2, 4).reshape(d, n)


def _reorder_w_in(w):
    pts = np.cumsum((0,) + IN_SIZES)
    sec = [w[:, pts[k]:pts[k + 1]] for k in range(len(IN_SIZES))]
    a_q, a_c, a_iq, a_ik, a_iw, b_q, b_k, b_v, c_q, c_k, c_v, gates = sec
    zeros = lambda n: jnp.zeros((w.shape[0], n), w.dtype)
    w_rope = _halves_layout(jnp.concatenate([a_q, a_iq, b_q, b_k, c_q, c_k, a_ik, zeros(HEAD_DIM)], axis=1))
    w_plain = jnp.concatenate([gates, b_v, a_c, a_iw, zeros(LANES - A_IDX_HEADS), c_v, zeros(LANES)], axis=1)
    assert w_rope.shape[1] == N_ROPE and w_plain.shape[1] == N_PLAIN
    return w_rope.astype(BF16), w_plain.astype(BF16)


def _kv_weights(w_uk, w_uv):
    half = HEAD_DIM // 2
    zeros = lambda n: jnp.zeros((w_uk.shape[0], n), w_uk.dtype)
    return jnp.concatenate([w_uk[:, :half], w_uv, zeros(HALF_TN - half - HEAD_DIM),
                            w_uk[:, half:], zeros(HALF_TN - half)], axis=1).astype(BF16)


def _moe_plan(cls, n_tokens):
    ntiles = n_tokens // MOE_TILE + N_CLASSES
    onehot = (cls[:, None] == jnp.arange(N_CLASSES, dtype=jnp.int32)[None, :]).astype(jnp.int32)
    csum = jnp.cumsum(onehot, axis=0)
    rank = jnp.sum(onehot * csum, axis=1) - 1
    counts = csum[-1]
    ptiles = (counts + MOE_TILE - 1) // MOE_TILE
    tile_end = jnp.cumsum(ptiles)
    tile_start = tile_end - ptiles
    pos = (tile_start[cls] * MOE_TILE + rank).astype(jnp.int32)
    nused = tile_end[-1:].astype(jnp.int32)
    tile_ids = jnp.arange(ntiles, dtype=jnp.int32)
    tile_cls = jnp.minimum(jnp.sum((tile_ids[:, None] >= tile_end[None, :]).astype(jnp.int32), axis=1),
                           N_CLASSES - 1).astype(jnp.int32)
    pair = np.array(PAIRS, dtype=np.int32)
    e_lo = jnp.asarray(np.repeat(np.arange(N_GROUPS), len(PAIRS)) * EXPERTS_PER_GROUP + np.tile(pair[:, 0], N_GROUPS), jnp.int32)
    e_hi = jnp.asarray(np.repeat(np.arange(N_GROUPS), len(PAIRS)) * EXPERTS_PER_GROUP + np.tile(pair[:, 1], N_GROUPS), jnp.int32)
    src = jnp.zeros((ntiles * MOE_TILE,), jnp.int32).at[pos].set(jnp.arange(n_tokens, dtype=jnp.int32))
    return e_lo[tile_cls], e_hi[tile_cls], nused, src.reshape(ntiles, 1, MOE_TILE), pos


def kernel(x, w_in, a_w_uk, a_w_uv, c_sinks, w_branch, w_o, ln1_g, ln1_b, router_w, router_b,
           moe_w_gate, moe_w_up, moe_w_down, ln2_g, ln2_b):
    bsz, seq, _ = x.shape
    n_tokens = bsz * seq
    topk = min(A_TOPK_MAX, seq // 4)
    nkb = seq // B_BLOCK
    tq = ATTN_TQ
    kc = min(DSA_KC, seq)
    tm_proj = min(PROJ_TM, seq)
    tm_row = ROW_TM
    half = HEAD_DIM // 2

    cos, sin = _rope_tables(seq)
    cos_in = jnp.tile(cos, (1, HEADS_PER_TILE))
    sin_in = jnp.tile(sin, (1, HEADS_PER_TILE))
    cos_kv = jnp.concatenate([cos, jnp.ones((seq, HALF_TN - half), F32)], axis=1)
    sin_kv = jnp.concatenate([sin, jnp.zeros((seq, HALF_TN - half), F32)], axis=1)
    rwT = router_w.T
    rb = router_b.reshape(N_EXPERTS, 1)

    xf = x.reshape(n_tokens, D_MODEL)
    xb = xf.astype(BF16)
    for l in range(DEPTH):
        w_rope, w_plain = _reorder_w_in(w_in[l])
        h_rope = _proj_rope(xb, 0, D_MODEL, w_rope, cos_in, sin_in, tm=tm_proj, seq=seq)
        h_plain = _proj_plain(xb, w_plain, tm=tm_proj)
        kv = _proj_rope(h_plain, COL_AC // A_KV_RANK, A_KV_RANK, _kv_weights(a_w_uk[l], a_w_uv[l]),
                        cos_kv, sin_kv, tm=tm_proj, seq=seq)

        bvT, cvT, kvT = _values_T(h_plain, kv, n_tokens=n_tokens)

        o_a = _dsa(h_rope, h_plain, kv, kvT, bsz=bsz, seq=seq, tq=min(DSA_TQ, seq), kc=kc, topk=topk)
        kmean = _moba_kmean(h_rope, n_tokens=n_tokens).reshape(bsz, nkb, B_W)
        sel = _moba_gate(h_rope, kmean, bsz=bsz, seq=seq, tq=min(GATE_TQ, seq))
        o_b = _moba(h_rope, bvT, sel, bsz=bsz, seq=seq)
        o_c = _swa(c_sinks[l], h_rope, cvT, bsz=bsz, seq=seq, tq=tq)

        wb_all = w_branch[l].astype(BF16)
        x1, cls = _merge(o_a, o_b, o_c, h_plain, xf,
                              wb_all[:A_W], wb_all[A_W:A_W + B_W], wb_all[A_W + B_W:], w_o[l].astype(BF16),
                              ln1_g[l].reshape(1, D_MODEL), ln1_b[l].reshape(1, D_MODEL), rwT, rb, tm=tm_row)

        e1, e2, nused, src3, pos = _moe_plan(cls[0], n_tokens)
        y_sorted = _experts(e1, e2, nused, src3, x1,
                            moe_w_gate[l].astype(BF16), moe_w_up[l].astype(BF16), moe_w_down[l].astype(BF16))
        xf, xb = _final(pos.reshape(n_tokens // tm_row, 1, tm_row), y_sorted, x1,
                        ln2_g[l].reshape(1, D_MODEL), ln2_b[l].reshape(1, D_MODEL), tm=tm_row)
    return xf.reshape(bsz, seq, D_MODEL)
```

```python
import functools
import math

import jax
import jax.numpy as jnp
import numpy as np
from jax import lax
from jax.experimental import pallas as pl
from jax.experimental.pallas import tpu as pltpu

D_MODEL = 1024
DEPTH = 2
HEAD_DIM = 64
ROPE_THETA = 10000.0
LN_EPS = 1e-5
A_HEADS = 4
A_KV_RANK = 128
A_IDX_HEADS = 4
A_IDX_DIM = 64
A_TOPK_MAX = 256
B_HEADS = 4
B_BLOCK = 256
B_TOPK_BLOCKS = 3
C_HEADS = 8
C_KV_HEADS = 2
C_WINDOW = 128
N_BRANCH = 3
A_W = A_HEADS * HEAD_DIM
B_W = B_HEADS * HEAD_DIM
C_W = C_HEADS * HEAD_DIM
IN_SIZES = (A_W, A_KV_RANK, A_IDX_HEADS * A_IDX_DIM, A_IDX_DIM, A_IDX_HEADS,
            B_W, B_W, B_W, C_W, C_KV_HEADS * HEAD_DIM, C_KV_HEADS * HEAD_DIM,
            N_BRANCH * D_MODEL)
N_EXPERTS = 16
N_GROUPS = 4
EXPERTS_PER_GROUP = 4
D_EXPERT = 512
DN_ALPHA = (2 * DEPTH) ** 0.25
ATTN_SCALE = HEAD_DIM ** -0.5
LOG2E = math.log2(math.e)
Q_SCALE = ATTN_SCALE * LOG2E

LANES = 128
SUBLANES = 8
VMEM_LIMIT_BYTES = 56 * 1024 * 1024

NEG_BIG = -1e30
BF16 = jnp.bfloat16
F32 = jnp.float32

PROJ_TN = 256
HALF_TN = PROJ_TN // 2
HEADS_PER_TILE = PROJ_TN // HEAD_DIM
TILE_AQ = 0
TILE_AIQ = 1
TILE_BQ = 2
TILE_BK = 3
TILE_CQ = 4
TILE_CK_AIK = 6
N_ROPE = 7 * PROJ_TN
COL_GATES = 0
COL_BV = 3072
COL_AC = 3328
COL_AIW = 3456
COL_CV = 3584
N_PLAIN = 3840
PLAIN_TN = 1920
KV_VA0 = HEAD_DIM // 2

PAIRS = ((0, 1), (0, 2), (0, 3), (1, 2), (1, 3), (2, 3))
N_CLASSES = N_GROUPS * len(PAIRS)
MOE_TILE = 256
X1_COLS = D_MODEL + LANES

PROJ_TM = 1024
PROJ_SUB = 256
ROW_TM = 512
ATTN_TQ = 256
DSA_TQ = 512
DSA_KC = 256
GATE_TQ = 1024
assert COL_GATES == 0

VALUE_MID_PASSES = 16
PEEL_MAX = 2.0
MIN_NORMAL_KEY = 0x00800000
MAX_SELECT_PASSES = 96


def _cparams(n_axes):
    return pltpu.CompilerParams(dimension_semantics=("arbitrary",) * n_axes,
                                vmem_limit_bytes=VMEM_LIMIT_BYTES)


def _proj_rope_body(x_ref, w_ref, cos_ref, sin_ref, o_ref, *, n_sub):
    sub = x_ref.shape[0] // n_sub
    for t in range(w_ref.shape[0]):
        for mi in range(n_sub):
            rows = slice(mi * sub, (mi + 1) * sub)
            acc = jnp.dot(x_ref[rows, :], w_ref[t], preferred_element_type=F32)
            a1, a2 = acc[:, :HALF_TN], acc[:, HALF_TN:]
            c, sn = cos_ref[rows, :], sin_ref[rows, :]
            c0 = t * PROJ_TN
            o_ref[rows, c0:c0 + HALF_TN] = (a1 * c - a2 * sn).astype(o_ref.dtype)
            o_ref[rows, c0 + HALF_TN:c0 + PROJ_TN] = (a2 * c + a1 * sn).astype(o_ref.dtype)


def _proj_plain_body(x_ref, w_ref, o_ref, *, n_sub):
    sub = x_ref.shape[0] // n_sub
    tn = o_ref.shape[1]
    is_gate = (pl.program_id(1) * tn + lax.broadcasted_iota(jnp.int32, (1, tn), 1)) < COL_GATES + N_BRANCH * D_MODEL
    for mi in range(n_sub):
        rows = slice(mi * sub, (mi + 1) * sub)
        acc = jnp.dot(x_ref[rows, :], w_ref[0], preferred_element_type=F32)
        o_ref[rows, :] = jnp.where(is_gate, jax.nn.sigmoid(acc), acc).astype(o_ref.dtype)


def _column_tiles(w, tn=PROJ_TN):
    k_dim, n = w.shape
    return w.reshape(k_dim, n // tn, tn).transpose(1, 0, 2)


def _proj_rope(x, x_col_block, k_dim, w, cos_t, sin_t, *, tm, seq):
    t_tokens = x.shape[0]
    n = w.shape[1]
    pos_blocks = seq // tm
    return pl.pallas_call(
        functools.partial(_proj_rope_body, n_sub=max(tm // PROJ_SUB, 1)),
        grid=(t_tokens // tm,),
        in_specs=[
            pl.BlockSpec((tm, k_dim), lambda i: (i, x_col_block)),
            pl.BlockSpec((n // PROJ_TN, k_dim, PROJ_TN), lambda i: (0, 0, 0)),
            pl.BlockSpec((tm, HALF_TN), lambda i: (i % pos_blocks, 0)),
            pl.BlockSpec((tm, HALF_TN), lambda i: (i % pos_blocks, 0)),
        ],
        out_specs=pl.BlockSpec((tm, n), lambda i: (i, 0)),
        out_shape=jax.ShapeDtypeStruct((t_tokens, n), BF16),
        compiler_params=_cparams(1),
        name="proj_rope",
    )(x, _column_tiles(w), cos_t, sin_t)


def _proj_plain(x, w, *, tm):
    t_tokens, k_dim = x.shape
    n = w.shape[1]
    return pl.pallas_call(
        functools.partial(_proj_plain_body, n_sub=max(tm // PROJ_SUB, 1)),
        grid=(t_tokens // tm, n // PLAIN_TN),
        in_specs=[
            pl.BlockSpec((tm, k_dim), lambda i, j: (i, 0)),
            pl.BlockSpec((1, k_dim, PLAIN_TN), lambda i, j: (j, 0, 0)),
        ],
        out_specs=pl.BlockSpec((tm, PLAIN_TN), lambda i, j: (i, j)),
        out_shape=jax.ShapeDtypeStruct((t_tokens, n), BF16),
        compiler_params=_cparams(2),
        name="proj_plain",
    )(x, _column_tiles(w, PLAIN_TN))


def _values_T_body(bv_ref, cv_ref, kv_ref, bvT_ref, cvT_ref, kvT_ref):
    for src, dst in ((bv_ref, bvT_ref), (cv_ref, cvT_ref), (kv_ref, kvT_ref)):
        for g in range(dst.shape[0]):
            dst[g] = src[g * B_BLOCK:(g + 1) * B_BLOCK, :].astype(F32).T.astype(dst.dtype)


def _values_T(h_plain, kv, *, n_tokens):
    nblk = n_tokens // B_BLOCK
    gb = math.gcd(nblk, SUBLANES)
    in_tile = lambda col: pl.BlockSpec((gb * B_BLOCK, PROJ_TN), lambda i: (i, col // PROJ_TN))
    out_tile = pl.BlockSpec((gb, PROJ_TN, B_BLOCK), lambda i: (i, 0, 0))
    shape = jax.ShapeDtypeStruct((nblk, PROJ_TN, B_BLOCK), BF16)
    return pl.pallas_call(
        _values_T_body,
        grid=(nblk // gb,),
        in_specs=[in_tile(COL_BV), in_tile(COL_CV), in_tile(0)],
        out_specs=[out_tile, out_tile, out_tile],
        out_shape=[shape, shape, shape],
        compiler_params=_cparams(1),
        name="values_transpose",
    )(h_plain, h_plain, kv)


def _f32_to_key(x):
    b = lax.bitcast_convert_type(x, jnp.int32)
    return jnp.where(b < 0, b ^ jnp.int32(0x7FFFFFFF), b)


def _key_to_f32(k):
    b = jnp.where(k < 0, k ^ jnp.int32(0x7FFFFFFF), k)
    return lax.bitcast_convert_type(b, F32)


def _fold8(x, op):
    return op(x.reshape(x.shape[0] // SUBLANES, SUBLANES, x.shape[1]), axis=0)


def _tile_T(x_ref):
    return x_ref[...].astype(F32).T


def _head_rows(xT, h):
    half = HEAD_DIM // 2
    return xT[h * half:(h + 1) * half], xT[HALF_TN + h * half:HALF_TN + (h + 1) * half]


def _place_head(first, second, slot):
    half = HEAD_DIM // 2
    w = first.shape[1]
    before, after = slot * half, HALF_TN - (slot + 1) * half
    pieces = []
    for part in (first, second):
        pieces += [jnp.zeros((before, w), part.dtype)] * (before > 0) + [part] + [jnp.zeros((after, w), part.dtype)] * (after > 0)
    return jnp.concatenate(pieces, axis=0)


def _store_rows(o_ref, oT_heads):
    o_ref[...] = jnp.concatenate(oT_heads, axis=0).T.astype(o_ref.dtype)


IK_SLOT = 2


def _dsa_body(q_ref, iq_ref, iw_ref, ik_ref, kv_ref, vT_ref, o_ref, sc_ref, acc_ref, sbuf_ref, *, tq, kc, topk):
    i = pl.program_id(1)
    q0 = i * tq
    nch = (q0 + tq + kc - 1) // kc
    qpos = q0 + lax.broadcasted_iota(jnp.int32, (1, tq), 1)
    iqT = _tile_T(iq_ref)
    iq_ops = [_place_head(*[p.astype(BF16) for p in _head_rows(iqT, h)], IK_SLOT) for h in range(A_IDX_HEADS)]
    w_idx = iw_ref[...].astype(F32).T[:A_IDX_HEADS]

    def score_chunk(c, carry, diagonal):
        mn, mx = carry
        k0 = pl.multiple_of(c * kc, kc)
        ikc = ik_ref[pl.ds(k0, kc), :]
        acc = jnp.zeros((kc, tq), F32)
        for h in range(A_IDX_HEADS):
            d = jnp.dot(ikc, iq_ops[h], preferred_element_type=F32)
            acc = acc + w_idx[h:h + 1, :] * jnp.maximum(d, 0.0)
        if diagonal:
            causal = (k0 + lax.broadcasted_iota(jnp.int32, (kc, 1), 0)) <= qpos
            lo_part, hi_part = jnp.where(causal, acc, jnp.inf), jnp.where(causal, acc, -jnp.inf)
        else:
            lo_part = hi_part = acc
        sc_ref[pl.ds(k0, kc), :] = hi_part
        return jnp.minimum(mn, _fold8(lo_part, jnp.min)), jnp.maximum(mx, _fold8(hi_part, jnp.max))

    n_full = (q0 + 1) // kc
    carry = lax.fori_loop(0, n_full, functools.partial(score_chunk, diagonal=False),
                          (jnp.full((SUBLANES, tq), jnp.inf, F32), jnp.full((SUBLANES, tq), -jnp.inf, F32)))
    mn8, mx8 = lax.fori_loop(n_full, nch, functools.partial(score_chunk, diagonal=True), carry)
    row_min = jnp.min(mn8, axis=0, keepdims=True)
    row_max = jnp.max(mx8, axis=0, keepdims=True)

    pc = 2 * kc
    npair = (nch + 1) // 2
    n_chunks_total = vT_ref.shape[0]

    @pl.when(nch % 2 == 1)
    def _():
        sc_ref[pl.ds(pl.multiple_of(nch * kc, kc), kc), :] = jnp.full((kc, tq), -jnp.inf, F32)

    def count_ge(thr):
        def body(j, cnt):
            s = sc_ref[pl.ds(pl.multiple_of(j * pc, pc), pc), :]
            return cnt + _fold8(jnp.where(s >= thr, 1.0, 0.0), jnp.sum)
        cnt8 = lax.fori_loop(0, npair, body, jnp.zeros((SUBLANES, tq), F32))
        return jnp.sum(cnt8, axis=0, keepdims=True)

    def min_ge(thr):
        def body(j, mn):
            s = sc_ref[pl.ds(pl.multiple_of(j * pc, pc), pc), :]
            return jnp.minimum(mn, _fold8(jnp.where(s >= thr, s, jnp.inf), jnp.min))
        mn8 = lax.fori_loop(0, npair, body, jnp.full((SUBLANES, tq), jnp.inf, F32))
        return jnp.min(mn8, axis=0, keepdims=True)

    def count_zero():
        def body(j, carry):
            c0, cp = carry
            s = sc_ref[pl.ds(pl.multiple_of(j * pc, pc), pc), :]
            return (c0 + _fold8(jnp.where(s >= 0.0, 1.0, 0.0), jnp.sum),
                    cp + _fold8(jnp.where(s > 0.0, 1.0, 0.0), jnp.sum))
        z8 = jnp.zeros((SUBLANES, tq), F32)
        c0, cp = lax.fori_loop(0, npair, body, (z8, z8))
        return jnp.sum(c0, axis=0, keepdims=True), jnp.sum(cp, axis=0, keepdims=True)

    kf = float(topk)
    n_valid = (qpos + 1).astype(F32)
    few = n_valid <= kf
    cnt_nonneg, cnt_pos = count_zero()
    is_zero = jnp.logical_and(cnt_pos < kf, cnt_nonneg >= kf)
    is_pos = cnt_pos >= kf
    lo0 = jnp.where(is_zero, 0, jnp.where(is_pos, MIN_NORMAL_KEY, _f32_to_key(row_min)))
    hi0 = jnp.where(is_zero, 1, jnp.where(is_pos, _f32_to_key(row_max) + 1, -1))
    cnt0 = jnp.where(is_zero, cnt_nonneg, jnp.where(is_pos, cnt_pos, n_valid))
    lo0 = jnp.where(few, _f32_to_key(row_min), lo0)
    hi0 = jnp.where(few, lo0 + 1, hi0)

    def unfinished(lo, hi, cnt_lo):
        return jnp.logical_and(lo + 1 != hi, cnt_lo != kf)

    def surplus(lo, hi, cnt_lo):
        return jnp.max(jnp.where(unfinished(lo, hi, cnt_lo), cnt_lo - kf, 0.0))

    def bis_cond(st):
        _, it, excess = st
        return jnp.logical_and(excess > 0.0, it < MAX_SELECT_PASSES)

    def bisect_once(st):
        lo, hi, cnt_lo, it = st
        nd = unfinished(lo, hi, cnt_lo)
        mid_val = _f32_to_key(0.5 * _key_to_f32(lo) + 0.5 * _key_to_f32(hi))
        mid_key = lo + lax.shift_right_logical(hi - lo, 1)
        mid = jnp.where(it < VALUE_MID_PASSES, jnp.clip(mid_val, lo + 1, hi - 1), mid_key)
        cnt = count_ge(_key_to_f32(mid))
        ge = cnt >= kf
        up = jnp.logical_and(nd, ge)
        down = jnp.logical_and(nd, jnp.logical_not(ge))
        return jnp.where(up, mid, lo), jnp.where(down, mid, hi), jnp.where(up, cnt, cnt_lo), it + 1

    def bisect(st):
        return bisect_once(bisect_once(st))

    def peel(st):
        lo, hi, cnt_lo, it = st
        nd = unfinished(lo, hi, cnt_lo)
        k1 = jnp.maximum(_f32_to_key(min_ge(_key_to_f32(lo))), lo)
        cnt = count_ge(_key_to_f32(k1 + 1))
        ge = cnt >= kf
        up = jnp.logical_and(nd, ge)
        down = jnp.logical_and(nd, jnp.logical_not(ge))
        new_lo = jnp.where(nd, jnp.where(ge, k1 + 1, k1), lo)
        return new_lo, jnp.where(down, k1 + 1, hi), jnp.where(up, cnt, cnt_lo), it + 2

    def bis_body(st):
        (lo, hi, cnt_lo), it, excess = st
        lo, hi, cnt_lo, it = lax.cond(excess > PEEL_MAX, bisect, peel, (lo, hi, cnt_lo, it))
        return (lo, hi, cnt_lo), it, surplus(lo, hi, cnt_lo)

    (lo, _, cnt_lo), _, _ = lax.while_loop(bis_cond, bis_body,
                                           ((lo0, hi0, cnt0), jnp.int32(0), surplus(lo0, hi0, cnt0)))
    thr = _key_to_f32(lo)

    tie = jnp.logical_and(cnt_lo > kf, jnp.logical_not(few))

    @pl.when(jnp.max(jnp.where(tie, 1.0, 0.0)) > 0.0)
    def _():
        def count_gt():
            def gt_body(c, cnt):
                k0 = pl.multiple_of(c * kc, kc)
                s = sc_ref[pl.ds(k0, kc), :]
                return cnt + _fold8(jnp.where(s > thr, 1.0, 0.0), jnp.sum)
            gt8 = lax.fori_loop(0, nch, gt_body, jnp.zeros((SUBLANES, tq), F32))
            return jnp.sum(gt8, axis=0, keepdims=True)

        zero_thr = thr == 0.0
        other = jnp.max(jnp.where(jnp.logical_and(tie, jnp.logical_not(zero_thr)), 1.0, 0.0)) > 0.0
        gt = lax.cond(other, count_gt, lambda: cnt_pos)
        need = kf - jnp.where(zero_thr, cnt_pos, gt)
        r = lax.broadcasted_iota(jnp.int32, (kc, kc), 0)
        cidx = lax.broadcasted_iota(jnp.int32, (kc, kc), 1)
        tri = jnp.where(cidx <= r, 1.0, 0.0).astype(BF16)

        thr_tie = jnp.where(tie, thr, jnp.nan)
        keep = jnp.where(tie, need, jnp.inf)

        def tie_body(c, seen):
            k0 = pl.multiple_of(c * kc, kc)
            s = sc_ref[pl.ds(k0, kc), :]
            eqf = jnp.where(s == thr_tie, 1.0, 0.0)
            rank = jnp.dot(tri, eqf.astype(BF16), preferred_element_type=F32) + seen
            sc_ref[pl.ds(k0, kc), :] = jnp.where(eqf * rank > keep, -jnp.inf, s)
            return seen + jnp.sum(eqf, axis=0, keepdims=True)

        lax.fori_loop(0, nch, tie_body, jnp.zeros((1, tq), F32))

    qT_all = (_tile_T(q_ref) * Q_SCALE).astype(BF16)
    qT = jnp.concatenate([_place_head(*_head_rows(qT_all, h), 0) for h in range(A_HEADS)], axis=1)
    acc_ref[...] = jnp.zeros_like(acc_ref)
    ones_rows = jnp.ones((SUBLANES, kc), BF16)

    def issue_scores(c, slot):
        k0 = pl.multiple_of(jnp.minimum(c, n_chunks_total - 1) * kc, kc)
        sbuf_ref[slot] = jnp.dot(kv_ref[pl.ds(k0, kc), :], qT, preferred_element_type=F32)

    def softmax_pv(c, slot, m):
        k0 = pl.multiple_of(c * kc, kc)
        v1 = jnp.concatenate([vT_ref[c, KV_VA0:KV_VA0 + HEAD_DIM, :], ones_rows], axis=0)
        sel = sc_ref[pl.ds(k0, kc), :] >= thr
        s = jnp.concatenate([jnp.where(sel, sbuf_ref[slot, :, h * tq:(h + 1) * tq], NEG_BIG)
                             for h in range(A_HEADS)], axis=1)
        m_new = jnp.maximum(m, jnp.max(s, axis=0, keepdims=True))
        alpha = jnp.exp2(m - m_new)
        p = jnp.exp2((s - m_new).astype(BF16))
        acc_ref[...] = acc_ref[...] * alpha + jnp.dot(v1, p, preferred_element_type=F32)
        return m_new

    def chunk_pair(j, m):
        c0 = 2 * j
        issue_scores(c0 + 1, 1)
        m = softmax_pv(c0, 0, m)
        issue_scores(c0 + 2, 0)
        return softmax_pv(c0 + 1, 1, m)

    issue_scores(0, 0)
    lax.fori_loop(0, npair, chunk_pair, jnp.full((1, A_HEADS * tq), NEG_BIG, F32))
    a = acc_ref[...]
    o = a[:HEAD_DIM] / a[HEAD_DIM:HEAD_DIM + 1]
    _store_rows(o_ref, [o[:, h * tq:(h + 1) * tq] for h in range(A_HEADS)])


def _dsa(h_rope, h_plain, kv, kvT, *, bsz, seq, tq, kc, topk):
    nq = seq // tq
    assert kc == B_BLOCK
    return pl.pallas_call(
        functools.partial(_dsa_body, tq=tq, kc=kc, topk=topk),
        grid=(bsz, nq),
        in_specs=[
            pl.BlockSpec((tq, PROJ_TN), lambda b, i: (b * nq + i, TILE_AQ)),
            pl.BlockSpec((tq, PROJ_TN), lambda b, i: (b * nq + i, TILE_AIQ)),
            pl.BlockSpec((tq, LANES), lambda b, i: (b * nq + i, COL_AIW // LANES)),
            pl.BlockSpec((seq, PROJ_TN), lambda b, i: (b, TILE_CK_AIK)),
            pl.BlockSpec((seq, PROJ_TN), lambda b, i: (b, 0)),
            pl.BlockSpec((seq // kc, PROJ_TN, kc), lambda b, i: (b, 0, 0)),
        ],
        out_specs=pl.BlockSpec((tq, A_W), lambda b, i: (b * nq + i, 0)),
        out_shape=jax.ShapeDtypeStruct((bsz * seq, A_W), BF16),
        scratch_shapes=[pltpu.VMEM((seq + kc, tq), F32),
                        pltpu.VMEM((HEAD_DIM + SUBLANES, A_HEADS * tq), F32),
                        pltpu.VMEM((2, kc, A_HEADS * tq), F32)],
        compiler_params=_cparams(2),
        name="dsa_attention",
    )(h_rope, h_rope, h_plain, h_rope, kv, kvT)


def _kmean_body(k_ref, o_ref):
    gb = o_ref.shape[0]
    o_ref[...] = jnp.mean(k_ref[...].astype(F32).reshape(gb, B_BLOCK, B_W), axis=1)


def _moba_kmean(h_rope, *, n_tokens):
    nblk = n_tokens // B_BLOCK
    gb = math.gcd(nblk, SUBLANES)
    return pl.pallas_call(
        _kmean_body,
        grid=(nblk // gb,),
        in_specs=[pl.BlockSpec((gb * B_BLOCK, B_W), lambda i: (i, TILE_BK))],
        out_specs=pl.BlockSpec((gb, B_W), lambda i: (i, 0)),
        out_shape=jax.ShapeDtypeStruct((nblk, B_W), F32),
        compiler_params=_cparams(1),
        name="moba_kmean",
    )(h_rope)


def _moba_gate_body(q_ref, km_ref, sel_ref, *, tq, nkb, nsel):
    i = pl.program_id(1)
    qpos = i * tq + lax.broadcasted_iota(jnp.int32, (1, tq), 1)
    cur = qpos // B_BLOCK
    row = lax.broadcasted_iota(jnp.int32, (nkb, tq), 0)
    past = row < cur
    qT = _tile_T(q_ref).astype(BF16)
    km = km_ref[0]
    col_head = (lax.broadcasted_iota(jnp.int32, km.shape, 1) % HALF_TN) // (HEAD_DIM // 2)
    for h in range(B_HEADS):
        km_h = jnp.where(col_head == h, km, 0.0).astype(BF16)
        g = jnp.dot(km_h, qT, preferred_element_type=F32)
        g = jnp.where(past, g, -jnp.inf)
        sel = jnp.zeros((nkb, tq), F32)
        for _ in range(nsel):
            mx = jnp.max(g, axis=0, keepdims=True)
            idx = jnp.min(jnp.where(g == mx, row, nkb), axis=0, keepdims=True)
            pick = row == idx
            sel = jnp.where(pick, 1.0, sel)
            g = jnp.where(pick, -jnp.inf, g)
        sel_ref[0, h] = jnp.where(past, sel, 0.0)


def _moba_gate(h_rope, kmean, *, bsz, seq, tq):
    nkb = seq // B_BLOCK
    nsel = min(B_TOPK_BLOCKS, nkb)
    nq = seq // tq
    return pl.pallas_call(
        functools.partial(_moba_gate_body, tq=tq, nkb=nkb, nsel=nsel),
        grid=(bsz, nq),
        in_specs=[
            pl.BlockSpec((tq, PROJ_TN), lambda b, i: (b * nq + i, TILE_BQ)),
            pl.BlockSpec((1, nkb, B_W), lambda b, i: (b, 0, 0)),
        ],
        out_specs=pl.BlockSpec((1, B_HEADS, nkb, tq), lambda b, i: (b, 0, 0, i)),
        out_shape=jax.ShapeDtypeStruct((bsz, B_HEADS, nkb, seq), F32),
        compiler_params=_cparams(2),
        name="moba_gate",
    )(h_rope, kmean)


def _moba_body(q_ref, k_ref, vT_ref, sel_ref, o_ref, acc_ref, sbuf_ref, *, tq):
    i = pl.program_id(1)
    qT_all = (_tile_T(q_ref) * Q_SCALE).astype(BF16)
    qT = [_place_head(*_head_rows(qT_all, h), h) for h in range(B_HEADS)]
    ones_rows = jnp.ones((SUBLANES, B_BLOCK), BF16)
    n_past = i

    def values(h, j):
        return jnp.concatenate([vT_ref[j, h * HEAD_DIM:(h + 1) * HEAD_DIM, :], ones_rows], axis=0)

    def issue_scores(j, slot):
        k0 = pl.multiple_of(jnp.minimum(j, i) * B_BLOCK, B_BLOCK)
        for h in range(B_HEADS):
            sbuf_ref[slot, h] = jnp.dot(k_ref[pl.ds(k0, B_BLOCK), :], qT[h], preferred_element_type=F32)

    def softmax_pv(j, slot, ms):
        ps, alphas, new_ms = [], [], []
        for h in range(B_HEADS):
            s = jnp.where(sel_ref[0, h, pl.ds(j, 1), :] > 0.0, sbuf_ref[slot, h], NEG_BIG)
            m_new = jnp.maximum(ms[h], jnp.max(s, axis=0, keepdims=True))
            alphas.append(jnp.exp2(ms[h] - m_new))
            ps.append(jnp.exp2((s - m_new).astype(BF16)))
            new_ms.append(m_new)
        for h in range(B_HEADS):
            acc_ref[h] = acc_ref[h] * alphas[h] + jnp.dot(values(h, j), ps[h], preferred_element_type=F32)
        return tuple(new_ms)

    own0 = pl.multiple_of(i * B_BLOCK, B_BLOCK)
    s_own = [jnp.dot(k_ref[pl.ds(own0, B_BLOCK), :], qT[h], preferred_element_type=F32) for h in range(B_HEADS)]
    issue_scores(0, 0)
    causal = lax.broadcasted_iota(jnp.int32, (B_BLOCK, tq), 0) <= lax.broadcasted_iota(jnp.int32, (B_BLOCK, tq), 1)
    ms = []
    for h in range(B_HEADS):
        s = jnp.where(causal, s_own[h], NEG_BIG)
        m0 = jnp.max(s, axis=0, keepdims=True)
        acc_ref[h] = jnp.dot(values(h, i), jnp.exp2((s - m0).astype(BF16)), preferred_element_type=F32)
        ms.append(m0)

    def block_pair(jj, ms):
        j0 = 2 * jj
        issue_scores(j0 + 1, 1)
        ms = softmax_pv(j0, 0, ms)
        issue_scores(j0 + 2, 0)
        return softmax_pv(j0 + 1, 1, ms)

    lax.fori_loop(0, (n_past + 1) // 2, block_pair, tuple(ms))
    outs = []
    for h in range(B_HEADS):
        a = acc_ref[h]
        outs.append(a[:HEAD_DIM] / a[HEAD_DIM:HEAD_DIM + 1])
    _store_rows(o_ref, outs)


def _moba(h_rope, bvT, sel, *, bsz, seq):
    tq = B_BLOCK
    nkb = seq // B_BLOCK
    nq = seq // tq
    return pl.pallas_call(
        functools.partial(_moba_body, tq=tq),
        grid=(bsz, nq),
        in_specs=[
            pl.BlockSpec((tq, PROJ_TN), lambda b, i: (b * nq + i, TILE_BQ)),
            pl.BlockSpec((seq, PROJ_TN), lambda b, i: (b, TILE_BK)),
            pl.BlockSpec((nkb, B_W, B_BLOCK), lambda b, i: (b, 0, 0)),
            pl.BlockSpec((1, B_HEADS, nkb, tq), lambda b, i: (b, 0, 0, i)),
        ],
        out_specs=pl.BlockSpec((tq, B_W), lambda b, i: (b * nq + i, 0)),
        out_shape=jax.ShapeDtypeStruct((bsz * seq, B_W), BF16),
        scratch_shapes=[pltpu.VMEM((B_HEADS, HEAD_DIM + SUBLANES, tq), F32),
                        pltpu.VMEM((2, B_HEADS, B_BLOCK, tq), F32)],
        compiler_params=_cparams(2),
        name="moba_attention",
    )(h_rope, h_rope, bvT, sel)


def _swa_body(sink_ref, q0_ref, q1_ref, k0_ref, k1_ref, k2_ref, vprev_ref, vcur_ref, o_ref, *, tq):
    i = pl.program_id(1)
    q0 = i * tq
    qpos = q0 + lax.broadcasted_iota(jnp.int32, (1, tq), 1)
    nk = tq + C_WINDOW
    kpos = q0 - C_WINDOW + lax.broadcasted_iota(jnp.int32, (nk, 1), 0)
    diff = qpos - kpos
    ok = jnp.where(diff >= 0, jnp.where(diff < C_WINDOW, jnp.where(kpos >= 0, 1.0, 0.0), 0.0), 0.0) > 0.0
    ones_rows = jnp.ones((SUBLANES, nk), BF16)
    group = C_HEADS // C_KV_HEADS
    assert group == HEADS_PER_TILE
    kk = jnp.concatenate([k0_ref[...], k1_ref[...], k2_ref[...]], axis=0)
    outs = []
    scores = []
    for kv, q_ref in enumerate((q0_ref, q1_ref)):
        qT_all = (_tile_T(q_ref) * Q_SCALE).astype(BF16)
        qT = jnp.concatenate([_place_head(*_head_rows(qT_all, g), kv) for g in range(group)], axis=1)
        scores.append(jnp.dot(kk, qT, preferred_element_type=F32))
    for kv in range(C_KV_HEADS):
        vrows = slice(kv * HEAD_DIM, (kv + 1) * HEAD_DIM)
        vv = jnp.concatenate([vprev_ref[0, vrows, tq - C_WINDOW:], vcur_ref[0, vrows, :]], axis=1)
        v1 = jnp.concatenate([vv, ones_rows], axis=0)
        heads = [kv * group + g for g in range(group)]
        sink = jnp.concatenate([jnp.full((1, tq), sink_ref[hd] * LOG2E, F32) for hd in heads], axis=1)
        s = jnp.concatenate([jnp.where(ok, scores[kv][:, g * tq:(g + 1) * tq], NEG_BIG) for g in range(group)], axis=1)
        m = jnp.maximum(jnp.max(s, axis=0, keepdims=True), sink)
        p = jnp.exp2((s - m).astype(BF16))
        a = jnp.dot(v1, p, preferred_element_type=F32)
        o = a[:HEAD_DIM] / (a[HEAD_DIM:HEAD_DIM + 1] + jnp.exp2(sink - m))
        outs += [o[:, g * tq:(g + 1) * tq] for g in range(group)]
    _store_rows(o_ref, outs)


def _swa(sinks, h_rope, cvT, *, bsz, seq, tq):
    r = tq // C_WINDOW
    assert r == 2 and tq == B_BLOCK
    nq = seq // tq
    nwb = seq // C_WINDOW

    def kspec(off):
        return pl.BlockSpec((C_WINDOW, PROJ_TN),
                            lambda b, i, s: (b * nwb + jnp.maximum(i * r + off, 0), TILE_CK_AIK))

    def vspec(off):
        return pl.BlockSpec((1, PROJ_TN, B_BLOCK), lambda b, i, s: (b * nq + jnp.maximum(i + off, 0), 0, 0))

    grid_spec = pltpu.PrefetchScalarGridSpec(
        num_scalar_prefetch=1,
        grid=(bsz, nq),
        in_specs=[pl.BlockSpec((tq, PROJ_TN), lambda b, i, s: (b * nq + i, TILE_CQ)),
                  pl.BlockSpec((tq, PROJ_TN), lambda b, i, s: (b * nq + i, TILE_CQ + 1)),
                  kspec(-1), kspec(0), kspec(1), vspec(-1), vspec(0)],
        out_specs=pl.BlockSpec((tq, C_W), lambda b, i, s: (b * nq + i, 0)),
    )
    return pl.pallas_call(
        functools.partial(_swa_body, tq=tq),
        grid_spec=grid_spec,
        out_shape=jax.ShapeDtypeStruct((bsz * seq, C_W), BF16),
        compiler_params=_cparams(2),
        name="swa_attention",
    )(sinks, h_rope, h_rope, h_rope, h_rope, h_rope, cvT, cvT)


def _layer_norm(z, g, b):
    mu = jnp.mean(z, axis=-1, keepdims=True)
    zc = z - mu
    var = jnp.mean(zc * zc, axis=-1, keepdims=True)
    return zc * lax.rsqrt(var + LN_EPS) * g + b


def _split_bf16(a):
    hi = a.astype(BF16)
    lo = (a - hi.astype(F32)).astype(BF16)
    return hi, lo


def _router(x1, rw_ref, rb_ref, cls_ref):
    nt = (((1,), (1,)), ((), ()))
    xh, xl = _split_bf16(x1)
    wh, wl = _split_bf16(rw_ref[...])
    logits = (lax.dot_general(wh, xh, nt, preferred_element_type=F32)
              + lax.dot_general(wh, xl, nt, preferred_element_type=F32)
              + lax.dot_general(wl, xh, nt, preferred_element_type=F32))
    aff = jax.nn.sigmoid(logits)
    score = aff + rb_ref[...]
    tm = x1.shape[0]
    sc = [score[e:e + 1, :] for e in range(N_EXPERTS)]
    af = [aff[e:e + 1, :] for e in range(N_EXPERTS)]
    in_top = []
    grp_score = []
    for gq in range(N_GROUPS):
        gs = jnp.zeros((1, tm), F32)
        for a in range(EXPERTS_PER_GROUP):
            ea = gq * EXPERTS_PER_GROUP + a
            rank = jnp.zeros((1, tm), F32)
            for b in range(EXPERTS_PER_GROUP):
                if b == a:
                    continue
                eb = gq * EXPERTS_PER_GROUP + b
                beats = (sc[eb] >= sc[ea]) if b < a else (sc[eb] > sc[ea])
                rank = rank + jnp.where(beats, 1.0, 0.0)
            top = rank < 2.0
            in_top.append(top)
            gs = gs + jnp.where(top, sc[ea], 0.0)
        grp_score.append(gs)
    best = grp_score[0]
    gstar = jnp.zeros((1, tm), jnp.int32)
    for gq in range(1, N_GROUPS):
        better = grp_score[gq] > best
        best = jnp.where(better, grp_score[gq], best)
        gstar = jnp.where(better, gq, gstar)
    cls = jnp.zeros((1, tm), jnp.int32)
    w_lo = jnp.zeros((1, tm), F32)
    w_hi = jnp.zeros((1, tm), F32)
    for gq in range(N_GROUPS):
        is_g = gstar == gq
        for pi, (a, b) in enumerate(PAIRS):
            ea, eb = gq * EXPERTS_PER_GROUP + a, gq * EXPERTS_PER_GROUP + b
            hit = jnp.where(is_g, jnp.where(in_top[ea], jnp.where(in_top[eb], 1.0, 0.0), 0.0), 0.0) > 0.0
            cls = jnp.where(hit, gq * len(PAIRS) + pi, cls)
            tot = af[ea] + af[eb]
            w_lo = jnp.where(hit, af[ea] / tot, w_lo)
            w_hi = jnp.where(hit, af[eb] / tot, w_hi)
    cls_ref[...] = cls
    return w_lo, w_hi


def _merge_body(oa_ref, ob_ref, oc_ref, ga_ref, gb_ref, gc_ref, x_ref, wa_ref, wb_ref, wc_ref, wo_ref,
                g_ref, b_ref, rw_ref, rb_ref, x1_ref, cls_ref):
    def branch(o_ref, w_ref, gate_ref):
        y = jnp.dot(o_ref[...], w_ref[...], preferred_element_type=F32)
        return gate_ref[...].astype(F32) * y

    merged = branch(oa_ref, wa_ref, ga_ref) + branch(ob_ref, wb_ref, gb_ref) + branch(oc_ref, wc_ref, gc_ref)
    y = jnp.dot(merged.astype(BF16), wo_ref[...], preferred_element_type=F32)
    x1 = _layer_norm(DN_ALPHA * x_ref[...] + y, g_ref[...], b_ref[...])
    w_lo, w_hi = _router(x1, rw_ref, rb_ref, cls_ref)
    x1_ref[:, :D_MODEL] = x1
    x1_ref[:, D_MODEL:] = jnp.concatenate([w_lo, w_hi, jnp.zeros((LANES - 2, x1.shape[0]), F32)], axis=0).T


def _merge(oa, ob, oc, h_plain, x, wa, wb, wc, wo, ln_g, ln_b, rwT, rb, *, tm):
    t_tokens = x.shape[0]
    gate_blk = COL_GATES // D_MODEL
    full = lambda shape: pl.BlockSpec(shape, lambda i: (0,) * len(shape))
    return pl.pallas_call(
        _merge_body,
        grid=(t_tokens // tm,),
        in_specs=[
            pl.BlockSpec((tm, A_W), lambda i: (i, 0)),
            pl.BlockSpec((tm, B_W), lambda i: (i, 0)),
            pl.BlockSpec((tm, C_W), lambda i: (i, 0)),
            pl.BlockSpec((tm, D_MODEL), lambda i: (i, gate_blk)),
            pl.BlockSpec((tm, D_MODEL), lambda i: (i, gate_blk + 1)),
            pl.BlockSpec((tm, D_MODEL), lambda i: (i, gate_blk + 2)),
            pl.BlockSpec((tm, D_MODEL), lambda i: (i, 0)),
            full((A_W, D_MODEL)), full((B_W, D_MODEL)), full((C_W, D_MODEL)), full((D_MODEL, D_MODEL)),
            full((1, D_MODEL)), full((1, D_MODEL)), full((N_EXPERTS, D_MODEL)), full((N_EXPERTS, 1)),
        ],
        out_specs=[
            pl.BlockSpec((tm, X1_COLS), lambda i: (i, 0)),
            pl.BlockSpec((1, tm), lambda i: (0, i)),
        ],
        out_shape=[
            jax.ShapeDtypeStruct((t_tokens, X1_COLS), F32),
            jax.ShapeDtypeStruct((1, t_tokens), jnp.int32),
        ],
        compiler_params=_cparams(1),
        name="merge_ln_router",
    )(oa, ob, oc, h_plain, h_plain, h_plain, x, wa, wb, wc, wo, ln_g, ln_b, rwT, rb)


def _gather_start(idx_ref, src_hbm, dst_ref, sem, n_rows):
    def start(r, _):
        pltpu.make_async_copy(src_hbm.at[pl.ds(idx_ref[0, 0, r], 1)], dst_ref.at[pl.ds(r, 1)], sem).start()
        return 0

    lax.fori_loop(0, n_rows, start, 0, unroll=8)


def _gather_wait(src_hbm, dst_ref, sem, n_rows):
    pltpu.make_async_copy(src_hbm.at[pl.ds(0, n_rows)], dst_ref, sem).wait()


def _experts_body(e1_ref, e2_ref, nused_ref, src_ref, src_next_ref, x_hbm,
                  wg1_ref, wu1_ref, wd1_ref, wg2_ref, wu2_ref, wd2_ref, o_ref, xbuf, sem):
    s = pl.program_id(0)
    nused = nused_ref[0]
    slot = lax.rem(s, 2)

    @pl.when(s == 0)
    def _():
        _gather_start(src_ref, x_hbm, xbuf.at[0], sem.at[0], MOE_TILE)

    @pl.when(s + 1 < nused)
    def _():
        _gather_start(src_next_ref, x_hbm, xbuf.at[1 - slot], sem.at[1 - slot], MOE_TILE)

    @pl.when(s < nused)
    def _():
        _gather_wait(x_hbm, xbuf.at[slot], sem.at[slot], MOE_TILE)
        xb = xbuf[slot, :, :D_MODEL].astype(BF16)
        wt = xbuf[slot, :, D_MODEL:D_MODEL + 2]

        def expert(wg_ref, wu_ref, wd_ref):
            g = jnp.dot(xb, wg_ref[0], preferred_element_type=F32)
            u = jnp.dot(xb, wu_ref[0], preferred_element_type=F32)
            he = (g * jax.nn.sigmoid(g) * u).astype(BF16)
            return jnp.dot(he, wd_ref[0], preferred_element_type=F32)

        o_ref[...] = wt[:, 0:1] * expert(wg1_ref, wu1_ref, wd1_ref) + wt[:, 1:2] * expert(wg2_ref, wu2_ref, wd2_ref)

    @pl.when(s >= nused)
    def _():
        o_ref[...] = jnp.zeros_like(o_ref)


def _experts(tile_e1, tile_e2, nused, src3, x1, wg, wu, wd):
    ntiles = src3.shape[0]
    wspec_up = lambda which: pl.BlockSpec((1, D_MODEL, D_EXPERT), lambda s, e1, e2, nu: ((e1, e2)[which][s], 0, 0))
    wspec_dn = lambda which: pl.BlockSpec((1, D_EXPERT, D_MODEL), lambda s, e1, e2, nu: ((e1, e2)[which][s], 0, 0))
    grid_spec = pltpu.PrefetchScalarGridSpec(
        num_scalar_prefetch=3,
        grid=(ntiles,),
        in_specs=[
            pl.BlockSpec((1, 1, MOE_TILE), lambda s, e1, e2, nu: (s, 0, 0), memory_space=pltpu.SMEM),
            pl.BlockSpec((1, 1, MOE_TILE), lambda s, e1, e2, nu: (jnp.minimum(s + 1, ntiles - 1), 0, 0),
                         memory_space=pltpu.SMEM),
            pl.BlockSpec(memory_space=pl.ANY),
            wspec_up(0), wspec_up(0), wspec_dn(0), wspec_up(1), wspec_up(1), wspec_dn(1),
        ],
        out_specs=pl.BlockSpec((MOE_TILE, D_MODEL), lambda s, e1, e2, nu: (s, 0)),
        scratch_shapes=[pltpu.VMEM((2, MOE_TILE, X1_COLS), F32), pltpu.SemaphoreType.DMA((2,))],
    )
    return pl.pallas_call(
        _experts_body,
        grid_spec=grid_spec,
        out_shape=jax.ShapeDtypeStruct((ntiles * MOE_TILE, D_MODEL), F32),
        compiler_params=_cparams(1),
        name="moe_experts",
    )(tile_e1, tile_e2, nused, src3, src3, x1, wg, wu, wd, wg, wu, wd)


def _final_body(pos_ref, pos_next_ref, y_hbm, x_ref, g_ref, b_ref, o_ref, ob_ref, ybuf, sem, *, tm):
    s = pl.program_id(0)
    slot = lax.rem(s, 2)

    @pl.when(s == 0)
    def _():
        _gather_start(pos_ref, y_hbm, ybuf.at[0], sem.at[0], tm)

    @pl.when(s + 1 < pl.num_programs(0))
    def _():
        _gather_start(pos_next_ref, y_hbm, ybuf.at[1 - slot], sem.at[1 - slot], tm)

    _gather_wait(y_hbm, ybuf.at[slot], sem.at[slot], tm)
    x2 = _layer_norm(DN_ALPHA * x_ref[...] + ybuf[slot], g_ref[...], b_ref[...])
    o_ref[...] = x2
    ob_ref[...] = x2.astype(BF16)


def _final(pos3, y_sorted, x1, ln_g, ln_b, *, tm):
    t_tokens = x1.shape[0]
    nsteps = t_tokens // tm
    return pl.pallas_call(
        functools.partial(_final_body, tm=tm),
        grid=(nsteps,),
        in_specs=[
            pl.BlockSpec((1, 1, tm), lambda i: (i, 0, 0), memory_space=pltpu.SMEM),
            pl.BlockSpec((1, 1, tm), lambda i: (jnp.minimum(i + 1, nsteps - 1), 0, 0), memory_space=pltpu.SMEM),
            pl.BlockSpec(memory_space=pl.ANY),
            pl.BlockSpec((tm, D_MODEL), lambda i: (i, 0)),
            pl.BlockSpec((1, D_MODEL), lambda i: (0, 0)),
            pl.BlockSpec((1, D_MODEL), lambda i: (0, 0)),
        ],
        out_specs=[pl.BlockSpec((tm, D_MODEL), lambda i: (i, 0)), pl.BlockSpec((tm, D_MODEL), lambda i: (i, 0))],
        out_shape=[jax.ShapeDtypeStruct((t_tokens, D_MODEL), F32), jax.ShapeDtypeStruct((t_tokens, D_MODEL), BF16)],
        scratch_shapes=[pltpu.VMEM((2, tm, D_MODEL), F32), pltpu.SemaphoreType.DMA((2,))],
        compiler_params=_cparams(1),
        name="moe_combine_ln",
    )(pos3, pos3, y_sorted, x1, ln_g, ln_b)


def _rope_tables(seq):
    inv = 1.0 / (ROPE_THETA ** (jnp.arange(0, HEAD_DIM, 2, dtype=F32) / HEAD_DIM))
    ang = jnp.arange(seq, dtype=F32)[:, None] * inv[None, :]
    return jnp.cos(ang), jnp.sin(ang)


def _halves_layout(w_heads):
    d, n = w_heads.shape
    t = w_heads.reshape(d, n // PROJ_TN, HEADS_PER_TILE, 2, HEAD_DIM // 2)
    return t.transpose(0, 1, 3, 2, 4).reshape(d, n)


def _reorder_w_in(w):
    pts = np.cumsum((0,) + IN_SIZES)
    sec = [w[:, pts[k]:pts[k + 1]] for k in range(len(IN_SIZES))]
    a_q, a_c, a_iq, a_ik, a_iw, b_q, b_k, b_v, c_q, c_k, c_v, gates = sec
    zeros = lambda n: jnp.zeros((w.shape[0], n), w.dtype)
    w_rope = _halves_layout(jnp.concatenate([a_q, a_iq, b_q, b_k, c_q, c_k, a_ik, zeros(HEAD_DIM)], axis=1))
    w_plain = jnp.concatenate([gates, b_v, a_c, a_iw, zeros(LANES - A_IDX_HEADS), c_v, zeros(LANES)], axis=1)
    assert w_rope.shape[1] == N_ROPE and w_plain.shape[1] == N_PLAIN
    return w_rope.astype(BF16), w_plain.astype(BF16)


def _kv_weights(w_uk, w_uv):
    half = HEAD_DIM // 2
    zeros = lambda n: jnp.zeros((w_uk.shape[0], n), w_uk.dtype)
    return jnp.concatenate([w_uk[:, :half], w_uv, zeros(HALF_TN - half - HEAD_DIM),
                            w_uk[:, half:], zeros(HALF_TN - half)], axis=1).astype(BF16)


def _moe_plan(cls, n_tokens):
    ntiles = n_tokens // MOE_TILE + N_CLASSES
    onehot = (cls[:, None] == jnp.arange(N_CLASSES, dtype=jnp.int32)[None, :]).astype(jnp.int32)
    csum = jnp.cumsum(onehot, axis=0)
    rank = jnp.sum(onehot * csum, axis=1) - 1
    counts = csum[-1]
    ptiles = (counts + MOE_TILE - 1) // MOE_TILE
    tile_end = jnp.cumsum(ptiles)
    tile_start = tile_end - ptiles
    pos = (tile_start[cls] * MOE_TILE + rank).astype(jnp.int32)
    nused = tile_end[-1:].astype(jnp.int32)
    tile_ids = jnp.arange(ntiles, dtype=jnp.int32)
    tile_cls = jnp.minimum(jnp.sum((tile_ids[:, None] >= tile_end[None, :]).astype(jnp.int32), axis=1),
                           N_CLASSES - 1).astype(jnp.int32)
    pair = np.array(PAIRS, dtype=np.int32)
    e_lo = jnp.asarray(np.repeat(np.arange(N_GROUPS), len(PAIRS)) * EXPERTS_PER_GROUP + np.tile(pair[:, 0], N_GROUPS), jnp.int32)
    e_hi = jnp.asarray(np.repeat(np.arange(N_GROUPS), len(PAIRS)) * EXPERTS_PER_GROUP + np.tile(pair[:, 1], N_GROUPS), jnp.int32)
    src = jnp.zeros((ntiles * MOE_TILE,), jnp.int32).at[pos].set(jnp.arange(n_tokens, dtype=jnp.int32))
    return e_lo[tile_cls], e_hi[tile_cls], nused, src.reshape(ntiles, 1, MOE_TILE), pos


def kernel(x, w_in, a_w_uk, a_w_uv, c_sinks, w_branch, w_o, ln1_g, ln1_b, router_w, router_b,
           moe_w_gate, moe_w_up, moe_w_down, ln2_g, ln2_b):
    bsz, seq, _ = x.shape
    n_tokens = bsz * seq
    topk = min(A_TOPK_MAX, seq // 4)
    nkb = seq // B_BLOCK
    tq = ATTN_TQ
    kc = min(DSA_KC, seq)
    tm_proj = min(PROJ_TM, seq)
    tm_row = ROW_TM
    half = HEAD_DIM // 2

    cos, sin = _rope_tables(seq)
    cos_in = jnp.tile(cos, (1, HEADS_PER_TILE))
    sin_in = jnp.tile(sin, (1, HEADS_PER_TILE))
    cos_kv = jnp.concatenate([cos, jnp.ones((seq, HALF_TN - half), F32)], axis=1)
    sin_kv = jnp.concatenate([sin, jnp.zeros((seq, HALF_TN - half), F32)], axis=1)
    rwT = router_w.T
    rb = router_b.reshape(N_EXPERTS, 1)

    xf = x.reshape(n_tokens, D_MODEL)
    xb = xf.astype(BF16)
    for l in range(DEPTH):
        w_rope, w_plain = _reorder_w_in(w_in[l])
        h_rope = _proj_rope(xb, 0, D_MODEL, w_rope, cos_in, sin_in, tm=tm_proj, seq=seq)
        h_plain = _proj_plain(xb, w_plain, tm=tm_proj)
        kv = _proj_rope(h_plain, COL_AC // A_KV_RANK, A_KV_RANK, _kv_weights(a_w_uk[l], a_w_uv[l]),
                        cos_kv, sin_kv, tm=tm_proj, seq=seq)

        bvT, cvT, kvT = _values_T(h_plain, kv, n_tokens=n_tokens)

        o_a = _dsa(h_rope, h_plain, kv, kvT, bsz=bsz, seq=seq, tq=min(DSA_TQ, seq), kc=kc, topk=topk)
        kmean = _moba_kmean(h_rope, n_tokens=n_tokens).reshape(bsz, nkb, B_W)
        sel = _moba_gate(h_rope, kmean, bsz=bsz, seq=seq, tq=min(GATE_TQ, seq))
        o_b = _moba(h_rope, bvT, sel, bsz=bsz, seq=seq)
        o_c = _swa(c_sinks[l], h_rope, cvT, bsz=bsz, seq=seq, tq=tq)

        wb_all = w_branch[l].astype(BF16)
        x1, cls = _merge(o_a, o_b, o_c, h_plain, xf,
                              wb_all[:A_W], wb_all[A_W:A_W + B_W], wb_all[A_W + B_W:], w_o[l].astype(BF16),
                              ln1_g[l].reshape(1, D_MODEL), ln1_b[l].reshape(1, D_MODEL), rwT, rb, tm=tm_row)

        e1, e2, nused, src3, pos = _moe_plan(cls[0], n_tokens)
        y_sorted = _experts(e1, e2, nused, src3, x1,
                            moe_w_gate[l].astype(BF16), moe_w_up[l].astype(BF16), moe_w_down[l].astype(BF16))
        xf, xb = _final(pos.reshape(n_tokens // tm_row, 1, tm_row), y_sorted, x1,
                        ln2_g[l].reshape(1, D_MODEL), ln2_b[l].reshape(1, D_MODEL), tm=tm_row)
    return xf.reshape(bsz, seq, D_MODEL)
```

```python
import functools
import math

import jax
import jax.numpy as jnp
import numpy as np
from jax import lax
from jax.experimental import pallas as pl
from jax.experimental.pallas import tpu as pltpu

D_MODEL = 1024
DEPTH = 2
HEAD_DIM = 64
ROPE_THETA = 10000.0
LN_EPS = 1e-5
A_HEADS = 4
A_KV_RANK = 128
A_IDX_HEADS = 4
A_IDX_DIM = 64
A_TOPK_MAX = 256
B_HEADS = 4
B_BLOCK = 256
B_TOPK_BLOCKS = 3
C_HEADS = 8
C_KV_HEADS = 2
C_WINDOW = 128
N_BRANCH = 3
A_W = A_HEADS * HEAD_DIM
B_W = B_HEADS * HEAD_DIM
C_W = C_HEADS * HEAD_DIM
IN_SIZES = (A_W, A_KV_RANK, A_IDX_HEADS * A_IDX_DIM, A_IDX_DIM, A_IDX_HEADS,
            B_W, B_W, B_W, C_W, C_KV_HEADS * HEAD_DIM, C_KV_HEADS * HEAD_DIM,
            N_BRANCH * D_MODEL)
N_EXPERTS = 16
N_GROUPS = 4
EXPERTS_PER_GROUP = 4
D_EXPERT = 512
DN_ALPHA = (2 * DEPTH) ** 0.25
ATTN_SCALE = HEAD_DIM ** -0.5
LOG2E = math.log2(math.e)
Q_SCALE = ATTN_SCALE * LOG2E

LANES = 128
SUBLANES = 8
VMEM_LIMIT_BYTES = 56 * 1024 * 1024

NEG_BIG = -1e30
BF16 = jnp.bfloat16
F32 = jnp.float32

PROJ_TN = 256
HALF_TN = PROJ_TN // 2
HEADS_PER_TILE = PROJ_TN // HEAD_DIM
TILE_AQ = 0
TILE_AIQ = 1
TILE_BQ = 2
TILE_BK = 3
TILE_CQ = 4
TILE_CK_AIK = 6
N_ROPE = 7 * PROJ_TN
COL_GATES = 0
COL_BV = 3072
COL_AC = 3328
COL_AIW = 3456
COL_CV = 3584
N_PLAIN = 3840
PLAIN_TN = 1920
KV_VA0 = HEAD_DIM // 2

PAIRS = ((0, 1), (0, 2), (0, 3), (1, 2), (1, 3), (2, 3))
N_CLASSES = N_GROUPS * len(PAIRS)
MOE_TILE = 256
X1_COLS = D_MODEL + LANES
GATHER_GROUP = 8
N_DMA_PRIORITIES = 2

PROJ_TM = 1024
PROJ_SUB = 256
ROW_TM = 512
ATTN_TQ = 256
DSA_TQ = 512
DSA_KC = 256
GATE_TQ = 1024
assert COL_GATES == 0

VALUE_MID_PASSES = 16
PEEL_MAX = 2.0
MIN_NORMAL_KEY = 0x00800000
MAX_SELECT_PASSES = 96


def _cparams(n_axes):
    return pltpu.CompilerParams(dimension_semantics=("arbitrary",) * n_axes,
                                vmem_limit_bytes=VMEM_LIMIT_BYTES)


def _proj_rope_body(x_ref, w_ref, cos_ref, sin_ref, o_ref, *, n_sub):
    sub = x_ref.shape[0] // n_sub
    for t in range(w_ref.shape[0]):
        for mi in range(n_sub):
            rows = slice(mi * sub, (mi + 1) * sub)
            acc = jnp.dot(x_ref[rows, :], w_ref[t], preferred_element_type=F32)
            a1, a2 = acc[:, :HALF_TN], acc[:, HALF_TN:]
            c, sn = cos_ref[rows, :], sin_ref[rows, :]
            c0 = t * PROJ_TN
            o_ref[rows, c0:c0 + HALF_TN] = (a1 * c - a2 * sn).astype(o_ref.dtype)
            o_ref[rows, c0 + HALF_TN:c0 + PROJ_TN] = (a2 * c + a1 * sn).astype(o_ref.dtype)


def _proj_plain_body(x_ref, w_ref, o_ref, *, n_sub):
    sub = x_ref.shape[0] // n_sub
    tn = o_ref.shape[1]
    is_gate = (pl.program_id(1) * tn + lax.broadcasted_iota(jnp.int32, (1, tn), 1)) < COL_GATES + N_BRANCH * D_MODEL
    for mi in range(n_sub):
        rows = slice(mi * sub, (mi + 1) * sub)
        acc = jnp.dot(x_ref[rows, :], w_ref[0], preferred_element_type=F32)
        o_ref[rows, :] = jnp.where(is_gate, jax.nn.sigmoid(acc), acc).astype(o_ref.dtype)


def _column_tiles(w, tn=PROJ_TN):
    k_dim, n = w.shape
    return w.reshape(k_dim, n // tn, tn).transpose(1, 0, 2)


def _proj_rope(x, x_col_block, k_dim, w, cos_t, sin_t, *, tm, seq):
    t_tokens = x.shape[0]
    n = w.shape[1]
    pos_blocks = seq // tm
    return pl.pallas_call(
        functools.partial(_proj_rope_body, n_sub=max(tm // PROJ_SUB, 1)),
        grid=(t_tokens // tm,),
        in_specs=[
            pl.BlockSpec((tm, k_dim), lambda i: (i, x_col_block)),
            pl.BlockSpec((n // PROJ_TN, k_dim, PROJ_TN), lambda i: (0, 0, 0)),
            pl.BlockSpec((tm, HALF_TN), lambda i: (i % pos_blocks, 0)),
            pl.BlockSpec((tm, HALF_TN), lambda i: (i % pos_blocks, 0)),
        ],
        out_specs=pl.BlockSpec((tm, n), lambda i: (i, 0)),
        out_shape=jax.ShapeDtypeStruct((t_tokens, n), BF16),
        compiler_params=_cparams(1),
        name="proj_rope",
    )(x, _column_tiles(w), cos_t, sin_t)


def _proj_plain(x, w, *, tm):
    t_tokens, k_dim = x.shape
    n = w.shape[1]
    return pl.pallas_call(
        functools.partial(_proj_plain_body, n_sub=max(tm // PROJ_SUB, 1)),
        grid=(t_tokens // tm, n // PLAIN_TN),
        in_specs=[
            pl.BlockSpec((tm, k_dim), lambda i, j: (i, 0)),
            pl.BlockSpec((1, k_dim, PLAIN_TN), lambda i, j: (j, 0, 0)),
        ],
        out_specs=pl.BlockSpec((tm, PLAIN_TN), lambda i, j: (i, j)),
        out_shape=jax.ShapeDtypeStruct((t_tokens, n), BF16),
        compiler_params=_cparams(2),
        name="proj_plain",
    )(x, _column_tiles(w, PLAIN_TN))


def _values_T_body(bv_ref, cv_ref, kv_ref, bvT_ref, cvT_ref, kvT_ref):
    for src, dst in ((bv_ref, bvT_ref), (cv_ref, cvT_ref), (kv_ref, kvT_ref)):
        for g in range(dst.shape[0]):
            dst[g] = src[g * B_BLOCK:(g + 1) * B_BLOCK, :].astype(F32).T.astype(dst.dtype)


def _values_T(h_plain, kv, *, n_tokens):
    nblk = n_tokens // B_BLOCK
    gb = math.gcd(nblk, SUBLANES)
    in_tile = lambda col: pl.BlockSpec((gb * B_BLOCK, PROJ_TN), lambda i: (i, col // PROJ_TN))
    out_tile = pl.BlockSpec((gb, PROJ_TN, B_BLOCK), lambda i: (i, 0, 0))
    shape = jax.ShapeDtypeStruct((nblk, PROJ_TN, B_BLOCK), BF16)
    return pl.pallas_call(
        _values_T_body,
        grid=(nblk // gb,),
        in_specs=[in_tile(COL_BV), in_tile(COL_CV), in_tile(0)],
        out_specs=[out_tile, out_tile, out_tile],
        out_shape=[shape, shape, shape],
        compiler_params=_cparams(1),
        name="values_transpose",
    )(h_plain, h_plain, kv)


def _f32_to_key(x):
    b = lax.bitcast_convert_type(x, jnp.int32)
    return jnp.where(b < 0, b ^ jnp.int32(0x7FFFFFFF), b)


def _key_to_f32(k):
    b = jnp.where(k < 0, k ^ jnp.int32(0x7FFFFFFF), k)
    return lax.bitcast_convert_type(b, F32)


def _fold8(x, op):
    return op(x.reshape(x.shape[0] // SUBLANES, SUBLANES, x.shape[1]), axis=0)


def _tile_T(x_ref):
    return x_ref[...].astype(F32).T


def _head_rows(xT, h):
    half = HEAD_DIM // 2
    return xT[h * half:(h + 1) * half], xT[HALF_TN + h * half:HALF_TN + (h + 1) * half]


def _place_head(first, second, slot):
    half = HEAD_DIM // 2
    w = first.shape[1]
    before, after = slot * half, HALF_TN - (slot + 1) * half
    pieces = []
    for part in (first, second):
        pieces += [jnp.zeros((before, w), part.dtype)] * (before > 0) + [part] + [jnp.zeros((after, w), part.dtype)] * (after > 0)
    return jnp.concatenate(pieces, axis=0)


def _store_rows(o_ref, oT_heads):
    o_ref[...] = jnp.concatenate(oT_heads, axis=0).T.astype(o_ref.dtype)


IK_SLOT = 2


def _dsa_body(q_ref, iq_ref, iw_ref, ik_ref, kv_ref, vT_ref, o_ref, sc_ref, acc_ref, sbuf_ref, *, tq, kc, topk):
    i = pl.program_id(1)
    q0 = i * tq
    nch = (q0 + tq + kc - 1) // kc
    qpos = q0 + lax.broadcasted_iota(jnp.int32, (1, tq), 1)
    iqT = _tile_T(iq_ref)
    iq_ops = [_place_head(*[p.astype(BF16) for p in _head_rows(iqT, h)], IK_SLOT) for h in range(A_IDX_HEADS)]
    w_idx = iw_ref[...].astype(F32).T[:A_IDX_HEADS]

    def score_chunk(c, carry, diagonal):
        mn, mx = carry
        k0 = pl.multiple_of(c * kc, kc)
        ikc = ik_ref[pl.ds(k0, kc), :]
        acc = jnp.zeros((kc, tq), F32)
        for h in range(A_IDX_HEADS):
            d = jnp.dot(ikc, iq_ops[h], preferred_element_type=F32)
            acc = acc + w_idx[h:h + 1, :] * jnp.maximum(d, 0.0)
        if diagonal:
            causal = (k0 + lax.broadcasted_iota(jnp.int32, (kc, 1), 0)) <= qpos
            lo_part, hi_part = jnp.where(causal, acc, jnp.inf), jnp.where(causal, acc, -jnp.inf)
        else:
            lo_part = hi_part = acc
        sc_ref[pl.ds(k0, kc), :] = hi_part
        return jnp.minimum(mn, _fold8(lo_part, jnp.min)), jnp.maximum(mx, _fold8(hi_part, jnp.max))

    n_full = (q0 + 1) // kc
    carry = lax.fori_loop(0, n_full, functools.partial(score_chunk, diagonal=False),
                          (jnp.full((SUBLANES, tq), jnp.inf, F32), jnp.full((SUBLANES, tq), -jnp.inf, F32)))
    mn8, mx8 = lax.fori_loop(n_full, nch, functools.partial(score_chunk, diagonal=True), carry)
    row_min = jnp.min(mn8, axis=0, keepdims=True)
    row_max = jnp.max(mx8, axis=0, keepdims=True)

    pc = 2 * kc
    npair = (nch + 1) // 2
    n_chunks_total = vT_ref.shape[0]

    @pl.when(nch % 2 == 1)
    def _():
        sc_ref[pl.ds(pl.multiple_of(nch * kc, kc), kc), :] = jnp.full((kc, tq), -jnp.inf, F32)

    def count_ge(thr):
        def body(j, cnt):
            s = sc_ref[pl.ds(pl.multiple_of(j * pc, pc), pc), :]
            return cnt + _fold8(jnp.where(s >= thr, 1.0, 0.0), jnp.sum)
        cnt8 = lax.fori_loop(0, npair, body, jnp.zeros((SUBLANES, tq), F32))
        return jnp.sum(cnt8, axis=0, keepdims=True)

    def min_ge(thr):
        def body(j, mn):
            s = sc_ref[pl.ds(pl.multiple_of(j * pc, pc), pc), :]
            return jnp.minimum(mn, _fold8(jnp.where(s >= thr, s, jnp.inf), jnp.min))
        mn8 = lax.fori_loop(0, npair, body, jnp.full((SUBLANES, tq), jnp.inf, F32))
        return jnp.min(mn8, axis=0, keepdims=True)

    def count_zero():
        def body(j, carry):
            c0, cp = carry
            s = sc_ref[pl.ds(pl.multiple_of(j * pc, pc), pc), :]
            return (c0 + _fold8(jnp.where(s >= 0.0, 1.0, 0.0), jnp.sum),
                    cp + _fold8(jnp.where(s > 0.0, 1.0, 0.0), jnp.sum))
        z8 = jnp.zeros((SUBLANES, tq), F32)
        c0, cp = lax.fori_loop(0, npair, body, (z8, z8))
        return jnp.sum(c0, axis=0, keepdims=True), jnp.sum(cp, axis=0, keepdims=True)

    kf = float(topk)
    n_valid = (qpos + 1).astype(F32)
    few = n_valid <= kf
    cnt_nonneg, cnt_pos = count_zero()
    is_zero = jnp.logical_and(cnt_pos < kf, cnt_nonneg >= kf)
    is_pos = cnt_pos >= kf
    lo0 = jnp.where(is_zero, 0, jnp.where(is_pos, MIN_NORMAL_KEY, _f32_to_key(row_min)))
    hi0 = jnp.where(is_zero, 1, jnp.where(is_pos, _f32_to_key(row_max) + 1, -1))
    cnt0 = jnp.where(is_zero, cnt_nonneg, jnp.where(is_pos, cnt_pos, n_valid))
    lo0 = jnp.where(few, _f32_to_key(row_min), lo0)
    hi0 = jnp.where(few, lo0 + 1, hi0)

    def unfinished(lo, hi, cnt_lo):
        return jnp.logical_and(lo + 1 != hi, cnt_lo != kf)

    def surplus(lo, hi, cnt_lo):
        return jnp.max(jnp.where(unfinished(lo, hi, cnt_lo), cnt_lo - kf, 0.0))

    def bis_cond(st):
        _, it, excess = st
        return jnp.logical_and(excess > 0.0, it < MAX_SELECT_PASSES)

    def bisect_once(st):
        lo, hi, cnt_lo, it = st
        nd = unfinished(lo, hi, cnt_lo)
        mid_val = _f32_to_key(0.5 * _key_to_f32(lo) + 0.5 * _key_to_f32(hi))
        mid_key = lo + lax.shift_right_logical(hi - lo, 1)
        mid = jnp.where(it < VALUE_MID_PASSES, jnp.clip(mid_val, lo + 1, hi - 1), mid_key)
        cnt = count_ge(_key_to_f32(mid))
        ge = cnt >= kf
        up = jnp.logical_and(nd, ge)
        down = jnp.logical_and(nd, jnp.logical_not(ge))
        return jnp.where(up, mid, lo), jnp.where(down, mid, hi), jnp.where(up, cnt, cnt_lo), it + 1

    def bisect(st):
        return bisect_once(bisect_once(st))

    def peel(st):
        lo, hi, cnt_lo, it = st
        nd = unfinished(lo, hi, cnt_lo)
        k1 = jnp.maximum(_f32_to_key(min_ge(_key_to_f32(lo))), lo)
        cnt = count_ge(_key_to_f32(k1 + 1))
        ge = cnt >= kf
        up = jnp.logical_and(nd, ge)
        down = jnp.logical_and(nd, jnp.logical_not(ge))
        new_lo = jnp.where(nd, jnp.where(ge, k1 + 1, k1), lo)
        return new_lo, jnp.where(down, k1 + 1, hi), jnp.where(up, cnt, cnt_lo), it + 2

    def bis_body(st):
        (lo, hi, cnt_lo), it, excess = st
        lo, hi, cnt_lo, it = lax.cond(excess > PEEL_MAX, bisect, peel, (lo, hi, cnt_lo, it))
        return (lo, hi, cnt_lo), it, surplus(lo, hi, cnt_lo)

    (lo, _, cnt_lo), _, _ = lax.while_loop(bis_cond, bis_body,
                                           ((lo0, hi0, cnt0), jnp.int32(0), surplus(lo0, hi0, cnt0)))
    thr = _key_to_f32(lo)

    tie = jnp.logical_and(cnt_lo > kf, jnp.logical_not(few))

    @pl.when(jnp.max(jnp.where(tie, 1.0, 0.0)) > 0.0)
    def _():
        def count_gt():
            def gt_body(c, cnt):
                k0 = pl.multiple_of(c * kc, kc)
                s = sc_ref[pl.ds(k0, kc), :]
                return cnt + _fold8(jnp.where(s > thr, 1.0, 0.0), jnp.sum)
            gt8 = lax.fori_loop(0, nch, gt_body, jnp.zeros((SUBLANES, tq), F32))
            return jnp.sum(gt8, axis=0, keepdims=True)

        zero_thr = thr == 0.0
        other = jnp.max(jnp.where(jnp.logical_and(tie, jnp.logical_not(zero_thr)), 1.0, 0.0)) > 0.0
        gt = lax.cond(other, count_gt, lambda: cnt_pos)
        need = kf - jnp.where(zero_thr, cnt_pos, gt)
        r = lax.broadcasted_iota(jnp.int32, (kc, kc), 0)
        cidx = lax.broadcasted_iota(jnp.int32, (kc, kc), 1)
        tri = jnp.where(cidx <= r, 1.0, 0.0).astype(BF16)

        thr_tie = jnp.where(tie, thr, jnp.nan)
        keep = jnp.where(tie, need, jnp.inf)

        def tie_body(c, seen):
            k0 = pl.multiple_of(c * kc, kc)
            s = sc_ref[pl.ds(k0, kc), :]
            eqf = jnp.where(s == thr_tie, 1.0, 0.0)
            rank = jnp.dot(tri, eqf.astype(BF16), preferred_element_type=F32) + seen
            sc_ref[pl.ds(k0, kc), :] = jnp.where(eqf * rank > keep, -jnp.inf, s)
            return seen + jnp.sum(eqf, axis=0, keepdims=True)

        lax.fori_loop(0, nch, tie_body, jnp.zeros((1, tq), F32))

    qT_all = (_tile_T(q_ref) * Q_SCALE).astype(BF16)
    qT = jnp.concatenate([_place_head(*_head_rows(qT_all, h), 0) for h in range(A_HEADS)], axis=1)
    acc_ref[...] = jnp.zeros_like(acc_ref)
    ones_rows = jnp.ones((SUBLANES, kc), BF16)

    def issue_scores(c, slot):
        k0 = pl.multiple_of(jnp.minimum(c, n_chunks_total - 1) * kc, kc)
        sbuf_ref[slot] = jnp.dot(kv_ref[pl.ds(k0, kc), :], qT, preferred_element_type=F32)

    def softmax_pv(c, slot, m):
        k0 = pl.multiple_of(c * kc, kc)
        v1 = jnp.concatenate([vT_ref[c, KV_VA0:KV_VA0 + HEAD_DIM, :], ones_rows], axis=0)
        sel = sc_ref[pl.ds(k0, kc), :] >= thr
        s = jnp.concatenate([jnp.where(sel, sbuf_ref[slot, :, h * tq:(h + 1) * tq], NEG_BIG)
                             for h in range(A_HEADS)], axis=1)
        m_new = jnp.maximum(m, jnp.max(s, axis=0, keepdims=True))
        alpha = jnp.exp2(m - m_new)
        p = jnp.exp2((s - m_new).astype(BF16))
        acc_ref[...] = acc_ref[...] * alpha + jnp.dot(v1, p, preferred_element_type=F32)
        return m_new

    def chunk_pair(j, m):
        c0 = 2 * j
        issue_scores(c0 + 1, 1)
        m = softmax_pv(c0, 0, m)
        issue_scores(c0 + 2, 0)
        return softmax_pv(c0 + 1, 1, m)

    issue_scores(0, 0)
    lax.fori_loop(0, npair, chunk_pair, jnp.full((1, A_HEADS * tq), NEG_BIG, F32))
    a = acc_ref[...]
    o = a[:HEAD_DIM] / a[HEAD_DIM:HEAD_DIM + 1]
    _store_rows(o_ref, [o[:, h * tq:(h + 1) * tq] for h in range(A_HEADS)])


def _dsa(h_rope, h_plain, kv, kvT, *, bsz, seq, tq, kc, topk):
    nq = seq // tq
    assert kc == B_BLOCK
    return pl.pallas_call(
        functools.partial(_dsa_body, tq=tq, kc=kc, topk=topk),
        grid=(bsz, nq),
        in_specs=[
            pl.BlockSpec((tq, PROJ_TN), lambda b, i: (b * nq + i, TILE_AQ)),
            pl.BlockSpec((tq, PROJ_TN), lambda b, i: (b * nq + i, TILE_AIQ)),
            pl.BlockSpec((tq, LANES), lambda b, i: (b * nq + i, COL_AIW // LANES)),
            pl.BlockSpec((seq, PROJ_TN), lambda b, i: (b, TILE_CK_AIK)),
            pl.BlockSpec((seq, PROJ_TN), lambda b, i: (b, 0)),
            pl.BlockSpec((seq // kc, PROJ_TN, kc), lambda b, i: (b, 0, 0)),
        ],
        out_specs=pl.BlockSpec((tq, A_W), lambda b, i: (b * nq + i, 0)),
        out_shape=jax.ShapeDtypeStruct((bsz * seq, A_W), BF16),
        scratch_shapes=[pltpu.VMEM((seq + kc, tq), F32),
                        pltpu.VMEM((HEAD_DIM + SUBLANES, A_HEADS * tq), F32),
                        pltpu.VMEM((2, kc, A_HEADS * tq), F32)],
        compiler_params=_cparams(2),
        name="dsa_attention",
    )(h_rope, h_rope, h_plain, h_rope, kv, kvT)


def _kmean_body(k_ref, o_ref):
    gb = o_ref.shape[0]
    o_ref[...] = jnp.mean(k_ref[...].astype(F32).reshape(gb, B_BLOCK, B_W), axis=1)


def _moba_kmean(h_rope, *, n_tokens):
    nblk = n_tokens // B_BLOCK
    gb = math.gcd(nblk, SUBLANES)
    return pl.pallas_call(
        _kmean_body,
        grid=(nblk // gb,),
        in_specs=[pl.BlockSpec((gb * B_BLOCK, B_W), lambda i: (i, TILE_BK))],
        out_specs=pl.BlockSpec((gb, B_W), lambda i: (i, 0)),
        out_shape=jax.ShapeDtypeStruct((nblk, B_W), F32),
        compiler_params=_cparams(1),
        name="moba_kmean",
    )(h_rope)


def _moba_gate_body(q_ref, km_ref, sel_ref, *, tq, nkb, nsel):
    i = pl.program_id(1)
    qpos = i * tq + lax.broadcasted_iota(jnp.int32, (1, tq), 1)
    cur = qpos // B_BLOCK
    row = lax.broadcasted_iota(jnp.int32, (nkb, tq), 0)
    past = row < cur
    qT = _tile_T(q_ref).astype(BF16)
    km = km_ref[0]
    col_head = (lax.broadcasted_iota(jnp.int32, km.shape, 1) % HALF_TN) // (HEAD_DIM // 2)
    for h in range(B_HEADS):
        km_h = jnp.where(col_head == h, km, 0.0).astype(BF16)
        g = jnp.dot(km_h, qT, preferred_element_type=F32)
        g = jnp.where(past, g, -jnp.inf)
        sel = jnp.zeros((nkb, tq), F32)
        for _ in range(nsel):
            mx = jnp.max(g, axis=0, keepdims=True)
            idx = jnp.min(jnp.where(g == mx, row, nkb), axis=0, keepdims=True)
            pick = row == idx
            sel = jnp.where(pick, 1.0, sel)
            g = jnp.where(pick, -jnp.inf, g)
        sel_ref[0, h] = jnp.where(past, sel, 0.0)


def _moba_gate(h_rope, kmean, *, bsz, seq, tq):
    nkb = seq // B_BLOCK
    nsel = min(B_TOPK_BLOCKS, nkb)
    nq = seq // tq
    return pl.pallas_call(
        functools.partial(_moba_gate_body, tq=tq, nkb=nkb, nsel=nsel),
        grid=(bsz, nq),
        in_specs=[
            pl.BlockSpec((tq, PROJ_TN), lambda b, i: (b * nq + i, TILE_BQ)),
            pl.BlockSpec((1, nkb, B_W), lambda b, i: (b, 0, 0)),
        ],
        out_specs=pl.BlockSpec((1, B_HEADS, nkb, tq), lambda b, i: (b, 0, 0, i)),
        out_shape=jax.ShapeDtypeStruct((bsz, B_HEADS, nkb, seq), F32),
        compiler_params=_cparams(2),
        name="moba_gate",
    )(h_rope, kmean)


def _moba_body(q_ref, k_ref, vT_ref, sel_ref, o_ref, acc_ref, sbuf_ref, *, tq):
    i = pl.program_id(1)
    qT_all = (_tile_T(q_ref) * Q_SCALE).astype(BF16)
    qT = [_place_head(*_head_rows(qT_all, h), h) for h in range(B_HEADS)]
    ones_rows = jnp.ones((SUBLANES, B_BLOCK), BF16)
    n_past = i

    def values(h, j):
        return jnp.concatenate([vT_ref[j, h * HEAD_DIM:(h + 1) * HEAD_DIM, :], ones_rows], axis=0)

    def issue_scores(j, slot):
        k0 = pl.multiple_of(jnp.minimum(j, i) * B_BLOCK, B_BLOCK)
        for h in range(B_HEADS):
            sbuf_ref[slot, h] = jnp.dot(k_ref[pl.ds(k0, B_BLOCK), :], qT[h], preferred_element_type=F32)

    def softmax_pv(j, slot, ms):
        ps, alphas, new_ms = [], [], []
        for h in range(B_HEADS):
            s = jnp.where(sel_ref[0, h, pl.ds(j, 1), :] > 0.0, sbuf_ref[slot, h], NEG_BIG)
            m_new = jnp.maximum(ms[h], jnp.max(s, axis=0, keepdims=True))
            alphas.append(jnp.exp2(ms[h] - m_new))
            ps.append(jnp.exp2((s - m_new).astype(BF16)))
            new_ms.append(m_new)
        for h in range(B_HEADS):
            acc_ref[h] = acc_ref[h] * alphas[h] + jnp.dot(values(h, j), ps[h], preferred_element_type=F32)
        return tuple(new_ms)

    own0 = pl.multiple_of(i * B_BLOCK, B_BLOCK)
    s_own = [jnp.dot(k_ref[pl.ds(own0, B_BLOCK), :], qT[h], preferred_element_type=F32) for h in range(B_HEADS)]
    issue_scores(0, 0)
    causal = lax.broadcasted_iota(jnp.int32, (B_BLOCK, tq), 0) <= lax.broadcasted_iota(jnp.int32, (B_BLOCK, tq), 1)
    ms = []
    for h in range(B_HEADS):
        s = jnp.where(causal, s_own[h], NEG_BIG)
        m0 = jnp.max(s, axis=0, keepdims=True)
        acc_ref[h] = jnp.dot(values(h, i), jnp.exp2((s - m0).astype(BF16)), preferred_element_type=F32)
        ms.append(m0)

    def block_pair(jj, ms):
        j0 = 2 * jj
        issue_scores(j0 + 1, 1)
        ms = softmax_pv(j0, 0, ms)
        issue_scores(j0 + 2, 0)
        return softmax_pv(j0 + 1, 1, ms)

    lax.fori_loop(0, (n_past + 1) // 2, block_pair, tuple(ms))
    outs = []
    for h in range(B_HEADS):
        a = acc_ref[h]
        outs.append(a[:HEAD_DIM] / a[HEAD_DIM:HEAD_DIM + 1])
    _store_rows(o_ref, outs)


def _moba(h_rope, bvT, sel, *, bsz, seq):
    tq = B_BLOCK
    nkb = seq // B_BLOCK
    nq = seq // tq
    return pl.pallas_call(
        functools.partial(_moba_body, tq=tq),
        grid=(bsz, nq),
        in_specs=[
            pl.BlockSpec((tq, PROJ_TN), lambda b, i: (b * nq + i, TILE_BQ)),
            pl.BlockSpec((seq, PROJ_TN), lambda b, i: (b, TILE_BK)),
            pl.BlockSpec((nkb, B_W, B_BLOCK), lambda b, i: (b, 0, 0)),
            pl.BlockSpec((1, B_HEADS, nkb, tq), lambda b, i: (b, 0, 0, i)),
        ],
        out_specs=pl.BlockSpec((tq, B_W), lambda b, i: (b * nq + i, 0)),
        out_shape=jax.ShapeDtypeStruct((bsz * seq, B_W), BF16),
        scratch_shapes=[pltpu.VMEM((B_HEADS, HEAD_DIM + SUBLANES, tq), F32),
                        pltpu.VMEM((2, B_HEADS, B_BLOCK, tq), F32)],
        compiler_params=_cparams(2),
        name="moba_attention",
    )(h_rope, h_rope, bvT, sel)


def _swa_body(sink_ref, q0_ref, q1_ref, k0_ref, k1_ref, k2_ref, vprev_ref, vcur_ref, o_ref, *, tq):
    i = pl.program_id(1)
    q0 = i * tq
    qpos = q0 + lax.broadcasted_iota(jnp.int32, (1, tq), 1)
    nk = tq + C_WINDOW
    kpos = q0 - C_WINDOW + lax.broadcasted_iota(jnp.int32, (nk, 1), 0)
    diff = qpos - kpos
    ok = jnp.where(diff >= 0, jnp.where(diff < C_WINDOW, jnp.where(kpos >= 0, 1.0, 0.0), 0.0), 0.0) > 0.0
    ones_rows = jnp.ones((SUBLANES, nk), BF16)
    group = C_HEADS // C_KV_HEADS
    assert group == HEADS_PER_TILE
    kk = jnp.concatenate([k0_ref[...], k1_ref[...], k2_ref[...]], axis=0)
    outs = []
    scores = []
    for kv, q_ref in enumerate((q0_ref, q1_ref)):
        qT_all = (_tile_T(q_ref) * Q_SCALE).astype(BF16)
        qT = jnp.concatenate([_place_head(*_head_rows(qT_all, g), kv) for g in range(group)], axis=1)
        scores.append(jnp.dot(kk, qT, preferred_element_type=F32))
    for kv in range(C_KV_HEADS):
        vrows = slice(kv * HEAD_DIM, (kv + 1) * HEAD_DIM)
        vv = jnp.concatenate([vprev_ref[0, vrows, tq - C_WINDOW:], vcur_ref[0, vrows, :]], axis=1)
        v1 = jnp.concatenate([vv, ones_rows], axis=0)
        heads = [kv * group + g for g in range(group)]
        sink = jnp.concatenate([jnp.full((1, tq), sink_ref[hd] * LOG2E, F32) for hd in heads], axis=1)
        s = jnp.concatenate([jnp.where(ok, scores[kv][:, g * tq:(g + 1) * tq], NEG_BIG) for g in range(group)], axis=1)
        m = jnp.maximum(jnp.max(s, axis=0, keepdims=True), sink)
        p = jnp.exp2((s - m).astype(BF16))
        a = jnp.dot(v1, p, preferred_element_type=F32)
        o = a[:HEAD_DIM] / (a[HEAD_DIM:HEAD_DIM + 1] + jnp.exp2(sink - m))
        outs += [o[:, g * tq:(g + 1) * tq] for g in range(group)]
    _store_rows(o_ref, outs)


def _swa(sinks, h_rope, cvT, *, bsz, seq, tq):
    r = tq // C_WINDOW
    assert r == 2 and tq == B_BLOCK
    nq = seq // tq
    nwb = seq // C_WINDOW

    def kspec(off):
        return pl.BlockSpec((C_WINDOW, PROJ_TN),
                            lambda b, i, s: (b * nwb + jnp.maximum(i * r + off, 0), TILE_CK_AIK))

    def vspec(off):
        return pl.BlockSpec((1, PROJ_TN, B_BLOCK), lambda b, i, s: (b * nq + jnp.maximum(i + off, 0), 0, 0))

    grid_spec = pltpu.PrefetchScalarGridSpec(
        num_scalar_prefetch=1,
        grid=(bsz, nq),
        in_specs=[pl.BlockSpec((tq, PROJ_TN), lambda b, i, s: (b * nq + i, TILE_CQ)),
                  pl.BlockSpec((tq, PROJ_TN), lambda b, i, s: (b * nq + i, TILE_CQ + 1)),
                  kspec(-1), kspec(0), kspec(1), vspec(-1), vspec(0)],
        out_specs=pl.BlockSpec((tq, C_W), lambda b, i, s: (b * nq + i, 0)),
    )
    return pl.pallas_call(
        functools.partial(_swa_body, tq=tq),
        grid_spec=grid_spec,
        out_shape=jax.ShapeDtypeStruct((bsz * seq, C_W), BF16),
        compiler_params=_cparams(2),
        name="swa_attention",
    )(sinks, h_rope, h_rope, h_rope, h_rope, h_rope, cvT, cvT)


def _layer_norm(z, g, b):
    mu = jnp.mean(z, axis=-1, keepdims=True)
    zc = z - mu
    var = jnp.mean(zc * zc, axis=-1, keepdims=True)
    return zc * lax.rsqrt(var + LN_EPS) * g + b


def _split_bf16(a):
    hi = a.astype(BF16)
    lo = (a - hi.astype(F32)).astype(BF16)
    return hi, lo


def _router(x1, rw_ref, rb_ref, cls_ref):
    nt = (((1,), (1,)), ((), ()))
    xh, xl = _split_bf16(x1)
    wh, wl = _split_bf16(rw_ref[...])
    logits = (lax.dot_general(wh, xh, nt, preferred_element_type=F32)
              + lax.dot_general(wh, xl, nt, preferred_element_type=F32)
              + lax.dot_general(wl, xh, nt, preferred_element_type=F32))
    aff = jax.nn.sigmoid(logits)
    score = aff + rb_ref[...]
    tm = x1.shape[0]
    sc = [score[e:e + 1, :] for e in range(N_EXPERTS)]
    af = [aff[e:e + 1, :] for e in range(N_EXPERTS)]
    in_top = []
    grp_score = []
    for gq in range(N_GROUPS):
        gs = jnp.zeros((1, tm), F32)
        for a in range(EXPERTS_PER_GROUP):
            ea = gq * EXPERTS_PER_GROUP + a
            rank = jnp.zeros((1, tm), F32)
            for b in range(EXPERTS_PER_GROUP):
                if b == a:
                    continue
                eb = gq * EXPERTS_PER_GROUP + b
                beats = (sc[eb] >= sc[ea]) if b < a else (sc[eb] > sc[ea])
                rank = rank + jnp.where(beats, 1.0, 0.0)
            top = rank < 2.0
            in_top.append(top)
            gs = gs + jnp.where(top, sc[ea], 0.0)
        grp_score.append(gs)
    best = grp_score[0]
    gstar = jnp.zeros((1, tm), jnp.int32)
    for gq in range(1, N_GROUPS):
        better = grp_score[gq] > best
        best = jnp.where(better, grp_score[gq], best)
        gstar = jnp.where(better, gq, gstar)
    cls = jnp.zeros((1, tm), jnp.int32)
    w_lo = jnp.zeros((1, tm), F32)
    w_hi = jnp.zeros((1, tm), F32)
    for gq in range(N_GROUPS):
        is_g = gstar == gq
        for pi, (a, b) in enumerate(PAIRS):
            ea, eb = gq * EXPERTS_PER_GROUP + a, gq * EXPERTS_PER_GROUP + b
            hit = jnp.where(is_g, jnp.where(in_top[ea], jnp.where(in_top[eb], 1.0, 0.0), 0.0), 0.0) > 0.0
            cls = jnp.where(hit, gq * len(PAIRS) + pi, cls)
            tot = af[ea] + af[eb]
            w_lo = jnp.where(hit, af[ea] / tot, w_lo)
            w_hi = jnp.where(hit, af[eb] / tot, w_hi)
    cls_ref[...] = cls
    return w_lo, w_hi


def _merge_body(oa_ref, ob_ref, oc_ref, ga_ref, gb_ref, gc_ref, x_ref, wa_ref, wb_ref, wc_ref, wo_ref,
                g_ref, b_ref, rw_ref, rb_ref, x1_ref, cls_ref):
    def branch(o_ref, w_ref, gate_ref):
        y = jnp.dot(o_ref[...], w_ref[...], preferred_element_type=F32)
        return gate_ref[...].astype(F32) * y

    merged = branch(oa_ref, wa_ref, ga_ref) + branch(ob_ref, wb_ref, gb_ref) + branch(oc_ref, wc_ref, gc_ref)
    y = jnp.dot(merged.astype(BF16), wo_ref[...], preferred_element_type=F32)
    x1 = _layer_norm(DN_ALPHA * x_ref[...] + y, g_ref[...], b_ref[...])
    w_lo, w_hi = _router(x1, rw_ref, rb_ref, cls_ref)
    x1_ref[:, :D_MODEL] = x1
    x1_ref[:, D_MODEL:] = jnp.concatenate([w_lo, w_hi, jnp.zeros((LANES - 2, x1.shape[0]), F32)], axis=0).T


def _merge(oa, ob, oc, h_plain, x, wa, wb, wc, wo, ln_g, ln_b, rwT, rb, *, tm):
    t_tokens = x.shape[0]
    gate_blk = COL_GATES // D_MODEL
    full = lambda shape: pl.BlockSpec(shape, lambda i: (0,) * len(shape))
    return pl.pallas_call(
        _merge_body,
        grid=(t_tokens // tm,),
        in_specs=[
            pl.BlockSpec((tm, A_W), lambda i: (i, 0)),
            pl.BlockSpec((tm, B_W), lambda i: (i, 0)),
            pl.BlockSpec((tm, C_W), lambda i: (i, 0)),
            pl.BlockSpec((tm, D_MODEL), lambda i: (i, gate_blk)),
            pl.BlockSpec((tm, D_MODEL), lambda i: (i, gate_blk + 1)),
            pl.BlockSpec((tm, D_MODEL), lambda i: (i, gate_blk + 2)),
            pl.BlockSpec((tm, D_MODEL), lambda i: (i, 0)),
            full((A_W, D_MODEL)), full((B_W, D_MODEL)), full((C_W, D_MODEL)), full((D_MODEL, D_MODEL)),
            full((1, D_MODEL)), full((1, D_MODEL)), full((N_EXPERTS, D_MODEL)), full((N_EXPERTS, 1)),
        ],
        out_specs=[
            pl.BlockSpec((tm, X1_COLS), lambda i: (i, 0)),
            pl.BlockSpec((1, tm), lambda i: (0, i)),
        ],
        out_shape=[
            jax.ShapeDtypeStruct((t_tokens, X1_COLS), F32),
            jax.ShapeDtypeStruct((1, t_tokens), jnp.int32),
        ],
        compiler_params=_cparams(1),
        name="merge_ln_router",
    )(oa, ob, oc, h_plain, h_plain, h_plain, x, wa, wb, wc, wo, ln_g, ln_b, rwT, rb)


def _gather_start(idx_ref, src_hbm, dst_ref, sem, n_rows):
    def start_group(g, _):
        for u in range(GATHER_GROUP):
            r = g * GATHER_GROUP + u
            pltpu.make_async_copy(src_hbm.at[pl.ds(idx_ref[0, 0, r], 1)], dst_ref.at[pl.ds(r, 1)],
                                  sem).start(priority=u % N_DMA_PRIORITIES)
        return 0

    lax.fori_loop(0, n_rows // GATHER_GROUP, start_group, 0)


def _gather_wait(src_hbm, dst_ref, sem, n_rows):
    pltpu.make_async_copy(src_hbm.at[pl.ds(0, n_rows)], dst_ref, sem).wait()


def _experts_body(e1_ref, e2_ref, nused_ref, src_ref, src_next_ref, x_hbm,
                  wg1_ref, wu1_ref, wd1_ref, wg2_ref, wu2_ref, wd2_ref, o_ref, xbuf, sem):
    s = pl.program_id(0)
    nused = nused_ref[0]
    slot = lax.rem(s, 2)

    @pl.when(s == 0)
    def _():
        _gather_start(src_ref, x_hbm, xbuf.at[0], sem.at[0], MOE_TILE)

    @pl.when(s + 1 < nused)
    def _():
        _gather_start(src_next_ref, x_hbm, xbuf.at[1 - slot], sem.at[1 - slot], MOE_TILE)

    @pl.when(s < nused)
    def _():
        _gather_wait(x_hbm, xbuf.at[slot], sem.at[slot], MOE_TILE)
        xb = xbuf[slot, :, :D_MODEL].astype(BF16)
        wt = xbuf[slot, :, D_MODEL:D_MODEL + 2]

        def expert(wg_ref, wu_ref, wd_ref):
            g = jnp.dot(xb, wg_ref[0], preferred_element_type=F32)
            u = jnp.dot(xb, wu_ref[0], preferred_element_type=F32)
            he = (g * jax.nn.sigmoid(g) * u).astype(BF16)
            return jnp.dot(he, wd_ref[0], preferred_element_type=F32)

        o_ref[...] = wt[:, 0:1] * expert(wg1_ref, wu1_ref, wd1_ref) + wt[:, 1:2] * expert(wg2_ref, wu2_ref, wd2_ref)

    @pl.when(s >= nused)
    def _():
        o_ref[...] = jnp.zeros_like(o_ref)


def _experts(tile_e1, tile_e2, nused, src3, x1, wg, wu, wd):
    ntiles = src3.shape[0]
    wspec_up = lambda which: pl.BlockSpec((1, D_MODEL, D_EXPERT), lambda s, e1, e2, nu: ((e1, e2)[which][s], 0, 0))
    wspec_dn = lambda which: pl.BlockSpec((1, D_EXPERT, D_MODEL), lambda s, e1, e2, nu: ((e1, e2)[which][s], 0, 0))
    grid_spec = pltpu.PrefetchScalarGridSpec(
        num_scalar_prefetch=3,
        grid=(ntiles,),
        in_specs=[
            pl.BlockSpec((1, 1, MOE_TILE), lambda s, e1, e2, nu: (s, 0, 0), memory_space=pltpu.SMEM),
            pl.BlockSpec((1, 1, MOE_TILE), lambda s, e1, e2, nu: (jnp.minimum(s + 1, ntiles - 1), 0, 0),
                         memory_space=pltpu.SMEM),
            pl.BlockSpec(memory_space=pl.ANY),
            wspec_up(0), wspec_up(0), wspec_dn(0), wspec_up(1), wspec_up(1), wspec_dn(1),
        ],
        out_specs=pl.BlockSpec((MOE_TILE, D_MODEL), lambda s, e1, e2, nu: (s, 0)),
        scratch_shapes=[pltpu.VMEM((2, MOE_TILE, X1_COLS), F32), pltpu.SemaphoreType.DMA((2,))],
    )
    return pl.pallas_call(
        _experts_body,
        grid_spec=grid_spec,
        out_shape=jax.ShapeDtypeStruct((ntiles * MOE_TILE, D_MODEL), F32),
        compiler_params=_cparams(1),
        name="moe_experts",
    )(tile_e1, tile_e2, nused, src3, src3, x1, wg, wu, wd, wg, wu, wd)


def _final_body(pos_ref, pos_next_ref, y_hbm, x_ref, g_ref, b_ref, o_ref, ob_ref, ybuf, sem, *, tm):
    s = pl.program_id(0)
    slot = lax.rem(s, 2)

    @pl.when(s == 0)
    def _():
        _gather_start(pos_ref, y_hbm, ybuf.at[0], sem.at[0], tm)

    @pl.when(s + 1 < pl.num_programs(0))
    def _():
        _gather_start(pos_next_ref, y_hbm, ybuf.at[1 - slot], sem.at[1 - slot], tm)

    _gather_wait(y_hbm, ybuf.at[slot], sem.at[slot], tm)
    x2 = _layer_norm(DN_ALPHA * x_ref[...] + ybuf[slot], g_ref[...], b_ref[...])
    o_ref[...] = x2
    ob_ref[...] = x2.astype(BF16)


def _final(pos3, y_sorted, x1, ln_g, ln_b, *, tm):
    t_tokens = x1.shape[0]
    nsteps = t_tokens // tm
    return pl.pallas_call(
        functools.partial(_final_body, tm=tm),
        grid=(nsteps,),
        in_specs=[
            pl.BlockSpec((1, 1, tm), lambda i: (i, 0, 0), memory_space=pltpu.SMEM),
            pl.BlockSpec((1, 1, tm), lambda i: (jnp.minimum(i + 1, nsteps - 1), 0, 0), memory_space=pltpu.SMEM),
            pl.BlockSpec(memory_space=pl.ANY),
            pl.BlockSpec((tm, D_MODEL), lambda i: (i, 0)),
            pl.BlockSpec((1, D_MODEL), lambda i: (0, 0)),
            pl.BlockSpec((1, D_MODEL), lambda i: (0, 0)),
        ],
        out_specs=[pl.BlockSpec((tm, D_MODEL), lambda i: (i, 0)), pl.BlockSpec((tm, D_MODEL), lambda i: (i, 0))],
        out_shape=[jax.ShapeDtypeStruct((t_tokens, D_MODEL), F32), jax.ShapeDtypeStruct((t_tokens, D_MODEL), BF16)],
        scratch_shapes=[pltpu.VMEM((2, tm, D_MODEL), F32), pltpu.SemaphoreType.DMA((2,))],
        compiler_params=_cparams(1),
        name="moe_combine_ln",
    )(pos3, pos3, y_sorted, x1, ln_g, ln_b)


def _rope_tables(seq):
    inv = 1.0 / (ROPE_THETA ** (jnp.arange(0, HEAD_DIM, 2, dtype=F32) / HEAD_DIM))
    ang = jnp.arange(seq, dtype=F32)[:, None] * inv[None, :]
    return jnp.cos(ang), jnp.sin(ang)


def _halves_layout(w_heads):
    d, n = w_heads.shape
    t = w_heads.reshape(d, n // PROJ_TN, HEADS_PER_TILE, 2, HEAD_DIM // 2)
    return t.transpose(0, 1, 3, 2, 4).reshape(d, n)


def _reorder_w_in(w):
    pts = np.cumsum((0,) + IN_SIZES)
    sec = [w[:, pts[k]:pts[k + 1]] for k in range(len(IN_SIZES))]
    a_q, a_c, a_iq, a_ik, a_iw, b_q, b_k, b_v, c_q, c_k, c_v, gates = sec
    zeros = lambda n: jnp.zeros((w.shape[0], n), w.dtype)
    w_rope = _halves_layout(jnp.concatenate([a_q, a_iq, b_q, b_k, c_q, c_k, a_ik, zeros(HEAD_DIM)], axis=1))
    w_plain = jnp.concatenate([gates, b_v, a_c, a_iw, zeros(LANES - A_IDX_HEADS), c_v, zeros(LANES)], axis=1)
    assert w_rope.shape[1] == N_ROPE and w_plain.shape[1] == N_PLAIN
    return w_rope.astype(BF16), w_plain.astype(BF16)


def _kv_weights(w_uk, w_uv):
    half = HEAD_DIM // 2
    zeros = lambda n: jnp.zeros((w_uk.shape[0], n), w_uk.dtype)
    return jnp.concatenate([w_uk[:, :half], w_uv, zeros(HALF_TN - half - HEAD_DIM),
                            w_uk[:, half:], zeros(HALF_TN - half)], axis=1).astype(BF16)


def _moe_plan(cls, n_tokens):
    ntiles = n_tokens // MOE_TILE + N_CLASSES
    onehot = (cls[:, None] == jnp.arange(N_CLASSES, dtype=jnp.int32)[None, :]).astype(jnp.int32)
    csum = jnp.cumsum(onehot, axis=0)
    rank = jnp.sum(onehot * csum, axis=1) - 1
    counts = csum[-1]
    ptiles = (counts + MOE_TILE - 1) // MOE_TILE
    tile_end = jnp.cumsum(ptiles)
    tile_start = tile_end - ptiles
    pos = (tile_start[cls] * MOE_TILE + rank).astype(jnp.int32)
    nused = tile_end[-1:].astype(jnp.int32)
    tile_ids = jnp.arange(ntiles, dtype=jnp.int32)
    tile_cls = jnp.minimum(jnp.sum((tile_ids[:, None] >= tile_end[None, :]).astype(jnp.int32), axis=1),
                           N_CLASSES - 1).astype(jnp.int32)
    pair = np.array(PAIRS, dtype=np.int32)
    e_lo = jnp.asarray(np.repeat(np.arange(N_GROUPS), len(PAIRS)) * EXPERTS_PER_GROUP + np.tile(pair[:, 0], N_GROUPS), jnp.int32)
    e_hi = jnp.asarray(np.repeat(np.arange(N_GROUPS), len(PAIRS)) * EXPERTS_PER_GROUP + np.tile(pair[:, 1], N_GROUPS), jnp.int32)
    src = jnp.zeros((ntiles * MOE_TILE,), jnp.int32).at[pos].set(jnp.arange(n_tokens, dtype=jnp.int32))
    return e_lo[tile_cls], e_hi[tile_cls], nused, src.reshape(ntiles, 1, MOE_TILE), pos


def kernel(x, w_in, a_w_uk, a_w_uv, c_sinks, w_branch, w_o, ln1_g, ln1_b, router_w, router_b,
           moe_w_gate, moe_w_up, moe_w_down, ln2_g, ln2_b):
    bsz, seq, _ = x.shape
    n_tokens = bsz * seq
    topk = min(A_TOPK_MAX, seq // 4)
    nkb = seq // B_BLOCK
    tq = ATTN_TQ
    kc = min(DSA_KC, seq)
    tm_proj = min(PROJ_TM, seq)
    tm_row = ROW_TM
    half = HEAD_DIM // 2

    cos, sin = _rope_tables(seq)
    cos_in = jnp.tile(cos, (1, HEADS_PER_TILE))
    sin_in = jnp.tile(sin, (1, HEADS_PER_TILE))
    cos_kv = jnp.concatenate([cos, jnp.ones((seq, HALF_TN - half), F32)], axis=1)
    sin_kv = jnp.concatenate([sin, jnp.zeros((seq, HALF_TN - half), F32)], axis=1)
    rwT = router_w.T
    rb = router_b.reshape(N_EXPERTS, 1)

    xf = x.reshape(n_tokens, D_MODEL)
    xb = xf.astype(BF16)
    for l in range(DEPTH):
        w_rope, w_plain = _reorder_w_in(w_in[l])
        h_rope = _proj_rope(xb, 0, D_MODEL, w_rope, cos_in, sin_in, tm=tm_proj, seq=seq)
        h_plain = _proj_plain(xb, w_plain, tm=tm_proj)
        kv = _proj_rope(h_plain, COL_AC // A_KV_RANK, A_KV_RANK, _kv_weights(a_w_uk[l], a_w_uv[l]),
                        cos_kv, sin_kv, tm=tm_proj, seq=seq)

        bvT, cvT, kvT = _values_T(h_plain, kv, n_tokens=n_tokens)

        o_a = _dsa(h_rope, h_plain, kv, kvT, bsz=bsz, seq=seq, tq=min(DSA_TQ, seq), kc=kc, topk=topk)
        kmean = _moba_kmean(h_rope, n_tokens=n_tokens).reshape(bsz, nkb, B_W)
        sel = _moba_gate(h_rope, kmean, bsz=bsz, seq=seq, tq=min(GATE_TQ, seq))
        o_b = _moba(h_rope, bvT, sel, bsz=bsz, seq=seq)
        o_c = _swa(c_sinks[l], h_rope, cvT, bsz=bsz, seq=seq, tq=tq)

        wb_all = w_branch[l].astype(BF16)
        x1, cls = _merge(o_a, o_b, o_c, h_plain, xf,
                              wb_all[:A_W], wb_all[A_W:A_W + B_W], wb_all[A_W + B_W:], w_o[l].astype(BF16),
                              ln1_g[l].reshape(1, D_MODEL), ln1_b[l].reshape(1, D_MODEL), rwT, rb, tm=tm_row)

        e1, e2, nused, src3, pos = _moe_plan(cls[0], n_tokens)
        y_sorted = _experts(e1, e2, nused, src3, x1,
                            moe_w_gate[l].astype(BF16), moe_w_up[l].astype(BF16), moe_w_down[l].astype(BF16))
        xf, xb = _final(pos.reshape(n_tokens // tm_row, 1, tm_row), y_sorted, x1,
                        ln2_g[l].reshape(1, D_MODEL), ln2_b[l].reshape(1, D_MODEL), tm=tm_row)
    return xf.reshape(bsz, seq, D_MODEL)
```

```python
import functools
import math

import jax
import jax.numpy as jnp
import numpy as np
from jax import lax
from jax.experimental import pallas as pl
from jax.experimental.pallas import tpu as pltpu

D_MODEL = 1024
DEPTH = 2
HEAD_DIM = 64
ROPE_THETA = 10000.0
LN_EPS = 1e-5
A_HEADS = 4
A_KV_RANK = 128
A_IDX_HEADS = 4
A_IDX_DIM = 64
A_TOPK_MAX = 256
B_HEADS = 4
B_BLOCK = 256
B_TOPK_BLOCKS = 3
C_HEADS = 8
C_KV_HEADS = 2
C_WINDOW = 128
N_BRANCH = 3
A_W = A_HEADS * HEAD_DIM
B_W = B_HEADS * HEAD_DIM
C_W = C_HEADS * HEAD_DIM
IN_SIZES = (A_W, A_KV_RANK, A_IDX_HEADS * A_IDX_DIM, A_IDX_DIM, A_IDX_HEADS,
            B_W, B_W, B_W, C_W, C_KV_HEADS * HEAD_DIM, C_KV_HEADS * HEAD_DIM,
            N_BRANCH * D_MODEL)
N_EXPERTS = 16
N_GROUPS = 4
EXPERTS_PER_GROUP = 4
D_EXPERT = 512
DN_ALPHA = (2 * DEPTH) ** 0.25
ATTN_SCALE = HEAD_DIM ** -0.5
LOG2E = math.log2(math.e)
Q_SCALE = ATTN_SCALE * LOG2E

LANES = 128
SUBLANES = 8
VMEM_LIMIT_BYTES = 56 * 1024 * 1024

NEG_BIG = -1e30
BF16 = jnp.bfloat16
F32 = jnp.float32

PROJ_TN = 256
HALF_TN = PROJ_TN // 2
HEADS_PER_TILE = PROJ_TN // HEAD_DIM
TILE_AQ = 0
TILE_AIQ = 1
TILE_BQ = 2
TILE_BK = 3
TILE_CQ = 4
TILE_CK_AIK = 6
N_ROPE = 7 * PROJ_TN
COL_GATES = 0
COL_BV = 3072
COL_AC = 3328
COL_AIW = 3456
COL_CV = 3584
N_PLAIN = 3840
PLAIN_TN = 1920
KV_VA0 = HEAD_DIM // 2

PAIRS = ((0, 1), (0, 2), (0, 3), (1, 2), (1, 3), (2, 3))
N_CLASSES = N_GROUPS * len(PAIRS)
MOE_TILE = 256
X1_COLS = D_MODEL + LANES

PROJ_TM = 1024
PROJ_SUB = 256
ROW_TM = 512
ATTN_TQ = 256
DSA_TQ = 512
DSA_KC = 256
assert COL_GATES == 0

VALUE_MID_PASSES = 16
PEEL_MAX = 2.0
MIN_NORMAL_KEY = 0x00800000
MAX_SELECT_PASSES = 96


def _cparams(n_axes):
    return pltpu.CompilerParams(dimension_semantics=("arbitrary",) * n_axes,
                                vmem_limit_bytes=VMEM_LIMIT_BYTES)


def _proj_rope_body(x_ref, w_ref, cos_ref, sin_ref, o_ref, *, n_sub):
    sub = x_ref.shape[0] // n_sub
    for t in range(w_ref.shape[0]):
        for mi in range(n_sub):
            rows = slice(mi * sub, (mi + 1) * sub)
            acc = jnp.dot(x_ref[rows, :], w_ref[t], preferred_element_type=F32)
            a1, a2 = acc[:, :HALF_TN], acc[:, HALF_TN:]
            c, sn = cos_ref[rows, :], sin_ref[rows, :]
            c0 = t * PROJ_TN
            o_ref[rows, c0:c0 + HALF_TN] = (a1 * c - a2 * sn).astype(o_ref.dtype)
            o_ref[rows, c0 + HALF_TN:c0 + PROJ_TN] = (a2 * c + a1 * sn).astype(o_ref.dtype)


def _proj_plain_body(x_ref, w_ref, o_ref, *, n_sub):
    sub = x_ref.shape[0] // n_sub
    tn = o_ref.shape[1]
    is_gate = (pl.program_id(1) * tn + lax.broadcasted_iota(jnp.int32, (1, tn), 1)) < COL_GATES + N_BRANCH * D_MODEL
    for mi in range(n_sub):
        rows = slice(mi * sub, (mi + 1) * sub)
        acc = jnp.dot(x_ref[rows, :], w_ref[0], preferred_element_type=F32)
        o_ref[rows, :] = jnp.where(is_gate, jax.nn.sigmoid(acc), acc).astype(o_ref.dtype)


def _column_tiles(w, tn=PROJ_TN):
    k_dim, n = w.shape
    return w.reshape(k_dim, n // tn, tn).transpose(1, 0, 2)


def _proj_rope(x, x_col_block, k_dim, w, cos_t, sin_t, *, tm, seq):
    t_tokens = x.shape[0]
    n = w.shape[1]
    pos_blocks = seq // tm
    return pl.pallas_call(
        functools.partial(_proj_rope_body, n_sub=max(tm // PROJ_SUB, 1)),
        grid=(t_tokens // tm,),
        in_specs=[
            pl.BlockSpec((tm, k_dim), lambda i: (i, x_col_block)),
            pl.BlockSpec((n // PROJ_TN, k_dim, PROJ_TN), lambda i: (0, 0, 0)),
            pl.BlockSpec((tm, HALF_TN), lambda i: (i % pos_blocks, 0)),
            pl.BlockSpec((tm, HALF_TN), lambda i: (i % pos_blocks, 0)),
        ],
        out_specs=pl.BlockSpec((tm, n), lambda i: (i, 0)),
        out_shape=jax.ShapeDtypeStruct((t_tokens, n), BF16),
        compiler_params=_cparams(1),
        name="proj_rope",
    )(x, _column_tiles(w), cos_t, sin_t)


def _proj_plain(x, w, *, tm):
    t_tokens, k_dim = x.shape
    n = w.shape[1]
    return pl.pallas_call(
        functools.partial(_proj_plain_body, n_sub=max(tm // PROJ_SUB, 1)),
        grid=(t_tokens // tm, n // PLAIN_TN),
        in_specs=[
            pl.BlockSpec((tm, k_dim), lambda i, j: (i, 0)),
            pl.BlockSpec((1, k_dim, PLAIN_TN), lambda i, j: (j, 0, 0)),
        ],
        out_specs=pl.BlockSpec((tm, PLAIN_TN), lambda i, j: (i, j)),
        out_shape=jax.ShapeDtypeStruct((t_tokens, n), BF16),
        compiler_params=_cparams(2),
        name="proj_plain",
    )(x, _column_tiles(w, PLAIN_TN))


def _values_T_body(bv_ref, cv_ref, kv_ref, bvT_ref, cvT_ref, kvT_ref):
    for src, dst in ((bv_ref, bvT_ref), (cv_ref, cvT_ref), (kv_ref, kvT_ref)):
        for g in range(dst.shape[0]):
            dst[g] = src[g * B_BLOCK:(g + 1) * B_BLOCK, :].astype(F32).T.astype(dst.dtype)


def _values_T(h_plain, kv, *, n_tokens):
    nblk = n_tokens // B_BLOCK
    gb = math.gcd(nblk, SUBLANES)
    in_tile = lambda col: pl.BlockSpec((gb * B_BLOCK, PROJ_TN), lambda i: (i, col // PROJ_TN))
    out_tile = pl.BlockSpec((gb, PROJ_TN, B_BLOCK), lambda i: (i, 0, 0))
    shape = jax.ShapeDtypeStruct((nblk, PROJ_TN, B_BLOCK), BF16)
    return pl.pallas_call(
        _values_T_body,
        grid=(nblk // gb,),
        in_specs=[in_tile(COL_BV), in_tile(COL_CV), in_tile(0)],
        out_specs=[out_tile, out_tile, out_tile],
        out_shape=[shape, shape, shape],
        compiler_params=_cparams(1),
        name="values_transpose",
    )(h_plain, h_plain, kv)


def _f32_to_key(x):
    b = lax.bitcast_convert_type(x, jnp.int32)
    return jnp.where(b < 0, b ^ jnp.int32(0x7FFFFFFF), b)


def _key_to_f32(k):
    b = jnp.where(k < 0, k ^ jnp.int32(0x7FFFFFFF), k)
    return lax.bitcast_convert_type(b, F32)


def _fold8(x, op):
    return op(x.reshape(x.shape[0] // SUBLANES, SUBLANES, x.shape[1]), axis=0)


def _tile_T(x_ref):
    return x_ref[...].astype(F32).T


def _head_rows(xT, h):
    half = HEAD_DIM // 2
    return xT[h * half:(h + 1) * half], xT[HALF_TN + h * half:HALF_TN + (h + 1) * half]


def _place_head(first, second, slot):
    half = HEAD_DIM // 2
    w = first.shape[1]
    before, after = slot * half, HALF_TN - (slot + 1) * half
    pieces = []
    for part in (first, second):
        pieces += [jnp.zeros((before, w), part.dtype)] * (before > 0) + [part] + [jnp.zeros((after, w), part.dtype)] * (after > 0)
    return jnp.concatenate(pieces, axis=0)


def _store_rows(o_ref, oT_heads):
    o_ref[...] = jnp.concatenate(oT_heads, axis=0).T.astype(o_ref.dtype)


IK_SLOT = 2


def _dsa_body(q_ref, iq_ref, iw_ref, ik_ref, kv_ref, vT_ref, o_ref, sc_ref, acc_ref, sbuf_ref, *, tq, kc, topk):
    i = pl.program_id(1)
    q0 = i * tq
    nch = (q0 + tq + kc - 1) // kc
    qpos = q0 + lax.broadcasted_iota(jnp.int32, (1, tq), 1)
    iqT = _tile_T(iq_ref)
    iq_ops = [_place_head(*[p.astype(BF16) for p in _head_rows(iqT, h)], IK_SLOT) for h in range(A_IDX_HEADS)]
    w_idx = iw_ref[...].astype(F32).T[:A_IDX_HEADS]

    def score_chunk(c, carry, diagonal):
        mn, mx = carry
        k0 = pl.multiple_of(c * kc, kc)
        ikc = ik_ref[pl.ds(k0, kc), :]
        acc = jnp.zeros((kc, tq), F32)
        for h in range(A_IDX_HEADS):
            d = jnp.dot(ikc, iq_ops[h], preferred_element_type=F32)
            acc = acc + w_idx[h:h + 1, :] * jnp.maximum(d, 0.0)
        if diagonal:
            causal = (k0 + lax.broadcasted_iota(jnp.int32, (kc, 1), 0)) <= qpos
            lo_part, hi_part = jnp.where(causal, acc, jnp.inf), jnp.where(causal, acc, -jnp.inf)
        else:
            lo_part = hi_part = acc
        sc_ref[pl.ds(k0, kc), :] = hi_part
        return jnp.minimum(mn, _fold8(lo_part, jnp.min)), jnp.maximum(mx, _fold8(hi_part, jnp.max))

    n_full = (q0 + 1) // kc
    carry = lax.fori_loop(0, n_full, functools.partial(score_chunk, diagonal=False),
                          (jnp.full((SUBLANES, tq), jnp.inf, F32), jnp.full((SUBLANES, tq), -jnp.inf, F32)))
    mn8, mx8 = lax.fori_loop(n_full, nch, functools.partial(score_chunk, diagonal=True), carry)
    row_min = jnp.min(mn8, axis=0, keepdims=True)
    row_max = jnp.max(mx8, axis=0, keepdims=True)

    pc = 2 * kc
    npair = (nch + 1) // 2
    n_chunks_total = vT_ref.shape[0]

    @pl.when(nch % 2 == 1)
    def _():
        sc_ref[pl.ds(pl.multiple_of(nch * kc, kc), kc), :] = jnp.full((kc, tq), -jnp.inf, F32)

    def count_ge(thr):
        def body(j, cnt):
            s = sc_ref[pl.ds(pl.multiple_of(j * pc, pc), pc), :]
            return cnt + _fold8(jnp.where(s >= thr, 1.0, 0.0), jnp.sum)
        cnt8 = lax.fori_loop(0, npair, body, jnp.zeros((SUBLANES, tq), F32))
        return jnp.sum(cnt8, axis=0, keepdims=True)

    def min_ge(thr):
        def body(j, mn):
            s = sc_ref[pl.ds(pl.multiple_of(j * pc, pc), pc), :]
            return jnp.minimum(mn, _fold8(jnp.where(s >= thr, s, jnp.inf), jnp.min))
        mn8 = lax.fori_loop(0, npair, body, jnp.full((SUBLANES, tq), jnp.inf, F32))
        return jnp.min(mn8, axis=0, keepdims=True)

    def count_zero():
        def body(j, carry):
            c0, cp = carry
            s = sc_ref[pl.ds(pl.multiple_of(j * pc, pc), pc), :]
            return (c0 + _fold8(jnp.where(s >= 0.0, 1.0, 0.0), jnp.sum),
                    cp + _fold8(jnp.where(s > 0.0, 1.0, 0.0), jnp.sum))
        z8 = jnp.zeros((SUBLANES, tq), F32)
        c0, cp = lax.fori_loop(0, npair, body, (z8, z8))
        return jnp.sum(c0, axis=0, keepdims=True), jnp.sum(cp, axis=0, keepdims=True)

    kf = float(topk)
    n_valid = (qpos + 1).astype(F32)
    few = n_valid <= kf
    cnt_nonneg, cnt_pos = count_zero()
    is_zero = jnp.logical_and(cnt_pos < kf, cnt_nonneg >= kf)
    is_pos = cnt_pos >= kf
    lo0 = jnp.where(is_zero, 0, jnp.where(is_pos, MIN_NORMAL_KEY, _f32_to_key(row_min)))
    hi0 = jnp.where(is_zero, 1, jnp.where(is_pos, _f32_to_key(row_max) + 1, -1))
    cnt0 = jnp.where(is_zero, cnt_nonneg, jnp.where(is_pos, cnt_pos, n_valid))
    lo0 = jnp.where(few, _f32_to_key(row_min), lo0)
    hi0 = jnp.where(few, lo0 + 1, hi0)

    def unfinished(lo, hi, cnt_lo):
        return jnp.logical_and(lo + 1 != hi, cnt_lo != kf)

    def surplus(lo, hi, cnt_lo):
        return jnp.max(jnp.where(unfinished(lo, hi, cnt_lo), cnt_lo - kf, 0.0))

    def bis_cond(st):
        _, it, excess = st
        return jnp.logical_and(excess > 0.0, it < MAX_SELECT_PASSES)

    def bisect_once(st):
        lo, hi, cnt_lo, it = st
        nd = unfinished(lo, hi, cnt_lo)
        mid_val = _f32_to_key(0.5 * _key_to_f32(lo) + 0.5 * _key_to_f32(hi))
        mid_key = lo + lax.shift_right_logical(hi - lo, 1)
        mid = jnp.where(it < VALUE_MID_PASSES, jnp.clip(mid_val, lo + 1, hi - 1), mid_key)
        cnt = count_ge(_key_to_f32(mid))
        ge = cnt >= kf
        up = jnp.logical_and(nd, ge)
        down = jnp.logical_and(nd, jnp.logical_not(ge))
        return jnp.where(up, mid, lo), jnp.where(down, mid, hi), jnp.where(up, cnt, cnt_lo), it + 1

    def bisect(st):
        return bisect_once(bisect_once(st))

    def peel(st):
        lo, hi, cnt_lo, it = st
        nd = unfinished(lo, hi, cnt_lo)
        k1 = jnp.maximum(_f32_to_key(min_ge(_key_to_f32(lo))), lo)
        cnt = count_ge(_key_to_f32(k1 + 1))
        ge = cnt >= kf
        up = jnp.logical_and(nd, ge)
        down = jnp.logical_and(nd, jnp.logical_not(ge))
        new_lo = jnp.where(nd, jnp.where(ge, k1 + 1, k1), lo)
        return new_lo, jnp.where(down, k1 + 1, hi), jnp.where(up, cnt, cnt_lo), it + 2

    def bis_body(st):
        (lo, hi, cnt_lo), it, excess = st
        lo, hi, cnt_lo, it = lax.cond(excess > PEEL_MAX, bisect, peel, (lo, hi, cnt_lo, it))
        return (lo, hi, cnt_lo), it, surplus(lo, hi, cnt_lo)

    (lo, _, cnt_lo), _, _ = lax.while_loop(bis_cond, bis_body,
                                           ((lo0, hi0, cnt0), jnp.int32(0), surplus(lo0, hi0, cnt0)))
    thr = _key_to_f32(lo)

    tie = jnp.logical_and(cnt_lo > kf, jnp.logical_not(few))

    @pl.when(jnp.max(jnp.where(tie, 1.0, 0.0)) > 0.0)
    def _():
        def count_gt():
            def gt_body(c, cnt):
                k0 = pl.multiple_of(c * kc, kc)
                s = sc_ref[pl.ds(k0, kc), :]
                return cnt + _fold8(jnp.where(s > thr, 1.0, 0.0), jnp.sum)
            gt8 = lax.fori_loop(0, nch, gt_body, jnp.zeros((SUBLANES, tq), F32))
            return jnp.sum(gt8, axis=0, keepdims=True)

        zero_thr = thr == 0.0
        other = jnp.max(jnp.where(jnp.logical_and(tie, jnp.logical_not(zero_thr)), 1.0, 0.0)) > 0.0
        gt = lax.cond(other, count_gt, lambda: cnt_pos)
        need = kf - jnp.where(zero_thr, cnt_pos, gt)
        r = lax.broadcasted_iota(jnp.int32, (kc, kc), 0)
        cidx = lax.broadcasted_iota(jnp.int32, (kc, kc), 1)
        tri = jnp.where(cidx <= r, 1.0, 0.0).astype(BF16)

        thr_tie = jnp.where(tie, thr, jnp.nan)
        keep = jnp.where(tie, need, jnp.inf)

        def tie_body(c, seen):
            k0 = pl.multiple_of(c * kc, kc)
            s = sc_ref[pl.ds(k0, kc), :]
            eqf = jnp.where(s == thr_tie, 1.0, 0.0)
            rank = jnp.dot(tri, eqf.astype(BF16), preferred_element_type=F32) + seen
            sc_ref[pl.ds(k0, kc), :] = jnp.where(eqf * rank > keep, -jnp.inf, s)
            return seen + jnp.sum(eqf, axis=0, keepdims=True)

        lax.fori_loop(0, nch, tie_body, jnp.zeros((1, tq), F32))

    qT_all = (_tile_T(q_ref) * Q_SCALE).astype(BF16)
    qT = jnp.concatenate([_place_head(*_head_rows(qT_all, h), 0) for h in range(A_HEADS)], axis=1)
    acc_ref[...] = jnp.zeros_like(acc_ref)
    ones_rows = jnp.ones((SUBLANES, kc), BF16)

    def issue_scores(c, slot):
        k0 = pl.multiple_of(jnp.minimum(c, n_chunks_total - 1) * kc, kc)
        sbuf_ref[slot] = jnp.dot(kv_ref[pl.ds(k0, kc), :], qT, preferred_element_type=F32)

    def softmax_pv(c, slot, m):
        k0 = pl.multiple_of(c * kc, kc)
        v1 = jnp.concatenate([vT_ref[c, KV_VA0:KV_VA0 + HEAD_DIM, :], ones_rows], axis=0)
        sel = sc_ref[pl.ds(k0, kc), :] >= thr
        s = jnp.concatenate([jnp.where(sel, sbuf_ref[slot, :, h * tq:(h + 1) * tq], NEG_BIG)
                             for h in range(A_HEADS)], axis=1)
        m_new = jnp.maximum(m, jnp.max(s, axis=0, keepdims=True))
        alpha = jnp.exp2(m - m_new)
        p = jnp.exp2((s - m_new).astype(BF16))
        acc_ref[...] = acc_ref[...] * alpha + jnp.dot(v1, p, preferred_element_type=F32)
        return m_new

    def chunk_pair(j, m):
        c0 = 2 * j
        issue_scores(c0 + 1, 1)
        m = softmax_pv(c0, 0, m)
        issue_scores(c0 + 2, 0)
        return softmax_pv(c0 + 1, 1, m)

    issue_scores(0, 0)
    lax.fori_loop(0, npair, chunk_pair, jnp.full((1, A_HEADS * tq), NEG_BIG, F32))
    a = acc_ref[...]
    o = a[:HEAD_DIM] / a[HEAD_DIM:HEAD_DIM + 1]
    _store_rows(o_ref, [o[:, h * tq:(h + 1) * tq] for h in range(A_HEADS)])


def _dsa(h_rope, h_plain, kv, kvT, *, bsz, seq, tq, kc, topk):
    nq = seq // tq
    assert kc == B_BLOCK
    return pl.pallas_call(
        functools.partial(_dsa_body, tq=tq, kc=kc, topk=topk),
        grid=(bsz, nq),
        in_specs=[
            pl.BlockSpec((tq, PROJ_TN), lambda b, i: (b * nq + i, TILE_AQ)),
            pl.BlockSpec((tq, PROJ_TN), lambda b, i: (b * nq + i, TILE_AIQ)),
            pl.BlockSpec((tq, LANES), lambda b, i: (b * nq + i, COL_AIW // LANES)),
            pl.BlockSpec((seq, PROJ_TN), lambda b, i: (b, TILE_CK_AIK)),
            pl.BlockSpec((seq, PROJ_TN), lambda b, i: (b, 0)),
            pl.BlockSpec((seq // kc, PROJ_TN, kc), lambda b, i: (b, 0, 0)),
        ],
        out_specs=pl.BlockSpec((tq, A_W), lambda b, i: (b * nq + i, 0)),
        out_shape=jax.ShapeDtypeStruct((bsz * seq, A_W), BF16),
        scratch_shapes=[pltpu.VMEM((seq + kc, tq), F32),
                        pltpu.VMEM((HEAD_DIM + SUBLANES, A_HEADS * tq), F32),
                        pltpu.VMEM((2, kc, A_HEADS * tq), F32)],
        compiler_params=_cparams(2),
        name="dsa_attention",
    )(h_rope, h_rope, h_plain, h_rope, kv, kvT)


def _kmean_body(k_ref, o_ref):
    gb = o_ref.shape[0]
    o_ref[...] = jnp.mean(k_ref[...].astype(F32).reshape(gb, B_BLOCK, B_W), axis=1)


def _moba_kmean(h_rope, *, n_tokens):
    nblk = n_tokens // B_BLOCK
    gb = math.gcd(nblk, SUBLANES)
    return pl.pallas_call(
        _kmean_body,
        grid=(nblk // gb,),
        in_specs=[pl.BlockSpec((gb * B_BLOCK, B_W), lambda i: (i, TILE_BK))],
        out_specs=pl.BlockSpec((gb, B_W), lambda i: (i, 0)),
        out_shape=jax.ShapeDtypeStruct((nblk, B_W), F32),
        compiler_params=_cparams(1),
        name="moba_kmean",
    )(h_rope)


def _moba_select(qT, km, qpos, nsel):
    nkb, tq = km.shape[0], qT.shape[1]
    cur = qpos // B_BLOCK
    row = lax.broadcasted_iota(jnp.int32, (nkb, tq), 0)
    past = row < cur
    col_head = (lax.broadcasted_iota(jnp.int32, km.shape, 1) % HALF_TN) // (HEAD_DIM // 2)
    chosen = []
    for h in range(B_HEADS):
        km_h = jnp.where(col_head == h, km, 0.0).astype(BF16)
        g = jnp.dot(km_h, qT, preferred_element_type=F32)
        g = jnp.where(past, g, -jnp.inf)
        sel = jnp.zeros((nkb, tq), F32)
        for _ in range(nsel):
            mx = jnp.max(g, axis=0, keepdims=True)
            idx = jnp.min(jnp.where(g == mx, row, nkb), axis=0, keepdims=True)
            pick = row == idx
            sel = jnp.where(pick, 1.0, sel)
            g = jnp.where(pick, -jnp.inf, g)
        chosen.append(jnp.where(past, sel, 0.0))
    return chosen


def _moba_body(q_ref, k_ref, vT_ref, km_ref, o_ref, acc_ref, sbuf_ref, sel_ref, *, tq, nsel):
    i = pl.program_id(1)
    q32 = _tile_T(q_ref)
    qpos = i * tq + lax.broadcasted_iota(jnp.int32, (1, tq), 1)
    for h, sel in enumerate(_moba_select(q32.astype(BF16), km_ref[0], qpos, nsel)):
        sel_ref[h] = sel
    qT_all = (q32 * Q_SCALE).astype(BF16)
    qT = [_place_head(*_head_rows(qT_all, h), h) for h in range(B_HEADS)]
    ones_rows = jnp.ones((SUBLANES, B_BLOCK), BF16)
    n_past = i

    def values(h, j):
        return jnp.concatenate([vT_ref[j, h * HEAD_DIM:(h + 1) * HEAD_DIM, :], ones_rows], axis=0)

    def issue_scores(j, slot):
        k0 = pl.multiple_of(jnp.minimum(j, i) * B_BLOCK, B_BLOCK)
        for h in range(B_HEADS):
            sbuf_ref[slot, h] = jnp.dot(k_ref[pl.ds(k0, B_BLOCK), :], qT[h], preferred_element_type=F32)

    def softmax_pv(j, slot, ms):
        ps, alphas, new_ms = [], [], []
        for h in range(B_HEADS):
            s = jnp.where(sel_ref[h, pl.ds(j, 1), :] > 0.0, sbuf_ref[slot, h], NEG_BIG)
            m_new = jnp.maximum(ms[h], jnp.max(s, axis=0, keepdims=True))
            alphas.append(jnp.exp2(ms[h] - m_new))
            ps.append(jnp.exp2((s - m_new).astype(BF16)))
            new_ms.append(m_new)
        for h in range(B_HEADS):
            acc_ref[h] = acc_ref[h] * alphas[h] + jnp.dot(values(h, j), ps[h], preferred_element_type=F32)
        return tuple(new_ms)

    own0 = pl.multiple_of(i * B_BLOCK, B_BLOCK)
    s_own = [jnp.dot(k_ref[pl.ds(own0, B_BLOCK), :], qT[h], preferred_element_type=F32) for h in range(B_HEADS)]
    issue_scores(0, 0)
    causal = lax.broadcasted_iota(jnp.int32, (B_BLOCK, tq), 0) <= lax.broadcasted_iota(jnp.int32, (B_BLOCK, tq), 1)
    ms = []
    for h in range(B_HEADS):
        s = jnp.where(causal, s_own[h], NEG_BIG)
        m0 = jnp.max(s, axis=0, keepdims=True)
        acc_ref[h] = jnp.dot(values(h, i), jnp.exp2((s - m0).astype(BF16)), preferred_element_type=F32)
        ms.append(m0)

    def block_pair(jj, ms):
        j0 = 2 * jj
        issue_scores(j0 + 1, 1)
        ms = softmax_pv(j0, 0, ms)
        issue_scores(j0 + 2, 0)
        return softmax_pv(j0 + 1, 1, ms)

    lax.fori_loop(0, (n_past + 1) // 2, block_pair, tuple(ms))
    outs = []
    for h in range(B_HEADS):
        a = acc_ref[h]
        outs.append(a[:HEAD_DIM] / a[HEAD_DIM:HEAD_DIM + 1])
    _store_rows(o_ref, outs)


def _moba(h_rope, bvT, kmean, *, bsz, seq):
    tq = B_BLOCK
    nkb = seq // B_BLOCK
    nq = seq // tq
    return pl.pallas_call(
        functools.partial(_moba_body, tq=tq, nsel=min(B_TOPK_BLOCKS, nkb)),
        grid=(bsz, nq),
        in_specs=[
            pl.BlockSpec((tq, PROJ_TN), lambda b, i: (b * nq + i, TILE_BQ)),
            pl.BlockSpec((seq, PROJ_TN), lambda b, i: (b, TILE_BK)),
            pl.BlockSpec((nkb, B_W, B_BLOCK), lambda b, i: (b, 0, 0)),
            pl.BlockSpec((1, nkb, B_W), lambda b, i: (b, 0, 0)),
        ],
        out_specs=pl.BlockSpec((tq, B_W), lambda b, i: (b * nq + i, 0)),
        out_shape=jax.ShapeDtypeStruct((bsz * seq, B_W), BF16),
        scratch_shapes=[pltpu.VMEM((B_HEADS, HEAD_DIM + SUBLANES, tq), F32),
                        pltpu.VMEM((2, B_HEADS, B_BLOCK, tq), F32),
                        pltpu.VMEM((B_HEADS, nkb, tq), F32)],
        compiler_params=_cparams(2),
        name="moba_attention",
    )(h_rope, h_rope, bvT, kmean)


def _swa_body(sink_ref, q0_ref, q1_ref, k0_ref, k1_ref, k2_ref, vprev_ref, vcur_ref, o_ref, *, tq):
    i = pl.program_id(1)
    q0 = i * tq
    qpos = q0 + lax.broadcasted_iota(jnp.int32, (1, tq), 1)
    nk = tq + C_WINDOW
    kpos = q0 - C_WINDOW + lax.broadcasted_iota(jnp.int32, (nk, 1), 0)
    diff = qpos - kpos
    ok = jnp.where(diff >= 0, jnp.where(diff < C_WINDOW, jnp.where(kpos >= 0, 1.0, 0.0), 0.0), 0.0) > 0.0
    ones_rows = jnp.ones((SUBLANES, nk), BF16)
    group = C_HEADS // C_KV_HEADS
    assert group == HEADS_PER_TILE
    kk = jnp.concatenate([k0_ref[...], k1_ref[...], k2_ref[...]], axis=0)
    outs = []
    scores = []
    for kv, q_ref in enumerate((q0_ref, q1_ref)):
        qT_all = (_tile_T(q_ref) * Q_SCALE).astype(BF16)
        qT = jnp.concatenate([_place_head(*_head_rows(qT_all, g), kv) for g in range(group)], axis=1)
        scores.append(jnp.dot(kk, qT, preferred_element_type=F32))
    for kv in range(C_KV_HEADS):
        vrows = slice(kv * HEAD_DIM, (kv + 1) * HEAD_DIM)
        vv = jnp.concatenate([vprev_ref[0, vrows, tq - C_WINDOW:], vcur_ref[0, vrows, :]], axis=1)
        v1 = jnp.concatenate([vv, ones_rows], axis=0)
        heads = [kv * group + g for g in range(group)]
        sink = jnp.concatenate([jnp.full((1, tq), sink_ref[hd] * LOG2E, F32) for hd in heads], axis=1)
        s = jnp.concatenate([jnp.where(ok, scores[kv][:, g * tq:(g + 1) * tq], NEG_BIG) for g in range(group)], axis=1)
        m = jnp.maximum(jnp.max(s, axis=0, keepdims=True), sink)
        p = jnp.exp2((s - m).astype(BF16))
        a = jnp.dot(v1, p, preferred_element_type=F32)
        o = a[:HEAD_DIM] / (a[HEAD_DIM:HEAD_DIM + 1] + jnp.exp2(sink - m))
        outs += [o[:, g * tq:(g + 1) * tq] for g in range(group)]
    _store_rows(o_ref, outs)


def _swa(sinks, h_rope, cvT, *, bsz, seq, tq):
    r = tq // C_WINDOW
    assert r == 2 and tq == B_BLOCK
    nq = seq // tq
    nwb = seq // C_WINDOW

    def kspec(off):
        return pl.BlockSpec((C_WINDOW, PROJ_TN),
                            lambda b, i, s: (b * nwb + jnp.maximum(i * r + off, 0), TILE_CK_AIK))

    def vspec(off):
        return pl.BlockSpec((1, PROJ_TN, B_BLOCK), lambda b, i, s: (b * nq + jnp.maximum(i + off, 0), 0, 0))

    grid_spec = pltpu.PrefetchScalarGridSpec(
        num_scalar_prefetch=1,
        grid=(bsz, nq),
        in_specs=[pl.BlockSpec((tq, PROJ_TN), lambda b, i, s: (b * nq + i, TILE_CQ)),
                  pl.BlockSpec((tq, PROJ_TN), lambda b, i, s: (b * nq + i, TILE_CQ + 1)),
                  kspec(-1), kspec(0), kspec(1), vspec(-1), vspec(0)],
        out_specs=pl.BlockSpec((tq, C_W), lambda b, i, s: (b * nq + i, 0)),
    )
    return pl.pallas_call(
        functools.partial(_swa_body, tq=tq),
        grid_spec=grid_spec,
        out_shape=jax.ShapeDtypeStruct((bsz * seq, C_W), BF16),
        compiler_params=_cparams(2),
        name="swa_attention",
    )(sinks, h_rope, h_rope, h_rope, h_rope, h_rope, cvT, cvT)


def _layer_norm(z, g, b):
    mu = jnp.mean(z, axis=-1, keepdims=True)
    zc = z - mu
    var = jnp.mean(zc * zc, axis=-1, keepdims=True)
    return zc * lax.rsqrt(var + LN_EPS) * g + b


def _split_bf16(a):
    hi = a.astype(BF16)
    lo = (a - hi.astype(F32)).astype(BF16)
    return hi, lo


def _router(x1, rw_ref, rb_ref, cls_ref):
    nt = (((1,), (1,)), ((), ()))
    xh, xl = _split_bf16(x1)
    wh, wl = _split_bf16(rw_ref[...])
    logits = (lax.dot_general(wh, xh, nt, preferred_element_type=F32)
              + lax.dot_general(wh, xl, nt, preferred_element_type=F32)
              + lax.dot_general(wl, xh, nt, preferred_element_type=F32))
    aff = jax.nn.sigmoid(logits)
    score = aff + rb_ref[...]
    tm = x1.shape[0]
    sc = [score[e:e + 1, :] for e in range(N_EXPERTS)]
    af = [aff[e:e + 1, :] for e in range(N_EXPERTS)]
    in_top = []
    grp_score = []
    for gq in range(N_GROUPS):
        gs = jnp.zeros((1, tm), F32)
        for a in range(EXPERTS_PER_GROUP):
            ea = gq * EXPERTS_PER_GROUP + a
            rank = jnp.zeros((1, tm), F32)
            for b in range(EXPERTS_PER_GROUP):
                if b == a:
                    continue
                eb = gq * EXPERTS_PER_GROUP + b
                beats = (sc[eb] >= sc[ea]) if b < a else (sc[eb] > sc[ea])
                rank = rank + jnp.where(beats, 1.0, 0.0)
            top = rank < 2.0
            in_top.append(top)
            gs = gs + jnp.where(top, sc[ea], 0.0)
        grp_score.append(gs)
    best = grp_score[0]
    gstar = jnp.zeros((1, tm), jnp.int32)
    for gq in range(1, N_GROUPS):
        better = grp_score[gq] > best
        best = jnp.where(better, grp_score[gq], best)
        gstar = jnp.where(better, gq, gstar)
    cls = jnp.zeros((1, tm), jnp.int32)
    w_lo = jnp.zeros((1, tm), F32)
    w_hi = jnp.zeros((1, tm), F32)
    for gq in range(N_GROUPS):
        is_g = gstar == gq
        for pi, (a, b) in enumerate(PAIRS):
            ea, eb = gq * EXPERTS_PER_GROUP + a, gq * EXPERTS_PER_GROUP + b
            hit = jnp.where(is_g, jnp.where(in_top[ea], jnp.where(in_top[eb], 1.0, 0.0), 0.0), 0.0) > 0.0
            cls = jnp.where(hit, gq * len(PAIRS) + pi, cls)
            tot = af[ea] + af[eb]
            w_lo = jnp.where(hit, af[ea] / tot, w_lo)
            w_hi = jnp.where(hit, af[eb] / tot, w_hi)
    cls_ref[...] = cls
    return w_lo, w_hi


def _merge_body(oa_ref, ob_ref, oc_ref, ga_ref, gb_ref, gc_ref, x_ref, wa_ref, wb_ref, wc_ref, wo_ref,
                g_ref, b_ref, rw_ref, rb_ref, x1_ref, cls_ref):
    def branch(o_ref, w_ref, gate_ref):
        y = jnp.dot(o_ref[...], w_ref[...], preferred_element_type=F32)
        return gate_ref[...].astype(F32) * y

    merged = branch(oa_ref, wa_ref, ga_ref) + branch(ob_ref, wb_ref, gb_ref) + branch(oc_ref, wc_ref, gc_ref)
    y = jnp.dot(merged.astype(BF16), wo_ref[...], preferred_element_type=F32)
    x1 = _layer_norm(DN_ALPHA * x_ref[...] + y, g_ref[...], b_ref[...])
    w_lo, w_hi = _router(x1, rw_ref, rb_ref, cls_ref)
    x1_ref[:, :D_MODEL] = x1
    x1_ref[:, D_MODEL:] = jnp.concatenate([w_lo, w_hi, jnp.zeros((LANES - 2, x1.shape[0]), F32)], axis=0).T


def _merge(oa, ob, oc, h_plain, x, wa, wb, wc, wo, ln_g, ln_b, rwT, rb, *, tm):
    t_tokens = x.shape[0]
    gate_blk = COL_GATES // D_MODEL
    full = lambda shape: pl.BlockSpec(shape, lambda i: (0,) * len(shape))
    return pl.pallas_call(
        _merge_body,
        grid=(t_tokens // tm,),
        in_specs=[
            pl.BlockSpec((tm, A_W), lambda i: (i, 0)),
            pl.BlockSpec((tm, B_W), lambda i: (i, 0)),
            pl.BlockSpec((tm, C_W), lambda i: (i, 0)),
            pl.BlockSpec((tm, D_MODEL), lambda i: (i, gate_blk)),
            pl.BlockSpec((tm, D_MODEL), lambda i: (i, gate_blk + 1)),
            pl.BlockSpec((tm, D_MODEL), lambda i: (i, gate_blk + 2)),
            pl.BlockSpec((tm, D_MODEL), lambda i: (i, 0)),
            full((A_W, D_MODEL)), full((B_W, D_MODEL)), full((C_W, D_MODEL)), full((D_MODEL, D_MODEL)),
            full((1, D_MODEL)), full((1, D_MODEL)), full((N_EXPERTS, D_MODEL)), full((N_EXPERTS, 1)),
        ],
        out_specs=[
            pl.BlockSpec((tm, X1_COLS), lambda i: (i, 0)),
            pl.BlockSpec((1, tm), lambda i: (0, i)),
        ],
        out_shape=[
            jax.ShapeDtypeStruct((t_tokens, X1_COLS), F32),
            jax.ShapeDtypeStruct((1, t_tokens), jnp.int32),
        ],
        compiler_params=_cparams(1),
        name="merge_ln_router",
    )(oa, ob, oc, h_plain, h_plain, h_plain, x, wa, wb, wc, wo, ln_g, ln_b, rwT, rb)


def _gather_start(idx_ref, src_hbm, dst_ref, sem, n_rows):
    def start(r, _):
        pltpu.make_async_copy(src_hbm.at[pl.ds(idx_ref[0, 0, r], 1)], dst_ref.at[pl.ds(r, 1)], sem).start()
        return 0

    lax.fori_loop(0, n_rows, start, 0, unroll=8)


def _gather_wait(src_hbm, dst_ref, sem, n_rows):
    pltpu.make_async_copy(src_hbm.at[pl.ds(0, n_rows)], dst_ref, sem).wait()


def _experts_body(e1_ref, e2_ref, nused_ref, src_ref, src_next_ref, x_hbm,
                  wg1_ref, wu1_ref, wd1_ref, wg2_ref, wu2_ref, wd2_ref, o_ref, xbuf, sem):
    s = pl.program_id(0)
    nused = nused_ref[0]
    slot = lax.rem(s, 2)

    @pl.when(s == 0)
    def _():
        _gather_start(src_ref, x_hbm, xbuf.at[0], sem.at[0], MOE_TILE)

    @pl.when(s + 1 < nused)
    def _():
        _gather_start(src_next_ref, x_hbm, xbuf.at[1 - slot], sem.at[1 - slot], MOE_TILE)

    @pl.when(s < nused)
    def _():
        _gather_wait(x_hbm, xbuf.at[slot], sem.at[slot], MOE_TILE)
        xb = xbuf[slot, :, :D_MODEL].astype(BF16)
        wt = xbuf[slot, :, D_MODEL:D_MODEL + 2]

        def expert(wg_ref, wu_ref, wd_ref):
            g = jnp.dot(xb, wg_ref[0], preferred_element_type=F32)
            u = jnp.dot(xb, wu_ref[0], preferred_element_type=F32)
            he = (g * jax.nn.sigmoid(g) * u).astype(BF16)
            return jnp.dot(he, wd_ref[0], preferred_element_type=F32)

        o_ref[...] = wt[:, 0:1] * expert(wg1_ref, wu1_ref, wd1_ref) + wt[:, 1:2] * expert(wg2_ref, wu2_ref, wd2_ref)

    @pl.when(s >= nused)
    def _():
        o_ref[...] = jnp.zeros_like(o_ref)


def _experts(tile_e1, tile_e2, nused, src3, x1, wg, wu, wd):
    ntiles = src3.shape[0]
    wspec_up = lambda which: pl.BlockSpec((1, D_MODEL, D_EXPERT), lambda s, e1, e2, nu: ((e1, e2)[which][s], 0, 0))
    wspec_dn = lambda which: pl.BlockSpec((1, D_EXPERT, D_MODEL), lambda s, e1, e2, nu: ((e1, e2)[which][s], 0, 0))
    grid_spec = pltpu.PrefetchScalarGridSpec(
        num_scalar_prefetch=3,
        grid=(ntiles,),
        in_specs=[
            pl.BlockSpec((1, 1, MOE_TILE), lambda s, e1, e2, nu: (s, 0, 0), memory_space=pltpu.SMEM),
            pl.BlockSpec((1, 1, MOE_TILE), lambda s, e1, e2, nu: (jnp.minimum(s + 1, ntiles - 1), 0, 0),
                         memory_space=pltpu.SMEM),
            pl.BlockSpec(memory_space=pl.ANY),
            wspec_up(0), wspec_up(0), wspec_dn(0), wspec_up(1), wspec_up(1), wspec_dn(1),
        ],
        out_specs=pl.BlockSpec((MOE_TILE, D_MODEL), lambda s, e1, e2, nu: (s, 0)),
        scratch_shapes=[pltpu.VMEM((2, MOE_TILE, X1_COLS), F32), pltpu.SemaphoreType.DMA((2,))],
    )
    return pl.pallas_call(
        _experts_body,
        grid_spec=grid_spec,
        out_shape=jax.ShapeDtypeStruct((ntiles * MOE_TILE, D_MODEL), F32),
        compiler_params=_cparams(1),
        name="moe_experts",
    )(tile_e1, tile_e2, nused, src3, src3, x1, wg, wu, wd, wg, wu, wd)


def _final_body(pos_ref, pos_next_ref, y_hbm, x_ref, g_ref, b_ref, o_ref, ob_ref, ybuf, sem, *, tm):
    s = pl.program_id(0)
    slot = lax.rem(s, 2)

    @pl.when(s == 0)
    def _():
        _gather_start(pos_ref, y_hbm, ybuf.at[0], sem.at[0], tm)

    @pl.when(s + 1 < pl.num_programs(0))
    def _():
        _gather_start(pos_next_ref, y_hbm, ybuf.at[1 - slot], sem.at[1 - slot], tm)

    _gather_wait(y_hbm, ybuf.at[slot], sem.at[slot], tm)
    x2 = _layer_norm(DN_ALPHA * x_ref[...] + ybuf[slot], g_ref[...], b_ref[...])
    o_ref[...] = x2
    ob_ref[...] = x2.astype(BF16)


def _final(pos3, y_sorted, x1, ln_g, ln_b, *, tm):
    t_tokens = x1.shape[0]
    nsteps = t_tokens // tm
    return pl.pallas_call(
        functools.partial(_final_body, tm=tm),
        grid=(nsteps,),
        in_specs=[
            pl.BlockSpec((1, 1, tm), lambda i: (i, 0, 0), memory_space=pltpu.SMEM),
            pl.BlockSpec((1, 1, tm), lambda i: (jnp.minimum(i + 1, nsteps - 1), 0, 0), memory_space=pltpu.SMEM),
            pl.BlockSpec(memory_space=pl.ANY),
            pl.BlockSpec((tm, D_MODEL), lambda i: (i, 0)),
            pl.BlockSpec((1, D_MODEL), lambda i: (0, 0)),
            pl.BlockSpec((1, D_MODEL), lambda i: (0, 0)),
        ],
        out_specs=[pl.BlockSpec((tm, D_MODEL), lambda i: (i, 0)), pl.BlockSpec((tm, D_MODEL), lambda i: (i, 0))],
        out_shape=[jax.ShapeDtypeStruct((t_tokens, D_MODEL), F32), jax.ShapeDtypeStruct((t_tokens, D_MODEL), BF16)],
        scratch_shapes=[pltpu.VMEM((2, tm, D_MODEL), F32), pltpu.SemaphoreType.DMA((2,))],
        compiler_params=_cparams(1),
        name="moe_combine_ln",
    )(pos3, pos3, y_sorted, x1, ln_g, ln_b)


def _rope_tables(seq):
    inv = 1.0 / (ROPE_THETA ** (jnp.arange(0, HEAD_DIM, 2, dtype=F32) / HEAD_DIM))
    ang = jnp.arange(seq, dtype=F32)[:, None] * inv[None, :]
    return jnp.cos(ang), jnp.sin(ang)


def _halves_layout(w_heads):
    d, n = w_heads.shape
    t = w_heads.reshape(d, n // PROJ_TN, HEADS_PER_TILE, 2, HEAD_DIM // 2)
    return t.transpose(0, 1, 3, 2, 4).reshape(d, n)


def _reorder_w_in(w):
    pts = np.cumsum((0,) + IN_SIZES)
    sec = [w[:, pts[k]:pts[k + 1]] for k in range(len(IN_SIZES))]
    a_q, a_c, a_iq, a_ik, a_iw, b_q, b_k, b_v, c_q, c_k, c_v, gates = sec
    zeros = lambda n: jnp.zeros((w.shape[0], n), w.dtype)
    w_rope = _halves_layout(jnp.concatenate([a_q, a_iq, b_q, b_k, c_q, c_k, a_ik, zeros(HEAD_DIM)], axis=1))
    w_plain = jnp.concatenate([gates, b_v, a_c, a_iw, zeros(LANES - A_IDX_HEADS), c_v, zeros(LANES)], axis=1)
    assert w_rope.shape[1] == N_ROPE and w_plain.shape[1] == N_PLAIN
    return w_rope.astype(BF16), w_plain.astype(BF16)


def _kv_weights(w_uk, w_uv):
    half = HEAD_DIM // 2
    zeros = lambda n: jnp.zeros((w_uk.shape[0], n), w_uk.dtype)
    return jnp.concatenate([w_uk[:, :half], w_uv, zeros(HALF_TN - half - HEAD_DIM),
                            w_uk[:, half:], zeros(HALF_TN - half)], axis=1).astype(BF16)


def _moe_plan(cls, n_tokens):
    ntiles = n_tokens // MOE_TILE + N_CLASSES
    onehot = (cls[:, None] == jnp.arange(N_CLASSES, dtype=jnp.int32)[None, :]).astype(jnp.int32)
    csum = jnp.cumsum(onehot, axis=0)
    rank = jnp.sum(onehot * csum, axis=1) - 1
    counts = csum[-1]
    ptiles = (counts + MOE_TILE - 1) // MOE_TILE
    tile_end = jnp.cumsum(ptiles)
    tile_start = tile_end - ptiles
    pos = (tile_start[cls] * MOE_TILE + rank).astype(jnp.int32)
    nused = tile_end[-1:].astype(jnp.int32)
    tile_ids = jnp.arange(ntiles, dtype=jnp.int32)
    tile_cls = jnp.minimum(jnp.sum((tile_ids[:, None] >= tile_end[None, :]).astype(jnp.int32), axis=1),
                           N_CLASSES - 1).astype(jnp.int32)
    pair = np.array(PAIRS, dtype=np.int32)
    e_lo = jnp.asarray(np.repeat(np.arange(N_GROUPS), len(PAIRS)) * EXPERTS_PER_GROUP + np.tile(pair[:, 0], N_GROUPS), jnp.int32)
    e_hi = jnp.asarray(np.repeat(np.arange(N_GROUPS), len(PAIRS)) * EXPERTS_PER_GROUP + np.tile(pair[:, 1], N_GROUPS), jnp.int32)
    src = jnp.zeros((ntiles * MOE_TILE,), jnp.int32).at[pos].set(jnp.arange(n_tokens, dtype=jnp.int32))
    return e_lo[tile_cls], e_hi[tile_cls], nused, src.reshape(ntiles, 1, MOE_TILE), pos


def kernel(x, w_in, a_w_uk, a_w_uv, c_sinks, w_branch, w_o, ln1_g, ln1_b, router_w, router_b,
           moe_w_gate, moe_w_up, moe_w_down, ln2_g, ln2_b):
    bsz, seq, _ = x.shape
    n_tokens = bsz * seq
    topk = min(A_TOPK_MAX, seq // 4)
    nkb = seq // B_BLOCK
    tq = ATTN_TQ
    kc = min(DSA_KC, seq)
    tm_proj = min(PROJ_TM, seq)
    tm_row = ROW_TM
    half = HEAD_DIM // 2

    cos, sin = _rope_tables(seq)
    cos_in = jnp.tile(cos, (1, HEADS_PER_TILE))
    sin_in = jnp.tile(sin, (1, HEADS_PER_TILE))
    cos_kv = jnp.concatenate([cos, jnp.ones((seq, HALF_TN - half), F32)], axis=1)
    sin_kv = jnp.concatenate([sin, jnp.zeros((seq, HALF_TN - half), F32)], axis=1)
    rwT = router_w.T
    rb = router_b.reshape(N_EXPERTS, 1)

    xf = x.reshape(n_tokens, D_MODEL)
    xb = xf.astype(BF16)
    for l in range(DEPTH):
        w_rope, w_plain = _reorder_w_in(w_in[l])
        h_rope = _proj_rope(xb, 0, D_MODEL, w_rope, cos_in, sin_in, tm=tm_proj, seq=seq)
        h_plain = _proj_plain(xb, w_plain, tm=tm_proj)
        kv = _proj_rope(h_plain, COL_AC // A_KV_RANK, A_KV_RANK, _kv_weights(a_w_uk[l], a_w_uv[l]),
                        cos_kv, sin_kv, tm=tm_proj, seq=seq)

        bvT, cvT, kvT = _values_T(h_plain, kv, n_tokens=n_tokens)

        o_a = _dsa(h_rope, h_plain, kv, kvT, bsz=bsz, seq=seq, tq=min(DSA_TQ, seq), kc=kc, topk=topk)
        kmean = _moba_kmean(h_rope, n_tokens=n_tokens).reshape(bsz, nkb, B_W)
        o_b = _moba(h_rope, bvT, kmean, bsz=bsz, seq=seq)
        o_c = _swa(c_sinks[l], h_rope, cvT, bsz=bsz, seq=seq, tq=tq)

        wb_all = w_branch[l].astype(BF16)
        x1, cls = _merge(o_a, o_b, o_c, h_plain, xf,
                              wb_all[:A_W], wb_all[A_W:A_W + B_W], wb_all[A_W + B_W:], w_o[l].astype(BF16),
                              ln1_g[l].reshape(1, D_MODEL), ln1_b[l].reshape(1, D_MODEL), rwT, rb, tm=tm_row)

        e1, e2, nused, src3, pos = _moe_plan(cls[0], n_tokens)
        y_sorted = _experts(e1, e2, nused, src3, x1,
                            moe_w_gate[l].astype(BF16), moe_w_up[l].astype(BF16), moe_w_down[l].astype(BF16))
        xf, xb = _final(pos.reshape(n_tokens // tm_row, 1, tm_row), y_sorted, x1,
                        ln2_g[l].reshape(1, D_MODEL), ln2_b[l].reshape(1, D_MODEL), tm=tm_row)
    return xf.reshape(bsz, seq, D_MODEL)
```
